```python
import jax, jax.numpy as jnp
from jax import lax
import numpy as np

D_MODEL = 2048
BATCH = 32
SEQ = 256
DEPTH = 1
DEC_BATCH = 4
DEC_SEQ = 4096
PAST_LEN = 256

GRID_W = 64
N_MOD = 6
EPS = 1e-6

GLA_HEADS = 4
GLA_DK = D_MODEL // 16
GLA_DV = D_MODEL // 8
GLA_KW = GLA_HEADS * GLA_DK
GLA_VW = GLA_HEADS * GLA_DV
GLA_RANK = 16
GLA_TAU = 16.0
GLA_CHUNK = 64

SWA_HEADS = 16
SWA_KV_HEADS = 4
SWA_GROUP = SWA_HEADS // SWA_KV_HEADS
SWA_HEAD_DIM = 64
SWA_QW = SWA_HEADS * SWA_HEAD_DIM
SWA_KVW = SWA_KV_HEADS * SWA_HEAD_DIM
WINDOW = 128
SWA_BLOCK = 128
ROPE_BASE = 10000.0

N_EXPERTS = 16
EXPERT_FF = D_MODEL // 2
CAPACITY_FACTOR = 2

IN_WIDTHS = (GLA_KW, GLA_KW, GLA_VW, GLA_VW, GLA_RANK, GLA_RANK, SWA_QW, SWA_KVW, SWA_KVW, D_MODEL, D_MODEL)
IN_DIM = 2 * GLA_KW + 2 * GLA_VW + 2 * GLA_RANK + SWA_QW + 2 * SWA_KVW + 2 * D_MODEL

kernel_name = 'hybrid_gla_swa_ec_diffusion_step'


def rmsnorm(x, g):
    xf = x.astype(jnp.float32)
    y = xf * lax.rsqrt(jnp.mean(xf * xf, axis=-1, keepdims=True) + EPS)
    return (y * g.astype(jnp.float32)).astype(x.dtype)


def _split_in(p):
    cuts = [int(v) for v in np.cumsum(IN_WIDTHS)[:-1]]
    return jnp.split(p, cuts, axis=-1)


def axial_rope(x):
    s = x.shape[1]
    rows = s // GRID_W
    row = jnp.repeat(jnp.arange(rows), GRID_W).astype(jnp.float32)
    col = jnp.tile(jnp.arange(GRID_W), rows).astype(jnp.float32)
    half = SWA_HEAD_DIM // 2
    npair = half // 2
    inv_freq = ROPE_BASE ** (-jnp.arange(npair, dtype=jnp.float32) / npair)
    xf = x.astype(jnp.float32)

    def rot(xa, pos):
        ang = pos[:, None] * inv_freq[None, :]
        cos = jnp.cos(ang)[None, :, None, :]
        sin = jnp.sin(ang)[None, :, None, :]
        x1, x2 = xa[..., :npair], xa[..., npair:]
        return jnp.concatenate([x1 * cos - x2 * sin, x2 * cos + x1 * sin], axis=-1)

    out = jnp.concatenate([rot(xf[..., :half], row), rot(xf[..., half:], col)], axis=-1)
    return out.astype(x.dtype)


def gla_chunk_scan(q, k, v, g, s0):
    bsz, nh, s, dk = q.shape
    dv = v.shape[-1]
    c = GLA_CHUNK
    n = s // c
    q, k, g = [t.reshape(bsz, nh, n, c, dk) for t in (q, k, g)]
    v = v.reshape(bsz, nh, n, c, dv)
    b = jnp.cumsum(g, axis=3)
    b_last = b[:, :, :, -1:, :]
    q_in = q * jnp.exp(b)
    k_in = k * jnp.exp(-b)
    k_st = k * jnp.exp(b_last - b)
    causal = jnp.tril(jnp.ones((c, c), dtype=bool))
    a = jnp.where(causal, jnp.einsum('bhnik,bhnjk->bhnij', q_in, k_in), 0.0)
    o_intra = jnp.einsum('bhnij,bhnjv->bhniv', a, v)
    decay = jnp.exp(b_last[:, :, :, 0, :])

    def step(state, inp):
        q_c, k_c, v_c, d_c = inp
        o_c = jnp.einsum('bhik,bhkv->bhiv', q_c, state)
        state = d_c[..., None] * state + jnp.einsum('bhik,bhiv->bhkv', k_c, v_c)
        return state, o_c

    xs = (jnp.moveaxis(q_in, 2, 0), jnp.moveaxis(k_st, 2, 0), jnp.moveaxis(v, 2, 0), jnp.moveaxis(decay, 2, 0))
    s_fin, o_inter = lax.scan(step, s0, xs)
    o = o_intra + jnp.moveaxis(o_inter, 0, 2)
    return o.reshape(bsz, nh, s, dv), s_fin


def _to_heads(t, n_heads, d):
    bsz, s, _ = t.shape
    return t.reshape(bsz, s, n_heads, d).transpose(0, 2, 1, 3).astype(jnp.float32)


def gla_mixer(gq, gk, gv, gr, gaf, gab, w_dec_f, b_dec_f, w_dec_b, b_dec_b, g_gla, s0_f, s0_b):
    bsz, s, _ = gq.shape
    q = _to_heads(gq, GLA_HEADS, GLA_DK) * (GLA_DK ** -0.5)
    k = _to_heads(gk, GLA_HEADS, GLA_DK)
    v = _to_heads(gv, GLA_HEADS, GLA_DV)
    log_a_f = jax.nn.log_sigmoid((gaf @ w_dec_f + b_dec_f).astype(jnp.float32)) / GLA_TAU
    log_a_b = jax.nn.log_sigmoid((gab @ w_dec_b + b_dec_b).astype(jnp.float32)) / GLA_TAU
    g_f = _to_heads(log_a_f, GLA_HEADS, GLA_DK)
    g_b = _to_heads(log_a_b, GLA_HEADS, GLA_DK)
    o_f, st_f = gla_chunk_scan(q, k, v, g_f, s0_f.astype(jnp.float32))
    flip = lambda t: jnp.flip(t, axis=2)
    o_b, st_b = gla_chunk_scan(flip(q), flip(k), flip(v), flip(g_b), s0_b.astype(jnp.float32))
    o = o_f + flip(o_b)
    o = o * lax.rsqrt(jnp.mean(o * o, axis=-1, keepdims=True) + EPS) * g_gla.astype(jnp.float32).reshape(GLA_HEADS, 1, GLA_DV)
    o = o.transpose(0, 2, 1, 3).reshape(bsz, s, GLA_VW) * jax.nn.silu(gr.astype(jnp.float32))
    return o.astype(gq.dtype), st_f, st_b


def sink_softmax(score_list, sink):
    s_sink = sink.astype(jnp.float32).reshape(SWA_KV_HEADS, SWA_GROUP, 1, 1)
    m = s_sink
    for sc in score_list:
        m = jnp.maximum(m, jnp.max(sc, axis=-1, keepdims=True))
    ps = [jnp.exp(sc - m) for sc in score_list]
    den = jnp.exp(s_sink - m)
    for p in ps:
        den = den + jnp.sum(p, axis=-1, keepdims=True)
    return [p / den for p in ps]


def context_attention(q, k, v, sink):
    bsz, s, _, _ = q.shape
    nb = s // SWA_BLOCK
    scale = SWA_HEAD_DIM ** -0.5
    qb = jnp.moveaxis(q.reshape(bsz, nb, SWA_BLOCK, SWA_KV_HEADS, SWA_GROUP, SWA_HEAD_DIM), 1, 0).astype(jnp.float32)
    kf = k.astype(jnp.float32)
    vf = v.astype(jnp.float32)

    def one_block(q_blk):
        sc = jnp.einsum('bqhgd,bkhd->bhgqk', q_blk, kf) * scale
        (p,) = sink_softmax([sc], sink)
        return jnp.einsum('bhgqk,bkhd->bqhgd', p, vf)

    o = lax.map(one_block, qb)
    return jnp.moveaxis(o, 0, 1).reshape(bsz, s, SWA_QW).astype(q.dtype)


def latent_attention(q, k, v, ctx_k, ctx_v, sink):
    bsz, s, _, _ = q.shape
    blk = SWA_BLOCK
    nb = s // blk
    scale = SWA_HEAD_DIM ** -0.5
    pad = ((0, 0), (blk, blk), (0, 0), (0, 0))

    def band(t):
        tp = jnp.pad(t.astype(jnp.float32), pad).reshape(bsz, nb + 2, blk, SWA_KV_HEADS, SWA_HEAD_DIM)
        return jnp.concatenate([tp[:, :-2], tp[:, 1:-1], tp[:, 2:]], axis=2)

    kw, vw = band(k), band(v)
    qb = q.reshape(bsz, nb, blk, SWA_KV_HEADS, SWA_GROUP, SWA_HEAD_DIM).astype(jnp.float32)
    ckf = ctx_k.astype(jnp.float32)
    cvf = ctx_v.astype(jnp.float32)
    s_ctx = jnp.einsum('bnqhgd,bkhd->bnhgqk', qb, ckf) * scale
    s_band = jnp.einsum('bnqhgd,bnkhd->bnhgqk', qb, kw) * scale
    qi = jnp.arange(blk)[:, None]
    kj = jnp.arange(3 * blk)[None, :]
    in_window = jnp.abs(kj - blk - qi) <= WINDOW
    k_abs = jnp.arange(nb)[:, None] * blk - blk + jnp.arange(3 * blk)[None, :]
    in_seq = (k_abs >= 0) & (k_abs < s)
    mask = in_window[None, :, :] & in_seq[:, None, :]
    s_band = jnp.where(mask[None, :, None, None], s_band, -jnp.inf)
    p_ctx, p_band = sink_softmax([s_ctx, s_band], sink)
    o = jnp.einsum('bnhgqk,bkhd->bnqhgd', p_ctx, cvf) + jnp.einsum('bnhgqk,bnkhd->bnqhgd', p_band, vw)
    return o.reshape(bsz, s, SWA_QW).astype(q.dtype)


def expert_choice_ffn(h, w_router, w_gate, w_up, w_down):
    shp = h.shape
    xf = h.reshape(-1, shp[-1])
    n = xf.shape[0]
    cap = CAPACITY_FACTOR * n // N_EXPERTS
    aff = jax.nn.softmax((xf @ w_router).astype(jnp.float32), axis=-1)
    gate, idx = lax.top_k(aff.T, cap)
    xs = xf[idx]
    hid = jax.nn.silu(jnp.einsum('ecd,edf->ecf', xs, w_gate)) * jnp.einsum('ecd,edf->ecf', xs, w_up)
    ye = jnp.einsum('ecf,efd->ecd', hid, w_down) * gate[..., None].astype(xf.dtype)
    y = jnp.zeros_like(xf).at[idx.reshape(-1)].add(ye.reshape(-1, shp[-1]))
    return y.reshape(shp)


def trunk_layer(x, cond, is_latent, s0_f, s0_b, ctx_k, ctx_v,
                w_mod, b_mod, g_norm1, w_in, w_dec_f, b_dec_f, w_dec_b, b_dec_b, g_gla,
                attn_sink, w_branch_a, w_branch_b, w_out, g_norm2, w_router,
                w_exp_gate, w_exp_up, w_exp_down):
    bsz, s, _ = x.shape
    mod = jax.nn.silu(cond) @ w_mod + b_mod
    sh1, sc1, gt1, sh2, sc2, gt2 = [m[:, None, :] for m in jnp.split(mod, N_MOD, axis=-1)]
    h = rmsnorm(x, g_norm1) * (1 + sc1) + sh1
    gq, gk, gv, gr, gaf, gab, sq, sk, sv, gate_a, gate_b = _split_in(h @ w_in)
    o_gla, st_f, st_b = gla_mixer(gq, gk, gv, gr, gaf, gab, w_dec_f, b_dec_f, w_dec_b, b_dec_b, g_gla, s0_f, s0_b)
    q = sq.reshape(bsz, s, SWA_HEADS, SWA_HEAD_DIM)
    k = sk.reshape(bsz, s, SWA_KV_HEADS, SWA_HEAD_DIM)
    v = sv.reshape(bsz, s, SWA_KV_HEADS, SWA_HEAD_DIM)
    if is_latent:
        o_swa = latent_attention(axial_rope(q), axial_rope(k), v, ctx_k, ctx_v, attn_sink)
    else:
        o_swa = context_attention(q, k, v, attn_sink)
    merged = jax.nn.sigmoid(gate_a) * (o_gla @ w_branch_a) + jax.nn.sigmoid(gate_b) * (o_swa @ w_branch_b)
    x = x + gt1 * (merged @ w_out)
    h2 = rmsnorm(x, g_norm2) * (1 + sc2) + sh2
    x = x + gt2 * expert_choice_ffn(h2, w_router, w_exp_gate, w_exp_up, w_exp_down)
    return x, (st_f, st_b, k, v)


def setup_inputs(seed: int = 0) -> dict:
    key = jax.random.key(seed)
    ks = jax.random.split(key, 32)
    f32 = jnp.float32

    def nrm(k, shape, scale=1.0):
        return jax.random.normal(k, shape, f32) * scale

    L = DEPTH
    return {
        'x_prompt': nrm(ks[0], (BATCH, SEQ, D_MODEL)),
        'x_sample': nrm(ks[1], (DEC_BATCH, DEC_SEQ, D_MODEL)),
        'state_gla_fwd': nrm(ks[2], (DEC_BATCH, L, GLA_HEADS, GLA_DK, GLA_DV)),
        'state_gla_bwd': nrm(ks[3], (DEC_BATCH, L, GLA_HEADS, GLA_DK, GLA_DV)),
        'cache_k': nrm(ks[4], (DEC_BATCH, L, PAST_LEN, SWA_KV_HEADS, SWA_HEAD_DIM)),
        'cache_v': nrm(ks[5], (DEC_BATCH, L, PAST_LEN, SWA_KV_HEADS, SWA_HEAD_DIM)),
        'c': nrm(ks[6], (DEC_BATCH, D_MODEL)),
        'c_ctx': nrm(ks[7], (D_MODEL,)),
        'w_mod': nrm(ks[8], (L, D_MODEL, N_MOD * D_MODEL), 0.5 * D_MODEL ** -0.5),
        'b_mod': nrm(ks[9], (L, N_MOD * D_MODEL), 0.02),
        'g_norm1': 1.0 + nrm(ks[10], (L, D_MODEL), 0.02),
        'w_in': nrm(ks[11], (L, D_MODEL, IN_DIM), D_MODEL ** -0.5),
        'w_dec_f': nrm(ks[12], (L, GLA_RANK, GLA_KW), GLA_RANK ** -0.5),
        'b_dec_f': nrm(ks[13], (L, GLA_KW), 0.1),
        'w_dec_b': nrm(ks[14], (L, GLA_RANK, GLA_KW), GLA_RANK ** -0.5),
        'b_dec_b': nrm(ks[15], (L, GLA_KW), 0.1),
        'g_gla': 1.0 + nrm(ks[16], (L, GLA_VW), 0.02),
        'attn_sink': nrm(ks[17], (L, SWA_HEADS), 0.5),
        'w_branch_a': nrm(ks[18], (L, GLA_VW, D_MODEL), GLA_VW ** -0.5),
        'w_branch_b': nrm(ks[19], (L, SWA_QW, D_MODEL), SWA_QW ** -0.5),
        'w_out': nrm(ks[20], (L, D_MODEL, D_MODEL), D_MODEL ** -0.5),
        'g_norm2': 1.0 + nrm(ks[21], (L, D_MODEL), 0.02),
        'w_router': nrm(ks[22], (L, D_MODEL, N_EXPERTS), D_MODEL ** -0.5),
        'w_exp_gate': nrm(ks[23], (L, N_EXPERTS, D_MODEL, EXPERT_FF), D_MODEL ** -0.5),
        'w_exp_up': nrm(ks[24], (L, N_EXPERTS, D_MODEL, EXPERT_FF), D_MODEL ** -0.5),
        'w_exp_down': nrm(ks[25], (L, N_EXPERTS, EXPERT_FF, D_MODEL), EXPERT_FF ** -0.5),
        'g_final': 1.0 + nrm(ks[26], (D_MODEL,), 0.02),
    }


def reference(x_prompt, x_sample, state_gla_fwd, state_gla_bwd, cache_k, cache_v, c, c_ctx,
              w_mod, b_mod, g_norm1, w_in, w_dec_f, b_dec_f, w_dec_b, b_dec_b, g_gla, attn_sink,
              w_branch_a, w_branch_b, w_out, g_norm2, w_router, w_exp_gate, w_exp_up, w_exp_down,
              g_final):
    bp = x_prompt.shape[0]
    h_ctx = x_prompt
    h_lat = x_sample
    zero_state = jnp.zeros((bp, GLA_HEADS, GLA_DK, GLA_DV), jnp.float32)
    sf_list, sb_list, k_list, v_list = [], [], [], []
    for l in range(DEPTH):
        lp = (w_mod[l], b_mod[l], g_norm1[l], w_in[l], w_dec_f[l], b_dec_f[l], w_dec_b[l], b_dec_b[l],
              g_gla[l], attn_sink[l], w_branch_a[l], w_branch_b[l], w_out[l], g_norm2[l], w_router[l],
              w_exp_gate[l], w_exp_up[l], w_exp_down[l])
        h_ctx, (st_f, st_b, k_ctx, v_ctx) = trunk_layer(h_ctx, c_ctx[None, :], False, zero_state, zero_state,
                                                        None, None, *lp)
        sf_list.append(st_f)
        sb_list.append(st_b)
        k_list.append(k_ctx)
        v_list.append(v_ctx)
        h_lat, _ = trunk_layer(h_lat, c, True, state_gla_fwd[:, l], state_gla_bwd[:, l],
                               cache_k[:, l], cache_v[:, l], *lp)
    y_prompt = rmsnorm(h_ctx, g_final)
    y_sample = rmsnorm(h_lat, g_final)
    new_state_gla_fwd = jnp.stack(sf_list, axis=1)
    new_state_gla_bwd = jnp.stack(sb_list, axis=1)
    new_cache_k = jnp.stack(k_list, axis=1)
    new_cache_v = jnp.stack(v_list, axis=1)
    return (y_prompt, y_sample, new_state_gla_fwd, new_state_gla_bwd, new_cache_k, new_cache_v)
```

```python
import functools

import jax
import jax.numpy as jnp
from jax import lax
from jax.experimental import pallas as pl
from jax.experimental.pallas import tpu as pltpu

F32 = jnp.float32
BF16 = jnp.bfloat16
HIGHEST = lax.Precision.HIGHEST

D_MODEL = 2048
N_MOD = 6
EPS = 1e-6

GLA_HEADS = 4
GLA_DK = 128
GLA_DV = 256
GLA_KW = GLA_HEADS * GLA_DK
GLA_VW = GLA_HEADS * GLA_DV
GLA_RANK = 16
GLA_TAU = 16.0
GLA_CHUNK = 64

SWA_HEADS = 16
SWA_KV_HEADS = 4
SWA_GROUP = 4
SWA_HEAD_DIM = 64
SWA_QW = SWA_HEADS * SWA_HEAD_DIM
SWA_KVW = SWA_KV_HEADS * SWA_HEAD_DIM
SWA_BLOCK = 128
GRID_W = 64
ROPE_BASE = 10000.0

N_EXPERTS = 16
EXPERT_FF = D_MODEL // 2
CAPACITY_FACTOR = 2

LANES = 128
ROW_CHUNKS = D_MODEL // LANES
VMEM_LIMIT = 56 * 1024 * 1024

COL_GA = 0
COL_GB = 2048
COL_GQ = 4096
COL_GK = 4608
COL_GV = 5120
COL_GR = 6144
COL_SQ = 7168
COL_SK = 8192
COL_SV = 8448
COL_DEC = 8704
IN_PAD = 8960
IN_TN = 1280


def _cparams(sem, **kw):
    return pltpu.CompilerParams(dimension_semantics=sem, vmem_limit_bytes=VMEM_LIMIT, **kw)


def _silu(x):
    return x * (1.0 / (1.0 + jnp.exp(-x)))


def _sigmoid(x):
    return 1.0 / (1.0 + jnp.exp(-x))


def _tiles_per_mod_row(mod, n, seq, tile):
    return seq // tile if mod.shape[0] > 1 else n // tile


def _mod_kernel(c_ref, w_ref, b_ref, o_ref):
    a = _silu(c_ref[...]).astype(BF16)
    o_ref[...] = jnp.dot(a, w_ref[...].astype(BF16), preferred_element_type=F32) + b_ref[...]


def _modulation(cond8, w_mod, b_mod):
    n_out = w_mod.shape[1]
    tn = 1536
    return pl.pallas_call(
        _mod_kernel,
        grid=(n_out // tn,),
        in_specs=[pl.BlockSpec((8, D_MODEL), lambda j: (0, 0)),
                  pl.BlockSpec((D_MODEL, tn), lambda j: (0, j)),
                  pl.BlockSpec((1, tn), lambda j: (0, j))],
        out_specs=pl.BlockSpec((8, tn), lambda j: (0, j)),
        out_shape=jax.ShapeDtypeStruct((8, n_out), F32),
        compiler_params=_cparams(("arbitrary",)),
        name="modulation",
    )(cond8, w_mod, b_mod.reshape(1, n_out))


def _in_proj_kernel(x_ref, sc_ref, sh_ref, g_ref, w_ref, o_ref, h_scr):
    @pl.when(pl.program_id(1) == 0)
    def _():
        x = x_ref[...]
        y = x * lax.rsqrt(jnp.mean(x * x, axis=-1, keepdims=True) + EPS) * g_ref[...]
        h_scr[...] = (y * (1.0 + sc_ref[0]) + sh_ref[0]).astype(BF16)

    o_ref[...] = jnp.dot(h_scr[...], w_ref[...], preferred_element_type=F32)


def _in_proj(x2d, seq, sc, sh, g, w_r):
    n = x2d.shape[0]
    tm = min(1024, seq)
    per_b = _tiles_per_mod_row(sc, n, seq, tm)
    return pl.pallas_call(
        _in_proj_kernel,
        grid=(n // tm, IN_PAD // IN_TN),
        in_specs=[pl.BlockSpec((tm, D_MODEL), lambda i, j: (i, 0)),
                  pl.BlockSpec((1, 1, D_MODEL), lambda i, j: (i // per_b, 0, 0)),
                  pl.BlockSpec((1, 1, D_MODEL), lambda i, j: (i // per_b, 0, 0)),
                  pl.BlockSpec((1, D_MODEL), lambda i, j: (0, 0)),
                  pl.BlockSpec((D_MODEL, IN_TN), lambda i, j: (0, j))],
        out_specs=pl.BlockSpec((tm, IN_TN), lambda i, j: (i, j)),
        out_shape=jax.ShapeDtypeStruct((n, IN_PAD), F32),
        scratch_shapes=[pltpu.VMEM((tm, D_MODEL), BF16)],
        compiler_params=_cparams(("parallel", "arbitrary")),
        name="in_proj",
    )(x2d, sc, sh, g, w_r)


GLA_BLK = 256


def _tn_dot(a, b, precision=None):
    return lax.dot_general(a, b, (((0,), (0,)), ((), ())), precision=precision,
                           preferred_element_type=F32)


def _nt_dot(a, b, precision=None):
    return lax.dot_general(a, b, (((1,), (1,)), ((), ())), precision=precision,
                           preferred_element_type=F32)


def _gla_kernel(q_ref, k_ref, v_ref, r_ref, dec_ref, wdf_ref, bdf_ref, wdb_ref, bdb_ref, gg_ref,
                s0f_ref, s0b_ref, o_ref, stf_ref, stb_ref, state, of_scr, *, nblk):
    i = pl.program_id(2)
    c = GLA_CHUNK
    ri = lax.broadcasted_iota(jnp.int32, (c, c), 0)
    ci = lax.broadcasted_iota(jnp.int32, (c, c), 1)
    ones_cv = jnp.ones((c, GLA_DV), F32)

    @pl.when(i == 0)
    def _():
        state[...] = s0f_ref[0, 0]

    @pl.when(i == nblk)
    def _():
        state[...] = s0b_ref[0, 0]

    def run(fwd):
        blk = i if fwd else 2 * nblk - 1 - i
        keep = (ci <= ri) if fwd else (ci >= ri)
        tri = keep.astype(F32)
        w_ref, b_ref = (wdf_ref, bdf_ref) if fwd else (wdb_ref, bdb_ref)
        for cc in range(GLA_BLK // c):
            ch = cc if fwd else GLA_BLK // c - 1 - cc
            rows = pl.ds(ch * c, c)
            q = q_ref[rows, :] * (GLA_DK ** -0.5)
            k = k_ref[rows, :]
            v = v_ref[rows, :].astype(BF16)
            logit = jnp.dot(dec_ref[rows, :], w_ref[...], precision=HIGHEST,
                            preferred_element_type=F32) + b_ref[...]
            g = (jnp.minimum(logit, 0.0) - jnp.log(1.0 + jnp.exp(-jnp.abs(logit)))) / GLA_TAU
            b = jnp.dot(tri, g, precision=HIGHEST, preferred_element_type=F32)
            tot = jnp.sum(g, axis=0, keepdims=True)
            q_in = (q * jnp.exp(b)).astype(BF16)
            k_in = (k * jnp.exp(-b)).astype(BF16)
            k_st = (k * jnp.exp(tot - b)).astype(BF16)
            a = jnp.where(keep, _nt_dot(q_in, k_in), 0.0).astype(BF16)
            st = state[...]
            o = (jnp.dot(a, v, preferred_element_type=F32)
                 + jnp.dot(q_in, st.astype(BF16), preferred_element_type=F32))
            decay = jnp.exp(_tn_dot(g, ones_cv, precision=HIGHEST))
            state[...] = decay * st + _tn_dot(k_st, v)
            srow = pl.multiple_of(blk * GLA_BLK + ch * c, c)
            if fwd:
                of_scr[pl.ds(srow, c), :] = o
            else:
                o = o + of_scr[pl.ds(srow, c), :]
                o = o * lax.rsqrt(jnp.mean(o * o, axis=-1, keepdims=True) + EPS) * gg_ref[...]
                o_ref[rows, :] = (o * _silu(r_ref[rows, :])).astype(BF16)

    @pl.when(i < nblk)
    def _():
        run(True)

    @pl.when(i >= nblk)
    def _():
        run(False)

    @pl.when(i == nblk - 1)
    def _():
        stf_ref[0, 0] = state[...]

    @pl.when(i == 2 * nblk - 1)
    def _():
        stb_ref[0, 0] = state[...]


def _gla(p, bsz, seq, wdf, bdf, wdb, bdb, g_gla, s0f, s0b):
    n = p.shape[0]
    nblk = seq // GLA_BLK

    def rb(b, i):
        return b * nblk + jnp.where(i < nblk, i, 2 * nblk - 1 - i)

    def orb(b, i):
        return b * nblk + jnp.where(i < nblk, nblk - 1, 2 * nblk - 1 - i)

    st_spec = pl.BlockSpec((1, 1, GLA_DK, GLA_DV), lambda b, h, i: (b, h, 0, 0))
    st_shape = jax.ShapeDtypeStruct((bsz, GLA_HEADS, GLA_DK, GLA_DV), F32)
    return pl.pallas_call(
        functools.partial(_gla_kernel, nblk=nblk),
        grid=(bsz, GLA_HEADS, 2 * nblk),
        in_specs=[pl.BlockSpec((GLA_BLK, GLA_DK), lambda b, h, i: (rb(b, i), COL_GQ // GLA_DK + h)),
                  pl.BlockSpec((GLA_BLK, GLA_DK), lambda b, h, i: (rb(b, i), COL_GK // GLA_DK + h)),
                  pl.BlockSpec((GLA_BLK, GLA_DV), lambda b, h, i: (rb(b, i), COL_GV // GLA_DV + h)),
                  pl.BlockSpec((GLA_BLK, GLA_DV), lambda b, h, i: (rb(b, i), COL_GR // GLA_DV + h)),
                  pl.BlockSpec((GLA_BLK, LANES), lambda b, h, i: (rb(b, i), COL_DEC // LANES)),
                  pl.BlockSpec((LANES, GLA_DK), lambda b, h, i: (0, h)),
                  pl.BlockSpec((1, GLA_DK), lambda b, h, i: (0, h)),
                  pl.BlockSpec((LANES, GLA_DK), lambda b, h, i: (0, h)),
                  pl.BlockSpec((1, GLA_DK), lambda b, h, i: (0, h)),
                  pl.BlockSpec((1, GLA_DV), lambda b, h, i: (0, h)),
                  st_spec, st_spec],
        out_specs=[pl.BlockSpec((GLA_BLK, GLA_DV), lambda b, h, i: (orb(b, i), h)), st_spec, st_spec],
        out_shape=[jax.ShapeDtypeStruct((n, GLA_VW), BF16), st_shape, st_shape],
        scratch_shapes=[pltpu.VMEM((GLA_DK, GLA_DV), F32), pltpu.VMEM((seq, GLA_DV), F32)],
        compiler_params=_cparams(("parallel", "parallel", "arbitrary")),
        name="gla",
    )(p, p, p, p, p, wdf, bdf, wdb, bdb, g_gla, s0f, s0b)


def _sink_column(sink_ref, kvh, rows_per_head):
    r = lax.broadcasted_iota(jnp.int32, (SWA_GROUP * rows_per_head, 1), 0)
    col = jnp.zeros((SWA_GROUP * rows_per_head, 1), F32)
    for g in range(SWA_GROUP):
        col = jnp.where(r // rows_per_head == g, sink_ref[kvh * SWA_GROUP + g], col)
    return col


def _rope(x, cos, sin):
    outs = []
    for cb in range(x.shape[1] // LANES):
        sl = slice(cb * LANES, (cb + 1) * LANES)
        xc = x[:, sl]
        lane = lax.broadcasted_iota(jnp.int32, xc.shape, 1)
        sw = jnp.where(lane % 32 < 16, pltpu.roll(xc, LANES - 16, 1), pltpu.roll(xc, 16, 1))
        outs.append(xc * cos[:, sl] + sw * sin[:, sl])
    return jnp.concatenate(outs, axis=1)


def _swa_lat_kernel(sink_ref, q_ref, kp_ref, kc_ref, kn_ref, vp_ref, vc_ref, vn_ref, ck_ref, cv_ref,
                    cp_ref, cc_ref, cn_ref, sp_ref, sc_ref, sn_ref, o_ref, *, nb):
    n = pl.program_id(1)
    blk = SWA_BLOCK
    hd = SWA_HEAD_DIM
    cos_c, sin_c = cc_ref[...], sc_ref[...]
    kband = jnp.concatenate([_rope(kp_ref[...], cp_ref[...], sp_ref[...]),
                             _rope(kc_ref[...], cos_c, sin_c),
                             _rope(kn_ref[...], cn_ref[...], sn_ref[...])], axis=0).astype(BF16)
    vband = jnp.concatenate([vp_ref[...], vc_ref[...], vn_ref[...]], axis=0).astype(BF16)
    ck = ck_ref[0].astype(BF16)
    cv = cv_ref[0].astype(BF16)
    qi = lax.broadcasted_iota(jnp.int32, (SWA_GROUP * blk, 3 * blk), 0) % blk
    kj = lax.broadcasted_iota(jnp.int32, (SWA_GROUP * blk, 3 * blk), 1)
    k_abs = kj + (n - 1) * blk
    mask = (jnp.abs(kj - blk - qi) <= SWA_BLOCK) & (k_abs >= 0) & (k_abs < nb * blk)
    for kvh in range(SWA_KV_HEADS):
        ks = slice(kvh * hd, (kvh + 1) * hd)
        qr = _rope(q_ref[:, kvh * SWA_KVW:(kvh + 1) * SWA_KVW], cos_c, sin_c) * (hd ** -0.5)
        qg = jnp.concatenate([qr[:, g * hd:(g + 1) * hd] for g in range(SWA_GROUP)],
                             axis=0).astype(BF16)
        s_b = jnp.where(mask, _nt_dot(qg, kband[:, ks]), -1e30)
        s_c = _nt_dot(qg, ck[:, ks])
        sink = _sink_column(sink_ref, kvh, blk)
        m = jnp.maximum(jnp.maximum(jnp.max(s_b, axis=-1, keepdims=True),
                                    jnp.max(s_c, axis=-1, keepdims=True)), sink)
        p_b = jnp.exp(s_b - m)
        p_c = jnp.exp(s_c - m)
        den = jnp.exp(sink - m) + jnp.sum(p_b, axis=-1, keepdims=True) + jnp.sum(p_c, axis=-1, keepdims=True)
        o = (jnp.dot(p_b.astype(BF16), vband[:, ks], preferred_element_type=F32)
             + jnp.dot(p_c.astype(BF16), cv[:, ks], preferred_element_type=F32)) / den
        for g in range(SWA_GROUP):
            cb = (kvh * SWA_GROUP + g) * hd
            o_ref[:, cb:cb + hd] = o[g * blk:(g + 1) * blk, :].astype(BF16)


def _swa_latent(p, bsz, seq, cache_k, cache_v, sink, cos_t, sin_t):
    n = p.shape[0]
    nb = seq // SWA_BLOCK
    kcol = COL_SK // SWA_KVW
    vcol = COL_SV // SWA_KVW
    prev = lambda i: jnp.maximum(i - 1, 0)
    nxt = lambda i: jnp.minimum(i + 1, nb - 1)

    def pspec(col, f):
        return pl.BlockSpec((SWA_BLOCK, SWA_KVW), lambda b, i, s: (b * nb + f(i), col))

    def tspec(f):
        return pl.BlockSpec((SWA_BLOCK, SWA_KVW), lambda b, i, s: (f(i), 0))

    same = lambda i: i
    cspec = pl.BlockSpec((1, cache_k.shape[1], SWA_KVW), lambda b, i, s: (b, 0, 0))
    grid_spec = pltpu.PrefetchScalarGridSpec(
        num_scalar_prefetch=1,
        grid=(bsz, nb),
        in_specs=[pl.BlockSpec((SWA_BLOCK, SWA_QW), lambda b, i, s: (b * nb + i, COL_SQ // SWA_QW)),
                  pspec(kcol, prev), pspec(kcol, same), pspec(kcol, nxt),
                  pspec(vcol, prev), pspec(vcol, same), pspec(vcol, nxt),
                  cspec, cspec,
                  tspec(prev), tspec(same), tspec(nxt), tspec(prev), tspec(same), tspec(nxt)],
        out_specs=pl.BlockSpec((SWA_BLOCK, SWA_QW), lambda b, i, s: (b * nb + i, 0)),
    )
    return pl.pallas_call(
        functools.partial(_swa_lat_kernel, nb=nb),
        grid_spec=grid_spec,
        out_shape=jax.ShapeDtypeStruct((n, SWA_QW), BF16),
        compiler_params=_cparams(("parallel", "arbitrary")),
        name="swa_latent",
    )(sink, p, p, p, p, p, p, p, cache_k, cache_v, cos_t, cos_t, cos_t, sin_t, sin_t, sin_t)


def _swa_ctx_kernel(sink_ref, q_ref, k_ref, v_ref, o_ref):
    s = q_ref.shape[0]
    hd = SWA_HEAD_DIM
    kb = k_ref[...].astype(BF16)
    vb = v_ref[...].astype(BF16)
    for kvh in range(SWA_KV_HEADS):
        ks = slice(kvh * hd, (kvh + 1) * hd)
        qg = jnp.concatenate([q_ref[:, (kvh * SWA_GROUP + g) * hd:(kvh * SWA_GROUP + g + 1) * hd]
                              for g in range(SWA_GROUP)], axis=0)
        qg = (qg * (hd ** -0.5)).astype(BF16)
        sc = _nt_dot(qg, kb[:, ks])
        sink = _sink_column(sink_ref, kvh, s)
        m = jnp.maximum(jnp.max(sc, axis=-1, keepdims=True), sink)
        pr = jnp.exp(sc - m)
        den = jnp.exp(sink - m) + jnp.sum(pr, axis=-1, keepdims=True)
        o = jnp.dot(pr.astype(BF16), vb[:, ks], preferred_element_type=F32) / den
        for g in range(SWA_GROUP):
            cb = (kvh * SWA_GROUP + g) * hd
            o_ref[:, cb:cb + hd] = o[g * s:(g + 1) * s, :].astype(BF16)


def _swa_context(p, bsz, seq, sink):
    n = p.shape[0]
    grid_spec = pltpu.PrefetchScalarGridSpec(
        num_scalar_prefetch=1,
        grid=(bsz,),
        in_specs=[pl.BlockSpec((seq, SWA_QW), lambda b, s: (b, COL_SQ // SWA_QW)),
                  pl.BlockSpec((seq, SWA_KVW), lambda b, s: (b, COL_SK // SWA_KVW)),
                  pl.BlockSpec((seq, SWA_KVW), lambda b, s: (b, COL_SV // SWA_KVW))],
        out_specs=pl.BlockSpec((seq, SWA_QW), lambda b, s: (b, 0)),
    )
    return pl.pallas_call(
        _swa_ctx_kernel,
        grid_spec=grid_spec,
        out_shape=jax.ShapeDtypeStruct((n, SWA_QW), BF16),
        compiler_params=_cparams(("parallel",)),
        name="swa_context",
    )(sink, p, p, p)


MERGE_TM = 256


def _merge_kernel(og_ref, os_ref, ga_ref, gb_ref, x_ref, gt1_ref, sc2_ref, sh2_ref, g2_ref,
                  wa_ref, wb_ref, wo_ref, wrt_ref, x1_ref, h2_ref, lg_ref):
    a = jnp.dot(og_ref[...], wa_ref[...], preferred_element_type=F32)
    b = jnp.dot(os_ref[...], wb_ref[...], preferred_element_type=F32)
    merged = (_sigmoid(ga_ref[...]) * a + _sigmoid(gb_ref[...]) * b).astype(BF16)
    x1 = x_ref[...] + gt1_ref[0] * jnp.dot(merged, wo_ref[...], preferred_element_type=F32)
    x1_ref[...] = x1
    y = x1 * lax.rsqrt(jnp.mean(x1 * x1, axis=-1, keepdims=True) + EPS) * g2_ref[...]
    h2 = y * (1.0 + sc2_ref[0]) + sh2_ref[0]
    h2_ref[...] = h2
    lt = _nt_dot(wrt_ref[...], h2, precision=HIGHEST)
    for cb in range(lt.shape[1] // LANES):
        lg_ref[cb] = lt[:, cb * LANES:(cb + 1) * LANES]


def _merge(o_gla, o_swa, p, x2d, seq, gt1, sc2, sh2, g2, wa, wb, wo, wrt):
    n = x2d.shape[0]
    tm = MERGE_TM
    per_b = _tiles_per_mod_row(gt1, n, seq, tm)
    mod_spec = pl.BlockSpec((1, 1, D_MODEL), lambda i: (i // per_b, 0, 0))
    once = pl.Buffered(1)
    return pl.pallas_call(
        _merge_kernel,
        grid=(n // tm,),
        in_specs=[pl.BlockSpec((tm, GLA_VW), lambda i: (i, 0)),
                  pl.BlockSpec((tm, SWA_QW), lambda i: (i, 0)),
                  pl.BlockSpec((tm, D_MODEL), lambda i: (i, COL_GA // D_MODEL)),
                  pl.BlockSpec((tm, D_MODEL), lambda i: (i, COL_GB // D_MODEL)),
                  pl.BlockSpec((tm, D_MODEL), lambda i: (i, 0)),
                  mod_spec, mod_spec, mod_spec,
                  pl.BlockSpec((1, D_MODEL), lambda i: (0, 0)),
                  pl.BlockSpec((GLA_VW, D_MODEL), lambda i: (0, 0), pipeline_mode=once),
                  pl.BlockSpec((SWA_QW, D_MODEL), lambda i: (0, 0), pipeline_mode=once),
                  pl.BlockSpec((D_MODEL, D_MODEL), lambda i: (0, 0), pipeline_mode=once),
                  pl.BlockSpec((N_EXPERTS, D_MODEL), lambda i: (0, 0))],
        out_specs=[pl.BlockSpec((tm, D_MODEL), lambda i: (i, 0)),
                   pl.BlockSpec((tm, D_MODEL), lambda i: (i, 0)),
                   pl.BlockSpec((tm // LANES, N_EXPERTS, LANES), lambda i: (i, 0, 0))],
        out_shape=[jax.ShapeDtypeStruct((n, D_MODEL), F32),
                   jax.ShapeDtypeStruct((n, D_MODEL), F32),
                   jax.ShapeDtypeStruct((n // LANES, N_EXPERTS, LANES), F32)],
        compiler_params=_cparams(("parallel",)),
        name="merge",
    )(o_gla, o_swa, p, p, x2d, gt1, sc2, sh2, g2, wa, wb, wo, wrt)


def _select_kernel(lg_ref, aff_ref, sel_ref, pos_ref, *, cap):
    nblk = lg_ref.shape[0]
    lg = lg_ref[...]
    ex = jnp.exp(lg - jnp.max(lg, axis=1, keepdims=True))
    aff = ex / jnp.sum(ex, axis=1, keepdims=True)
    aff_ref[...] = aff
    bits = lax.bitcast_convert_type(aff, jnp.int32)

    def count(pred):
        c = jnp.sum(jnp.where(pred, 1.0, 0.0), axis=0, keepdims=True)
        return jnp.sum(c, axis=2, keepdims=True)

    def bit_step(t, cur):
        cand = cur | jnp.left_shift(jnp.int32(1), 30 - t)
        return jnp.where(count(bits >= cand) >= cap, cand, cur)

    thr = lax.fori_loop(0, 31, bit_step, jnp.zeros((1, N_EXPERTS, 1), jnp.int32))
    need = (cap - count(bits > thr))[0]
    thr2 = thr[0]
    ri = lax.broadcasted_iota(jnp.int32, (LANES, LANES), 0)
    ci = lax.broadcasted_iota(jnp.int32, (LANES, LANES), 1)
    upper = (ri <= ci).astype(BF16)

    def blk_step(b, carry):
        run_eq, run_sel = carry
        bb = lax.bitcast_convert_type(aff_ref[b], jnp.int32)
        eq = (bb == thr2).astype(F32)
        eq_excl = jnp.dot(eq.astype(BF16), upper, preferred_element_type=F32) - eq + run_eq
        sel = jnp.where((bb > thr2) | ((eq > 0) & (eq_excl < need)), 1.0, 0.0)
        sel_ref[b] = sel
        pos_ref[b] = jnp.dot(sel.astype(BF16), upper, preferred_element_type=F32) - sel + run_sel
        return (run_eq + jnp.sum(eq, axis=1, keepdims=True),
                run_sel + jnp.sum(sel, axis=1, keepdims=True))

    zero = jnp.zeros((N_EXPERTS, 1), F32)
    lax.fori_loop(0, nblk, blk_step, (zero, zero))


def _select(logits3, cap):
    shp = jax.ShapeDtypeStruct(logits3.shape, F32)
    return pl.pallas_call(
        functools.partial(_select_kernel, cap=float(cap)),
        out_shape=[shp, shp, shp],
        compiler_params=pltpu.CompilerParams(vmem_limit_bytes=VMEM_LIMIT),
        name="select",
    )(logits3)


COMPACT_WIN = 136


def _compact_kernel(base_ref, aff_ref, sel_ref, pos_ref, idx_ref, gate_ref):
    nblk = aff_ref.shape[0]
    idx_ref[...] = jnp.zeros_like(idx_ref)
    gate_ref[...] = jnp.zeros_like(gate_ref)
    slot = lax.broadcasted_iota(jnp.int32, (COMPACT_WIN, LANES), 0).astype(F32)
    lane = lax.broadcasted_iota(jnp.int32, (COMPACT_WIN, LANES), 1)

    def blk_step(b, _):
        aff = aff_ref[b]
        sel = sel_ref[b]
        pos = pos_ref[b]
        tok = (lane + b * LANES).astype(F32)
        for e in range(N_EXPERTS):
            base8 = pl.multiple_of((base_ref[b * N_EXPERTS + e] >> 3) << 3, 8)
            rel = pos[e:e + 1, :] - base8.astype(F32)
            hit = (rel == slot) & (sel[e:e + 1, :] > 0)
            rows = pl.ds(base8, COMPACT_WIN)
            mine = lane == e
            ic = jnp.sum(jnp.where(hit, tok, 0.0), axis=1, keepdims=True)
            gc = jnp.sum(jnp.where(hit, aff[e:e + 1, :], 0.0), axis=1, keepdims=True)
            idx_ref[rows, :] += jnp.where(mine, ic, 0.0)
            gate_ref[rows, :] += jnp.where(mine, gc, 0.0)
        return 0

    lax.fori_loop(0, nblk, blk_step, 0)


def _compact(base, aff3, sel3, pos3, cap):
    cp = cap + 2 * LANES
    full = pl.BlockSpec(aff3.shape, lambda i, s: (0, 0, 0))
    ospec = pl.BlockSpec((cp, LANES), lambda i, s: (0, 0))
    oshape = jax.ShapeDtypeStruct((cp, LANES), F32)
    grid_spec = pltpu.PrefetchScalarGridSpec(
        num_scalar_prefetch=1, grid=(1,), in_specs=[full, full, full], out_specs=[ospec, ospec])
    return pl.pallas_call(
        _compact_kernel,
        grid_spec=grid_spec,
        out_shape=[oshape, oshape],
        compiler_params=_cparams(("arbitrary",)),
        name="compact",
    )(base, aff3, sel3, pos3)


def _ffn_kernel(idx_ref, h2_hbm, wg_ref, wu_ref, wd_ref, o_ref, gbuf, xs, sem, *, tm, total):
    step = pl.program_id(0) * pl.num_programs(1) + pl.program_id(1)
    slot = step % 2

    def row_copy(step_, slot_, i):
        tok = idx_ref[step_ * tm + i]
        return pltpu.make_async_copy(
            h2_hbm.at[pl.ds(pl.multiple_of(tok * ROW_CHUNKS, ROW_CHUNKS), ROW_CHUNKS), :],
            gbuf.at[slot_, pl.ds(pl.multiple_of(i * ROW_CHUNKS, ROW_CHUNKS), ROW_CHUNKS), :],
            sem.at[slot_])

    def issue(step_, slot_):
        def body(i, _):
            row_copy(step_, slot_, i).start()
            return 0
        lax.fori_loop(0, tm, body, 0)

    @pl.when(step == 0)
    def _():
        issue(step, slot)

    @pl.when(step + 1 < total)
    def _():
        issue(step + 1, 1 - slot)

    def wait_body(i, _):
        row_copy(step, slot, i).wait()
        return 0
    lax.fori_loop(0, tm, wait_body, 0)

    for s in range(ROW_CHUNKS):
        xs[:, s * LANES:(s + 1) * LANES] = gbuf[slot, pl.ds(s, tm, stride=ROW_CHUNKS), :].astype(BF16)
    x = xs[...]
    hg = jnp.dot(x, wg_ref[0], preferred_element_type=F32)
    hu = jnp.dot(x, wu_ref[0], preferred_element_type=F32)
    hid = (_silu(hg) * hu).astype(BF16)
    o_ref[...] = jnp.dot(hid, wd_ref[0], preferred_element_type=F32)


def _expert_ffn(idx_flat, h2_lin, wg, wu, wd, cap):
    tm = min(256, cap)
    nr = cap // tm
    grid_spec = pltpu.PrefetchScalarGridSpec(
        num_scalar_prefetch=1,
        grid=(N_EXPERTS, nr),
        in_specs=[pl.BlockSpec(memory_space=pl.ANY),
                  pl.BlockSpec((1, D_MODEL, EXPERT_FF), lambda e, r, s: (e, 0, 0)),
                  pl.BlockSpec((1, D_MODEL, EXPERT_FF), lambda e, r, s: (e, 0, 0)),
                  pl.BlockSpec((1, EXPERT_FF, D_MODEL), lambda e, r, s: (e, 0, 0))],
        out_specs=pl.BlockSpec((tm, D_MODEL), lambda e, r, s: (e * nr + r, 0)),
        scratch_shapes=[pltpu.VMEM((2, tm * ROW_CHUNKS, LANES), F32),
                        pltpu.VMEM((tm, D_MODEL), BF16),
                        pltpu.SemaphoreType.DMA((2,))],
    )
    return pl.pallas_call(
        functools.partial(_ffn_kernel, tm=tm, total=N_EXPERTS * nr),
        grid_spec=grid_spec,
        out_shape=jax.ShapeDtypeStruct((N_EXPERTS * cap, D_MODEL), F32),
        compiler_params=_cparams(("arbitrary", "arbitrary")),
        name="expert_ffn",
    )(idx_flat, h2_lin, wg, wu, wd)


def _combine_kernel(idx_ref, gate_ref, lo_ref, ye_hbm, x1_ref, gt2_ref, gf_ref, o_ref, stage, acc, sem,
                    *, tt, ch, cap, ntiles):
    tile = pl.program_id(0)
    slot = tile % 2

    def chunk_start(first_row):
        return jnp.minimum(first_row, cap - ch)

    def chunk_copy(slot_, e, start):
        src = pl.multiple_of((e * cap + start) * ROW_CHUNKS, ROW_CHUNKS)
        return pltpu.make_async_copy(ye_hbm.at[pl.ds(src, ch * ROW_CHUNKS), :], stage.at[slot_, e],
                                     sem.at[slot_, e])

    def issue(tile_, slot_):
        for e in range(N_EXPERTS):
            chunk_copy(slot_, e, chunk_start(lo_ref[tile_ * N_EXPERTS + e])).start()

    @pl.when(tile == 0)
    def _():
        issue(tile, slot)

    @pl.when(tile + 1 < ntiles)
    def _():
        issue(tile + 1, 1 - slot)

    acc[...] = jnp.zeros_like(acc)

    def add_rows(e, start, r0, r1):
        def body(r, _):
            tok = idx_ref[e * cap + r] - tile * tt
            dst = pl.ds(pl.multiple_of(tok * ROW_CHUNKS, ROW_CHUNKS), ROW_CHUNKS)
            src = pl.ds(pl.multiple_of((r - start) * ROW_CHUNKS, ROW_CHUNKS), ROW_CHUNKS)
            acc[dst, :] = acc[dst, :] + gate_ref[e * cap + r] * stage[slot, e, src, :]
            return 0
        lax.fori_loop(r0, r1, body, 0)

    for e in range(N_EXPERTS):
        lo = lo_ref[tile * N_EXPERTS + e]
        hi = lo_ref[(tile + 1) * N_EXPERTS + e]
        start = chunk_start(lo)
        chunk_copy(slot, e, start).wait()
        first_end = jnp.minimum(hi, start + ch)
        add_rows(e, start, lo, first_end)

        def more(r0):
            st = chunk_start(r0)
            cp = chunk_copy(slot, e, st)
            cp.start()
            cp.wait()
            r1 = jnp.minimum(hi, st + ch)
            add_rows(e, st, r0, r1)
            return r1
        lax.while_loop(lambda r0: r0 < hi, more, first_end)

    ssq = jnp.zeros((tt, 1), F32)
    for s in range(ROW_CHUNKS):
        cols = slice(s * LANES, (s + 1) * LANES)
        y = acc[pl.ds(s, tt, stride=ROW_CHUNKS), :]
        x2 = x1_ref[:, cols] + gt2_ref[0][:, cols] * y
        o_ref[:, cols] = x2
        ssq = ssq + jnp.sum(x2 * x2, axis=-1, keepdims=True)
    o_ref[...] = o_ref[...] * lax.rsqrt(ssq / D_MODEL + EPS) * gf_ref[...]


def _combine(idx_flat, gate_flat, lo_tab, ye_lin, x1, seq, gt2, g_final, cap, tt):
    n = x1.shape[0]
    ch = min(64, cap)
    ntiles = n // tt
    per_b = _tiles_per_mod_row(gt2, n, seq, tt)
    grid_spec = pltpu.PrefetchScalarGridSpec(
        num_scalar_prefetch=3,
        grid=(ntiles,),
        in_specs=[pl.BlockSpec(memory_space=pl.ANY),
                  pl.BlockSpec((tt, D_MODEL), lambda i, a, b, c: (i, 0)),
                  pl.BlockSpec((1, 1, D_MODEL), lambda i, a, b, c: (i // per_b, 0, 0)),
                  pl.BlockSpec((1, D_MODEL), lambda i, a, b, c: (0, 0))],
        out_specs=pl.BlockSpec((tt, D_MODEL), lambda i, a, b, c: (i, 0)),
        scratch_shapes=[pltpu.VMEM((2, N_EXPERTS, ch * ROW_CHUNKS, LANES), F32),
                        pltpu.VMEM((tt * ROW_CHUNKS, LANES), F32),
                        pltpu.SemaphoreType.DMA((2, N_EXPERTS))],
    )
    return pl.pallas_call(
        functools.partial(_combine_kernel, tt=tt, ch=ch, cap=cap, ntiles=ntiles),
        grid_spec=grid_spec,
        out_shape=jax.ShapeDtypeStruct((n, D_MODEL), F32),
        compiler_params=_cparams(("arbitrary",)),
        name="combine",
    )(idx_flat, gate_flat, lo_tab, ye_lin, x1, gt2, g_final)


def _rope_tables(seq):
    pos = jnp.arange(seq)
    row = (pos // GRID_W).astype(F32)
    col = (pos % GRID_W).astype(F32)
    npair = SWA_HEAD_DIM // 4
    inv_freq = ROPE_BASE ** (-jnp.arange(npair, dtype=F32) / npair)
    ar = row[:, None] * inv_freq[None, :]
    ac = col[:, None] * inv_freq[None, :]
    cos = jnp.concatenate([jnp.cos(ar), jnp.cos(ar), jnp.cos(ac), jnp.cos(ac)], axis=1)
    sin = jnp.concatenate([-jnp.sin(ar), jnp.sin(ar), -jnp.sin(ac), jnp.sin(ac)], axis=1)
    reps = SWA_KVW // SWA_HEAD_DIM
    return jnp.tile(cos, (1, reps)), jnp.tile(sin, (1, reps))


def _layer(x, mods, wts, latent, s0f, s0b, cache_k, cache_v):
    bsz, seq, _ = x.shape
    n = bsz * seq
    x2d = x.reshape(n, D_MODEL)
    sh1, sc1, gt1, sh2, sc2, gt2 = mods
    p = _in_proj(x2d, seq, sc1, sh1, wts["g1"], wts["w_in_r"])
    o_gla, st_f, st_b = _gla(p, bsz, seq, wts["wdf"], wts["bdf"], wts["wdb"], wts["bdb"], wts["g_gla"],
                             s0f, s0b)
    if latent:
        cos_t, sin_t = _rope_tables(seq)
        o_swa = _swa_latent(p, bsz, seq, cache_k, cache_v, wts["sink"], cos_t, sin_t)
    else:
        o_swa = _swa_context(p, bsz, seq, wts["sink"])
    x1, h2, logits3 = _merge(o_gla, o_swa, p, x2d, seq, gt1, sc2, sh2, wts["g2"],
                             wts["wa"], wts["wb"], wts["wo"], wts["wrt"])
    cap = CAPACITY_FACTOR * n // N_EXPERTS
    aff3, sel3, pos3 = _select(logits3, cap)
    base = pos3[:, :, 0].astype(jnp.int32)
    idx_c, gate_c = _compact(base.reshape(-1), aff3, sel3, pos3, cap)
    idx_flat = idx_c[:cap, :N_EXPERTS].T.astype(jnp.int32).reshape(-1)
    gate_flat = gate_c[:cap, :N_EXPERTS].T.reshape(-1)
    ye = _expert_ffn(idx_flat, h2.reshape(n * ROW_CHUNKS, LANES), wts["wg"], wts["wu"], wts["wd"], cap)
    tt = 256
    lo_tab = jnp.concatenate([base[::tt // LANES], jnp.full((1, N_EXPERTS), cap, jnp.int32)], axis=0)
    y = _combine(idx_flat, gate_flat, lo_tab.reshape(-1), ye.reshape(N_EXPERTS * cap * ROW_CHUNKS, LANES),
                 x1, seq, gt2, wts["g_final"], cap, tt)
    return y.reshape(bsz, seq, D_MODEL), p, st_f, st_b


def kernel(x_prompt, x_sample, state_gla_fwd, state_gla_bwd, cache_k, cache_v, c, c_ctx, w_mod, b_mod,
           g_norm1, w_in, w_dec_f, b_dec_f, w_dec_b, b_dec_b, g_gla, attn_sink, w_branch_a, w_branch_b,
           w_out, g_norm2, w_router, w_exp_gate, w_exp_up, w_exp_down, g_final):
    bp, sp, _ = x_prompt.shape
    bl = x_sample.shape[0]
    l = 0
    gla_end = 2 * GLA_KW + 2 * GLA_VW
    dec_end = gla_end + 2 * GLA_RANK
    swa_end = dec_end + SWA_QW + 2 * SWA_KVW
    w = w_in[l]
    w_in_r = jnp.concatenate(
        [w[:, swa_end:], w[:, :gla_end], w[:, dec_end:swa_end], w[:, gla_end:dec_end],
         jnp.zeros((D_MODEL, IN_PAD - w.shape[1]), w.dtype)], axis=1).astype(BF16)
    zpad = jnp.zeros((LANES - 2 * GLA_RANK, GLA_KW), F32)
    zr = jnp.zeros((GLA_RANK, GLA_KW), F32)
    wts = {
        "g1": g_norm1[l].reshape(1, D_MODEL),
        "w_in_r": w_in_r,
        "wdf": jnp.concatenate([w_dec_f[l], zr, zpad], axis=0),
        "wdb": jnp.concatenate([zr, w_dec_b[l], zpad], axis=0),
        "bdf": b_dec_f[l].reshape(1, GLA_KW),
        "bdb": b_dec_b[l].reshape(1, GLA_KW),
        "g_gla": g_gla[l].reshape(1, GLA_VW),
        "sink": attn_sink[l],
        "wa": w_branch_a[l].astype(BF16),
        "wb": w_branch_b[l].astype(BF16),
        "wo": w_out[l].astype(BF16),
        "g2": g_norm2[l].reshape(1, D_MODEL),
        "wrt": w_router[l].T,
        "wg": w_exp_gate[l].astype(BF16),
        "wu": w_exp_up[l].astype(BF16),
        "wd": w_exp_down[l].astype(BF16),
        "g_final": g_final.reshape(1, D_MODEL),
    }
    cond8 = jnp.concatenate([c_ctx[None, :], c, jnp.zeros((8 - 1 - bl, D_MODEL), F32)], axis=0)
    mod = _modulation(cond8, w_mod[l], b_mod[l]).reshape(8, N_MOD, 1, D_MODEL)
    mods_ctx = tuple(mod[0:1, j] for j in range(N_MOD))
    mods_lat = tuple(mod[1:1 + bl, j] for j in range(N_MOD))

    zero_state = jnp.zeros((bp, GLA_HEADS, GLA_DK, GLA_DV), F32)
    y_prompt, p_ctx, st_f, st_b = _layer(x_prompt, mods_ctx, wts, False, zero_state, zero_state, None, None)
    ck = cache_k[:, l].reshape(bl, -1, SWA_KVW)
    cv = cache_v[:, l].reshape(bl, -1, SWA_KVW)
    y_sample, _, _, _ = _layer(x_sample, mods_lat, wts, True, state_gla_fwd[:, l], state_gla_bwd[:, l], ck, cv)

    new_k = p_ctx[:, COL_SK:COL_SK + SWA_KVW].reshape(bp, 1, sp, SWA_KV_HEADS, SWA_HEAD_DIM)
    new_v = p_ctx[:, COL_SV:COL_SV + SWA_KVW].reshape(bp, 1, sp, SWA_KV_HEADS, SWA_HEAD_DIM)
    return (y_prompt, y_sample, st_f[:, None], st_b[:, None], new_k, new_v)
```

```python
import functools

import jax
import jax.numpy as jnp
from jax import lax
from jax.experimental import pallas as pl
from jax.experimental.pallas import tpu as pltpu

F32 = jnp.float32
BF16 = jnp.bfloat16
HIGHEST = lax.Precision.HIGHEST

D_MODEL = 2048
N_MOD = 6
EPS = 1e-6

GLA_HEADS = 4
GLA_DK = 128
GLA_DV = 256
GLA_KW = GLA_HEADS * GLA_DK
GLA_VW = GLA_HEADS * GLA_DV
GLA_RANK = 16
GLA_TAU = 16.0
GLA_CHUNK = 64

SWA_HEADS = 16
SWA_KV_HEADS = 4
SWA_GROUP = 4
SWA_HEAD_DIM = 64
SWA_QW = SWA_HEADS * SWA_HEAD_DIM
SWA_KVW = SWA_KV_HEADS * SWA_HEAD_DIM
SWA_BLOCK = 128
GRID_W = 64
ROPE_BASE = 10000.0

N_EXPERTS = 16
EXPERT_FF = D_MODEL // 2
CAPACITY_FACTOR = 2

LANES = 128
ROW_CHUNKS = D_MODEL // LANES
VMEM_LIMIT = 56 * 1024 * 1024

COL_GA = 0
COL_GB = 2048
COL_GQ = 4096
COL_GK = 4608
COL_GV = 5120
COL_GR = 6144
COL_SQ = 7168
COL_SK = 8192
COL_SV = 8448
COL_DEC = 8704
IN_PAD = 8960
IN_TN = 1280


def _cparams(sem, **kw):
    return pltpu.CompilerParams(dimension_semantics=sem, vmem_limit_bytes=VMEM_LIMIT, **kw)


def _silu(x):
    return x * (1.0 / (1.0 + jnp.exp(-x)))


def _sigmoid(x):
    return 1.0 / (1.0 + jnp.exp(-x))


def _store_row_major(ref, x):
    rows = x.shape[0]
    for s in range(ROW_CHUNKS):
        ref[pl.ds(s, rows, stride=ROW_CHUNKS), :] = x[:, s * LANES:(s + 1) * LANES]


def _tiles_per_mod_row(mod, n, seq, tile):
    return seq // tile if mod.shape[0] > 1 else n // tile


def _mod_kernel(c_ref, w_ref, b_ref, o_ref):
    a = _silu(c_ref[...]).astype(BF16)
    o_ref[...] = jnp.dot(a, w_ref[...].astype(BF16), preferred_element_type=F32) + b_ref[...]


def _modulation(cond8, w_mod, b_mod):
    n_out = w_mod.shape[1]
    tn = 1536
    return pl.pallas_call(
        _mod_kernel,
        grid=(n_out // tn,),
        in_specs=[pl.BlockSpec((8, D_MODEL), lambda j: (0, 0)),
                  pl.BlockSpec((D_MODEL, tn), lambda j: (0, j)),
                  pl.BlockSpec((1, tn), lambda j: (0, j))],
        out_specs=pl.BlockSpec((8, tn), lambda j: (0, j)),
        out_shape=jax.ShapeDtypeStruct((8, n_out), F32),
        compiler_params=_cparams(("arbitrary",)),
        name="modulation",
    )(cond8, w_mod, b_mod.reshape(1, n_out))


def _in_proj_kernel(x_ref, sc_ref, sh_ref, g_ref, w_ref, o_ref, h_scr):
    @pl.when(pl.program_id(1) == 0)
    def _():
        x = x_ref[...]
        y = x * lax.rsqrt(jnp.mean(x * x, axis=-1, keepdims=True) + EPS) * g_ref[...]
        h_scr[...] = (y * (1.0 + sc_ref[0]) + sh_ref[0]).astype(BF16)

    o_ref[...] = jnp.dot(h_scr[...], w_ref[...], preferred_element_type=F32)


def _in_proj(x2d, seq, sc, sh, g, w_r):
    n = x2d.shape[0]
    tm = min(1024, seq if sc.shape[0] > 1 else n)
    per_b = _tiles_per_mod_row(sc, n, seq, tm)
    return pl.pallas_call(
        _in_proj_kernel,
        grid=(n // tm, IN_PAD // IN_TN),
        in_specs=[pl.BlockSpec((tm, D_MODEL), lambda i, j: (i, 0)),
                  pl.BlockSpec((1, 1, D_MODEL), lambda i, j: (i // per_b, 0, 0)),
                  pl.BlockSpec((1, 1, D_MODEL), lambda i, j: (i // per_b, 0, 0)),
                  pl.BlockSpec((1, D_MODEL), lambda i, j: (0, 0)),
                  pl.BlockSpec((D_MODEL, IN_TN), lambda i, j: (0, j))],
        out_specs=pl.BlockSpec((tm, IN_TN), lambda i, j: (i, j)),
        out_shape=jax.ShapeDtypeStruct((n, IN_PAD), F32),
        scratch_shapes=[pltpu.VMEM((tm, D_MODEL), BF16)],
        compiler_params=_cparams(("parallel", "arbitrary")),
        name="in_proj",
    )(x2d, sc, sh, g, w_r)


GLA_BLK = 256


def _tn_dot(a, b, precision=None):
    return lax.dot_general(a, b, (((0,), (0,)), ((), ())), precision=precision,
                           preferred_element_type=F32)


def _nt_dot(a, b, precision=None):
    return lax.dot_general(a, b, (((1,), (1,)), ((), ())), precision=precision,
                           preferred_element_type=F32)


def _split_bf16(x, parts):
    out = []
    for _ in range(parts):
        piece = x.astype(BF16)
        out.append(piece)
        x = x - piece.astype(F32)
    return out


def _dot3(x, w):
    xh, xl = _split_bf16(x, 2)
    wh, wl = _split_bf16(w, 2)
    d = functools.partial(jnp.dot, preferred_element_type=F32)
    return d(xh, wh) + d(xl, wh) + d(xh, wl)


def _gla_kernel(q_ref, k_ref, v_ref, r_ref, dec_ref, wdf_ref, bdf_ref, wdb_ref, bdb_ref, gg_ref,
                s0f_ref, s0b_ref, o_ref, stf_ref, stb_ref, state_t, of_scr, *, nblk):
    i = pl.program_id(2)
    c = GLA_CHUNK
    nch = GLA_BLK // c
    ri = lax.broadcasted_iota(jnp.int32, (GLA_BLK, GLA_BLK), 0)
    ci = lax.broadcasted_iota(jnp.int32, (GLA_BLK, GLA_BLK), 1)
    same_chunk = (ri // c) == (ci // c)

    @pl.when(i == 0)
    def _():
        state_t[...] = s0f_ref[0, 0].T

    @pl.when(i == nblk)
    def _():
        state_t[...] = s0b_ref[0, 0].T

    def run(fwd):
        blk = i if fwd else 2 * nblk - 1 - i
        keep = same_chunk & ((ci <= ri) if fwd else (ci >= ri))
        w_ref, b_ref = (wdf_ref, bdf_ref) if fwd else (wdb_ref, bdb_ref)
        logit = _dot3(dec_ref[...], w_ref[...]) + b_ref[...]
        g = (jnp.minimum(logit, 0.0) - jnp.log(1.0 + jnp.exp(-jnp.abs(logit)))) / GLA_TAU
        tri = keep.astype(BF16)
        b = sum(jnp.dot(tri, piece, preferred_element_type=F32) for piece in _split_bf16(g, 3))
        edge = c - 1 if fwd else 0
        tots = [b[ch * c + edge:ch * c + edge + 1, :] for ch in range(nch)]
        totb = jnp.concatenate([jnp.broadcast_to(t, (c, GLA_DK)) for t in tots], axis=0)
        q = q_ref[...] * (GLA_DK ** -0.5)
        k = k_ref[...]
        v = v_ref[...].astype(BF16)
        q_in = (q * jnp.exp(b)).astype(BF16)
        k_in = (k * jnp.exp(-b)).astype(BF16)
        k_st = (k * jnp.exp(totb - b)).astype(BF16)
        a = jnp.where(keep, _nt_dot(q_in, k_in), 0.0).astype(BF16)
        o_intra = jnp.dot(a, v, preferred_element_type=F32)
        st = state_t[...]
        o_inter = [None] * nch
        for cc in range(nch):
            ch = cc if fwd else nch - 1 - cc
            rows = slice(ch * c, (ch + 1) * c)
            o_inter[ch] = _nt_dot(q_in[rows, :], st.astype(BF16))
            st = st * jnp.exp(tots[ch]) + _tn_dot(v[rows, :], k_st[rows, :])
        state_t[...] = st
        o = o_intra + jnp.concatenate(o_inter, axis=0)
        srow = pl.multiple_of(blk * GLA_BLK, GLA_BLK)
        if fwd:
            of_scr[pl.ds(srow, GLA_BLK), :] = o
        else:
            o = o + of_scr[pl.ds(srow, GLA_BLK), :]
            o = o * lax.rsqrt(jnp.mean(o * o, axis=-1, keepdims=True) + EPS) * gg_ref[...]
            o_ref[...] = (o * _silu(r_ref[...])).astype(BF16)

    @pl.when(i < nblk)
    def _():
        run(True)

    @pl.when(i >= nblk)
    def _():
        run(False)

    @pl.when(i == nblk - 1)
    def _():
        stf_ref[0, 0] = state_t[...].T

    @pl.when(i == 2 * nblk - 1)
    def _():
        stb_ref[0, 0] = state_t[...].T


def _gla(p, bsz, seq, wdf, bdf, wdb, bdb, g_gla, s0f, s0b):
    n = p.shape[0]
    nblk = seq // GLA_BLK

    def rb(b, i):
        return b * nblk + jnp.where(i < nblk, i, 2 * nblk - 1 - i)

    def orb(b, i):
        return b * nblk + jnp.where(i < nblk, nblk - 1, 2 * nblk - 1 - i)

    st_spec = pl.BlockSpec((1, 1, GLA_DK, GLA_DV), lambda b, h, i: (b, h, 0, 0))
    st_shape = jax.ShapeDtypeStruct((bsz, GLA_HEADS, GLA_DK, GLA_DV), F32)
    return pl.pallas_call(
        functools.partial(_gla_kernel, nblk=nblk),
        grid=(bsz, GLA_HEADS, 2 * nblk),
        in_specs=[pl.BlockSpec((GLA_BLK, GLA_DK), lambda b, h, i: (rb(b, i), COL_GQ // GLA_DK + h)),
                  pl.BlockSpec((GLA_BLK, GLA_DK), lambda b, h, i: (rb(b, i), COL_GK // GLA_DK + h)),
                  pl.BlockSpec((GLA_BLK, GLA_DV), lambda b, h, i: (rb(b, i), COL_GV // GLA_DV + h)),
                  pl.BlockSpec((GLA_BLK, GLA_DV), lambda b, h, i: (rb(b, i), COL_GR // GLA_DV + h)),
                  pl.BlockSpec((GLA_BLK, LANES), lambda b, h, i: (rb(b, i), COL_DEC // LANES)),
                  pl.BlockSpec((LANES, GLA_DK), lambda b, h, i: (0, h)),
                  pl.BlockSpec((1, GLA_DK), lambda b, h, i: (0, h)),
                  pl.BlockSpec((LANES, GLA_DK), lambda b, h, i: (0, h)),
                  pl.BlockSpec((1, GLA_DK), lambda b, h, i: (0, h)),
                  pl.BlockSpec((1, GLA_DV), lambda b, h, i: (0, h)),
                  st_spec, st_spec],
        out_specs=[pl.BlockSpec((GLA_BLK, GLA_DV), lambda b, h, i: (orb(b, i), h)), st_spec, st_spec],
        out_shape=[jax.ShapeDtypeStruct((n, GLA_VW), BF16), st_shape, st_shape],
        scratch_shapes=[pltpu.VMEM((GLA_DV, GLA_DK), F32), pltpu.VMEM((seq, GLA_DV), F32)],
        compiler_params=_cparams(("parallel", "parallel", "arbitrary")),
        name="gla",
    )(p, p, p, p, p, wdf, bdf, wdb, bdb, g_gla, s0f, s0b)


def _sink_column(sink_ref, kvh, rows_per_head):
    r = lax.broadcasted_iota(jnp.int32, (SWA_GROUP * rows_per_head, 1), 0)
    col = jnp.zeros((SWA_GROUP * rows_per_head, 1), F32)
    for g in range(SWA_GROUP):
        col = jnp.where(r // rows_per_head == g, sink_ref[kvh * SWA_GROUP + g], col)
    return col


def _rope(x, cos, sin):
    outs = []
    for cb in range(x.shape[1] // LANES):
        sl = slice(cb * LANES, (cb + 1) * LANES)
        xc = x[:, sl]
        lane = lax.broadcasted_iota(jnp.int32, xc.shape, 1)
        sw = jnp.where(lane % 32 < 16, pltpu.roll(xc, LANES - 16, 1), pltpu.roll(xc, 16, 1))
        outs.append(xc * cos[:, sl] + sw * sin[:, sl])
    return jnp.concatenate(outs, axis=1)


def _swa_lat_kernel(sink_ref, q_ref, kp_ref, kc_ref, kn_ref, vp_ref, vc_ref, vn_ref, ck_ref, cv_ref,
                    cp_ref, cc_ref, cn_ref, sp_ref, sc_ref, sn_ref, o_ref, *, nb):
    n = pl.program_id(1)
    blk = SWA_BLOCK
    hd = SWA_HEAD_DIM
    cos_c, sin_c = cc_ref[...], sc_ref[...]
    kband = jnp.concatenate([_rope(kp_ref[...], cp_ref[...], sp_ref[...]),
                             _rope(kc_ref[...], cos_c, sin_c),
                             _rope(kn_ref[...], cn_ref[...], sn_ref[...])], axis=0).astype(BF16)
    vband = jnp.concatenate([vp_ref[...], vc_ref[...], vn_ref[...]], axis=0).astype(BF16)
    ck = ck_ref[0].astype(BF16)
    cv = cv_ref[0].astype(BF16)
    qi = lax.broadcasted_iota(jnp.int32, (SWA_GROUP * blk, 3 * blk), 0) % blk
    kj = lax.broadcasted_iota(jnp.int32, (SWA_GROUP * blk, 3 * blk), 1)
    k_abs = kj + (n - 1) * blk
    mask = (jnp.abs(kj - blk - qi) <= SWA_BLOCK) & (k_abs >= 0) & (k_abs < nb * blk)
    for kvh in range(SWA_KV_HEADS):
        ks = slice(kvh * hd, (kvh + 1) * hd)
        qr = _rope(q_ref[:, kvh * SWA_KVW:(kvh + 1) * SWA_KVW], cos_c, sin_c) * (hd ** -0.5)
        qg = jnp.concatenate([qr[:, g * hd:(g + 1) * hd] for g in range(SWA_GROUP)],
                             axis=0).astype(BF16)
        s_b = jnp.where(mask, _nt_dot(qg, kband[:, ks]), -1e30)
        s_c = _nt_dot(qg, ck[:, ks])
        sink = _sink_column(sink_ref, kvh, blk)
        m = jnp.maximum(jnp.maximum(jnp.max(s_b, axis=-1, keepdims=True),
                                    jnp.max(s_c, axis=-1, keepdims=True)), sink)
        p_b = jnp.exp(s_b - m)
        p_c = jnp.exp(s_c - m)
        den = jnp.exp(sink - m) + jnp.sum(p_b, axis=-1, keepdims=True) + jnp.sum(p_c, axis=-1, keepdims=True)
        o = (jnp.dot(p_b.astype(BF16), vband[:, ks], preferred_element_type=F32)
             + jnp.dot(p_c.astype(BF16), cv[:, ks], preferred_element_type=F32)) / den
        for g in range(SWA_GROUP):
            cb = (kvh * SWA_GROUP + g) * hd
            o_ref[:, cb:cb + hd] = o[g * blk:(g + 1) * blk, :].astype(BF16)


def _swa_latent(p, bsz, seq, cache_k, cache_v, sink, cos_t, sin_t):
    n = p.shape[0]
    nb = seq // SWA_BLOCK
    kcol = COL_SK // SWA_KVW
    vcol = COL_SV // SWA_KVW
    prev = lambda i: jnp.maximum(i - 1, 0)
    nxt = lambda i: jnp.minimum(i + 1, nb - 1)

    def pspec(col, f):
        return pl.BlockSpec((SWA_BLOCK, SWA_KVW), lambda b, i, s: (b * nb + f(i), col))

    def tspec(f):
        return pl.BlockSpec((SWA_BLOCK, SWA_KVW), lambda b, i, s: (f(i), 0))

    same = lambda i: i
    cspec = pl.BlockSpec((1, cache_k.shape[1], SWA_KVW), lambda b, i, s: (b, 0, 0))
    grid_spec = pltpu.PrefetchScalarGridSpec(
        num_scalar_prefetch=1,
        grid=(bsz, nb),
        in_specs=[pl.BlockSpec((SWA_BLOCK, SWA_QW), lambda b, i, s: (b * nb + i, COL_SQ // SWA_QW)),
                  pspec(kcol, prev), pspec(kcol, same), pspec(kcol, nxt),
                  pspec(vcol, prev), pspec(vcol, same), pspec(vcol, nxt),
                  cspec, cspec,
                  tspec(prev), tspec(same), tspec(nxt), tspec(prev), tspec(same), tspec(nxt)],
        out_specs=pl.BlockSpec((SWA_BLOCK, SWA_QW), lambda b, i, s: (b * nb + i, 0)),
    )
    return pl.pallas_call(
        functools.partial(_swa_lat_kernel, nb=nb),
        grid_spec=grid_spec,
        out_shape=jax.ShapeDtypeStruct((n, SWA_QW), BF16),
        compiler_params=_cparams(("parallel", "arbitrary")),
        name="swa_latent",
    )(sink, p, p, p, p, p, p, p, cache_k, cache_v, cos_t, cos_t, cos_t, sin_t, sin_t, sin_t)


def _swa_ctx_kernel(sink_ref, q_ref, k_ref, v_ref, o_ref):
    s = q_ref.shape[0]
    hd = SWA_HEAD_DIM
    kb = k_ref[...].astype(BF16)
    vb = v_ref[...].astype(BF16)
    for kvh in range(SWA_KV_HEADS):
        ks = slice(kvh * hd, (kvh + 1) * hd)
        qg = jnp.concatenate([q_ref[:, (kvh * SWA_GROUP + g) * hd:(kvh * SWA_GROUP + g + 1) * hd]
                              for g in range(SWA_GROUP)], axis=0)
        qg = (qg * (hd ** -0.5)).astype(BF16)
        sc = _nt_dot(qg, kb[:, ks])
        sink = _sink_column(sink_ref, kvh, s)
        m = jnp.maximum(jnp.max(sc, axis=-1, keepdims=True), sink)
        pr = jnp.exp(sc - m)
        den = jnp.exp(sink - m) + jnp.sum(pr, axis=-1, keepdims=True)
        o = jnp.dot(pr.astype(BF16), vb[:, ks], preferred_element_type=F32) / den
        for g in range(SWA_GROUP):
            cb = (kvh * SWA_GROUP + g) * hd
            o_ref[:, cb:cb + hd] = o[g * s:(g + 1) * s, :].astype(BF16)


def _swa_context(p, bsz, seq, sink):
    n = p.shape[0]
    grid_spec = pltpu.PrefetchScalarGridSpec(
        num_scalar_prefetch=1,
        grid=(bsz,),
        in_specs=[pl.BlockSpec((seq, SWA_QW), lambda b, s: (b, COL_SQ // SWA_QW)),
                  pl.BlockSpec((seq, SWA_KVW), lambda b, s: (b, COL_SK // SWA_KVW)),
                  pl.BlockSpec((seq, SWA_KVW), lambda b, s: (b, COL_SV // SWA_KVW))],
        out_specs=pl.BlockSpec((seq, SWA_QW), lambda b, s: (b, 0)),
    )
    return pl.pallas_call(
        _swa_ctx_kernel,
        grid_spec=grid_spec,
        out_shape=jax.ShapeDtypeStruct((n, SWA_QW), BF16),
        compiler_params=_cparams(("parallel",)),
        name="swa_context",
    )(sink, p, p, p)


MERGE_TM = 256


def _merge_kernel(og_ref, os_ref, ga_ref, gb_ref, x_ref, gt1_ref, sc2_ref, sh2_ref, g2_ref,
                  wa_ref, wb_ref, wo_ref, wrt_ref, x1_ref, h2_ref, lg_ref):
    a = jnp.dot(og_ref[...], wa_ref[...], preferred_element_type=F32)
    b = jnp.dot(os_ref[...], wb_ref[...], preferred_element_type=F32)
    merged = (_sigmoid(ga_ref[...]) * a + _sigmoid(gb_ref[...]) * b).astype(BF16)
    x1 = x_ref[...] + gt1_ref[0] * jnp.dot(merged, wo_ref[...], preferred_element_type=F32)
    x1_ref[...] = x1
    y = x1 * lax.rsqrt(jnp.mean(x1 * x1, axis=-1, keepdims=True) + EPS) * g2_ref[...]
    h2 = y * (1.0 + sc2_ref[0]) + sh2_ref[0]
    _store_row_major(h2_ref, h2)
    lt = _nt_dot(wrt_ref[...], h2, precision=HIGHEST)
    for cb in range(lt.shape[1] // LANES):
        lg_ref[cb] = lt[:, cb * LANES:(cb + 1) * LANES]


def _merge(o_gla, o_swa, p, x2d, seq, gt1, sc2, sh2, g2, wa, wb, wo, wrt):
    n = x2d.shape[0]
    tm = MERGE_TM
    per_b = _tiles_per_mod_row(gt1, n, seq, tm)
    mod_spec = pl.BlockSpec((1, 1, D_MODEL), lambda i: (i // per_b, 0, 0))
    once = pl.Buffered(1)
    return pl.pallas_call(
        _merge_kernel,
        grid=(n // tm,),
        in_specs=[pl.BlockSpec((tm, GLA_VW), lambda i: (i, 0)),
                  pl.BlockSpec((tm, SWA_QW), lambda i: (i, 0)),
                  pl.BlockSpec((tm, D_MODEL), lambda i: (i, COL_GA // D_MODEL)),
                  pl.BlockSpec((tm, D_MODEL), lambda i: (i, COL_GB // D_MODEL)),
                  pl.BlockSpec((tm, D_MODEL), lambda i: (i, 0)),
                  mod_spec, mod_spec, mod_spec,
                  pl.BlockSpec((1, D_MODEL), lambda i: (0, 0)),
                  pl.BlockSpec((GLA_VW, D_MODEL), lambda i: (0, 0), pipeline_mode=once),
                  pl.BlockSpec((SWA_QW, D_MODEL), lambda i: (0, 0), pipeline_mode=once),
                  pl.BlockSpec((D_MODEL, D_MODEL), lambda i: (0, 0), pipeline_mode=once),
                  pl.BlockSpec((N_EXPERTS, D_MODEL), lambda i: (0, 0))],
        out_specs=[pl.BlockSpec((tm, D_MODEL), lambda i: (i, 0)),
                   pl.BlockSpec((tm * ROW_CHUNKS, LANES), lambda i: (i, 0)),
                   pl.BlockSpec((tm // LANES, N_EXPERTS, LANES), lambda i: (i, 0, 0))],
        out_shape=[jax.ShapeDtypeStruct((n, D_MODEL), F32),
                   jax.ShapeDtypeStruct((n * ROW_CHUNKS, LANES), F32),
                   jax.ShapeDtypeStruct((n // LANES, N_EXPERTS, LANES), F32)],
        compiler_params=_cparams(("parallel",)),
        name="merge",
    )(o_gla, o_swa, p, p, x2d, gt1, sc2, sh2, g2, wa, wb, wo, wrt)


def _select_kernel(lg_ref, aff_ref, sel_ref, pos_ref, *, cap):
    nblk = lg_ref.shape[0]
    lg = lg_ref[...]
    ex = jnp.exp(lg - jnp.max(lg, axis=1, keepdims=True))
    aff = ex / jnp.sum(ex, axis=1, keepdims=True)
    aff_ref[...] = aff
    bits = lax.bitcast_convert_type(aff, jnp.int32)

    def count(pred):
        c = jnp.sum(jnp.where(pred, 1.0, 0.0), axis=0, keepdims=True)
        return jnp.sum(c, axis=2, keepdims=True)

    def bit_step(t, cur):
        cand = cur | jnp.left_shift(jnp.int32(1), 30 - t)
        return jnp.where(count(bits >= cand) >= cap, cand, cur)

    thr = lax.fori_loop(0, 31, bit_step, jnp.zeros((1, N_EXPERTS, 1), jnp.int32))
    need = (cap - count(bits > thr))[0]
    thr2 = thr[0]
    ri = lax.broadcasted_iota(jnp.int32, (LANES, LANES), 0)
    ci = lax.broadcasted_iota(jnp.int32, (LANES, LANES), 1)
    upper = (ri <= ci).astype(BF16)

    def blk_step(b, carry):
        run_eq, run_sel = carry
        bb = lax.bitcast_convert_type(aff_ref[b], jnp.int32)
        eq = (bb == thr2).astype(F32)
        eq_excl = jnp.dot(eq.astype(BF16), upper, preferred_element_type=F32) - eq + run_eq
        sel = jnp.where((bb > thr2) | ((eq > 0) & (eq_excl < need)), 1.0, 0.0)
        sel_ref[b] = sel
        pos_ref[b] = jnp.dot(sel.astype(BF16), upper, preferred_element_type=F32) - sel + run_sel
        return (run_eq + jnp.sum(eq, axis=1, keepdims=True),
                run_sel + jnp.sum(sel, axis=1, keepdims=True))

    zero = jnp.zeros((N_EXPERTS, 1), F32)
    lax.fori_loop(0, nblk, blk_step, (zero, zero))


def _select(logits3, cap):
    shp = jax.ShapeDtypeStruct(logits3.shape, F32)
    return pl.pallas_call(
        functools.partial(_select_kernel, cap=float(cap)),
        out_shape=[shp, shp, shp],
        compiler_params=pltpu.CompilerParams(vmem_limit_bytes=VMEM_LIMIT),
        name="select",
    )(logits3)


COMPACT_WIN = 136


def _compact_kernel(base_ref, aff_ref, sel_ref, pos_ref, idx_ref, gate_ref):
    nblk = aff_ref.shape[0]
    idx_ref[...] = jnp.zeros_like(idx_ref)
    gate_ref[...] = jnp.zeros_like(gate_ref)
    slot = lax.broadcasted_iota(jnp.int32, (COMPACT_WIN, LANES), 0).astype(F32)
    lane = lax.broadcasted_iota(jnp.int32, (COMPACT_WIN, LANES), 1)

    def blk_step(b, _):
        aff = aff_ref[b]
        sel = sel_ref[b]
        pos = pos_ref[b]
        tok = (lane + b * LANES).astype(F32)
        for e in range(N_EXPERTS):
            base8 = pl.multiple_of((base_ref[b * N_EXPERTS + e] >> 3) << 3, 8)
            rel = pos[e:e + 1, :] - base8.astype(F32)
            hit = (rel == slot) & (sel[e:e + 1, :] > 0)
            rows = pl.ds(base8, COMPACT_WIN)
            mine = lane == e
            ic = jnp.sum(jnp.where(hit, tok, 0.0), axis=1, keepdims=True)
            gc = jnp.sum(jnp.where(hit, aff[e:e + 1, :], 0.0), axis=1, keepdims=True)
            idx_ref[rows, :] += jnp.where(mine, ic, 0.0)
            gate_ref[rows, :] += jnp.where(mine, gc, 0.0)
        return 0

    lax.fori_loop(0, nblk, blk_step, 0)


def _compact(base, aff3, sel3, pos3, cap):
    cp = cap + 2 * LANES
    full = pl.BlockSpec(aff3.shape, lambda i, s: (0, 0, 0))
    ospec = pl.BlockSpec((cp, LANES), lambda i, s: (0, 0))
    oshape = jax.ShapeDtypeStruct((cp, LANES), F32)
    grid_spec = pltpu.PrefetchScalarGridSpec(
        num_scalar_prefetch=1, grid=(1,), in_specs=[full, full, full], out_specs=[ospec, ospec])
    return pl.pallas_call(
        _compact_kernel,
        grid_spec=grid_spec,
        out_shape=[oshape, oshape],
        compiler_params=_cparams(("arbitrary",)),
        name="compact",
    )(base, aff3, sel3, pos3)


def _ffn_kernel(idx_ref, h2_hbm, wg_ref, wu_ref, wd_ref, o_ref, gbuf, xs, sem, *, tm, total):
    step = pl.program_id(0) * pl.num_programs(1) + pl.program_id(1)
    slot = step % 2

    def row_copy(step_, slot_, i):
        tok = idx_ref[step_ * tm + i]
        return pltpu.make_async_copy(
            h2_hbm.at[pl.ds(pl.multiple_of(tok * ROW_CHUNKS, ROW_CHUNKS), ROW_CHUNKS), :],
            gbuf.at[slot_, pl.ds(pl.multiple_of(i * ROW_CHUNKS, ROW_CHUNKS), ROW_CHUNKS), :],
            sem.at[slot_])

    def issue(step_, slot_):
        def body(i, _):
            row_copy(step_, slot_, i).start()
            return 0
        lax.fori_loop(0, tm, body, 0)

    @pl.when(step == 0)
    def _():
        issue(step, slot)

    @pl.when(step + 1 < total)
    def _():
        issue(step + 1, 1 - slot)

    def wait_body(i, _):
        row_copy(step, slot, i).wait()
        return 0
    lax.fori_loop(0, tm, wait_body, 0)

    for s in range(ROW_CHUNKS):
        xs[:, s * LANES:(s + 1) * LANES] = gbuf[slot, pl.ds(s, tm, stride=ROW_CHUNKS), :].astype(BF16)
    x = xs[...]
    hg = jnp.dot(x, wg_ref[0], preferred_element_type=F32)
    hu = jnp.dot(x, wu_ref[0], preferred_element_type=F32)
    hid = (_silu(hg) * hu).astype(BF16)
    _store_row_major(o_ref, jnp.dot(hid, wd_ref[0], preferred_element_type=F32))


def _expert_ffn(idx_flat, h2_lin, wg, wu, wd, cap):
    tm = min(256, cap)
    nr = cap // tm
    grid_spec = pltpu.PrefetchScalarGridSpec(
        num_scalar_prefetch=1,
        grid=(N_EXPERTS, nr),
        in_specs=[pl.BlockSpec(memory_space=pl.ANY),
                  pl.BlockSpec((1, D_MODEL, EXPERT_FF), lambda e, r, s: (e, 0, 0)),
                  pl.BlockSpec((1, D_MODEL, EXPERT_FF), lambda e, r, s: (e, 0, 0)),
                  pl.BlockSpec((1, EXPERT_FF, D_MODEL), lambda e, r, s: (e, 0, 0))],
        out_specs=pl.BlockSpec((tm * ROW_CHUNKS, LANES), lambda e, r, s: (e * nr + r, 0)),
        scratch_shapes=[pltpu.VMEM((2, tm * ROW_CHUNKS, LANES), F32),
                        pltpu.VMEM((tm, D_MODEL), BF16),
                        pltpu.SemaphoreType.DMA((2,))],
    )
    return pl.pallas_call(
        functools.partial(_ffn_kernel, tm=tm, total=N_EXPERTS * nr),
        grid_spec=grid_spec,
        out_shape=jax.ShapeDtypeStruct((N_EXPERTS * cap * ROW_CHUNKS, LANES), F32),
        compiler_params=_cparams(("arbitrary", "arbitrary")),
        name="expert_ffn",
    )(idx_flat, h2_lin, wg, wu, wd)


def _combine_kernel(idx_ref, gate_ref, lo_ref, ye_hbm, x1_ref, gt2_ref, gf_ref, o_ref, stage, acc, sem,
                    *, tt, ch, cap, ntiles):
    tile = pl.program_id(0)
    slot = tile % 2

    def chunk_start(first_row):
        return jnp.minimum(first_row, cap - ch)

    def chunk_copy(slot_, e, start):
        src = pl.multiple_of((e * cap + start) * ROW_CHUNKS, ROW_CHUNKS)
        return pltpu.make_async_copy(ye_hbm.at[pl.ds(src, ch * ROW_CHUNKS), :], stage.at[slot_, e],
                                     sem.at[slot_, e])

    def issue(tile_, slot_):
        for e in range(N_EXPERTS):
            chunk_copy(slot_, e, chunk_start(lo_ref[tile_ * N_EXPERTS + e])).start()

    @pl.when(tile == 0)
    def _():
        issue(tile, slot)

    @pl.when(tile + 1 < ntiles)
    def _():
        issue(tile + 1, 1 - slot)

    acc[...] = jnp.zeros_like(acc)

    def add_rows(e, start, r0, r1):
        def body(r, _):
            tok = idx_ref[e * cap + r] - tile * tt
            dst = pl.ds(pl.multiple_of(tok * ROW_CHUNKS, ROW_CHUNKS), ROW_CHUNKS)
            src = pl.ds(pl.multiple_of((r - start) * ROW_CHUNKS, ROW_CHUNKS), ROW_CHUNKS)
            acc[dst, :] = acc[dst, :] + gate_ref[e * cap + r] * stage[slot, e, src, :]
            return 0
        lax.fori_loop(r0, r1, body, 0)

    for e in range(N_EXPERTS):
        lo = lo_ref[tile * N_EXPERTS + e]
        hi = lo_ref[(tile + 1) * N_EXPERTS + e]
        start = chunk_start(lo)
        chunk_copy(slot, e, start).wait()
        first_end = jnp.minimum(hi, start + ch)
        add_rows(e, start, lo, first_end)

        def more(r0):
            st = chunk_start(r0)
            cp = chunk_copy(slot, e, st)
            cp.start()
            cp.wait()
            r1 = jnp.minimum(hi, st + ch)
            add_rows(e, st, r0, r1)
            return r1
        lax.while_loop(lambda r0: r0 < hi, more, first_end)

    ssq = jnp.zeros((tt, 1), F32)
    for s in range(ROW_CHUNKS):
        cols = slice(s * LANES, (s + 1) * LANES)
        y = acc[pl.ds(s, tt, stride=ROW_CHUNKS), :]
        x2 = x1_ref[:, cols] + gt2_ref[0][:, cols] * y
        o_ref[:, cols] = x2
        ssq = ssq + jnp.sum(x2 * x2, axis=-1, keepdims=True)
    o_ref[...] = o_ref[...] * lax.rsqrt(ssq / D_MODEL + EPS) * gf_ref[...]


def _combine(idx_flat, gate_flat, lo_tab, ye_lin, x1, seq, gt2, g_final, cap, tt):
    n = x1.shape[0]
    ch = min(64, cap)
    ntiles = n // tt
    per_b = _tiles_per_mod_row(gt2, n, seq, tt)
    grid_spec = pltpu.PrefetchScalarGridSpec(
        num_scalar_prefetch=3,
        grid=(ntiles,),
        in_specs=[pl.BlockSpec(memory_space=pl.ANY),
                  pl.BlockSpec((tt, D_MODEL), lambda i, a, b, c: (i, 0)),
                  pl.BlockSpec((1, 1, D_MODEL), lambda i, a, b, c: (i // per_b, 0, 0)),
                  pl.BlockSpec((1, D_MODEL), lambda i, a, b, c: (0, 0))],
        out_specs=pl.BlockSpec((tt, D_MODEL), lambda i, a, b, c: (i, 0)),
        scratch_shapes=[pltpu.VMEM((2, N_EXPERTS, ch * ROW_CHUNKS, LANES), F32),
                        pltpu.VMEM((tt * ROW_CHUNKS, LANES), F32),
                        pltpu.SemaphoreType.DMA((2, N_EXPERTS))],
    )
    return pl.pallas_call(
        functools.partial(_combine_kernel, tt=tt, ch=ch, cap=cap, ntiles=ntiles),
        grid_spec=grid_spec,
        out_shape=jax.ShapeDtypeStruct((n, D_MODEL), F32),
        compiler_params=_cparams(("arbitrary",)),
        name="combine",
    )(idx_flat, gate_flat, lo_tab, ye_lin, x1, gt2, g_final)


def _rope_tables(seq):
    pos = jnp.arange(seq)
    row = (pos // GRID_W).astype(F32)
    col = (pos % GRID_W).astype(F32)
    npair = SWA_HEAD_DIM // 4
    inv_freq = ROPE_BASE ** (-jnp.arange(npair, dtype=F32) / npair)
    ar = row[:, None] * inv_freq[None, :]
    ac = col[:, None] * inv_freq[None, :]
    cos = jnp.concatenate([jnp.cos(ar), jnp.cos(ar), jnp.cos(ac), jnp.cos(ac)], axis=1)
    sin = jnp.concatenate([-jnp.sin(ar), jnp.sin(ar), -jnp.sin(ac), jnp.sin(ac)], axis=1)
    reps = SWA_KVW // SWA_HEAD_DIM
    return jnp.tile(cos, (1, reps)), jnp.tile(sin, (1, reps))


def _layer(x, mods, wts, latent, s0f, s0b, cache_k, cache_v):
    bsz, seq, _ = x.shape
    n = bsz * seq
    x2d = x.reshape(n, D_MODEL)
    sh1, sc1, gt1, sh2, sc2, gt2 = mods
    p = _in_proj(x2d, seq, sc1, sh1, wts["g1"], wts["w_in_r"])
    o_gla, st_f, st_b = _gla(p, bsz, seq, wts["wdf"], wts["bdf"], wts["wdb"], wts["bdb"], wts["g_gla"],
                             s0f, s0b)
    if latent:
        cos_t, sin_t = _rope_tables(seq)
        o_swa = _swa_latent(p, bsz, seq, cache_k, cache_v, wts["sink"], cos_t, sin_t)
    else:
        o_swa = _swa_context(p, bsz, seq, wts["sink"])
    x1, h2, logits3 = _merge(o_gla, o_swa, p, x2d, seq, gt1, sc2, sh2, wts["g2"],
                             wts["wa"], wts["wb"], wts["wo"], wts["wrt"])
    cap = CAPACITY_FACTOR * n // N_EXPERTS
    aff3, sel3, pos3 = _select(logits3, cap)
    base = pos3[:, :, 0].astype(jnp.int32)
    idx_c, gate_c = _compact(base.reshape(-1), aff3, sel3, pos3, cap)
    idx_flat = idx_c[:cap, :N_EXPERTS].T.astype(jnp.int32).reshape(-1)
    gate_flat = gate_c[:cap, :N_EXPERTS].T.reshape(-1)
    ye = _expert_ffn(idx_flat, h2, wts["wg"], wts["wu"], wts["wd"], cap)
    tt = 256
    lo_tab = jnp.concatenate([base[::tt // LANES], jnp.full((1, N_EXPERTS), cap, jnp.int32)], axis=0)
    y = _combine(idx_flat, gate_flat, lo_tab.reshape(-1), ye, x1, seq, gt2, wts["g_final"], cap, tt)
    return y.reshape(bsz, seq, D_MODEL), p, st_f, st_b


def kernel(x_prompt, x_sample, state_gla_fwd, state_gla_bwd, cache_k, cache_v, c, c_ctx, w_mod, b_mod,
           g_norm1, w_in, w_dec_f, b_dec_f, w_dec_b, b_dec_b, g_gla, attn_sink, w_branch_a, w_branch_b,
           w_out, g_norm2, w_router, w_exp_gate, w_exp_up, w_exp_down, g_final):
    bp, sp, _ = x_prompt.shape
    bl = x_sample.shape[0]
    l = 0
    gla_end = 2 * GLA_KW + 2 * GLA_VW
    dec_end = gla_end + 2 * GLA_RANK
    swa_end = dec_end + SWA_QW + 2 * SWA_KVW
    w = w_in[l]
    w_in_r = jnp.concatenate(
        [w[:, swa_end:], w[:, :gla_end], w[:, dec_end:swa_end], w[:, gla_end:dec_end],
         jnp.zeros((D_MODEL, IN_PAD - w.shape[1]), w.dtype)], axis=1).astype(BF16)
    zpad = jnp.zeros((LANES - 2 * GLA_RANK, GLA_KW), F32)
    zr = jnp.zeros((GLA_RANK, GLA_KW), F32)
    wts = {
        "g1": g_norm1[l].reshape(1, D_MODEL),
        "w_in_r": w_in_r,
        "wdf": jnp.concatenate([w_dec_f[l], zr, zpad], axis=0),
        "wdb": jnp.concatenate([zr, w_dec_b[l], zpad], axis=0),
        "bdf": b_dec_f[l].reshape(1, GLA_KW),
        "bdb": b_dec_b[l].reshape(1, GLA_KW),
        "g_gla": g_gla[l].reshape(1, GLA_VW),
        "sink": attn_sink[l],
        "wa": w_branch_a[l].astype(BF16),
        "wb": w_branch_b[l].astype(BF16),
        "wo": w_out[l].astype(BF16),
        "g2": g_norm2[l].reshape(1, D_MODEL),
        "wrt": w_router[l].T,
        "wg": w_exp_gate[l].astype(BF16),
        "wu": w_exp_up[l].astype(BF16),
        "wd": w_exp_down[l].astype(BF16),
        "g_final": g_final.reshape(1, D_MODEL),
    }
    cond8 = jnp.concatenate([c_ctx[None, :], c, jnp.zeros((8 - 1 - bl, D_MODEL), F32)], axis=0)
    mod = _modulation(cond8, w_mod[l], b_mod[l]).reshape(8, N_MOD, 1, D_MODEL)
    mods_ctx = tuple(mod[0:1, j] for j in range(N_MOD))
    mods_lat = tuple(mod[1:1 + bl, j] for j in range(N_MOD))

    zero_state = jnp.zeros((bp, GLA_HEADS, GLA_DK, GLA_DV), F32)
    y_prompt, p_ctx, st_f, st_b = _layer(x_prompt, mods_ctx, wts, False, zero_state, zero_state, None, None)
    ck = cache_k[:, l].reshape(bl, -1, SWA_KVW)
    cv = cache_v[:, l].reshape(bl, -1, SWA_KVW)
    y_sample, _, _, _ = _layer(x_sample, mods_lat, wts, True, state_gla_fwd[:, l], state_gla_bwd[:, l], ck, cv)

    new_k = p_ctx[:, COL_SK:COL_SK + SWA_KVW].reshape(bp, 1, sp, SWA_KV_HEADS, SWA_HEAD_DIM)
    new_v = p_ctx[:, COL_SV:COL_SV + SWA_KVW].reshape(bp, 1, sp, SWA_KV_HEADS, SWA_HEAD_DIM)
    return (y_prompt, y_sample, st_f[:, None], st_b[:, None], new_k, new_v)
```

```python
import functools

import jax
import jax.numpy as jnp
from jax import lax
from jax.experimental import pallas as pl
from jax.experimental.pallas import tpu as pltpu

F32 = jnp.float32
BF16 = jnp.bfloat16
HIGHEST = lax.Precision.HIGHEST

D_MODEL = 2048
N_MOD = 6
EPS = 1e-6

GLA_HEADS = 4
GLA_DK = 128
GLA_DV = 256
GLA_KW = GLA_HEADS * GLA_DK
GLA_VW = GLA_HEADS * GLA_DV
GLA_RANK = 16
GLA_TAU = 16.0
GLA_CHUNK = 64

SWA_HEADS = 16
SWA_KV_HEADS = 4
SWA_GROUP = 4
SWA_HEAD_DIM = 64
SWA_QW = SWA_HEADS * SWA_HEAD_DIM
SWA_KVW = SWA_KV_HEADS * SWA_HEAD_DIM
SWA_BLOCK = 128
GRID_W = 64
ROPE_BASE = 10000.0

N_EXPERTS = 16
EXPERT_FF = D_MODEL // 2
CAPACITY_FACTOR = 2

LANES = 128
ROW_CHUNKS = D_MODEL // LANES
VMEM_LIMIT = 56 * 1024 * 1024

COL_GA = 0
COL_GB = 2048
COL_GQ = 4096
COL_GK = 4608
COL_GV = 5120
COL_GR = 6144
COL_SQ = 7168
COL_SK = 8192
COL_SV = 8448
COL_DEC = 8704
IN_PAD = 8960
IN_TN = 1280


def _cparams(sem, **kw):
    return pltpu.CompilerParams(dimension_semantics=sem, vmem_limit_bytes=VMEM_LIMIT, **kw)


def _silu(x):
    return x * (1.0 / (1.0 + jnp.exp(-x)))


def _sigmoid(x):
    return 1.0 / (1.0 + jnp.exp(-x))


def _store_row_major(ref, x):
    rows = x.shape[0]
    for s in range(ROW_CHUNKS):
        ref[pl.ds(s, rows, stride=ROW_CHUNKS), :] = x[:, s * LANES:(s + 1) * LANES]


def _tiles_per_mod_row(mod, n, seq, tile):
    return seq // tile if mod.shape[0] > 1 else n // tile


def _mod_kernel(c_ref, w_ref, b_ref, o_ref):
    a = _silu(c_ref[...]).astype(BF16)
    o_ref[...] = jnp.dot(a, w_ref[...].astype(BF16), preferred_element_type=F32) + b_ref[...]


def _modulation(cond8, w_mod, b_mod):
    n_out = w_mod.shape[1]
    tn = 1536
    return pl.pallas_call(
        _mod_kernel,
        grid=(n_out // tn,),
        in_specs=[pl.BlockSpec((8, D_MODEL), lambda j: (0, 0)),
                  pl.BlockSpec((D_MODEL, tn), lambda j: (0, j)),
                  pl.BlockSpec((1, tn), lambda j: (0, j))],
        out_specs=pl.BlockSpec((8, tn), lambda j: (0, j)),
        out_shape=jax.ShapeDtypeStruct((8, n_out), F32),
        compiler_params=_cparams(("arbitrary",)),
        name="modulation",
    )(cond8, w_mod, b_mod.reshape(1, n_out))


def _in_proj_kernel(x_ref, sc_ref, sh_ref, g_ref, w_ref, o_ref, h_scr):
    @pl.when(pl.program_id(1) == 0)
    def _():
        x = x_ref[...]
        y = x * lax.rsqrt(jnp.mean(x * x, axis=-1, keepdims=True) + EPS) * g_ref[...]
        h_scr[...] = (y * (1.0 + sc_ref[0]) + sh_ref[0]).astype(BF16)

    o_ref[...] = jnp.dot(h_scr[...], w_ref[...], preferred_element_type=F32)


def _in_proj(x2d, seq, sc, sh, g, w_r):
    n = x2d.shape[0]
    tm = min(1024, seq if sc.shape[0] > 1 else n)
    per_b = _tiles_per_mod_row(sc, n, seq, tm)
    return pl.pallas_call(
        _in_proj_kernel,
        grid=(n // tm, IN_PAD // IN_TN),
        in_specs=[pl.BlockSpec((tm, D_MODEL), lambda i, j: (i, 0)),
                  pl.BlockSpec((1, 1, D_MODEL), lambda i, j: (i // per_b, 0, 0)),
                  pl.BlockSpec((1, 1, D_MODEL), lambda i, j: (i // per_b, 0, 0)),
                  pl.BlockSpec((1, D_MODEL), lambda i, j: (0, 0)),
                  pl.BlockSpec((D_MODEL, IN_TN), lambda i, j: (0, j))],
        out_specs=pl.BlockSpec((tm, IN_TN), lambda i, j: (i, j)),
        out_shape=jax.ShapeDtypeStruct((n, IN_PAD), F32),
        scratch_shapes=[pltpu.VMEM((tm, D_MODEL), BF16)],
        compiler_params=_cparams(("parallel", "arbitrary")),
        name="in_proj",
    )(x2d, sc, sh, g, w_r)


GLA_BLK = 256
GLA_HPS = 2


def _tn_dot(a, b, precision=None):
    return lax.dot_general(a, b, (((0,), (0,)), ((), ())), precision=precision,
                           preferred_element_type=F32)


def _nt_dot(a, b, precision=None):
    return lax.dot_general(a, b, (((1,), (1,)), ((), ())), precision=precision,
                           preferred_element_type=F32)


def _split_bf16(x, parts):
    out = []
    for _ in range(parts):
        piece = x.astype(BF16)
        out.append(piece)
        x = x - piece.astype(F32)
    return out


def _dot3(x, w):
    xh, xl = _split_bf16(x, 2)
    wh, wl = _split_bf16(w, 2)
    d = functools.partial(jnp.dot, preferred_element_type=F32)
    return d(xh, wh) + d(xl, wh) + d(xh, wl)


def _gla_kernel(q_ref, k_ref, v_ref, r_ref, dec_ref, wdf_ref, bdf_ref, wdb_ref, bdb_ref, gg_ref,
                s0f_ref, s0b_ref, o_ref, stf_ref, stb_ref, state_t, of_scr, *, nblk):
    i = pl.program_id(2)
    c = GLA_CHUNK
    nch = GLA_BLK // c
    ri = lax.broadcasted_iota(jnp.int32, (GLA_BLK, GLA_BLK), 0)
    ci = lax.broadcasted_iota(jnp.int32, (GLA_BLK, GLA_BLK), 1)
    same_chunk = (ri // c) == (ci // c)

    @pl.when(i == 0)
    def _():
        for h in range(GLA_HPS):
            state_t[h] = s0f_ref[0, h].T

    @pl.when(i == nblk)
    def _():
        for h in range(GLA_HPS):
            state_t[h] = s0b_ref[0, h].T

    def run(fwd):
        blk = i if fwd else 2 * nblk - 1 - i
        keep = same_chunk & ((ci <= ri) if fwd else (ci >= ri))
        w_ref, b_ref = (wdf_ref, bdf_ref) if fwd else (wdb_ref, bdb_ref)
        logit = _dot3(dec_ref[...], w_ref[...]) + b_ref[...]
        g = (jnp.minimum(logit, 0.0) - jnp.log(1.0 + jnp.exp(-jnp.abs(logit)))) / GLA_TAU
        tri = keep.astype(BF16)
        b = sum(jnp.dot(tri, piece, preferred_element_type=F32) for piece in _split_bf16(g, 3))
        edge = c - 1 if fwd else 0
        tots = [b[ch * c + edge:ch * c + edge + 1, :] for ch in range(nch)]
        totb = jnp.concatenate([jnp.broadcast_to(t, (c, t.shape[1])) for t in tots], axis=0)
        q_in = (q_ref[...] * (GLA_DK ** -0.5) * jnp.exp(b)).astype(BF16)
        k = k_ref[...]
        k_in = (k * jnp.exp(-b)).astype(BF16)
        k_st = (k * jnp.exp(totb - b)).astype(BF16)
        srow = pl.multiple_of(blk * GLA_BLK, GLA_BLK)
        for h in range(GLA_HPS):
            ks = slice(h * GLA_DK, (h + 1) * GLA_DK)
            vs = slice(h * GLA_DV, (h + 1) * GLA_DV)
            v = v_ref[:, vs].astype(BF16)
            a = jnp.where(keep, _nt_dot(q_in[:, ks], k_in[:, ks]), 0.0).astype(BF16)
            o_intra = jnp.dot(a, v, preferred_element_type=F32)
            st = state_t[h]
            o_inter = [None] * nch
            for cc in range(nch):
                ch = cc if fwd else nch - 1 - cc
                rows = slice(ch * c, (ch + 1) * c)
                o_inter[ch] = _nt_dot(q_in[rows, ks], st.astype(BF16))
                st = st * jnp.exp(tots[ch][:, ks]) + _tn_dot(v[rows, :], k_st[rows, ks])
            state_t[h] = st
            o = o_intra + jnp.concatenate(o_inter, axis=0)
            if fwd:
                of_scr[pl.ds(srow, GLA_BLK), vs] = o
            else:
                o = o + of_scr[pl.ds(srow, GLA_BLK), vs]
                o = o * lax.rsqrt(jnp.mean(o * o, axis=-1, keepdims=True) + EPS) * gg_ref[:, vs]
                o_ref[:, vs] = (o * _silu(r_ref[:, vs])).astype(BF16)

    @pl.when(i < nblk)
    def _():
        run(True)

    @pl.when(i >= nblk)
    def _():
        run(False)

    @pl.when(i == nblk - 1)
    def _():
        for h in range(GLA_HPS):
            stf_ref[0, h] = state_t[h].T

    @pl.when(i == 2 * nblk - 1)
    def _():
        for h in range(GLA_HPS):
            stb_ref[0, h] = state_t[h].T


def _gla(p, bsz, seq, wdf, bdf, wdb, bdb, g_gla, s0f, s0b):
    n = p.shape[0]
    nblk = seq // GLA_BLK
    kw = GLA_HPS * GLA_DK
    vw = GLA_HPS * GLA_DV

    def rb(b, i):
        return b * nblk + jnp.where(i < nblk, i, 2 * nblk - 1 - i)

    def orb(b, i):
        return b * nblk + jnp.where(i < nblk, nblk - 1, 2 * nblk - 1 - i)

    st_spec = pl.BlockSpec((1, GLA_HPS, GLA_DK, GLA_DV), lambda b, h, i: (b, h, 0, 0))
    st_shape = jax.ShapeDtypeStruct((bsz, GLA_HEADS, GLA_DK, GLA_DV), F32)
    return pl.pallas_call(
        functools.partial(_gla_kernel, nblk=nblk),
        grid=(bsz, GLA_HEADS // GLA_HPS, 2 * nblk),
        in_specs=[pl.BlockSpec((GLA_BLK, kw), lambda b, h, i: (rb(b, i), COL_GQ // kw + h)),
                  pl.BlockSpec((GLA_BLK, kw), lambda b, h, i: (rb(b, i), COL_GK // kw + h)),
                  pl.BlockSpec((GLA_BLK, vw), lambda b, h, i: (rb(b, i), COL_GV // vw + h)),
                  pl.BlockSpec((GLA_BLK, vw), lambda b, h, i: (rb(b, i), COL_GR // vw + h)),
                  pl.BlockSpec((GLA_BLK, LANES), lambda b, h, i: (rb(b, i), COL_DEC // LANES)),
                  pl.BlockSpec((LANES, kw), lambda b, h, i: (0, h)),
                  pl.BlockSpec((1, kw), lambda b, h, i: (0, h)),
                  pl.BlockSpec((LANES, kw), lambda b, h, i: (0, h)),
                  pl.BlockSpec((1, kw), lambda b, h, i: (0, h)),
                  pl.BlockSpec((1, vw), lambda b, h, i: (0, h)),
                  st_spec, st_spec],
        out_specs=[pl.BlockSpec((GLA_BLK, vw), lambda b, h, i: (orb(b, i), h)), st_spec, st_spec],
        out_shape=[jax.ShapeDtypeStruct((n, GLA_VW), BF16), st_shape, st_shape],
        scratch_shapes=[pltpu.VMEM((GLA_HPS, GLA_DV, GLA_DK), F32), pltpu.VMEM((seq, vw), F32)],
        compiler_params=_cparams(("parallel", "parallel", "arbitrary")),
        name="gla",
    )(p, p, p, p, p, wdf, bdf, wdb, bdb, g_gla, s0f, s0b)


def _sink_column(sink_ref, kvh, rows_per_head):
    r = lax.broadcasted_iota(jnp.int32, (SWA_GROUP * rows_per_head, 1), 0)
    col = jnp.zeros((SWA_GROUP * rows_per_head, 1), F32)
    for g in range(SWA_GROUP):
        col = jnp.where(r // rows_per_head == g, sink_ref[kvh * SWA_GROUP + g], col)
    return col


def _rope(x, cos, sin):
    outs = []
    for cb in range(x.shape[1] // LANES):
        sl = slice(cb * LANES, (cb + 1) * LANES)
        xc = x[:, sl]
        lane = lax.broadcasted_iota(jnp.int32, xc.shape, 1)
        sw = jnp.where(lane % 32 < 16, pltpu.roll(xc, LANES - 16, 1), pltpu.roll(xc, 16, 1))
        outs.append(xc * cos[:, sl] + sw * sin[:, sl])
    return jnp.concatenate(outs, axis=1)


def _swa_lat_kernel(sink_ref, q_ref, kp_ref, kc_ref, kn_ref, vp_ref, vc_ref, vn_ref, ck_ref, cv_ref,
                    cp_ref, cc_ref, cn_ref, sp_ref, sc_ref, sn_ref, o_ref, *, nb):
    n = pl.program_id(1)
    blk = SWA_BLOCK
    hd = SWA_HEAD_DIM
    cos_c, sin_c = cc_ref[...], sc_ref[...]
    kband = jnp.concatenate([_rope(kp_ref[...], cp_ref[...], sp_ref[...]),
                             _rope(kc_ref[...], cos_c, sin_c),
                             _rope(kn_ref[...], cn_ref[...], sn_ref[...])], axis=0).astype(BF16)
    vband = jnp.concatenate([vp_ref[...], vc_ref[...], vn_ref[...]], axis=0).astype(BF16)
    ck = ck_ref[0].astype(BF16)
    cv = cv_ref[0].astype(BF16)
    qi = lax.broadcasted_iota(jnp.int32, (SWA_GROUP * blk, 3 * blk), 0) % blk
    kj = lax.broadcasted_iota(jnp.int32, (SWA_GROUP * blk, 3 * blk), 1)
    k_abs = kj + (n - 1) * blk
    mask = (jnp.abs(kj - blk - qi) <= SWA_BLOCK) & (k_abs >= 0) & (k_abs < nb * blk)
    for kvh in range(SWA_KV_HEADS):
        ks = slice(kvh * hd, (kvh + 1) * hd)
        qr = _rope(q_ref[:, kvh * SWA_KVW:(kvh + 1) * SWA_KVW], cos_c, sin_c) * (hd ** -0.5)
        qg = jnp.concatenate([qr[:, g * hd:(g + 1) * hd] for g in range(SWA_GROUP)],
                             axis=0).astype(BF16)
        s_b = jnp.where(mask, _nt_dot(qg, kband[:, ks]), -1e30)
        s_c = _nt_dot(qg, ck[:, ks])
        sink = _sink_column(sink_ref, kvh, blk)
        m = jnp.maximum(jnp.maximum(jnp.max(s_b, axis=-1, keepdims=True),
                                    jnp.max(s_c, axis=-1, keepdims=True)), sink)
        p_b = jnp.exp(s_b - m)
        p_c = jnp.exp(s_c - m)
        den = jnp.exp(sink - m) + jnp.sum(p_b, axis=-1, keepdims=True) + jnp.sum(p_c, axis=-1, keepdims=True)
        o = (jnp.dot(p_b.astype(BF16), vband[:, ks], preferred_element_type=F32)
             + jnp.dot(p_c.astype(BF16), cv[:, ks], preferred_element_type=F32)) / den
        for g in range(SWA_GROUP):
            cb = (kvh * SWA_GROUP + g) * hd
            o_ref[:, cb:cb + hd] = o[g * blk:(g + 1) * blk, :].astype(BF16)


def _swa_latent(p, bsz, seq, cache_k, cache_v, sink, cos_t, sin_t):
    n = p.shape[0]
    nb = seq // SWA_BLOCK
    kcol = COL_SK // SWA_KVW
    vcol = COL_SV // SWA_KVW
    prev = lambda i: jnp.maximum(i - 1, 0)
    nxt = lambda i: jnp.minimum(i + 1, nb - 1)

    def pspec(col, f):
        return pl.BlockSpec((SWA_BLOCK, SWA_KVW), lambda b, i, s: (b * nb + f(i), col))

    def tspec(f):
        return pl.BlockSpec((SWA_BLOCK, SWA_KVW), lambda b, i, s: (f(i), 0))

    same = lambda i: i
    cspec = pl.BlockSpec((1, cache_k.shape[1], SWA_KVW), lambda b, i, s: (b, 0, 0))
    grid_spec = pltpu.PrefetchScalarGridSpec(
        num_scalar_prefetch=1,
        grid=(bsz, nb),
        in_specs=[pl.BlockSpec((SWA_BLOCK, SWA_QW), lambda b, i, s: (b * nb + i, COL_SQ // SWA_QW)),
                  pspec(kcol, prev), pspec(kcol, same), pspec(kcol, nxt),
                  pspec(vcol, prev), pspec(vcol, same), pspec(vcol, nxt),
                  cspec, cspec,
                  tspec(prev), tspec(same), tspec(nxt), tspec(prev), tspec(same), tspec(nxt)],
        out_specs=pl.BlockSpec((SWA_BLOCK, SWA_QW), lambda b, i, s: (b * nb + i, 0)),
    )
    return pl.pallas_call(
        functools.partial(_swa_lat_kernel, nb=nb),
        grid_spec=grid_spec,
        out_shape=jax.ShapeDtypeStruct((n, SWA_QW), BF16),
        compiler_params=_cparams(("parallel", "arbitrary")),
        name="swa_latent",
    )(sink, p, p, p, p, p, p, p, cache_k, cache_v, cos_t, cos_t, cos_t, sin_t, sin_t, sin_t)


def _swa_ctx_kernel(sink_ref, q_ref, k_ref, v_ref, o_ref):
    s = q_ref.shape[0]
    hd = SWA_HEAD_DIM
    kb = k_ref[...].astype(BF16)
    vb = v_ref[...].astype(BF16)
    for kvh in range(SWA_KV_HEADS):
        ks = slice(kvh * hd, (kvh + 1) * hd)
        qg = jnp.concatenate([q_ref[:, (kvh * SWA_GROUP + g) * hd:(kvh * SWA_GROUP + g + 1) * hd]
                              for g in range(SWA_GROUP)], axis=0)
        qg = (qg * (hd ** -0.5)).astype(BF16)
        sc = _nt_dot(qg, kb[:, ks])
        sink = _sink_column(sink_ref, kvh, s)
        m = jnp.maximum(jnp.max(sc, axis=-1, keepdims=True), sink)
        pr = jnp.exp(sc - m)
        den = jnp.exp(sink - m) + jnp.sum(pr, axis=-1, keepdims=True)
        o = jnp.dot(pr.astype(BF16), vb[:, ks], preferred_element_type=F32) / den
        for g in range(SWA_GROUP):
            cb = (kvh * SWA_GROUP + g) * hd
            o_ref[:, cb:cb + hd] = o[g * s:(g + 1) * s, :].astype(BF16)


def _swa_context(p, bsz, seq, sink):
    n = p.shape[0]
    grid_spec = pltpu.PrefetchScalarGridSpec(
        num_scalar_prefetch=1,
        grid=(bsz,),
        in_specs=[pl.BlockSpec((seq, SWA_QW), lambda b, s: (b, COL_SQ // SWA_QW)),
                  pl.BlockSpec((seq, SWA_KVW), lambda b, s: (b, COL_SK // SWA_KVW)),
                  pl.BlockSpec((seq, SWA_KVW), lambda b, s: (b, COL_SV // SWA_KVW))],
        out_specs=pl.BlockSpec((seq, SWA_QW), lambda b, s: (b, 0)),
    )
    return pl.pallas_call(
        _swa_ctx_kernel,
        grid_spec=grid_spec,
        out_shape=jax.ShapeDtypeStruct((n, SWA_QW), BF16),
        compiler_params=_cparams(("parallel",)),
        name="swa_context",
    )(sink, p, p, p)


MERGE_TM = 512


def _branch_kernel(og_ref, os_ref, ga_ref, gb_ref, wa_ref, wb_ref, m_ref):
    a = jnp.dot(og_ref[...], wa_ref[...], preferred_element_type=F32)
    b = jnp.dot(os_ref[...], wb_ref[...], preferred_element_type=F32)
    m_ref[...] = (_sigmoid(ga_ref[...]) * a + _sigmoid(gb_ref[...]) * b).astype(BF16)


def _branch_merge(o_gla, o_swa, p, wa, wb):
    n = o_gla.shape[0]
    tm = MERGE_TM
    once = pl.Buffered(1)
    return pl.pallas_call(
        _branch_kernel,
        grid=(n // tm,),
        in_specs=[pl.BlockSpec((tm, GLA_VW), lambda i: (i, 0)),
                  pl.BlockSpec((tm, SWA_QW), lambda i: (i, 0)),
                  pl.BlockSpec((tm, D_MODEL), lambda i: (i, COL_GA // D_MODEL)),
                  pl.BlockSpec((tm, D_MODEL), lambda i: (i, COL_GB // D_MODEL)),
                  pl.BlockSpec((GLA_VW, D_MODEL), lambda i: (0, 0), pipeline_mode=once),
                  pl.BlockSpec((SWA_QW, D_MODEL), lambda i: (0, 0), pipeline_mode=once)],
        out_specs=pl.BlockSpec((tm, D_MODEL), lambda i: (i, 0)),
        out_shape=jax.ShapeDtypeStruct((n, D_MODEL), BF16),
        compiler_params=_cparams(("parallel",)),
        name="branch_merge",
    )(o_gla, o_swa, p, p, wa, wb)


def _out_kernel(m_ref, x_ref, gt1_ref, sc2_ref, sh2_ref, g2_ref, wo_ref, wr2_ref, wr1_ref,
                x1_ref, h2_ref, lg_ref):
    x1 = x_ref[...] + gt1_ref[0] * jnp.dot(m_ref[...], wo_ref[...], preferred_element_type=F32)
    x1_ref[...] = x1
    y = x1 * lax.rsqrt(jnp.mean(x1 * x1, axis=-1, keepdims=True) + EPS) * g2_ref[...]
    h2 = y * (1.0 + sc2_ref[0]) + sh2_ref[0]
    _store_row_major(h2_ref, h2)
    h_hi, h_lo = _split_bf16(h2, 2)
    l2 = (jnp.dot(h_hi, wr2_ref[...], preferred_element_type=F32)
          + jnp.dot(h_lo, wr1_ref[...], preferred_element_type=F32))
    lt = l2.T
    lt = lt[0:N_EXPERTS, :] + lt[N_EXPERTS:2 * N_EXPERTS, :]
    for cb in range(lt.shape[1] // LANES):
        lg_ref[cb] = lt[:, cb * LANES:(cb + 1) * LANES]


def _out_proj(merged, x2d, seq, gt1, sc2, sh2, g2, wo, wr2, wr1):
    n = x2d.shape[0]
    tm = MERGE_TM
    per_b = _tiles_per_mod_row(gt1, n, seq, tm)
    mod_spec = pl.BlockSpec((1, 1, D_MODEL), lambda i: (i // per_b, 0, 0))
    once = pl.Buffered(1)
    return pl.pallas_call(
        _out_kernel,
        grid=(n // tm,),
        in_specs=[pl.BlockSpec((tm, D_MODEL), lambda i: (i, 0)),
                  pl.BlockSpec((tm, D_MODEL), lambda i: (i, 0)),
                  mod_spec, mod_spec, mod_spec,
                  pl.BlockSpec((1, D_MODEL), lambda i: (0, 0)),
                  pl.BlockSpec((D_MODEL, D_MODEL), lambda i: (0, 0), pipeline_mode=once),
                  pl.BlockSpec((D_MODEL, LANES), lambda i: (0, 0)),
                  pl.BlockSpec((D_MODEL, LANES), lambda i: (0, 0))],
        out_specs=[pl.BlockSpec((tm, D_MODEL), lambda i: (i, 0)),
                   pl.BlockSpec((tm * ROW_CHUNKS, LANES), lambda i: (i, 0)),
                   pl.BlockSpec((tm // LANES, N_EXPERTS, LANES), lambda i: (i, 0, 0))],
        out_shape=[jax.ShapeDtypeStruct((n, D_MODEL), F32),
                   jax.ShapeDtypeStruct((n * ROW_CHUNKS, LANES), F32),
                   jax.ShapeDtypeStruct((n // LANES, N_EXPERTS, LANES), F32)],
        compiler_params=_cparams(("parallel",)),
        name="out_proj",
    )(merged, x2d, gt1, sc2, sh2, g2, wo, wr2, wr1)


def _select_kernel(lg_ref, aff_ref, sel_ref, pos_ref, *, cap):
    nblk = lg_ref.shape[0]
    lg = lg_ref[...]
    ex = jnp.exp(lg - jnp.max(lg, axis=1, keepdims=True))
    aff = ex / jnp.sum(ex, axis=1, keepdims=True)
    aff_ref[...] = aff
    bits = lax.bitcast_convert_type(aff, jnp.int32)

    def count(pred):
        c = jnp.sum(jnp.where(pred, 1.0, 0.0), axis=0, keepdims=True)
        return jnp.sum(c, axis=2, keepdims=True)

    def bit_step(t, cur):
        cand = cur | jnp.left_shift(jnp.int32(1), 30 - t)
        return jnp.where(count(bits >= cand) >= cap, cand, cur)

    thr = lax.fori_loop(0, 31, bit_step, jnp.zeros((1, N_EXPERTS, 1), jnp.int32))
    need = (cap - count(bits > thr))[0]
    thr2 = thr[0]
    ri = lax.broadcasted_iota(jnp.int32, (LANES, LANES), 0)
    ci = lax.broadcasted_iota(jnp.int32, (LANES, LANES), 1)
    upper = (ri <= ci).astype(BF16)

    def blk_step(b, carry):
        run_eq, run_sel = carry
        bb = lax.bitcast_convert_type(aff_ref[b], jnp.int32)
        eq = (bb == thr2).astype(F32)
        eq_excl = jnp.dot(eq.astype(BF16), upper, preferred_element_type=F32) - eq + run_eq
        sel = jnp.where((bb > thr2) | ((eq > 0) & (eq_excl < need)), 1.0, 0.0)
        sel_ref[b] = sel
        pos_ref[b] = jnp.dot(sel.astype(BF16), upper, preferred_element_type=F32) - sel + run_sel
        return (run_eq + jnp.sum(eq, axis=1, keepdims=True),
                run_sel + jnp.sum(sel, axis=1, keepdims=True))

    zero = jnp.zeros((N_EXPERTS, 1), F32)
    lax.fori_loop(0, nblk, blk_step, (zero, zero))


def _select(logits3, cap):
    shp = jax.ShapeDtypeStruct(logits3.shape, F32)
    return pl.pallas_call(
        functools.partial(_select_kernel, cap=float(cap)),
        out_shape=[shp, shp, shp],
        compiler_params=pltpu.CompilerParams(vmem_limit_bytes=VMEM_LIMIT),
        name="select",
    )(logits3)


COMPACT_WIN = 136


def _compact_kernel(base_ref, aff_ref, sel_ref, pos_ref, idx_ref, gate_ref):
    nblk = aff_ref.shape[0]
    idx_ref[...] = jnp.zeros_like(idx_ref)
    gate_ref[...] = jnp.zeros_like(gate_ref)
    slot = lax.broadcasted_iota(jnp.int32, (COMPACT_WIN, LANES), 0).astype(F32)
    lane = lax.broadcasted_iota(jnp.int32, (COMPACT_WIN, LANES), 1)

    def blk_step(b, _):
        aff = aff_ref[b]
        sel = sel_ref[b]
        pos = pos_ref[b]
        tok = (lane + b * LANES).astype(F32)
        for e in range(N_EXPERTS):
            base8 = pl.multiple_of((base_ref[b * N_EXPERTS + e] >> 3) << 3, 8)
            rel = pos[e:e + 1, :] - base8.astype(F32)
            hit = (rel == slot) & (sel[e:e + 1, :] > 0)
            rows = pl.ds(base8, COMPACT_WIN)
            mine = lane == e
            ic = jnp.sum(jnp.where(hit, tok, 0.0), axis=1, keepdims=True)
            gc = jnp.sum(jnp.where(hit, aff[e:e + 1, :], 0.0), axis=1, keepdims=True)
            idx_ref[rows, :] += jnp.where(mine, ic, 0.0)
            gate_ref[rows, :] += jnp.where(mine, gc, 0.0)
        return 0

    lax.fori_loop(0, nblk, blk_step, 0)


def _compact(base, aff3, sel3, pos3, cap):
    cp = cap + 2 * LANES
    full = pl.BlockSpec(aff3.shape, lambda i, s: (0, 0, 0))
    ospec = pl.BlockSpec((cp, LANES), lambda i, s: (0, 0))
    oshape = jax.ShapeDtypeStruct((cp, LANES), F32)
    grid_spec = pltpu.PrefetchScalarGridSpec(
        num_scalar_prefetch=1, grid=(1,), in_specs=[full, full, full], out_specs=[ospec, ospec])
    return pl.pallas_call(
        _compact_kernel,
        grid_spec=grid_spec,
        out_shape=[oshape, oshape],
        compiler_params=_cparams(("arbitrary",)),
        name="compact",
    )(base, aff3, sel3, pos3)


def _ffn_kernel(idx_ref, h2_hbm, wg_ref, wu_ref, wd_ref, o_ref, gbuf, xs, sem, *, tm, total):
    step = pl.program_id(0) * pl.num_programs(1) + pl.program_id(1)
    slot = step % 2

    def row_copy(step_, slot_, i):
        tok = idx_ref[step_ * tm + i]
        return pltpu.make_async_copy(
            h2_hbm.at[pl.ds(pl.multiple_of(tok * ROW_CHUNKS, ROW_CHUNKS), ROW_CHUNKS), :],
            gbuf.at[slot_, pl.ds(pl.multiple_of(i * ROW_CHUNKS, ROW_CHUNKS), ROW_CHUNKS), :],
            sem.at[slot_])

    def wait_rows(slot_):
        pltpu.make_async_copy(h2_hbm.at[pl.ds(0, tm * ROW_CHUNKS), :], gbuf.at[slot_], sem.at[slot_]).wait()

    @pl.when(step == 0)
    def _():
        def body(i, _):
            row_copy(step, slot, i).start()
            return 0
        lax.fori_loop(0, tm, body, 0)

    wait_rows(slot)
    for s in range(ROW_CHUNKS):
        xs[:, s * LANES:(s + 1) * LANES] = gbuf[slot, pl.ds(s, tm, stride=ROW_CHUNKS), :].astype(BF16)

    nxt = jnp.minimum(step + 1, total - 1)
    for i in range(tm):
        row_copy(nxt, 1 - slot, i).start()

    x = xs[...]
    hg = jnp.dot(x, wg_ref[0], preferred_element_type=F32)
    hu = jnp.dot(x, wu_ref[0], preferred_element_type=F32)
    hid = (_silu(hg) * hu).astype(BF16)
    _store_row_major(o_ref, jnp.dot(hid, wd_ref[0], preferred_element_type=F32))

    @pl.when(step == total - 1)
    def _():
        wait_rows(1 - slot)


def _expert_ffn(idx_flat, h2_lin, wg, wu, wd, cap):
    tm = min(512, cap)
    nr = cap // tm
    grid_spec = pltpu.PrefetchScalarGridSpec(
        num_scalar_prefetch=1,
        grid=(N_EXPERTS, nr),
        in_specs=[pl.BlockSpec(memory_space=pl.ANY),
                  pl.BlockSpec((1, D_MODEL, EXPERT_FF), lambda e, r, s: (e, 0, 0)),
                  pl.BlockSpec((1, D_MODEL, EXPERT_FF), lambda e, r, s: (e, 0, 0)),
                  pl.BlockSpec((1, EXPERT_FF, D_MODEL), lambda e, r, s: (e, 0, 0))],
        out_specs=pl.BlockSpec((tm * ROW_CHUNKS, LANES), lambda e, r, s: (e * nr + r, 0)),
        scratch_shapes=[pltpu.VMEM((2, tm * ROW_CHUNKS, LANES), F32),
                        pltpu.VMEM((tm, D_MODEL), BF16),
                        pltpu.SemaphoreType.DMA((2,))],
    )
    return pl.pallas_call(
        functools.partial(_ffn_kernel, tm=tm, total=N_EXPERTS * nr),
        grid_spec=grid_spec,
        out_shape=jax.ShapeDtypeStruct((N_EXPERTS * cap * ROW_CHUNKS, LANES), F32),
        compiler_params=_cparams(("arbitrary", "arbitrary")),
        name="expert_ffn",
    )(idx_flat, h2_lin, wg, wu, wd)


COMBINE_UNROLL = 4


def _combine_kernel(idx_ref, gate_ref, lo_ref, ye_hbm, x1_ref, gt2_ref, gf_ref, o_ref, stage, acc, sem,
                    *, tt, ch, cap, ntiles):
    tile = pl.program_id(0)
    slot = tile % 2

    def chunk_start(first_row):
        return jnp.minimum(first_row, cap - ch)

    def chunk_copy(slot_, e, start):
        src = pl.multiple_of((e * cap + start) * ROW_CHUNKS, ROW_CHUNKS)
        return pltpu.make_async_copy(ye_hbm.at[pl.ds(src, ch * ROW_CHUNKS), :], stage.at[slot_, e],
                                     sem.at[slot_, e])

    def issue(tile_, slot_):
        for e in range(N_EXPERTS):
            chunk_copy(slot_, e, chunk_start(lo_ref[tile_ * N_EXPERTS + e])).start()

    @pl.when(tile == 0)
    def _():
        issue(tile, slot)

    @pl.when(tile + 1 < ntiles)
    def _():
        issue(tile + 1, 1 - slot)

    acc[...] = jnp.zeros_like(acc)

    def add_rows(e, start, r0, r1):
        def group(r, width):
            dsts, vals = [], []
            for u in range(width):
                tok = idx_ref[e * cap + r + u] - tile * tt
                dst = pl.ds(pl.multiple_of(tok * ROW_CHUNKS, ROW_CHUNKS), ROW_CHUNKS)
                src = pl.ds(pl.multiple_of((r + u - start) * ROW_CHUNKS, ROW_CHUNKS), ROW_CHUNKS)
                dsts.append(dst)
                vals.append(acc[dst, :] + gate_ref[e * cap + r + u] * stage[slot, e, src, :])
            for dst, val in zip(dsts, vals):
                acc[dst, :] = val

        ngroups = (r1 - r0) // COMBINE_UNROLL

        def body_group(j, _):
            group(r0 + j * COMBINE_UNROLL, COMBINE_UNROLL)
            return 0
        lax.fori_loop(0, ngroups, body_group, 0)

        def body_one(r, _):
            group(r, 1)
            return 0
        lax.fori_loop(r0 + ngroups * COMBINE_UNROLL, r1, body_one, 0)

    for e in range(N_EXPERTS):
        lo = lo_ref[tile * N_EXPERTS + e]
        hi = lo_ref[(tile + 1) * N_EXPERTS + e]
        start = chunk_start(lo)
        chunk_copy(slot, e, start).wait()
        first_end = jnp.minimum(hi, start + ch)
        add_rows(e, start, lo, first_end)

        def more(r0):
            st = chunk_start(r0)
            cp = chunk_copy(slot, e, st)
            cp.start()
            cp.wait()
            r1 = jnp.minimum(hi, st + ch)
            add_rows(e, st, r0, r1)
            return r1
        lax.while_loop(lambda r0: r0 < hi, more, first_end)

    ssq = jnp.zeros((tt, 1), F32)
    for s in range(ROW_CHUNKS):
        cols = slice(s * LANES, (s + 1) * LANES)
        y = acc[pl.ds(s, tt, stride=ROW_CHUNKS), :]
        x2 = x1_ref[:, cols] + gt2_ref[0][:, cols] * y
        o_ref[:, cols] = x2
        ssq = ssq + jnp.sum(x2 * x2, axis=-1, keepdims=True)
    o_ref[...] = o_ref[...] * lax.rsqrt(ssq / D_MODEL + EPS) * gf_ref[...]


def _combine(idx_flat, gate_flat, lo_tab, ye_lin, x1, seq, gt2, g_final, cap, tt):
    n = x1.shape[0]
    ch = min(64, cap)
    ntiles = n // tt
    per_b = _tiles_per_mod_row(gt2, n, seq, tt)
    grid_spec = pltpu.PrefetchScalarGridSpec(
        num_scalar_prefetch=3,
        grid=(ntiles,),
        in_specs=[pl.BlockSpec(memory_space=pl.ANY),
                  pl.BlockSpec((tt, D_MODEL), lambda i, a, b, c: (i, 0)),
                  pl.BlockSpec((1, 1, D_MODEL), lambda i, a, b, c: (i // per_b, 0, 0)),
                  pl.BlockSpec((1, D_MODEL), lambda i, a, b, c: (0, 0))],
        out_specs=pl.BlockSpec((tt, D_MODEL), lambda i, a, b, c: (i, 0)),
        scratch_shapes=[pltpu.VMEM((2, N_EXPERTS, ch * ROW_CHUNKS, LANES), F32),
                        pltpu.VMEM((tt * ROW_CHUNKS, LANES), F32),
                        pltpu.SemaphoreType.DMA((2, N_EXPERTS))],
    )
    return pl.pallas_call(
        functools.partial(_combine_kernel, tt=tt, ch=ch, cap=cap, ntiles=ntiles),
        grid_spec=grid_spec,
        out_shape=jax.ShapeDtypeStruct((n, D_MODEL), F32),
        compiler_params=_cparams(("arbitrary",)),
        name="combine",
    )(idx_flat, gate_flat, lo_tab, ye_lin, x1, gt2, g_final)


def _rope_tables(seq):
    pos = jnp.arange(seq)
    row = (pos // GRID_W).astype(F32)
    col = (pos % GRID_W).astype(F32)
    npair = SWA_HEAD_DIM // 4
    inv_freq = ROPE_BASE ** (-jnp.arange(npair, dtype=F32) / npair)
    ar = row[:, None] * inv_freq[None, :]
    ac = col[:, None] * inv_freq[None, :]
    cos = jnp.concatenate([jnp.cos(ar), jnp.cos(ar), jnp.cos(ac), jnp.cos(ac)], axis=1)
    sin = jnp.concatenate([-jnp.sin(ar), jnp.sin(ar), -jnp.sin(ac), jnp.sin(ac)], axis=1)
    reps = SWA_KVW // SWA_HEAD_DIM
    return jnp.tile(cos, (1, reps)), jnp.tile(sin, (1, reps))


def _layer(x, mods, wts, latent, s0f, s0b, cache_k, cache_v):
    bsz, seq, _ = x.shape
    n = bsz * seq
    x2d = x.reshape(n, D_MODEL)
    sh1, sc1, gt1, sh2, sc2, gt2 = mods
    p = _in_proj(x2d, seq, sc1, sh1, wts["g1"], wts["w_in_r"])
    o_gla, st_f, st_b = _gla(p, bsz, seq, wts["wdf"], wts["bdf"], wts["wdb"], wts["bdb"], wts["g_gla"],
                             s0f, s0b)
    if latent:
        cos_t, sin_t = _rope_tables(seq)
        o_swa = _swa_latent(p, bsz, seq, cache_k, cache_v, wts["sink"], cos_t, sin_t)
    else:
        o_swa = _swa_context(p, bsz, seq, wts["sink"])
    merged = _branch_merge(o_gla, o_swa, p, wts["wa"], wts["wb"])
    x1, h2, logits3 = _out_proj(merged, x2d, seq, gt1, sc2, sh2, wts["g2"], wts["wo"], wts["wr2"], wts["wr1"])
    cap = CAPACITY_FACTOR * n // N_EXPERTS
    aff3, sel3, pos3 = _select(logits3, cap)
    base = pos3[:, :, 0].astype(jnp.int32)
    idx_c, gate_c = _compact(base.reshape(-1), aff3, sel3, pos3, cap)
    idx_flat = idx_c[:cap, :N_EXPERTS].T.astype(jnp.int32).reshape(-1)
    gate_flat = gate_c[:cap, :N_EXPERTS].T.reshape(-1)
    ye = _expert_ffn(idx_flat, h2, wts["wg"], wts["wu"], wts["wd"], cap)
    tt = 256
    lo_tab = jnp.concatenate([base[::tt // LANES], jnp.full((1, N_EXPERTS), cap, jnp.int32)], axis=0)
    y = _combine(idx_flat, gate_flat, lo_tab.reshape(-1), ye, x1, seq, gt2, wts["g_final"], cap, tt)
    return y.reshape(bsz, seq, D_MODEL), p, st_f, st_b


def kernel(x_prompt, x_sample, state_gla_fwd, state_gla_bwd, cache_k, cache_v, c, c_ctx, w_mod, b_mod,
           g_norm1, w_in, w_dec_f, b_dec_f, w_dec_b, b_dec_b, g_gla, attn_sink, w_branch_a, w_branch_b,
           w_out, g_norm2, w_router, w_exp_gate, w_exp_up, w_exp_down, g_final):
    bp, sp, _ = x_prompt.shape
    bl = x_sample.shape[0]
    l = 0
    gla_end = 2 * GLA_KW + 2 * GLA_VW
    dec_end = gla_end + 2 * GLA_RANK
    swa_end = dec_end + SWA_QW + 2 * SWA_KVW
    w = w_in[l]
    w_in_r = jnp.concatenate(
        [w[:, swa_end:], w[:, :gla_end], w[:, dec_end:swa_end], w[:, gla_end:dec_end],
         jnp.zeros((D_MODEL, IN_PAD - w.shape[1]), w.dtype)], axis=1).astype(BF16)
    zpad = jnp.zeros((LANES - 2 * GLA_RANK, GLA_KW), F32)
    wr_hi = w_router[l].astype(BF16)
    wr_lo = (w_router[l] - wr_hi.astype(F32)).astype(BF16)
    wr_pad = jnp.zeros((D_MODEL, LANES - 2 * N_EXPERTS), BF16)
    zr = jnp.zeros((GLA_RANK, GLA_KW), F32)
    wts = {
        "g1": g_norm1[l].reshape(1, D_MODEL),
        "w_in_r": w_in_r,
        "wdf": jnp.concatenate([w_dec_f[l], zr, zpad], axis=0),
        "wdb": jnp.concatenate([zr, w_dec_b[l], zpad], axis=0),
        "bdf": b_dec_f[l].reshape(1, GLA_KW),
        "bdb": b_dec_b[l].reshape(1, GLA_KW),
        "g_gla": g_gla[l].reshape(1, GLA_VW),
        "sink": attn_sink[l],
        "wa": w_branch_a[l].astype(BF16),
        "wb": w_branch_b[l].astype(BF16),
        "wo": w_out[l].astype(BF16),
        "g2": g_norm2[l].reshape(1, D_MODEL),
        "wr2": jnp.concatenate([wr_hi, wr_lo, wr_pad], axis=1),
        "wr1": jnp.concatenate([wr_hi, jnp.zeros_like(wr_lo), wr_pad], axis=1),
        "wg": w_exp_gate[l].astype(BF16),
        "wu": w_exp_up[l].astype(BF16),
        "wd": w_exp_down[l].astype(BF16),
        "g_final": g_final.reshape(1, D_MODEL),
    }
    cond8 = jnp.concatenate([c_ctx[None, :], c, jnp.zeros((8 - 1 - bl, D_MODEL), F32)], axis=0)
    mod = _modulation(cond8, w_mod[l], b_mod[l]).reshape(8, N_MOD, 1, D_MODEL)
    mods_ctx = tuple(mod[0:1, j] for j in range(N_MOD))
    mods_lat = tuple(mod[1:1 + bl, j] for j in range(N_MOD))

    zero_state = jnp.zeros((bp, GLA_HEADS, GLA_DK, GLA_DV), F32)
    y_prompt, p_ctx, st_f, st_b = _layer(x_prompt, mods_ctx, wts, False, zero_state, zero_state, None, None)
    ck = cache_k[:, l].reshape(bl, -1, SWA_KVW)
    cv = cache_v[:, l].reshape(bl, -1, SWA_KVW)
    y_sample, _, _, _ = _layer(x_sample, mods_lat, wts, True, state_gla_fwd[:, l], state_gla_bwd[:, l], ck, cv)

    new_k = p_ctx[:, COL_SK:COL_SK + SWA_KVW].reshape(bp, 1, sp, SWA_KV_HEADS, SWA_HEAD_DIM)
    new_v = p_ctx[:, COL_SV:COL_SV + SWA_KVW].reshape(bp, 1, sp, SWA_KV_HEADS, SWA_HEAD_DIM)
    return (y_prompt, y_sample, st_f[:, None], st_b[:, None], new_k, new_v)
```

```python
import functools

import jax
import jax.numpy as jnp
from jax import lax
from jax.experimental import pallas as pl
from jax.experimental.pallas import tpu as pltpu

F32 = jnp.float32
BF16 = jnp.bfloat16
HIGHEST = lax.Precision.HIGHEST

D_MODEL = 2048
N_MOD = 6
EPS = 1e-6

GLA_HEADS = 4
GLA_DK = 128
GLA_DV = 256
GLA_KW = GLA_HEADS * GLA_DK
GLA_VW = GLA_HEADS * GLA_DV
GLA_RANK = 16
GLA_TAU = 16.0
GLA_CHUNK = 64

SWA_HEADS = 16
SWA_KV_HEADS = 4
SWA_GROUP = 4
SWA_HEAD_DIM = 64
SWA_QW = SWA_HEADS * SWA_HEAD_DIM
SWA_KVW = SWA_KV_HEADS * SWA_HEAD_DIM
SWA_BLOCK = 128
GRID_W = 64
ROPE_BASE = 10000.0

N_EXPERTS = 16
EXPERT_FF = D_MODEL // 2
CAPACITY_FACTOR = 2

LANES = 128
ROW_CHUNKS = D_MODEL // LANES
VMEM_LIMIT = 56 * 1024 * 1024
FFN_VMEM_LIMIT = 61 * 1024 * 1024

COL_GA = 0
COL_GB = 2048
COL_GQ = 4096
COL_GK = 4608
COL_GV = 5120
COL_GR = 6144
COL_SQ = 7168
COL_SK = 8192
COL_SV = 8448
COL_DEC = 8704
IN_PAD = 8960
IN_TN = 1280
TAIL_COL0 = COL_SK
TAIL_W = COL_DEC + LANES - COL_SK


def _cparams(sem, **kw):
    return pltpu.CompilerParams(dimension_semantics=sem, vmem_limit_bytes=VMEM_LIMIT, **kw)


def _silu(x):
    return x * (1.0 / (1.0 + jnp.exp(-x)))


def _sigmoid(x):
    return 1.0 / (1.0 + jnp.exp(-x))


def _store_row_major(ref, x):
    rows = x.shape[0]
    for s in range(ROW_CHUNKS):
        ref[pl.ds(s, rows, stride=ROW_CHUNKS), :] = x[:, s * LANES:(s + 1) * LANES]


def _tiles_per_mod_row(mod, n, seq, tile):
    return seq // tile if mod.shape[0] > 1 else n // tile


def _mod_kernel(c_ref, w_ref, b_ref, o_ref):
    a = _silu(c_ref[...]).astype(BF16)
    o_ref[...] = jnp.dot(a, w_ref[...].astype(BF16), preferred_element_type=F32) + b_ref[...]


def _modulation(cond8, w_mod, b_mod):
    n_out = w_mod.shape[1]
    tn = 1536
    return pl.pallas_call(
        _mod_kernel,
        grid=(n_out // tn,),
        in_specs=[pl.BlockSpec((8, D_MODEL), lambda j: (0, 0)),
                  pl.BlockSpec((D_MODEL, tn), lambda j: (0, j)),
                  pl.BlockSpec((1, tn), lambda j: (0, j))],
        out_specs=pl.BlockSpec((8, tn), lambda j: (0, j)),
        out_shape=jax.ShapeDtypeStruct((8, n_out), F32),
        compiler_params=_cparams(("arbitrary",)),
        name="modulation",
    )(cond8, w_mod, b_mod.reshape(1, n_out))


def _in_proj_kernel(x_ref, sc_ref, sh_ref, g_ref, w_ref, o_ref, tail_ref, h_scr):
    j = pl.program_id(1)

    @pl.when(j == 0)
    def _():
        x = x_ref[...]
        y = x * lax.rsqrt(jnp.mean(x * x, axis=-1, keepdims=True) + EPS) * g_ref[...]
        h_scr[...] = (y * (1.0 + sc_ref[0]) + sh_ref[0]).astype(BF16)

    acc = jnp.dot(h_scr[...], w_ref[...], preferred_element_type=F32)
    o_ref[...] = acc.astype(BF16)

    @pl.when(j == IN_PAD // IN_TN - 1)
    def _():
        first = TAIL_COL0 - (IN_PAD - IN_TN)
        tail_ref[...] = acc[:, first:first + TAIL_W]


def _in_proj(x2d, seq, sc, sh, g, w_r):
    n = x2d.shape[0]
    tm = min(1024, seq if sc.shape[0] > 1 else n)
    per_b = _tiles_per_mod_row(sc, n, seq, tm)
    return pl.pallas_call(
        _in_proj_kernel,
        grid=(n // tm, IN_PAD // IN_TN),
        in_specs=[pl.BlockSpec((tm, D_MODEL), lambda i, j: (i, 0)),
                  pl.BlockSpec((1, 1, D_MODEL), lambda i, j: (i // per_b, 0, 0)),
                  pl.BlockSpec((1, 1, D_MODEL), lambda i, j: (i // per_b, 0, 0)),
                  pl.BlockSpec((1, D_MODEL), lambda i, j: (0, 0)),
                  pl.BlockSpec((D_MODEL, IN_TN), lambda i, j: (0, j))],
        out_specs=[pl.BlockSpec((tm, IN_TN), lambda i, j: (i, j)),
                   pl.BlockSpec((tm, TAIL_W), lambda i, j: (i, 0))],
        out_shape=[jax.ShapeDtypeStruct((n, IN_PAD), BF16),
                   jax.ShapeDtypeStruct((n, TAIL_W), F32)],
        scratch_shapes=[pltpu.VMEM((tm, D_MODEL), BF16)],
        compiler_params=_cparams(("parallel", "arbitrary")),
        name="in_proj",
    )(x2d, sc, sh, g, w_r)


GLA_BLK = 256
GLA_HPS = 4


def _tn_dot(a, b, precision=None):
    return lax.dot_general(a, b, (((0,), (0,)), ((), ())), precision=precision,
                           preferred_element_type=F32)


def _nt_dot(a, b, precision=None):
    return lax.dot_general(a, b, (((1,), (1,)), ((), ())), precision=precision,
                           preferred_element_type=F32)


def _split_bf16(x, parts):
    out = []
    for _ in range(parts):
        piece = x.astype(BF16)
        out.append(piece)
        x = x - piece.astype(F32)
    return out


def _dot3(x, w):
    xh, xl = _split_bf16(x, 2)
    wh, wl = _split_bf16(w, 2)
    d = functools.partial(jnp.dot, preferred_element_type=F32)
    return d(xh, wh) + d(xl, wh) + d(xh, wl)


def _gla_kernel(q_ref, k_ref, v_ref, r_ref, dec_ref, wdf_ref, bdf_ref, wdb_ref, bdb_ref, gg_ref,
                s0f_ref, s0b_ref, o_ref, stf_ref, stb_ref, state_t, of_scr, *, nblk):
    i = pl.program_id(2)
    c = GLA_CHUNK
    nch = GLA_BLK // c
    ri = lax.broadcasted_iota(jnp.int32, (GLA_BLK, GLA_BLK), 0)
    ci = lax.broadcasted_iota(jnp.int32, (GLA_BLK, GLA_BLK), 1)
    same_chunk = (ri // c) == (ci // c)

    @pl.when(i == 0)
    def _():
        for h in range(GLA_HPS):
            state_t[h] = s0f_ref[0, h].T

    @pl.when(i == nblk)
    def _():
        for h in range(GLA_HPS):
            state_t[h] = s0b_ref[0, h].T

    def run(fwd):
        blk = i if fwd else 2 * nblk - 1 - i
        keep = same_chunk & ((ci <= ri) if fwd else (ci >= ri))
        w_ref, b_ref = (wdf_ref, bdf_ref) if fwd else (wdb_ref, bdb_ref)
        logit = _dot3(dec_ref[...], w_ref[...]) + b_ref[...]
        g = (jnp.minimum(logit, 0.0) - jnp.log(1.0 + jnp.exp(-jnp.abs(logit)))) / GLA_TAU
        tri = keep.astype(BF16)
        b = sum(jnp.dot(tri, piece, preferred_element_type=F32) for piece in _split_bf16(g, 3))
        edge = c - 1 if fwd else 0
        tots = [b[ch * c + edge:ch * c + edge + 1, :] for ch in range(nch)]
        totb = jnp.concatenate([jnp.broadcast_to(t, (c, t.shape[1])) for t in tots], axis=0)
        q_in = (q_ref[...].astype(F32) * (GLA_DK ** -0.5) * jnp.exp(b)).astype(BF16)
        k = k_ref[...].astype(F32)
        k_in = (k * jnp.exp(-b)).astype(BF16)
        k_st = (k * jnp.exp(totb - b)).astype(BF16)
        srow = pl.multiple_of(blk * GLA_BLK, GLA_BLK)
        for h in range(GLA_HPS):
            ks = slice(h * GLA_DK, (h + 1) * GLA_DK)
            vs = slice(h * GLA_DV, (h + 1) * GLA_DV)
            v = v_ref[:, vs].astype(BF16)
            a = jnp.where(keep, _nt_dot(q_in[:, ks], k_in[:, ks]), 0.0).astype(BF16)
            o_intra = jnp.dot(a, v, preferred_element_type=F32)
            st = state_t[h]
            o_inter = [None] * nch
            for cc in range(nch):
                ch = cc if fwd else nch - 1 - cc
                rows = slice(ch * c, (ch + 1) * c)
                o_inter[ch] = _nt_dot(q_in[rows, ks], st.astype(BF16))
                st = st * jnp.exp(tots[ch][:, ks]) + _tn_dot(v[rows, :], k_st[rows, ks])
            state_t[h] = st
            o = o_intra + jnp.concatenate(o_inter, axis=0)
            if fwd:
                of_scr[pl.ds(srow, GLA_BLK), vs] = o
            else:
                o = o + of_scr[pl.ds(srow, GLA_BLK), vs]
                o = o * lax.rsqrt(jnp.mean(o * o, axis=-1, keepdims=True) + EPS) * gg_ref[:, vs]
                o_ref[:, vs] = (o * _silu(r_ref[:, vs].astype(F32))).astype(BF16)

    @pl.when(i < nblk)
    def _():
        run(True)

    @pl.when(i >= nblk)
    def _():
        run(False)

    @pl.when(i == nblk - 1)
    def _():
        for h in range(GLA_HPS):
            stf_ref[0, h] = state_t[h].T

    @pl.when(i == 2 * nblk - 1)
    def _():
        for h in range(GLA_HPS):
            stb_ref[0, h] = state_t[h].T


def _gla(p, dec, bsz, seq, wdf, bdf, wdb, bdb, g_gla, s0f, s0b):
    n = p.shape[0]
    nblk = seq // GLA_BLK
    kw = GLA_HPS * GLA_DK
    vw = GLA_HPS * GLA_DV

    def rb(b, i):
        return b * nblk + jnp.where(i < nblk, i, 2 * nblk - 1 - i)

    def orb(b, i):
        return b * nblk + jnp.where(i < nblk, nblk - 1, 2 * nblk - 1 - i)

    st_spec = pl.BlockSpec((1, GLA_HPS, GLA_DK, GLA_DV), lambda b, h, i: (b, h, 0, 0))
    st_shape = jax.ShapeDtypeStruct((bsz, GLA_HEADS, GLA_DK, GLA_DV), F32)
    return pl.pallas_call(
        functools.partial(_gla_kernel, nblk=nblk),
        grid=(bsz, GLA_HEADS // GLA_HPS, 2 * nblk),
        in_specs=[pl.BlockSpec((GLA_BLK, kw), lambda b, h, i: (rb(b, i), COL_GQ // kw + h)),
                  pl.BlockSpec((GLA_BLK, kw), lambda b, h, i: (rb(b, i), COL_GK // kw + h)),
                  pl.BlockSpec((GLA_BLK, vw), lambda b, h, i: (rb(b, i), COL_GV // vw + h)),
                  pl.BlockSpec((GLA_BLK, vw), lambda b, h, i: (rb(b, i), COL_GR // vw + h)),
                  pl.BlockSpec((GLA_BLK, LANES), lambda b, h, i: (rb(b, i), (COL_DEC - TAIL_COL0) // LANES)),
                  pl.BlockSpec((LANES, kw), lambda b, h, i: (0, h)),
                  pl.BlockSpec((1, kw), lambda b, h, i: (0, h)),
                  pl.BlockSpec((LANES, kw), lambda b, h, i: (0, h)),
                  pl.BlockSpec((1, kw), lambda b, h, i: (0, h)),
                  pl.BlockSpec((1, vw), lambda b, h, i: (0, h)),
                  st_spec, st_spec],
        out_specs=[pl.BlockSpec((GLA_BLK, vw), lambda b, h, i: (orb(b, i), h)), st_spec, st_spec],
        out_shape=[jax.ShapeDtypeStruct((n, GLA_VW), BF16), st_shape, st_shape],
        scratch_shapes=[pltpu.VMEM((GLA_HPS, GLA_DV, GLA_DK), F32), pltpu.VMEM((seq, vw), F32)],
        compiler_params=_cparams(("parallel", "parallel", "arbitrary")),
        name="gla",
    )(p, p, p, p, dec, wdf, bdf, wdb, bdb, g_gla, s0f, s0b)


def _sink_column(sink_ref, kvh, rows_per_head):
    r = lax.broadcasted_iota(jnp.int32, (SWA_GROUP * rows_per_head, 1), 0)
    col = jnp.zeros((SWA_GROUP * rows_per_head, 1), F32)
    for g in range(SWA_GROUP):
        col = jnp.where(r // rows_per_head == g, sink_ref[kvh * SWA_GROUP + g], col)
    return col


def _rope(x, cos, sin):
    outs = []
    for cb in range(x.shape[1] // LANES):
        sl = slice(cb * LANES, (cb + 1) * LANES)
        xc = x[:, sl]
        lane = lax.broadcasted_iota(jnp.int32, xc.shape, 1)
        sw = jnp.where(lane % 32 < 16, pltpu.roll(xc, LANES - 16, 1), pltpu.roll(xc, 16, 1))
        outs.append(xc * cos[:, sl] + sw * sin[:, sl])
    return jnp.concatenate(outs, axis=1)


def _swa_lat_kernel(sink_ref, q_ref, kp_ref, kc_ref, kn_ref, vp_ref, vc_ref, vn_ref, ck_ref, cv_ref,
                    cp_ref, cc_ref, cn_ref, sp_ref, sc_ref, sn_ref, o_ref, *, nb):
    n = pl.program_id(1)
    blk = SWA_BLOCK
    hd = SWA_HEAD_DIM
    cos_c, sin_c = cc_ref[...], sc_ref[...]
    kband = jnp.concatenate([_rope(kp_ref[...].astype(F32), cp_ref[...], sp_ref[...]),
                             _rope(kc_ref[...].astype(F32), cos_c, sin_c),
                             _rope(kn_ref[...].astype(F32), cn_ref[...], sn_ref[...])], axis=0).astype(BF16)
    vband = jnp.concatenate([vp_ref[...], vc_ref[...], vn_ref[...]], axis=0).astype(BF16)
    ck = ck_ref[0].astype(BF16)
    cv = cv_ref[0].astype(BF16)
    qi = lax.broadcasted_iota(jnp.int32, (SWA_GROUP * blk, 3 * blk), 0) % blk
    kj = lax.broadcasted_iota(jnp.int32, (SWA_GROUP * blk, 3 * blk), 1)
    k_abs = kj + (n - 1) * blk
    mask = (jnp.abs(kj - blk - qi) <= SWA_BLOCK) & (k_abs >= 0) & (k_abs < nb * blk)
    for kvh in range(SWA_KV_HEADS):
        ks = slice(kvh * hd, (kvh + 1) * hd)
        qr = _rope(q_ref[:, kvh * SWA_KVW:(kvh + 1) * SWA_KVW].astype(F32), cos_c, sin_c) * (hd ** -0.5)
        qg = jnp.concatenate([qr[:, g * hd:(g + 1) * hd] for g in range(SWA_GROUP)],
                             axis=0).astype(BF16)
        s_b = jnp.where(mask, _nt_dot(qg, kband[:, ks]), -1e30)
        s_c = _nt_dot(qg, ck[:, ks])
        sink = _sink_column(sink_ref, kvh, blk)
        m = jnp.maximum(jnp.maximum(jnp.max(s_b, axis=-1, keepdims=True),
                                    jnp.max(s_c, axis=-1, keepdims=True)), sink)
        p_b = jnp.exp(s_b - m)
        p_c = jnp.exp(s_c - m)
        den = jnp.exp(sink - m) + jnp.sum(p_b, axis=-1, keepdims=True) + jnp.sum(p_c, axis=-1, keepdims=True)
        o = (jnp.dot(p_b.astype(BF16), vband[:, ks], preferred_element_type=F32)
             + jnp.dot(p_c.astype(BF16), cv[:, ks], preferred_element_type=F32)) / den
        for g in range(SWA_GROUP):
            cb = (kvh * SWA_GROUP + g) * hd
            o_ref[:, cb:cb + hd] = o[g * blk:(g + 1) * blk, :].astype(BF16)


def _swa_latent(p, bsz, seq, cache_k, cache_v, sink, cos_t, sin_t):
    n = p.shape[0]
    nb = seq // SWA_BLOCK
    kcol = COL_SK // SWA_KVW
    vcol = COL_SV // SWA_KVW
    prev = lambda i: jnp.maximum(i - 1, 0)
    nxt = lambda i: jnp.minimum(i + 1, nb - 1)

    def pspec(col, f):
        return pl.BlockSpec((SWA_BLOCK, SWA_KVW), lambda b, i, s: (b * nb + f(i), col))

    def tspec(f):
        return pl.BlockSpec((SWA_BLOCK, SWA_KVW), lambda b, i, s: (f(i), 0))

    same = lambda i: i
    cspec = pl.BlockSpec((1, cache_k.shape[1], SWA_KVW), lambda b, i, s: (b, 0, 0))
    grid_spec = pltpu.PrefetchScalarGridSpec(
        num_scalar_prefetch=1,
        grid=(bsz, nb),
        in_specs=[pl.BlockSpec((SWA_BLOCK, SWA_QW), lambda b, i, s: (b * nb + i, COL_SQ // SWA_QW)),
                  pspec(kcol, prev), pspec(kcol, same), pspec(kcol, nxt),
                  pspec(vcol, prev), pspec(vcol, same), pspec(vcol, nxt),
                  cspec, cspec,
                  tspec(prev), tspec(same), tspec(nxt), tspec(prev), tspec(same), tspec(nxt)],
        out_specs=pl.BlockSpec((SWA_BLOCK, SWA_QW), lambda b, i, s: (b * nb + i, 0)),
    )
    return pl.pallas_call(
        functools.partial(_swa_lat_kernel, nb=nb),
        grid_spec=grid_spec,
        out_shape=jax.ShapeDtypeStruct((n, SWA_QW), BF16),
        compiler_params=_cparams(("parallel", "arbitrary")),
        name="swa_latent",
    )(sink, p, p, p, p, p, p, p, cache_k, cache_v, cos_t, cos_t, cos_t, sin_t, sin_t, sin_t)


def _swa_ctx_kernel(sink_ref, q_ref, k_ref, v_ref, o_ref):
    s = q_ref.shape[0]
    hd = SWA_HEAD_DIM
    kb = k_ref[...].astype(BF16)
    vb = v_ref[...].astype(BF16)
    for kvh in range(SWA_KV_HEADS):
        ks = slice(kvh * hd, (kvh + 1) * hd)
        qg = jnp.concatenate([q_ref[:, (kvh * SWA_GROUP + g) * hd:(kvh * SWA_GROUP + g + 1) * hd]
                              for g in range(SWA_GROUP)], axis=0)
        qg = (qg.astype(F32) * (hd ** -0.5)).astype(BF16)
        sc = _nt_dot(qg, kb[:, ks])
        sink = _sink_column(sink_ref, kvh, s)
        m = jnp.maximum(jnp.max(sc, axis=-1, keepdims=True), sink)
        pr = jnp.exp(sc - m)
        den = jnp.exp(sink - m) + jnp.sum(pr, axis=-1, keepdims=True)
        o = jnp.dot(pr.astype(BF16), vb[:, ks], preferred_element_type=F32) / den
        for g in range(SWA_GROUP):
            cb = (kvh * SWA_GROUP + g) * hd
            o_ref[:, cb:cb + hd] = o[g * s:(g + 1) * s, :].astype(BF16)


def _swa_context(p, bsz, seq, sink):
    n = p.shape[0]
    grid_spec = pltpu.PrefetchScalarGridSpec(
        num_scalar_prefetch=1,
        grid=(bsz,),
        in_specs=[pl.BlockSpec((seq, SWA_QW), lambda b, s: (b, COL_SQ // SWA_QW)),
                  pl.BlockSpec((seq, SWA_KVW), lambda b, s: (b, COL_SK // SWA_KVW)),
                  pl.BlockSpec((seq, SWA_KVW), lambda b, s: (b, COL_SV // SWA_KVW))],
        out_specs=pl.BlockSpec((seq, SWA_QW), lambda b, s: (b, 0)),
    )
    return pl.pallas_call(
        _swa_ctx_kernel,
        grid_spec=grid_spec,
        out_shape=jax.ShapeDtypeStruct((n, SWA_QW), BF16),
        compiler_params=_cparams(("parallel",)),
        name="swa_context",
    )(sink, p, p, p)


MERGE_TM = 512


def _branch_kernel(og_ref, os_ref, ga_ref, gb_ref, wa_ref, wb_ref, m_ref):
    a = jnp.dot(og_ref[...], wa_ref[...], preferred_element_type=F32)
    b = jnp.dot(os_ref[...], wb_ref[...], preferred_element_type=F32)
    m_ref[...] = (_sigmoid(ga_ref[...].astype(F32)) * a + _sigmoid(gb_ref[...].astype(F32)) * b).astype(BF16)


def _branch_merge(o_gla, o_swa, p, wa, wb):
    n = o_gla.shape[0]
    tm = MERGE_TM
    once = pl.Buffered(1)
    return pl.pallas_call(
        _branch_kernel,
        grid=(n // tm,),
        in_specs=[pl.BlockSpec((tm, GLA_VW), lambda i: (i, 0)),
                  pl.BlockSpec((tm, SWA_QW), lambda i: (i, 0)),
                  pl.BlockSpec((tm, D_MODEL), lambda i: (i, COL_GA // D_MODEL)),
                  pl.BlockSpec((tm, D_MODEL), lambda i: (i, COL_GB // D_MODEL)),
                  pl.BlockSpec((GLA_VW, D_MODEL), lambda i: (0, 0), pipeline_mode=once),
                  pl.BlockSpec((SWA_QW, D_MODEL), lambda i: (0, 0), pipeline_mode=once)],
        out_specs=pl.BlockSpec((tm, D_MODEL), lambda i: (i, 0)),
        out_shape=jax.ShapeDtypeStruct((n, D_MODEL), BF16),
        compiler_params=_cparams(("parallel",)),
        name="branch_merge",
    )(o_gla, o_swa, p, p, wa, wb)


def _out_kernel(m_ref, x_ref, gt1_ref, sc2_ref, sh2_ref, g2_ref, wo_ref, wr2_ref, wr1_ref,
                x1_ref, h2_ref, lg_ref):
    x1 = x_ref[...] + gt1_ref[0] * jnp.dot(m_ref[...], wo_ref[...], preferred_element_type=F32)
    x1_ref[...] = x1
    y = x1 * lax.rsqrt(jnp.mean(x1 * x1, axis=-1, keepdims=True) + EPS) * g2_ref[...]
    h2 = y * (1.0 + sc2_ref[0]) + sh2_ref[0]
    _store_row_major(h2_ref, h2)
    h_hi, h_lo = _split_bf16(h2, 2)
    l2 = (jnp.dot(h_hi, wr2_ref[...], preferred_element_type=F32)
          + jnp.dot(h_lo, wr1_ref[...], preferred_element_type=F32))
    lt = l2.T
    lt = lt[0:N_EXPERTS, :] + lt[N_EXPERTS:2 * N_EXPERTS, :]
    for cb in range(lt.shape[1] // LANES):
        lg_ref[cb] = lt[:, cb * LANES:(cb + 1) * LANES]


def _out_proj(merged, x2d, seq, gt1, sc2, sh2, g2, wo, wr2, wr1):
    n = x2d.shape[0]
    tm = MERGE_TM
    per_b = _tiles_per_mod_row(gt1, n, seq, tm)
    mod_spec = pl.BlockSpec((1, 1, D_MODEL), lambda i: (i // per_b, 0, 0))
    once = pl.Buffered(1)
    return pl.pallas_call(
        _out_kernel,
        grid=(n // tm,),
        in_specs=[pl.BlockSpec((tm, D_MODEL), lambda i: (i, 0)),
                  pl.BlockSpec((tm, D_MODEL), lambda i: (i, 0)),
                  mod_spec, mod_spec, mod_spec,
                  pl.BlockSpec((1, D_MODEL), lambda i: (0, 0)),
                  pl.BlockSpec((D_MODEL, D_MODEL), lambda i: (0, 0), pipeline_mode=once),
                  pl.BlockSpec((D_MODEL, LANES), lambda i: (0, 0)),
                  pl.BlockSpec((D_MODEL, LANES), lambda i: (0, 0))],
        out_specs=[pl.BlockSpec((tm, D_MODEL), lambda i: (i, 0)),
                   pl.BlockSpec((tm * ROW_CHUNKS, LANES), lambda i: (i, 0)),
                   pl.BlockSpec((tm // LANES, N_EXPERTS, LANES), lambda i: (i, 0, 0))],
        out_shape=[jax.ShapeDtypeStruct((n, D_MODEL), F32),
                   jax.ShapeDtypeStruct((n * ROW_CHUNKS, LANES), F32),
                   jax.ShapeDtypeStruct((n // LANES, N_EXPERTS, LANES), F32)],
        compiler_params=_cparams(("parallel",)),
        name="out_proj",
    )(merged, x2d, gt1, sc2, sh2, g2, wo, wr2, wr1)


def _select_kernel(lg_ref, aff_ref, sel_ref, pos_ref, *, cap):
    nblk = lg_ref.shape[0]
    lg = lg_ref[...]
    ex = jnp.exp(lg - jnp.max(lg, axis=1, keepdims=True))
    aff = ex / jnp.sum(ex, axis=1, keepdims=True)
    aff_ref[...] = aff
    bits = lax.bitcast_convert_type(aff, jnp.int32)

    def count(pred):
        c = jnp.sum(jnp.where(pred, 1.0, 0.0), axis=0, keepdims=True)
        return jnp.sum(c, axis=2, keepdims=True)

    def bit_step(t, cur):
        cand = cur | jnp.left_shift(jnp.int32(1), 30 - t)
        return jnp.where(count(bits >= cand) >= cap, cand, cur)

    thr = lax.fori_loop(0, 31, bit_step, jnp.zeros((1, N_EXPERTS, 1), jnp.int32))
    need = (cap - count(bits > thr))[0]
    thr2 = thr[0]
    ri = lax.broadcasted_iota(jnp.int32, (LANES, LANES), 0)
    ci = lax.broadcasted_iota(jnp.int32, (LANES, LANES), 1)
    upper = (ri <= ci).astype(BF16)

    def blk_step(b, carry):
        run_eq, run_sel = carry
        bb = lax.bitcast_convert_type(aff_ref[b], jnp.int32)
        eq = (bb == thr2).astype(F32)
        eq_excl = jnp.dot(eq.astype(BF16), upper, preferred_element_type=F32) - eq + run_eq
        sel = jnp.where((bb > thr2) | ((eq > 0) & (eq_excl < need)), 1.0, 0.0)
        sel_ref[b] = sel
        pos_ref[b] = jnp.dot(sel.astype(BF16), upper, preferred_element_type=F32) - sel + run_sel
        return (run_eq + jnp.sum(eq, axis=1, keepdims=True),
                run_sel + jnp.sum(sel, axis=1, keepdims=True))

    zero = jnp.zeros((N_EXPERTS, 1), F32)
    lax.fori_loop(0, nblk, blk_step, (zero, zero))


def _select(logits3, cap):
    shp = jax.ShapeDtypeStruct(logits3.shape, F32)
    return pl.pallas_call(
        functools.partial(_select_kernel, cap=float(cap)),
        out_shape=[shp, shp, shp],
        compiler_params=pltpu.CompilerParams(vmem_limit_bytes=VMEM_LIMIT),
        name="select",
    )(logits3)


COMPACT_WIN = 136


def _compact_kernel(base_ref, aff_ref, sel_ref, pos_ref, idx_ref, gate_ref):
    nblk = aff_ref.shape[0]
    idx_ref[...] = jnp.zeros_like(idx_ref)
    gate_ref[...] = jnp.zeros_like(gate_ref)
    slot = lax.broadcasted_iota(jnp.int32, (COMPACT_WIN, LANES), 0).astype(F32)
    lane = lax.broadcasted_iota(jnp.int32, (COMPACT_WIN, LANES), 1)

    def blk_step(b, _):
        aff = aff_ref[b]
        sel = sel_ref[b]
        pos = pos_ref[b]
        tok = (lane + b * LANES).astype(F32)
        for e in range(N_EXPERTS):
            base8 = pl.multiple_of((base_ref[b * N_EXPERTS + e] >> 3) << 3, 8)
            rel = pos[e:e + 1, :] - base8.astype(F32)
            hit = (rel == slot) & (sel[e:e + 1, :] > 0)
            rows = pl.ds(base8, COMPACT_WIN)
            mine = lane == e
            ic = jnp.sum(jnp.where(hit, tok, 0.0), axis=1, keepdims=True)
            gc = jnp.sum(jnp.where(hit, aff[e:e + 1, :], 0.0), axis=1, keepdims=True)
            idx_ref[rows, :] += jnp.where(mine, ic, 0.0)
            gate_ref[rows, :] += jnp.where(mine, gc, 0.0)
        return 0

    lax.fori_loop(0, nblk, blk_step, 0)


def _compact(base, aff3, sel3, pos3, cap):
    cp = cap + 2 * LANES
    full = pl.BlockSpec(aff3.shape, lambda i, s: (0, 0, 0))
    ospec = pl.BlockSpec((cp, LANES), lambda i, s: (0, 0))
    oshape = jax.ShapeDtypeStruct((cp, LANES), F32)
    grid_spec = pltpu.PrefetchScalarGridSpec(
        num_scalar_prefetch=1, grid=(1,), in_specs=[full, full, full], out_specs=[ospec, ospec])
    return pl.pallas_call(
        _compact_kernel,
        grid_spec=grid_spec,
        out_shape=[oshape, oshape],
        compiler_params=_cparams(("arbitrary",)),
        name="compact",
    )(base, aff3, sel3, pos3)


FFN_COLS = 256


def _ffn_kernel(idx_ref, h2_hbm, wg_ref, wu_ref, wd_ref, o_ref, gbuf, xs, hid, sem, *, tm, total):
    step = pl.program_id(0) * pl.num_programs(1) + pl.program_id(1)

    def row_copy(step_, i):
        tok = idx_ref[step_ * tm + i]
        return pltpu.make_async_copy(
            h2_hbm.at[pl.ds(pl.multiple_of(tok * ROW_CHUNKS, ROW_CHUNKS), ROW_CHUNKS), :],
            gbuf.at[pl.ds(pl.multiple_of(i * ROW_CHUNKS, ROW_CHUNKS), ROW_CHUNKS), :],
            sem)

    def wait_rows():
        pltpu.make_async_copy(h2_hbm.at[pl.ds(0, tm * ROW_CHUNKS), :], gbuf, sem).wait()

    @pl.when(step == 0)
    def _():
        def body(i, _):
            row_copy(step, i).start()
            return 0
        lax.fori_loop(0, tm, body, 0)

    wait_rows()
    for s in range(ROW_CHUNKS):
        xs[:, s * LANES:(s + 1) * LANES] = gbuf[pl.ds(s, tm, stride=ROW_CHUNKS), :].astype(BF16)

    nxt = jnp.minimum(step + 1, total - 1)
    for i in range(tm):
        row_copy(nxt, i).start()

    for f in range(EXPERT_FF // FFN_COLS):
        cols = slice(f * FFN_COLS, (f + 1) * FFN_COLS)
        hg = jnp.dot(xs[...], wg_ref[0, :, cols], preferred_element_type=F32)
        hu = jnp.dot(xs[...], wu_ref[0, :, cols], preferred_element_type=F32)
        hid[:, cols] = (_silu(hg) * hu).astype(BF16)
    for c in range(D_MODEL // FFN_COLS):
        y = jnp.dot(hid[...], wd_ref[0, :, c * FFN_COLS:(c + 1) * FFN_COLS], preferred_element_type=F32)
        for s in range(FFN_COLS // LANES):
            o_ref[pl.ds(c * (FFN_COLS // LANES) + s, tm, stride=ROW_CHUNKS), :] = y[:, s * LANES:(s + 1) * LANES]

    @pl.when(step == total - 1)
    def _():
        wait_rows()


def _expert_ffn(idx_flat, h2_lin, wg, wu, wd, cap):
    tm = min(1024, cap)
    nr = cap // tm
    grid_spec = pltpu.PrefetchScalarGridSpec(
        num_scalar_prefetch=1,
        grid=(N_EXPERTS, nr),
        in_specs=[pl.BlockSpec(memory_space=pl.ANY),
                  pl.BlockSpec((1, D_MODEL, EXPERT_FF), lambda e, r, s: (e, 0, 0)),
                  pl.BlockSpec((1, D_MODEL, EXPERT_FF), lambda e, r, s: (e, 0, 0)),
                  pl.BlockSpec((1, EXPERT_FF, D_MODEL), lambda e, r, s: (e, 0, 0))],
        out_specs=pl.BlockSpec((tm * ROW_CHUNKS, LANES), lambda e, r, s: (e * nr + r, 0)),
        scratch_shapes=[pltpu.VMEM((tm * ROW_CHUNKS, LANES), F32),
                        pltpu.VMEM((tm, D_MODEL), BF16),
                        pltpu.VMEM((tm, EXPERT_FF), BF16),
                        pltpu.SemaphoreType.DMA(())],
    )
    return pl.pallas_call(
        functools.partial(_ffn_kernel, tm=tm, total=N_EXPERTS * nr),
        grid_spec=grid_spec,
        out_shape=jax.ShapeDtypeStruct((N_EXPERTS * cap * ROW_CHUNKS, LANES), F32),
        compiler_params=pltpu.CompilerParams(dimension_semantics=("arbitrary", "arbitrary"),
                                             vmem_limit_bytes=FFN_VMEM_LIMIT),
        name="expert_ffn",
    )(idx_flat, h2_lin, wg, wu, wd)


COMBINE_UNROLL = 4
COMBINE_CHUNK = 48


def _combine_kernel(idx_ref, gate_ref, lo_ref, ye_hbm, x1_ref, gt2_ref, gf_ref, o_ref, stage, acc, sem,
                    *, tt, ch, cap, ntiles):
    tile = pl.program_id(0)
    slot = tile % 2

    def chunk_start(first_row):
        return jnp.minimum(first_row, cap - ch)

    def chunk_copy(slot_, e, start):
        src = pl.multiple_of((e * cap + start) * ROW_CHUNKS, ROW_CHUNKS)
        return pltpu.make_async_copy(ye_hbm.at[pl.ds(src, ch * ROW_CHUNKS), :], stage.at[slot_, e],
                                     sem.at[slot_, e])

    def issue(tile_, slot_):
        for e in range(N_EXPERTS):
            chunk_copy(slot_, e, chunk_start(lo_ref[tile_ * N_EXPERTS + e])).start()

    @pl.when(tile == 0)
    def _():
        issue(tile, slot)

    @pl.when(tile + 1 < ntiles)
    def _():
        issue(tile + 1, 1 - slot)

    acc[...] = jnp.zeros_like(acc)

    def add_rows(e, start, r0, r1):
        def group(r, width):
            dsts, vals = [], []
            for u in range(width):
                tok = idx_ref[e * cap + r + u] - tile * tt
                dst = pl.ds(pl.multiple_of(tok * ROW_CHUNKS, ROW_CHUNKS), ROW_CHUNKS)
                src = pl.ds(pl.multiple_of((r + u - start) * ROW_CHUNKS, ROW_CHUNKS), ROW_CHUNKS)
                dsts.append(dst)
                vals.append(acc[dst, :] + gate_ref[e * cap + r + u] * stage[slot, e, src, :])
            for dst, val in zip(dsts, vals):
                acc[dst, :] = val

        ngroups = (r1 - r0) // COMBINE_UNROLL

        def body_group(j, _):
            group(r0 + j * COMBINE_UNROLL, COMBINE_UNROLL)
            return 0
        lax.fori_loop(0, ngroups, body_group, 0)

        def body_one(r, _):
            group(r, 1)
            return 0
        lax.fori_loop(r0 + ngroups * COMBINE_UNROLL, r1, body_one, 0)

    for e in range(N_EXPERTS):
        lo = lo_ref[tile * N_EXPERTS + e]
        hi = lo_ref[(tile + 1) * N_EXPERTS + e]
        start = chunk_start(lo)
        chunk_copy(slot, e, start).wait()
        first_end = jnp.minimum(hi, start + ch)
        add_rows(e, start, lo, first_end)

        def more(r0):
            st = chunk_start(r0)
            cp = chunk_copy(slot, e, st)
            cp.start()
            cp.wait()
            r1 = jnp.minimum(hi, st + ch)
            add_rows(e, st, r0, r1)
            return r1
        lax.while_loop(lambda r0: r0 < hi, more, first_end)

    ssq = jnp.zeros((tt, 1), F32)
    for s in range(ROW_CHUNKS):
        cols = slice(s * LANES, (s + 1) * LANES)
        y = acc[pl.ds(s, tt, stride=ROW_CHUNKS), :]
        x2 = x1_ref[:, cols] + gt2_ref[0][:, cols] * y
        o_ref[:, cols] = x2
        ssq = ssq + jnp.sum(x2 * x2, axis=-1, keepdims=True)
    o_ref[...] = o_ref[...] * lax.rsqrt(ssq / D_MODEL + EPS) * gf_ref[...]


def _combine(idx_flat, gate_flat, lo_tab, ye_lin, x1, seq, gt2, g_final, cap, tt):
    n = x1.shape[0]
    ch = min(COMBINE_CHUNK, cap)
    ntiles = n // tt
    per_b = _tiles_per_mod_row(gt2, n, seq, tt)
    grid_spec = pltpu.PrefetchScalarGridSpec(
        num_scalar_prefetch=3,
        grid=(ntiles,),
        in_specs=[pl.BlockSpec(memory_space=pl.ANY),
                  pl.BlockSpec((tt, D_MODEL), lambda i, a, b, c: (i, 0)),
                  pl.BlockSpec((1, 1, D_MODEL), lambda i, a, b, c: (i // per_b, 0, 0)),
                  pl.BlockSpec((1, D_MODEL), lambda i, a, b, c: (0, 0))],
        out_specs=pl.BlockSpec((tt, D_MODEL), lambda i, a, b, c: (i, 0)),
        scratch_shapes=[pltpu.VMEM((2, N_EXPERTS, ch * ROW_CHUNKS, LANES), F32),
                        pltpu.VMEM((tt * ROW_CHUNKS, LANES), F32),
                        pltpu.SemaphoreType.DMA((2, N_EXPERTS))],
    )
    return pl.pallas_call(
        functools.partial(_combine_kernel, tt=tt, ch=ch, cap=cap, ntiles=ntiles),
        grid_spec=grid_spec,
        out_shape=jax.ShapeDtypeStruct((n, D_MODEL), F32),
        compiler_params=_cparams(("arbitrary",)),
        name="combine",
    )(idx_flat, gate_flat, lo_tab, ye_lin, x1, gt2, g_final)


def _rope_tables(seq):
    pos = jnp.arange(seq)
    row = (pos // GRID_W).astype(F32)
    col = (pos % GRID_W).astype(F32)
    npair = SWA_HEAD_DIM // 4
    inv_freq = ROPE_BASE ** (-jnp.arange(npair, dtype=F32) / npair)
    ar = row[:, None] * inv_freq[None, :]
    ac = col[:, None] * inv_freq[None, :]
    cos = jnp.concatenate([jnp.cos(ar), jnp.cos(ar), jnp.cos(ac), jnp.cos(ac)], axis=1)
    sin = jnp.concatenate([-jnp.sin(ar), jnp.sin(ar), -jnp.sin(ac), jnp.sin(ac)], axis=1)
    reps = SWA_KVW // SWA_HEAD_DIM
    return jnp.tile(cos, (1, reps)), jnp.tile(sin, (1, reps))


def _layer(x, mods, wts, latent, s0f, s0b, cache_k, cache_v):
    bsz, seq, _ = x.shape
    n = bsz * seq
    x2d = x.reshape(n, D_MODEL)
    sh1, sc1, gt1, sh2, sc2, gt2 = mods
    p, tail = _in_proj(x2d, seq, sc1, sh1, wts["g1"], wts["w_in_r"])
    o_gla, st_f, st_b = _gla(p, tail, bsz, seq, wts["wdf"], wts["bdf"], wts["wdb"], wts["bdb"], wts["g_gla"],
                             s0f, s0b)
    if latent:
        cos_t, sin_t = _rope_tables(seq)
        o_swa = _swa_latent(p, bsz, seq, cache_k, cache_v, wts["sink"], cos_t, sin_t)
    else:
        o_swa = _swa_context(p, bsz, seq, wts["sink"])
    merged = _branch_merge(o_gla, o_swa, p, wts["wa"], wts["wb"])
    x1, h2, logits3 = _out_proj(merged, x2d, seq, gt1, sc2, sh2, wts["g2"], wts["wo"], wts["wr2"], wts["wr1"])
    cap = CAPACITY_FACTOR * n // N_EXPERTS
    aff3, sel3, pos3 = _select(logits3, cap)
    base = pos3[:, :, 0].astype(jnp.int32)
    idx_c, gate_c = _compact(base.reshape(-1), aff3, sel3, pos3, cap)
    idx_flat = idx_c[:cap, :N_EXPERTS].T.astype(jnp.int32).reshape(-1)
    gate_flat = gate_c[:cap, :N_EXPERTS].T.reshape(-1)
    ye = _expert_ffn(idx_flat, h2, wts["wg"], wts["wu"], wts["wd"], cap)
    tt = 256
    lo_tab = jnp.concatenate([base[::tt // LANES], jnp.full((1, N_EXPERTS), cap, jnp.int32)], axis=0)
    y = _combine(idx_flat, gate_flat, lo_tab.reshape(-1), ye, x1, seq, gt2, wts["g_final"], cap, tt)
    return y.reshape(bsz, seq, D_MODEL), tail, st_f, st_b


def kernel(x_prompt, x_sample, state_gla_fwd, state_gla_bwd, cache_k, cache_v, c, c_ctx, w_mod, b_mod,
           g_norm1, w_in, w_dec_f, b_dec_f, w_dec_b, b_dec_b, g_gla, attn_sink, w_branch_a, w_branch_b,
           w_out, g_norm2, w_router, w_exp_gate, w_exp_up, w_exp_down, g_final):
    bp, sp, _ = x_prompt.shape
    bl = x_sample.shape[0]
    l = 0
    gla_end = 2 * GLA_KW + 2 * GLA_VW
    dec_end = gla_end + 2 * GLA_RANK
    swa_end = dec_end + SWA_QW + 2 * SWA_KVW
    w = w_in[l]
    w_in_r = jnp.concatenate(
        [w[:, swa_end:], w[:, :gla_end], w[:, dec_end:swa_end], w[:, gla_end:dec_end],
         jnp.zeros((D_MODEL, IN_PAD - w.shape[1]), w.dtype)], axis=1).astype(BF16)
    zpad = jnp.zeros((LANES - 2 * GLA_RANK, GLA_KW), F32)
    wr_hi = w_router[l].astype(BF16)
    wr_lo = (w_router[l] - wr_hi.astype(F32)).astype(BF16)
    wr_pad = jnp.zeros((D_MODEL, LANES - 2 * N_EXPERTS), BF16)
    zr = jnp.zeros((GLA_RANK, GLA_KW), F32)
    wts = {
        "g1": g_norm1[l].reshape(1, D_MODEL),
        "w_in_r": w_in_r,
        "wdf": jnp.concatenate([w_dec_f[l], zr, zpad], axis=0),
        "wdb": jnp.concatenate([zr, w_dec_b[l], zpad], axis=0),
        "bdf": b_dec_f[l].reshape(1, GLA_KW),
        "bdb": b_dec_b[l].reshape(1, GLA_KW),
        "g_gla": g_gla[l].reshape(1, GLA_VW),
        "sink": attn_sink[l],
        "wa": w_branch_a[l].astype(BF16),
        "wb": w_branch_b[l].astype(BF16),
        "wo": w_out[l].astype(BF16),
        "g2": g_norm2[l].reshape(1, D_MODEL),
        "wr2": jnp.concatenate([wr_hi, wr_lo, wr_pad], axis=1),
        "wr1": jnp.concatenate([wr_hi, jnp.zeros_like(wr_lo), wr_pad], axis=1),
        "wg": w_exp_gate[l].astype(BF16),
        "wu": w_exp_up[l].astype(BF16),
        "wd": w_exp_down[l].astype(BF16),
        "g_final": g_final.reshape(1, D_MODEL),
    }
    cond8 = jnp.concatenate([c_ctx[None, :], c, jnp.zeros((8 - 1 - bl, D_MODEL), F32)], axis=0)
    mod = _modulation(cond8, w_mod[l], b_mod[l]).reshape(8, N_MOD, 1, D_MODEL)
    mods_ctx = tuple(mod[0:1, j] for j in range(N_MOD))
    mods_lat = tuple(mod[1:1 + bl, j] for j in range(N_MOD))

    zero_state = jnp.zeros((bp, GLA_HEADS, GLA_DK, GLA_DV), F32)
    y_prompt, tail_ctx, st_f, st_b = _layer(x_prompt, mods_ctx, wts, False, zero_state, zero_state, None, None)
    ck = cache_k[:, l].reshape(bl, -1, SWA_KVW)
    cv = cache_v[:, l].reshape(bl, -1, SWA_KVW)
    y_sample, _, _, _ = _layer(x_sample, mods_lat, wts, True, state_gla_fwd[:, l], state_gla_bwd[:, l], ck, cv)

    ksl = slice(COL_SK - TAIL_COL0, COL_SK - TAIL_COL0 + SWA_KVW)
    vsl = slice(COL_SV - TAIL_COL0, COL_SV - TAIL_COL0 + SWA_KVW)
    new_k = tail_ctx[:, ksl].reshape(bp, 1, sp, SWA_KV_HEADS, SWA_HEAD_DIM)
    new_v = tail_ctx[:, vsl].reshape(bp, 1, sp, SWA_KV_HEADS, SWA_HEAD_DIM)
    return (y_prompt, y_sample, st_f[:, None], st_b[:, None], new_k, new_v)
```

```python
import functools

import jax
import jax.numpy as jnp
from jax import lax
from jax.experimental import pallas as pl
from jax.experimental.pallas import tpu as pltpu

F32 = jnp.float32
BF16 = jnp.bfloat16
HIGHEST = lax.Precision.HIGHEST

D_MODEL = 2048
N_MOD = 6
EPS = 1e-6

GLA_HEADS = 4
GLA_DK = 128
GLA_DV = 256
GLA_KW = GLA_HEADS * GLA_DK
GLA_VW = GLA_HEADS * GLA_DV
GLA_RANK = 16
GLA_TAU = 16.0
GLA_CHUNK = 64

SWA_HEADS = 16
SWA_KV_HEADS = 4
SWA_GROUP = 4
SWA_HEAD_DIM = 64
SWA_QW = SWA_HEADS * SWA_HEAD_DIM
SWA_KVW = SWA_KV_HEADS * SWA_HEAD_DIM
SWA_BLOCK = 128
GRID_W = 64
ROPE_BASE = 10000.0

N_EXPERTS = 16
EXPERT_FF = D_MODEL // 2
CAPACITY_FACTOR = 2

LANES = 128
ROW_CHUNKS = D_MODEL // LANES
VMEM_LIMIT = 56 * 1024 * 1024
FFN_VMEM_LIMIT = 61 * 1024 * 1024

COL_GA = 0
COL_GB = 2048
COL_GQ = 4096
COL_GK = 4608
COL_GV = 5120
COL_GR = 6144
COL_SQ = 7168
COL_SK = 8192
COL_SV = 8448
COL_DEC = 8704
IN_PAD = 8960
IN_TN = 1280
TAIL_COL0 = COL_SK
TAIL_W = COL_DEC + LANES - COL_SK


def _cparams(sem, **kw):
    return pltpu.CompilerParams(dimension_semantics=sem, vmem_limit_bytes=VMEM_LIMIT, **kw)


def _silu(x):
    return x * (1.0 / (1.0 + jnp.exp(-x)))


def _sigmoid(x):
    return 1.0 / (1.0 + jnp.exp(-x))


SLAB = 128


def _store_row_major(ref, x):
    for tb in range(x.shape[0] // SLAB):
        rows = slice(tb * SLAB, (tb + 1) * SLAB)
        parts = jnp.stack([x[rows, s * LANES:(s + 1) * LANES] for s in range(ROW_CHUNKS)], axis=0)
        ref[rows] = pltpu.einshape("stl->tsl", parts)


def _load_row_major(ref, rows):
    xt = pltpu.einshape("tsl->stl", ref[rows])
    return [xt[s] for s in range(ROW_CHUNKS)]


def _tiles_per_mod_row(mod, n, seq, tile):
    return seq // tile if mod.shape[0] > 1 else n // tile


def _mod_kernel(c_ref, w_ref, b_ref, o_ref):
    a = _silu(c_ref[...]).astype(BF16)
    o_ref[...] = jnp.dot(a, w_ref[...].astype(BF16), preferred_element_type=F32) + b_ref[...]


def _modulation(cond8, w_mod, b_mod):
    n_out = w_mod.shape[1]
    tn = 1536
    return pl.pallas_call(
        _mod_kernel,
        grid=(n_out // tn,),
        in_specs=[pl.BlockSpec((8, D_MODEL), lambda j: (0, 0)),
                  pl.BlockSpec((D_MODEL, tn), lambda j: (0, j)),
                  pl.BlockSpec((1, tn), lambda j: (0, j))],
        out_specs=pl.BlockSpec((8, tn), lambda j: (0, j)),
        out_shape=jax.ShapeDtypeStruct((8, n_out), F32),
        compiler_params=_cparams(("arbitrary",)),
        name="modulation",
    )(cond8, w_mod, b_mod.reshape(1, n_out))


def _in_proj_kernel(x_ref, sc_ref, sh_ref, g_ref, w_ref, o_ref, tail_ref, h_scr):
    j = pl.program_id(1)

    @pl.when(j == 0)
    def _():
        x = x_ref[...]
        y = x * lax.rsqrt(jnp.mean(x * x, axis=-1, keepdims=True) + EPS) * g_ref[...]
        h_scr[...] = (y * (1.0 + sc_ref[0]) + sh_ref[0]).astype(BF16)

    acc = jnp.dot(h_scr[...], w_ref[...], preferred_element_type=F32)
    o_ref[...] = acc.astype(BF16)

    @pl.when(j == IN_PAD // IN_TN - 1)
    def _():
        first = TAIL_COL0 - (IN_PAD - IN_TN)
        tail_ref[...] = acc[:, first:first + TAIL_W]


def _in_proj(x2d, seq, sc, sh, g, w_r):
    n = x2d.shape[0]
    tm = min(1024, seq if sc.shape[0] > 1 else n)
    per_b = _tiles_per_mod_row(sc, n, seq, tm)
    return pl.pallas_call(
        _in_proj_kernel,
        grid=(n // tm, IN_PAD // IN_TN),
        in_specs=[pl.BlockSpec((tm, D_MODEL), lambda i, j: (i, 0)),
                  pl.BlockSpec((1, 1, D_MODEL), lambda i, j: (i // per_b, 0, 0)),
                  pl.BlockSpec((1, 1, D_MODEL), lambda i, j: (i // per_b, 0, 0)),
                  pl.BlockSpec((1, D_MODEL), lambda i, j: (0, 0)),
                  pl.BlockSpec((D_MODEL, IN_TN), lambda i, j: (0, j))],
        out_specs=[pl.BlockSpec((tm, IN_TN), lambda i, j: (i, j)),
                   pl.BlockSpec((tm, TAIL_W), lambda i, j: (i, 0))],
        out_shape=[jax.ShapeDtypeStruct((n, IN_PAD), BF16),
                   jax.ShapeDtypeStruct((n, TAIL_W), F32)],
        scratch_shapes=[pltpu.VMEM((tm, D_MODEL), BF16)],
        compiler_params=_cparams(("parallel", "arbitrary")),
        name="in_proj",
    )(x2d, sc, sh, g, w_r)


GLA_BLK = 256
GLA_HPS = 4


def _tn_dot(a, b, precision=None):
    return lax.dot_general(a, b, (((0,), (0,)), ((), ())), precision=precision,
                           preferred_element_type=F32)


def _nt_dot(a, b, precision=None):
    return lax.dot_general(a, b, (((1,), (1,)), ((), ())), precision=precision,
                           preferred_element_type=F32)


def _split_bf16(x, parts):
    out = []
    for _ in range(parts):
        piece = x.astype(BF16)
        out.append(piece)
        x = x - piece.astype(F32)
    return out


def _dot3(x, w):
    xh, xl = _split_bf16(x, 2)
    wh, wl = _split_bf16(w, 2)
    d = functools.partial(jnp.dot, preferred_element_type=F32)
    return d(xh, wh) + d(xl, wh) + d(xh, wl)


def _gla_kernel(q_ref, k_ref, v_ref, r_ref, dec_ref, wdf_ref, bdf_ref, wdb_ref, bdb_ref, gg_ref,
                s0f_ref, s0b_ref, o_ref, stf_ref, stb_ref, state_t, of_scr, *, nblk):
    i = pl.program_id(2)
    c = GLA_CHUNK
    nch = GLA_BLK // c
    ri = lax.broadcasted_iota(jnp.int32, (GLA_BLK, GLA_BLK), 0)
    ci = lax.broadcasted_iota(jnp.int32, (GLA_BLK, GLA_BLK), 1)
    same_chunk = (ri // c) == (ci // c)

    @pl.when(i == 0)
    def _():
        for h in range(GLA_HPS):
            state_t[h] = s0f_ref[0, h].T

    @pl.when(i == nblk)
    def _():
        for h in range(GLA_HPS):
            state_t[h] = s0b_ref[0, h].T

    def run(fwd):
        blk = i if fwd else 2 * nblk - 1 - i
        keep = same_chunk & ((ci <= ri) if fwd else (ci >= ri))
        w_ref, b_ref = (wdf_ref, bdf_ref) if fwd else (wdb_ref, bdb_ref)
        logit = _dot3(dec_ref[...], w_ref[...]) + b_ref[...]
        g = (jnp.minimum(logit, 0.0) - jnp.log(1.0 + jnp.exp(-jnp.abs(logit)))) / GLA_TAU
        tri = keep.astype(BF16)
        b = sum(jnp.dot(tri, piece, preferred_element_type=F32) for piece in _split_bf16(g, 3))
        edge = c - 1 if fwd else 0
        tots = [b[ch * c + edge:ch * c + edge + 1, :] for ch in range(nch)]
        totb = jnp.concatenate([jnp.broadcast_to(t, (c, t.shape[1])) for t in tots], axis=0)
        q_in = (q_ref[...].astype(F32) * (GLA_DK ** -0.5) * jnp.exp(b)).astype(BF16)
        k = k_ref[...].astype(F32)
        k_in = (k * jnp.exp(-b)).astype(BF16)
        k_st = (k * jnp.exp(totb - b)).astype(BF16)
        srow = pl.multiple_of(blk * GLA_BLK, GLA_BLK)
        for h in range(GLA_HPS):
            ks = slice(h * GLA_DK, (h + 1) * GLA_DK)
            vs = slice(h * GLA_DV, (h + 1) * GLA_DV)
            v = v_ref[:, vs].astype(BF16)
            a = jnp.where(keep, _nt_dot(q_in[:, ks], k_in[:, ks]), 0.0).astype(BF16)
            o_intra = jnp.dot(a, v, preferred_element_type=F32)
            st = state_t[h]
            o_inter = [None] * nch
            for cc in range(nch):
                ch = cc if fwd else nch - 1 - cc
                rows = slice(ch * c, (ch + 1) * c)
                o_inter[ch] = _nt_dot(q_in[rows, ks], st.astype(BF16))
                st = st * jnp.exp(tots[ch][:, ks]) + _tn_dot(v[rows, :], k_st[rows, ks])
            state_t[h] = st
            o = o_intra + jnp.concatenate(o_inter, axis=0)
            if fwd:
                of_scr[pl.ds(srow, GLA_BLK), vs] = o
            else:
                o = o + of_scr[pl.ds(srow, GLA_BLK), vs]
                o = o * lax.rsqrt(jnp.mean(o * o, axis=-1, keepdims=True) + EPS) * gg_ref[:, vs]
                o_ref[:, vs] = (o * _silu(r_ref[:, vs].astype(F32))).astype(BF16)

    @pl.when(i < nblk)
    def _():
        run(True)

    @pl.when(i >= nblk)
    def _():
        run(False)

    @pl.when(i == nblk - 1)
    def _():
        for h in range(GLA_HPS):
            stf_ref[0, h] = state_t[h].T

    @pl.when(i == 2 * nblk - 1)
    def _():
        for h in range(GLA_HPS):
            stb_ref[0, h] = state_t[h].T


def _gla(p, dec, bsz, seq, wdf, bdf, wdb, bdb, g_gla, s0f, s0b):
    n = p.shape[0]
    nblk = seq // GLA_BLK
    kw = GLA_HPS * GLA_DK
    vw = GLA_HPS * GLA_DV

    def rb(b, i):
        return b * nblk + jnp.where(i < nblk, i, 2 * nblk - 1 - i)

    def orb(b, i):
        return b * nblk + jnp.where(i < nblk, nblk - 1, 2 * nblk - 1 - i)

    st_spec = pl.BlockSpec((1, GLA_HPS, GLA_DK, GLA_DV), lambda b, h, i: (b, h, 0, 0))
    st_shape = jax.ShapeDtypeStruct((bsz, GLA_HEADS, GLA_DK, GLA_DV), F32)
    return pl.pallas_call(
        functools.partial(_gla_kernel, nblk=nblk),
        grid=(bsz, GLA_HEADS // GLA_HPS, 2 * nblk),
        in_specs=[pl.BlockSpec((GLA_BLK, kw), lambda b, h, i: (rb(b, i), COL_GQ // kw + h)),
                  pl.BlockSpec((GLA_BLK, kw), lambda b, h, i: (rb(b, i), COL_GK // kw + h)),
                  pl.BlockSpec((GLA_BLK, vw), lambda b, h, i: (rb(b, i), COL_GV // vw + h)),
                  pl.BlockSpec((GLA_BLK, vw), lambda b, h, i: (rb(b, i), COL_GR // vw + h)),
                  pl.BlockSpec((GLA_BLK, LANES), lambda b, h, i: (rb(b, i), (COL_DEC - TAIL_COL0) // LANES)),
                  pl.BlockSpec((LANES, kw), lambda b, h, i: (0, h)),
                  pl.BlockSpec((1, kw), lambda b, h, i: (0, h)),
                  pl.BlockSpec((LANES, kw), lambda b, h, i: (0, h)),
                  pl.BlockSpec((1, kw), lambda b, h, i: (0, h)),
                  pl.BlockSpec((1, vw), lambda b, h, i: (0, h)),
                  st_spec, st_spec],
        out_specs=[pl.BlockSpec((GLA_BLK, vw), lambda b, h, i: (orb(b, i), h)), st_spec, st_spec],
        out_shape=[jax.ShapeDtypeStruct((n, GLA_VW), BF16), st_shape, st_shape],
        scratch_shapes=[pltpu.VMEM((GLA_HPS, GLA_DV, GLA_DK), F32), pltpu.VMEM((seq, vw), F32)],
        compiler_params=_cparams(("parallel", "parallel", "arbitrary")),
        name="gla",
    )(p, p, p, p, dec, wdf, bdf, wdb, bdb, g_gla, s0f, s0b)


def _sink_column(sink_ref, kvh, rows_per_head):
    r = lax.broadcasted_iota(jnp.int32, (SWA_GROUP * rows_per_head, 1), 0)
    col = jnp.zeros((SWA_GROUP * rows_per_head, 1), F32)
    for g in range(SWA_GROUP):
        col = jnp.where(r // rows_per_head == g, sink_ref[kvh * SWA_GROUP + g], col)
    return col


def _rope(x, cos, sin):
    outs = []
    for cb in range(x.shape[1] // LANES):
        sl = slice(cb * LANES, (cb + 1) * LANES)
        xc = x[:, sl]
        lane = lax.broadcasted_iota(jnp.int32, xc.shape, 1)
        sw = jnp.where(lane % 32 < 16, pltpu.roll(xc, LANES - 16, 1), pltpu.roll(xc, 16, 1))
        outs.append(xc * cos[:, sl] + sw * sin[:, sl])
    return jnp.concatenate(outs, axis=1)


def _swa_lat_kernel(sink_ref, q_ref, kp_ref, kc_ref, kn_ref, vp_ref, vc_ref, vn_ref, ck_ref, cv_ref,
                    cp_ref, cc_ref, cn_ref, sp_ref, sc_ref, sn_ref, o_ref, *, nb):
    n = pl.program_id(1)
    blk = SWA_BLOCK
    hd = SWA_HEAD_DIM
    cos_c, sin_c = cc_ref[...], sc_ref[...]
    kband = jnp.concatenate([_rope(kp_ref[...].astype(F32), cp_ref[...], sp_ref[...]),
                             _rope(kc_ref[...].astype(F32), cos_c, sin_c),
                             _rope(kn_ref[...].astype(F32), cn_ref[...], sn_ref[...])], axis=0).astype(BF16)
    vband = jnp.concatenate([vp_ref[...], vc_ref[...], vn_ref[...]], axis=0).astype(BF16)
    ck = ck_ref[0].astype(BF16)
    cv = cv_ref[0].astype(BF16)
    qi = lax.broadcasted_iota(jnp.int32, (SWA_GROUP * blk, 3 * blk), 0) % blk
    kj = lax.broadcasted_iota(jnp.int32, (SWA_GROUP * blk, 3 * blk), 1)
    k_abs = kj + (n - 1) * blk
    mask = (jnp.abs(kj - blk - qi) <= SWA_BLOCK) & (k_abs >= 0) & (k_abs < nb * blk)
    for kvh in range(SWA_KV_HEADS):
        ks = slice(kvh * hd, (kvh + 1) * hd)
        qr = _rope(q_ref[:, kvh * SWA_KVW:(kvh + 1) * SWA_KVW].astype(F32), cos_c, sin_c) * (hd ** -0.5)
        qg = jnp.concatenate([qr[:, g * hd:(g + 1) * hd] for g in range(SWA_GROUP)],
                             axis=0).astype(BF16)
        s_b = jnp.where(mask, _nt_dot(qg, kband[:, ks]), -1e30)
        s_c = _nt_dot(qg, ck[:, ks])
        sink = _sink_column(sink_ref, kvh, blk)
        m = jnp.maximum(jnp.maximum(jnp.max(s_b, axis=-1, keepdims=True),
                                    jnp.max(s_c, axis=-1, keepdims=True)), sink)
        p_b = jnp.exp(s_b - m)
        p_c = jnp.exp(s_c - m)
        den = jnp.exp(sink - m) + jnp.sum(p_b, axis=-1, keepdims=True) + jnp.sum(p_c, axis=-1, keepdims=True)
        o = (jnp.dot(p_b.astype(BF16), vband[:, ks], preferred_element_type=F32)
             + jnp.dot(p_c.astype(BF16), cv[:, ks], preferred_element_type=F32)) / den
        for g in range(SWA_GROUP):
            cb = (kvh * SWA_GROUP + g) * hd
            o_ref[:, cb:cb + hd] = o[g * blk:(g + 1) * blk, :].astype(BF16)


def _swa_latent(p, bsz, seq, cache_k, cache_v, sink, cos_t, sin_t):
    n = p.shape[0]
    nb = seq // SWA_BLOCK
    kcol = COL_SK // SWA_KVW
    vcol = COL_SV // SWA_KVW
    prev = lambda i: jnp.maximum(i - 1, 0)
    nxt = lambda i: jnp.minimum(i + 1, nb - 1)

    def pspec(col, f):
        return pl.BlockSpec((SWA_BLOCK, SWA_KVW), lambda b, i, s: (b * nb + f(i), col))

    def tspec(f):
        return pl.BlockSpec((SWA_BLOCK, SWA_KVW), lambda b, i, s: (f(i), 0))

    same = lambda i: i
    cspec = pl.BlockSpec((1, cache_k.shape[1], SWA_KVW), lambda b, i, s: (b, 0, 0))
    grid_spec = pltpu.PrefetchScalarGridSpec(
        num_scalar_prefetch=1,
        grid=(bsz, nb),
        in_specs=[pl.BlockSpec((SWA_BLOCK, SWA_QW), lambda b, i, s: (b * nb + i, COL_SQ // SWA_QW)),
                  pspec(kcol, prev), pspec(kcol, same), pspec(kcol, nxt),
                  pspec(vcol, prev), pspec(vcol, same), pspec(vcol, nxt),
                  cspec, cspec,
                  tspec(prev), tspec(same), tspec(nxt), tspec(prev), tspec(same), tspec(nxt)],
        out_specs=pl.BlockSpec((SWA_BLOCK, SWA_QW), lambda b, i, s: (b * nb + i, 0)),
    )
    return pl.pallas_call(
        functools.partial(_swa_lat_kernel, nb=nb),
        grid_spec=grid_spec,
        out_shape=jax.ShapeDtypeStruct((n, SWA_QW), BF16),
        compiler_params=_cparams(("parallel", "arbitrary")),
        name="swa_latent",
    )(sink, p, p, p, p, p, p, p, cache_k, cache_v, cos_t, cos_t, cos_t, sin_t, sin_t, sin_t)


def _swa_ctx_kernel(sink_ref, q_ref, k_ref, v_ref, o_ref):
    s = q_ref.shape[0]
    hd = SWA_HEAD_DIM
    kb = k_ref[...].astype(BF16)
    vb = v_ref[...].astype(BF16)
    for kvh in range(SWA_KV_HEADS):
        ks = slice(kvh * hd, (kvh + 1) * hd)
        qg = jnp.concatenate([q_ref[:, (kvh * SWA_GROUP + g) * hd:(kvh * SWA_GROUP + g + 1) * hd]
                              for g in range(SWA_GROUP)], axis=0)
        qg = (qg.astype(F32) * (hd ** -0.5)).astype(BF16)
        sc = _nt_dot(qg, kb[:, ks])
        sink = _sink_column(sink_ref, kvh, s)
        m = jnp.maximum(jnp.max(sc, axis=-1, keepdims=True), sink)
        pr = jnp.exp(sc - m)
        den = jnp.exp(sink - m) + jnp.sum(pr, axis=-1, keepdims=True)
        o = jnp.dot(pr.astype(BF16), vb[:, ks], preferred_element_type=F32) / den
        for g in range(SWA_GROUP):
            cb = (kvh * SWA_GROUP + g) * hd
            o_ref[:, cb:cb + hd] = o[g * s:(g + 1) * s, :].astype(BF16)


def _swa_context(p, bsz, seq, sink):
    n = p.shape[0]
    grid_spec = pltpu.PrefetchScalarGridSpec(
        num_scalar_prefetch=1,
        grid=(bsz,),
        in_specs=[pl.BlockSpec((seq, SWA_QW), lambda b, s: (b, COL_SQ // SWA_QW)),
                  pl.BlockSpec((seq, SWA_KVW), lambda b, s: (b, COL_SK // SWA_KVW)),
                  pl.BlockSpec((seq, SWA_KVW), lambda b, s: (b, COL_SV // SWA_KVW))],
        out_specs=pl.BlockSpec((seq, SWA_QW), lambda b, s: (b, 0)),
    )
    return pl.pallas_call(
        _swa_ctx_kernel,
        grid_spec=grid_spec,
        out_shape=jax.ShapeDtypeStruct((n, SWA_QW), BF16),
        compiler_params=_cparams(("parallel",)),
        name="swa_context",
    )(sink, p, p, p)


MERGE_TM = 512


def _branch_kernel(og_ref, os_ref, ga_ref, gb_ref, wa_ref, wb_ref, m_ref):
    a = jnp.dot(og_ref[...], wa_ref[...], preferred_element_type=F32)
    b = jnp.dot(os_ref[...], wb_ref[...], preferred_element_type=F32)
    m_ref[...] = (_sigmoid(ga_ref[...].astype(F32)) * a + _sigmoid(gb_ref[...].astype(F32)) * b).astype(BF16)


def _branch_merge(o_gla, o_swa, p, wa, wb):
    n = o_gla.shape[0]
    tm = MERGE_TM
    once = pl.Buffered(1)
    return pl.pallas_call(
        _branch_kernel,
        grid=(n // tm,),
        in_specs=[pl.BlockSpec((tm, GLA_VW), lambda i: (i, 0)),
                  pl.BlockSpec((tm, SWA_QW), lambda i: (i, 0)),
                  pl.BlockSpec((tm, D_MODEL), lambda i: (i, COL_GA // D_MODEL)),
                  pl.BlockSpec((tm, D_MODEL), lambda i: (i, COL_GB // D_MODEL)),
                  pl.BlockSpec((GLA_VW, D_MODEL), lambda i: (0, 0), pipeline_mode=once),
                  pl.BlockSpec((SWA_QW, D_MODEL), lambda i: (0, 0), pipeline_mode=once)],
        out_specs=pl.BlockSpec((tm, D_MODEL), lambda i: (i, 0)),
        out_shape=jax.ShapeDtypeStruct((n, D_MODEL), BF16),
        compiler_params=_cparams(("parallel",)),
        name="branch_merge",
    )(o_gla, o_swa, p, p, wa, wb)


def _out_kernel(m_ref, x_ref, gt1_ref, sc2_ref, sh2_ref, g2_ref, wo_ref, wr2_ref, wr1_ref,
                x1_ref, h2_ref, lg_ref):
    x1 = x_ref[...] + gt1_ref[0] * jnp.dot(m_ref[...], wo_ref[...], preferred_element_type=F32)
    x1_ref[...] = x1
    y = x1 * lax.rsqrt(jnp.mean(x1 * x1, axis=-1, keepdims=True) + EPS) * g2_ref[...]
    h2 = y * (1.0 + sc2_ref[0]) + sh2_ref[0]
    _store_row_major(h2_ref, h2)
    h_hi, h_lo = _split_bf16(h2, 2)
    l2 = (jnp.dot(h_hi, wr2_ref[...], preferred_element_type=F32)
          + jnp.dot(h_lo, wr1_ref[...], preferred_element_type=F32))
    lt = l2.T
    lt = lt[0:N_EXPERTS, :] + lt[N_EXPERTS:2 * N_EXPERTS, :]
    for cb in range(lt.shape[1] // LANES):
        lg_ref[cb] = lt[:, cb * LANES:(cb + 1) * LANES]


def _out_proj(merged, x2d, seq, gt1, sc2, sh2, g2, wo, wr2, wr1):
    n = x2d.shape[0]
    tm = MERGE_TM
    per_b = _tiles_per_mod_row(gt1, n, seq, tm)
    mod_spec = pl.BlockSpec((1, 1, D_MODEL), lambda i: (i // per_b, 0, 0))
    once = pl.Buffered(1)
    return pl.pallas_call(
        _out_kernel,
        grid=(n // tm,),
        in_specs=[pl.BlockSpec((tm, D_MODEL), lambda i: (i, 0)),
                  pl.BlockSpec((tm, D_MODEL), lambda i: (i, 0)),
                  mod_spec, mod_spec, mod_spec,
                  pl.BlockSpec((1, D_MODEL), lambda i: (0, 0)),
                  pl.BlockSpec((D_MODEL, D_MODEL), lambda i: (0, 0), pipeline_mode=once),
                  pl.BlockSpec((D_MODEL, LANES), lambda i: (0, 0)),
                  pl.BlockSpec((D_MODEL, LANES), lambda i: (0, 0))],
        out_specs=[pl.BlockSpec((tm, D_MODEL), lambda i: (i, 0)),
                   pl.BlockSpec((tm, ROW_CHUNKS, LANES), lambda i: (i, 0, 0)),
                   pl.BlockSpec((tm // LANES, N_EXPERTS, LANES), lambda i: (i, 0, 0))],
        out_shape=[jax.ShapeDtypeStruct((n, D_MODEL), F32),
                   jax.ShapeDtypeStruct((n, ROW_CHUNKS, LANES), F32),
                   jax.ShapeDtypeStruct((n // LANES, N_EXPERTS, LANES), F32)],
        compiler_params=_cparams(("parallel",)),
        name="out_proj",
    )(merged, x2d, gt1, sc2, sh2, g2, wo, wr2, wr1)


def _select_kernel(lg_ref, aff_ref, sel_ref, pos_ref, *, cap):
    nblk = lg_ref.shape[0]
    lg = lg_ref[...]
    ex = jnp.exp(lg - jnp.max(lg, axis=1, keepdims=True))
    aff = ex / jnp.sum(ex, axis=1, keepdims=True)
    aff_ref[...] = aff
    bits = lax.bitcast_convert_type(aff, jnp.int32)

    def count(pred):
        c = jnp.sum(jnp.where(pred, 1.0, 0.0), axis=0, keepdims=True)
        return jnp.sum(c, axis=2, keepdims=True)

    def bit_step(t, cur):
        cand = cur | jnp.left_shift(jnp.int32(1), 30 - t)
        return jnp.where(count(bits >= cand) >= cap, cand, cur)

    thr = lax.fori_loop(0, 31, bit_step, jnp.zeros((1, N_EXPERTS, 1), jnp.int32))
    need = (cap - count(bits > thr))[0]
    thr2 = thr[0]
    ri = lax.broadcasted_iota(jnp.int32, (LANES, LANES), 0)
    ci = lax.broadcasted_iota(jnp.int32, (LANES, LANES), 1)
    upper = (ri <= ci).astype(BF16)

    def blk_step(b, carry):
        run_eq, run_sel = carry
        bb = lax.bitcast_convert_type(aff_ref[b], jnp.int32)
        eq = (bb == thr2).astype(F32)
        eq_excl = jnp.dot(eq.astype(BF16), upper, preferred_element_type=F32) - eq + run_eq
        sel = jnp.where((bb > thr2) | ((eq > 0) & (eq_excl < need)), 1.0, 0.0)
        sel_ref[b] = sel
        pos_ref[b] = jnp.dot(sel.astype(BF16), upper, preferred_element_type=F32) - sel + run_sel
        return (run_eq + jnp.sum(eq, axis=1, keepdims=True),
                run_sel + jnp.sum(sel, axis=1, keepdims=True))

    zero = jnp.zeros((N_EXPERTS, 1), F32)
    lax.fori_loop(0, nblk, blk_step, (zero, zero))


def _select(logits3, cap):
    shp = jax.ShapeDtypeStruct(logits3.shape, F32)
    return pl.pallas_call(
        functools.partial(_select_kernel, cap=float(cap)),
        out_shape=[shp, shp, shp],
        compiler_params=pltpu.CompilerParams(vmem_limit_bytes=VMEM_LIMIT),
        name="select",
    )(logits3)


COMPACT_WIN = 136


def _compact_kernel(base_ref, aff_ref, sel_ref, pos_ref, idx_ref, gate_ref):
    nblk = aff_ref.shape[0]
    idx_ref[...] = jnp.zeros_like(idx_ref)
    gate_ref[...] = jnp.zeros_like(gate_ref)
    slot = lax.broadcasted_iota(jnp.int32, (COMPACT_WIN, LANES), 0).astype(F32)
    lane = lax.broadcasted_iota(jnp.int32, (COMPACT_WIN, LANES), 1)

    def blk_step(b, _):
        aff = aff_ref[b]
        sel = sel_ref[b]
        pos = pos_ref[b]
        tok = (lane + b * LANES).astype(F32)
        for e in range(N_EXPERTS):
            base8 = pl.multiple_of((base_ref[b * N_EXPERTS + e] >> 3) << 3, 8)
            rel = pos[e:e + 1, :] - base8.astype(F32)
            hit = (rel == slot) & (sel[e:e + 1, :] > 0)
            rows = pl.ds(base8, COMPACT_WIN)
            mine = lane == e
            ic = jnp.sum(jnp.where(hit, tok, 0.0), axis=1, keepdims=True)
            gc = jnp.sum(jnp.where(hit, aff[e:e + 1, :], 0.0), axis=1, keepdims=True)
            idx_ref[rows, :] += jnp.where(mine, ic, 0.0)
            gate_ref[rows, :] += jnp.where(mine, gc, 0.0)
        return 0

    lax.fori_loop(0, nblk, blk_step, 0)


def _compact(base, aff3, sel3, pos3, cap):
    cp = cap + 2 * LANES
    full = pl.BlockSpec(aff3.shape, lambda i, s: (0, 0, 0))
    ospec = pl.BlockSpec((cp, LANES), lambda i, s: (0, 0))
    oshape = jax.ShapeDtypeStruct((cp, LANES), F32)
    grid_spec = pltpu.PrefetchScalarGridSpec(
        num_scalar_prefetch=1, grid=(1,), in_specs=[full, full, full], out_specs=[ospec, ospec])
    return pl.pallas_call(
        _compact_kernel,
        grid_spec=grid_spec,
        out_shape=[oshape, oshape],
        compiler_params=_cparams(("arbitrary",)),
        name="compact",
    )(base, aff3, sel3, pos3)


FFN_COLS = 256


def _ffn_kernel(idx_ref, h2_hbm, wg_ref, wu_ref, wd_ref, o_ref, gbuf, xs, hid, sem, *, tm, total):
    step = pl.program_id(0) * pl.num_programs(1) + pl.program_id(1)

    def row_copy(step_, i):
        return pltpu.make_async_copy(h2_hbm.at[idx_ref[step_ * tm + i]], gbuf.at[i], sem)

    def wait_rows():
        pltpu.make_async_copy(h2_hbm.at[pl.ds(0, tm)], gbuf, sem).wait()

    @pl.when(step == 0)
    def _():
        def body(i, _):
            row_copy(step, i).start()
            return 0
        lax.fori_loop(0, tm, body, 0)

    wait_rows()
    slab = min(SLAB, tm)
    for tb in range(tm // slab):
        rows = slice(tb * slab, (tb + 1) * slab)
        for s, chunk in enumerate(_load_row_major(gbuf, rows)):
            xs[rows, s * LANES:(s + 1) * LANES] = chunk.astype(BF16)

    nxt = jnp.minimum(step + 1, total - 1)
    for i in range(tm):
        row_copy(nxt, i).start()

    for f in range(EXPERT_FF // FFN_COLS):
        cols = slice(f * FFN_COLS, (f + 1) * FFN_COLS)
        hg = jnp.dot(xs[...], wg_ref[0, :, cols], preferred_element_type=F32)
        hu = jnp.dot(xs[...], wu_ref[0, :, cols], preferred_element_type=F32)
        hid[:, cols] = (_silu(hg) * hu).astype(BF16)
    for c in range(D_MODEL // FFN_COLS):
        y = jnp.dot(hid[...], wd_ref[0, :, c * FFN_COLS:(c + 1) * FFN_COLS], preferred_element_type=F32)
        for s in range(FFN_COLS // LANES):
            o_ref[pl.ds(c * (FFN_COLS // LANES) + s, tm, stride=ROW_CHUNKS), :] = y[:, s * LANES:(s + 1) * LANES]

    @pl.when(step == total - 1)
    def _():
        wait_rows()


def _expert_ffn(idx_flat, h2_lin, wg, wu, wd, cap):
    tm = min(1024, cap)
    nr = cap // tm
    grid_spec = pltpu.PrefetchScalarGridSpec(
        num_scalar_prefetch=1,
        grid=(N_EXPERTS, nr),
        in_specs=[pl.BlockSpec(memory_space=pl.ANY),
                  pl.BlockSpec((1, D_MODEL, EXPERT_FF), lambda e, r, s: (e, 0, 0)),
                  pl.BlockSpec((1, D_MODEL, EXPERT_FF), lambda e, r, s: (e, 0, 0)),
                  pl.BlockSpec((1, EXPERT_FF, D_MODEL), lambda e, r, s: (e, 0, 0))],
        out_specs=pl.BlockSpec((tm * ROW_CHUNKS, LANES), lambda e, r, s: (e * nr + r, 0)),
        scratch_shapes=[pltpu.VMEM((tm, ROW_CHUNKS, LANES), F32),
                        pltpu.VMEM((tm, D_MODEL), BF16),
                        pltpu.VMEM((tm, EXPERT_FF), BF16),
                        pltpu.SemaphoreType.DMA(())],
    )
    return pl.pallas_call(
        functools.partial(_ffn_kernel, tm=tm, total=N_EXPERTS * nr),
        grid_spec=grid_spec,
        out_shape=jax.ShapeDtypeStruct((N_EXPERTS * cap * ROW_CHUNKS, LANES), F32),
        compiler_params=pltpu.CompilerParams(dimension_semantics=("arbitrary", "arbitrary"),
                                             vmem_limit_bytes=FFN_VMEM_LIMIT),
        name="expert_ffn",
    )(idx_flat, h2_lin, wg, wu, wd)


COMBINE_UNROLL = 4
COMBINE_CHUNK = 32
COMBINE_NCHUNK = 3


def _combine_kernel(idx_ref, gate_ref, lo_ref, ye_hbm, x1_ref, gt2_ref, gf_ref, o_ref, stage, acc, sem,
                    *, tt, ch, cap, ntiles):
    tile = pl.program_id(0)
    slot = tile % 2

    window = COMBINE_NCHUNK * ch

    def chunk_start(first_row):
        return jnp.minimum(first_row, cap - window)

    def chunk_copy(slot_, e, start, c):
        src = pl.multiple_of((e * cap + start + c * ch) * ROW_CHUNKS, ROW_CHUNKS)
        return pltpu.make_async_copy(ye_hbm.at[pl.ds(src, ch * ROW_CHUNKS), :],
                                     stage.at[slot_, e, pl.ds(c * ch * ROW_CHUNKS, ch * ROW_CHUNKS), :],
                                     sem.at[slot_, e])

    def for_needed_chunks(tile_, e, fn):
        lo = lo_ref[tile_ * N_EXPERTS + e]
        hi = lo_ref[(tile_ + 1) * N_EXPERTS + e]
        start = chunk_start(lo)
        for c in range(COMBINE_NCHUNK):
            @pl.when((start + c * ch < hi) & (start + (c + 1) * ch > lo))
            def _():
                fn(start, c)

    def issue(tile_, slot_):
        for e in range(N_EXPERTS):
            for_needed_chunks(tile_, e, lambda start, c: chunk_copy(slot_, e, start, c).start())

    @pl.when(tile == 0)
    def _():
        issue(tile, slot)

    @pl.when(tile + 1 < ntiles)
    def _():
        issue(tile + 1, 1 - slot)

    acc[...] = jnp.zeros_like(acc)

    def add_rows(e, start, r0, r1):
        def group(r, width):
            dsts, vals = [], []
            for u in range(width):
                dst = idx_ref[e * cap + r + u] - tile * tt
                src = pl.ds(pl.multiple_of((r + u - start) * ROW_CHUNKS, ROW_CHUNKS), ROW_CHUNKS)
                dsts.append(dst)
                vals.append(acc[dst] + gate_ref[e * cap + r + u] * stage[slot, e, src, :])
            for dst, val in zip(dsts, vals):
                acc[dst] = val

        ngroups = (r1 - r0) // COMBINE_UNROLL

        def body_group(j, _):
            group(r0 + j * COMBINE_UNROLL, COMBINE_UNROLL)
            return 0
        lax.fori_loop(0, ngroups, body_group, 0)

        def body_one(r, _):
            group(r, 1)
            return 0
        lax.fori_loop(r0 + ngroups * COMBINE_UNROLL, r1, body_one, 0)

    for e in range(N_EXPERTS):
        lo = lo_ref[tile * N_EXPERTS + e]
        hi = lo_ref[(tile + 1) * N_EXPERTS + e]
        start = chunk_start(lo)
        for_needed_chunks(tile, e, lambda start_, c: chunk_copy(slot, e, start_, c).wait())
        first_end = jnp.minimum(hi, start + window)
        add_rows(e, start, lo, first_end)

        def more(r0):
            st = jnp.minimum(r0, cap - ch)
            cp = chunk_copy(slot, e, st, 0)
            cp.start()
            cp.wait()
            r1 = jnp.minimum(hi, st + ch)
            add_rows(e, st, r0, r1)
            return r1
        lax.while_loop(lambda r0: r0 < hi, more, first_end)

    for tb in range(tt // SLAB):
        rows = slice(tb * SLAB, (tb + 1) * SLAB)
        ssq = jnp.zeros((SLAB, 1), F32)
        for s, y in enumerate(_load_row_major(acc, rows)):
            cols = slice(s * LANES, (s + 1) * LANES)
            x2 = x1_ref[rows, cols] + gt2_ref[0][:, cols] * y
            o_ref[rows, cols] = x2
            ssq = ssq + jnp.sum(x2 * x2, axis=-1, keepdims=True)
        o_ref[rows, :] = o_ref[rows, :] * lax.rsqrt(ssq / D_MODEL + EPS) * gf_ref[...]


def _combine(idx_flat, gate_flat, lo_tab, ye_lin, x1, seq, gt2, g_final, cap, tt):
    n = x1.shape[0]
    ch = min(COMBINE_CHUNK, cap // COMBINE_NCHUNK)
    ntiles = n // tt
    per_b = _tiles_per_mod_row(gt2, n, seq, tt)
    grid_spec = pltpu.PrefetchScalarGridSpec(
        num_scalar_prefetch=3,
        grid=(ntiles,),
        in_specs=[pl.BlockSpec(memory_space=pl.ANY),
                  pl.BlockSpec((tt, D_MODEL), lambda i, a, b, c: (i, 0)),
                  pl.BlockSpec((1, 1, D_MODEL), lambda i, a, b, c: (i // per_b, 0, 0)),
                  pl.BlockSpec((1, D_MODEL), lambda i, a, b, c: (0, 0))],
        out_specs=pl.BlockSpec((tt, D_MODEL), lambda i, a, b, c: (i, 0)),
        scratch_shapes=[pltpu.VMEM((2, N_EXPERTS, COMBINE_NCHUNK * ch * ROW_CHUNKS, LANES), F32),
                        pltpu.VMEM((tt, ROW_CHUNKS, LANES), F32),
                        pltpu.SemaphoreType.DMA((2, N_EXPERTS))],
    )
    return pl.pallas_call(
        functools.partial(_combine_kernel, tt=tt, ch=ch, cap=cap, ntiles=ntiles),
        grid_spec=grid_spec,
        out_shape=jax.ShapeDtypeStruct((n, D_MODEL), F32),
        compiler_params=_cparams(("arbitrary",)),
        name="combine",
    )(idx_flat, gate_flat, lo_tab, ye_lin, x1, gt2, g_final)


def _rope_tables(seq):
    pos = jnp.arange(seq)
    row = (pos // GRID_W).astype(F32)
    col = (pos % GRID_W).astype(F32)
    npair = SWA_HEAD_DIM // 4
    inv_freq = ROPE_BASE ** (-jnp.arange(npair, dtype=F32) / npair)
    ar = row[:, None] * inv_freq[None, :]
    ac = col[:, None] * inv_freq[None, :]
    cos = jnp.concatenate([jnp.cos(ar), jnp.cos(ar), jnp.cos(ac), jnp.cos(ac)], axis=1)
    sin = jnp.concatenate([-jnp.sin(ar), jnp.sin(ar), -jnp.sin(ac), jnp.sin(ac)], axis=1)
    reps = SWA_KVW // SWA_HEAD_DIM
    return jnp.tile(cos, (1, reps)), jnp.tile(sin, (1, reps))


def _layer(x, mods, wts, latent, s0f, s0b, cache_k, cache_v):
    bsz, seq, _ = x.shape
    n = bsz * seq
    x2d = x.reshape(n, D_MODEL)
    sh1, sc1, gt1, sh2, sc2, gt2 = mods
    p, tail = _in_proj(x2d, seq, sc1, sh1, wts["g1"], wts["w_in_r"])
    o_gla, st_f, st_b = _gla(p, tail, bsz, seq, wts["wdf"], wts["bdf"], wts["wdb"], wts["bdb"], wts["g_gla"],
                             s0f, s0b)
    if latent:
        cos_t, sin_t = _rope_tables(seq)
        o_swa = _swa_latent(p, bsz, seq, cache_k, cache_v, wts["sink"], cos_t, sin_t)
    else:
        o_swa = _swa_context(p, bsz, seq, wts["sink"])
    merged = _branch_merge(o_gla, o_swa, p, wts["wa"], wts["wb"])
    x1, h2, logits3 = _out_proj(merged, x2d, seq, gt1, sc2, sh2, wts["g2"], wts["wo"], wts["wr2"], wts["wr1"])
    cap = CAPACITY_FACTOR * n // N_EXPERTS
    aff3, sel3, pos3 = _select(logits3, cap)
    base = pos3[:, :, 0].astype(jnp.int32)
    idx_c, gate_c = _compact(base.reshape(-1), aff3, sel3, pos3, cap)
    idx_flat = idx_c[:cap, :N_EXPERTS].T.astype(jnp.int32).reshape(-1)
    gate_flat = gate_c[:cap, :N_EXPERTS].T.reshape(-1)
    ye = _expert_ffn(idx_flat, h2, wts["wg"], wts["wu"], wts["wd"], cap)
    tt = 256
    lo_tab = jnp.concatenate([base[::tt // LANES], jnp.full((1, N_EXPERTS), cap, jnp.int32)], axis=0)
    y = _combine(idx_flat, gate_flat, lo_tab.reshape(-1), ye, x1, seq, gt2, wts["g_final"], cap, tt)
    return y.reshape(bsz, seq, D_MODEL), tail, st_f, st_b


def kernel(x_prompt, x_sample, state_gla_fwd, state_gla_bwd, cache_k, cache_v, c, c_ctx, w_mod, b_mod,
           g_norm1, w_in, w_dec_f, b_dec_f, w_dec_b, b_dec_b, g_gla, attn_sink, w_branch_a, w_branch_b,
           w_out, g_norm2, w_router, w_exp_gate, w_exp_up, w_exp_down, g_final):
    bp, sp, _ = x_prompt.shape
    bl = x_sample.shape[0]
    l = 0
    gla_end = 2 * GLA_KW + 2 * GLA_VW
    dec_end = gla_end + 2 * GLA_RANK
    swa_end = dec_end + SWA_QW + 2 * SWA_KVW
    w = w_in[l]
    w_in_r = jnp.concatenate(
        [w[:, swa_end:].astype(BF16), w[:, :gla_end].astype(BF16), w[:, dec_end:swa_end].astype(BF16),
         w[:, gla_end:dec_end].astype(BF16), jnp.zeros((D_MODEL, IN_PAD - w.shape[1]), BF16)], axis=1)
    zpad = jnp.zeros((LANES - 2 * GLA_RANK, GLA_KW), F32)
    wr_hi = w_router[l].astype(BF16)
    wr_lo = (w_router[l] - wr_hi.astype(F32)).astype(BF16)
    wr_pad = jnp.zeros((D_MODEL, LANES - 2 * N_EXPERTS), BF16)
    zr = jnp.zeros((GLA_RANK, GLA_KW), F32)
    wts = {
        "g1": g_norm1[l].reshape(1, D_MODEL),
        "w_in_r": w_in_r,
        "wdf": jnp.concatenate([w_dec_f[l], zr, zpad], axis=0),
        "wdb": jnp.concatenate([zr, w_dec_b[l], zpad], axis=0),
        "bdf": b_dec_f[l].reshape(1, GLA_KW),
        "bdb": b_dec_b[l].reshape(1, GLA_KW),
        "g_gla": g_gla[l].reshape(1, GLA_VW),
        "sink": attn_sink[l],
        "wa": w_branch_a[l].astype(BF16),
        "wb": w_branch_b[l].astype(BF16),
        "wo": w_out[l].astype(BF16),
        "g2": g_norm2[l].reshape(1, D_MODEL),
        "wr2": jnp.concatenate([wr_hi, wr_lo, wr_pad], axis=1),
        "wr1": jnp.concatenate([wr_hi, jnp.zeros_like(wr_lo), wr_pad], axis=1),
        "wg": w_exp_gate[l].astype(BF16),
        "wu": w_exp_up[l].astype(BF16),
        "wd": w_exp_down[l].astype(BF16),
        "g_final": g_final.reshape(1, D_MODEL),
    }
    cond8 = jnp.concatenate([c_ctx[None, :], c, jnp.zeros((8 - 1 - bl, D_MODEL), F32)], axis=0)
    mod = _modulation(cond8, w_mod[l], b_mod[l]).reshape(8, N_MOD, 1, D_MODEL)
    mods_ctx = tuple(mod[0:1, j] for j in range(N_MOD))
    mods_lat = tuple(mod[1:1 + bl, j] for j in range(N_MOD))

    zero_state = jnp.zeros((bp, GLA_HEADS, GLA_DK, GLA_DV), F32)
    y_prompt, tail_ctx, st_f, st_b = _layer(x_prompt, mods_ctx, wts, False, zero_state, zero_state, None, None)
    ck = cache_k[:, l].reshape(bl, -1, SWA_KVW)
    cv = cache_v[:, l].reshape(bl, -1, SWA_KVW)
    y_sample, _, _, _ = _layer(x_sample, mods_lat, wts, True, state_gla_fwd[:, l], state_gla_bwd[:, l], ck, cv)

    ksl = slice(COL_SK - TAIL_COL0, COL_SK - TAIL_COL0 + SWA_KVW)
    vsl = slice(COL_SV - TAIL_COL0, COL_SV - TAIL_COL0 + SWA_KVW)
    new_k = tail_ctx[:, ksl].reshape(bp, 1, sp, SWA_KV_HEADS, SWA_HEAD_DIM)
    new_v = tail_ctx[:, vsl].reshape(bp, 1, sp, SWA_KV_HEADS, SWA_HEAD_DIM)
    return (y_prompt, y_sample, st_f[:, None], st_b[:, None], new_k, new_v)
```

```python
import functools

import jax
import jax.numpy as jnp
from jax import lax
from jax.experimental import pallas as pl
from jax.experimental.pallas import tpu as pltpu

F32 = jnp.float32
BF16 = jnp.bfloat16
HIGHEST = lax.Precision.HIGHEST

D_MODEL = 2048
N_MOD = 6
EPS = 1e-6

GLA_HEADS = 4
GLA_DK = 128
GLA_DV = 256
GLA_KW = GLA_HEADS * GLA_DK
GLA_VW = GLA_HEADS * GLA_DV
GLA_RANK = 16
GLA_TAU = 16.0
GLA_CHUNK = 64

SWA_HEADS = 16
SWA_KV_HEADS = 4
SWA_GROUP = 4
SWA_HEAD_DIM = 64
SWA_QW = SWA_HEADS * SWA_HEAD_DIM
SWA_KVW = SWA_KV_HEADS * SWA_HEAD_DIM
SWA_BLOCK = 128
GRID_W = 64
ROPE_BASE = 10000.0

N_EXPERTS = 16
EXPERT_FF = D_MODEL // 2
CAPACITY_FACTOR = 2

LANES = 128
ROW_CHUNKS = D_MODEL // LANES
VMEM_LIMIT = 56 * 1024 * 1024

COL_GA = 0
COL_GB = 2048
COL_GQ = 4096
COL_GK = 4608
COL_GV = 5120
COL_GR = 6144
COL_SQ = 7168
COL_SK = 8192
COL_SV = 8448
COL_DEC = 8704
IN_PAD = 8960
IN_TN = 1280
TAIL_COL0 = COL_SK
TAIL_W = COL_DEC + LANES - COL_SK


def _cparams(sem, **kw):
    return pltpu.CompilerParams(dimension_semantics=sem, vmem_limit_bytes=VMEM_LIMIT, **kw)


def _silu(x):
    return x * (1.0 / (1.0 + jnp.exp(-x)))


def _sigmoid(x):
    return 1.0 / (1.0 + jnp.exp(-x))


SLAB = 128


def _store_row_major(ref, x):
    for tb in range(x.shape[0] // SLAB):
        rows = slice(tb * SLAB, (tb + 1) * SLAB)
        parts = jnp.stack([x[rows, s * LANES:(s + 1) * LANES] for s in range(ROW_CHUNKS)], axis=0)
        ref[rows] = pltpu.einshape("stl->tsl", parts)


def _load_row_major(ref, rows):
    xt = pltpu.einshape("tsl->stl", ref[rows])
    return [xt[s] for s in range(ROW_CHUNKS)]


def _tiles_per_mod_row(mod, n, seq, tile):
    return seq // tile if mod.shape[0] > 1 else n // tile


def _mod_kernel(c_ref, w_ref, b_ref, o_ref):
    a = _silu(c_ref[...]).astype(BF16)
    o_ref[...] = jnp.dot(a, w_ref[...].astype(BF16), preferred_element_type=F32) + b_ref[...]


def _modulation(cond8, w_mod, b_mod):
    n_out = w_mod.shape[1]
    tn = 1536
    return pl.pallas_call(
        _mod_kernel,
        grid=(n_out // tn,),
        in_specs=[pl.BlockSpec((8, D_MODEL), lambda j: (0, 0)),
                  pl.BlockSpec((D_MODEL, tn), lambda j: (0, j)),
                  pl.BlockSpec((1, tn), lambda j: (0, j))],
        out_specs=pl.BlockSpec((8, tn), lambda j: (0, j)),
        out_shape=jax.ShapeDtypeStruct((8, n_out), F32),
        compiler_params=_cparams(("arbitrary",)),
        name="modulation",
    )(cond8, w_mod, b_mod.reshape(1, n_out))


def _in_proj_kernel(x_ref, sc_ref, sh_ref, g_ref, w_ref, o_ref, tail_ref, h_scr):
    j = pl.program_id(1)

    @pl.when(j == 0)
    def _():
        x = x_ref[...]
        y = x * lax.rsqrt(jnp.mean(x * x, axis=-1, keepdims=True) + EPS) * g_ref[...]
        h_scr[...] = (y * (1.0 + sc_ref[0]) + sh_ref[0]).astype(BF16)

    acc = jnp.dot(h_scr[...], w_ref[...], preferred_element_type=F32)
    o_ref[...] = acc.astype(BF16)

    @pl.when(j == IN_PAD // IN_TN - 1)
    def _():
        first = TAIL_COL0 - (IN_PAD - IN_TN)
        tail_ref[...] = acc[:, first:first + TAIL_W]


def _in_proj(x2d, seq, sc, sh, g, w_r):
    n = x2d.shape[0]
    tm = min(1024, seq if sc.shape[0] > 1 else n)
    per_b = _tiles_per_mod_row(sc, n, seq, tm)
    return pl.pallas_call(
        _in_proj_kernel,
        grid=(n // tm, IN_PAD // IN_TN),
        in_specs=[pl.BlockSpec((tm, D_MODEL), lambda i, j: (i, 0)),
                  pl.BlockSpec((1, 1, D_MODEL), lambda i, j: (i // per_b, 0, 0)),
                  pl.BlockSpec((1, 1, D_MODEL), lambda i, j: (i // per_b, 0, 0)),
                  pl.BlockSpec((1, D_MODEL), lambda i, j: (0, 0)),
                  pl.BlockSpec((D_MODEL, IN_TN), lambda i, j: (0, j))],
        out_specs=[pl.BlockSpec((tm, IN_TN), lambda i, j: (i, j)),
                   pl.BlockSpec((tm, TAIL_W), lambda i, j: (i, 0))],
        out_shape=[jax.ShapeDtypeStruct((n, IN_PAD), BF16),
                   jax.ShapeDtypeStruct((n, TAIL_W), F32)],
        scratch_shapes=[pltpu.VMEM((tm, D_MODEL), BF16)],
        compiler_params=_cparams(("parallel", "arbitrary")),
        name="in_proj",
    )(x2d, sc, sh, g, w_r)


GLA_BLK = 256
GLA_HPS = 4


def _tn_dot(a, b, precision=None):
    return lax.dot_general(a, b, (((0,), (0,)), ((), ())), precision=precision,
                           preferred_element_type=F32)


def _nt_dot(a, b, precision=None):
    return lax.dot_general(a, b, (((1,), (1,)), ((), ())), precision=precision,
                           preferred_element_type=F32)


def _split_bf16(x, parts):
    out = []
    for _ in range(parts):
        piece = x.astype(BF16)
        out.append(piece)
        x = x - piece.astype(F32)
    return out


def _dot3(x, w):
    xh, xl = _split_bf16(x, 2)
    wh, wl = _split_bf16(w, 2)
    d = functools.partial(jnp.dot, preferred_element_type=F32)
    return d(xh, wh) + d(xl, wh) + d(xh, wl)


def _gla_kernel(q_ref, k_ref, v_ref, r_ref, dec_ref, wdf_ref, bdf_ref, wdb_ref, bdb_ref, gg_ref,
                s0f_ref, s0b_ref, o_ref, stf_ref, stb_ref, state_t, of_scr, *, nblk):
    i = pl.program_id(2)
    c = GLA_CHUNK
    nch = GLA_BLK // c
    ri = lax.broadcasted_iota(jnp.int32, (GLA_BLK, GLA_BLK), 0)
    ci = lax.broadcasted_iota(jnp.int32, (GLA_BLK, GLA_BLK), 1)
    same_chunk = (ri // c) == (ci // c)

    @pl.when(i == 0)
    def _():
        for h in range(GLA_HPS):
            state_t[h] = s0f_ref[0, h].T

    @pl.when(i == nblk)
    def _():
        for h in range(GLA_HPS):
            state_t[h] = s0b_ref[0, h].T

    def run(fwd):
        blk = i if fwd else 2 * nblk - 1 - i
        keep = same_chunk & ((ci <= ri) if fwd else (ci >= ri))
        w_ref, b_ref = (wdf_ref, bdf_ref) if fwd else (wdb_ref, bdb_ref)
        logit = _dot3(dec_ref[...], w_ref[...]) + b_ref[...]
        g = (jnp.minimum(logit, 0.0) - jnp.log(1.0 + jnp.exp(-jnp.abs(logit)))) / GLA_TAU
        tri = keep.astype(BF16)
        b = sum(jnp.dot(tri, piece, preferred_element_type=F32) for piece in _split_bf16(g, 3))
        edge = c - 1 if fwd else 0
        tots = [b[ch * c + edge:ch * c + edge + 1, :] for ch in range(nch)]
        totb = jnp.concatenate([jnp.broadcast_to(t, (c, t.shape[1])) for t in tots], axis=0)
        q_in = (q_ref[...].astype(F32) * (GLA_DK ** -0.5) * jnp.exp(b)).astype(BF16)
        k = k_ref[...].astype(F32)
        k_in = (k * jnp.exp(-b)).astype(BF16)
        k_st = (k * jnp.exp(totb - b)).astype(BF16)
        srow = pl.multiple_of(blk * GLA_BLK, GLA_BLK)
        for h in range(GLA_HPS):
            ks = slice(h * GLA_DK, (h + 1) * GLA_DK)
            vs = slice(h * GLA_DV, (h + 1) * GLA_DV)
            v = v_ref[:, vs].astype(BF16)
            a = jnp.where(keep, _nt_dot(q_in[:, ks], k_in[:, ks]), 0.0).astype(BF16)
            o_intra = jnp.dot(a, v, preferred_element_type=F32)
            st = state_t[h]
            o_inter = [None] * nch
            for cc in range(nch):
                ch = cc if fwd else nch - 1 - cc
                rows = slice(ch * c, (ch + 1) * c)
                o_inter[ch] = _nt_dot(q_in[rows, ks], st.astype(BF16))
                st = st * jnp.exp(tots[ch][:, ks]) + _tn_dot(v[rows, :], k_st[rows, ks])
            state_t[h] = st
            o = o_intra + jnp.concatenate(o_inter, axis=0)
            if fwd:
                of_scr[pl.ds(srow, GLA_BLK), vs] = o
            else:
                o = o + of_scr[pl.ds(srow, GLA_BLK), vs]
                o = o * lax.rsqrt(jnp.mean(o * o, axis=-1, keepdims=True) + EPS) * gg_ref[:, vs]
                o_ref[:, vs] = (o * _silu(r_ref[:, vs].astype(F32))).astype(BF16)

    @pl.when(i < nblk)
    def _():
        run(True)

    @pl.when(i >= nblk)
    def _():
        run(False)

    @pl.when(i == nblk - 1)
    def _():
        for h in range(GLA_HPS):
            stf_ref[0, h] = state_t[h].T

    @pl.when(i == 2 * nblk - 1)
    def _():
        for h in range(GLA_HPS):
            stb_ref[0, h] = state_t[h].T


def _gla(p, dec, bsz, seq, wdf, bdf, wdb, bdb, g_gla, s0f, s0b):
    n = p.shape[0]
    nblk = seq // GLA_BLK
    kw = GLA_HPS * GLA_DK
    vw = GLA_HPS * GLA_DV

    def rb(b, i):
        return b * nblk + jnp.where(i < nblk, i, 2 * nblk - 1 - i)

    def orb(b, i):
        return b * nblk + jnp.where(i < nblk, nblk - 1, 2 * nblk - 1 - i)

    st_spec = pl.BlockSpec((1, GLA_HPS, GLA_DK, GLA_DV), lambda b, h, i: (b, h, 0, 0))
    st_shape = jax.ShapeDtypeStruct((bsz, GLA_HEADS, GLA_DK, GLA_DV), F32)
    return pl.pallas_call(
        functools.partial(_gla_kernel, nblk=nblk),
        grid=(bsz, GLA_HEADS // GLA_HPS, 2 * nblk),
        in_specs=[pl.BlockSpec((GLA_BLK, kw), lambda b, h, i: (rb(b, i), COL_GQ // kw + h)),
                  pl.BlockSpec((GLA_BLK, kw), lambda b, h, i: (rb(b, i), COL_GK // kw + h)),
                  pl.BlockSpec((GLA_BLK, vw), lambda b, h, i: (rb(b, i), COL_GV // vw + h)),
                  pl.BlockSpec((GLA_BLK, vw), lambda b, h, i: (rb(b, i), COL_GR // vw + h)),
                  pl.BlockSpec((GLA_BLK, LANES), lambda b, h, i: (rb(b, i), (COL_DEC - TAIL_COL0) // LANES)),
                  pl.BlockSpec((LANES, kw), lambda b, h, i: (0, h)),
                  pl.BlockSpec((1, kw), lambda b, h, i: (0, h)),
                  pl.BlockSpec((LANES, kw), lambda b, h, i: (0, h)),
                  pl.BlockSpec((1, kw), lambda b, h, i: (0, h)),
                  pl.BlockSpec((1, vw), lambda b, h, i: (0, h)),
                  st_spec, st_spec],
        out_specs=[pl.BlockSpec((GLA_BLK, vw), lambda b, h, i: (orb(b, i), h)), st_spec, st_spec],
        out_shape=[jax.ShapeDtypeStruct((n, GLA_VW), BF16), st_shape, st_shape],
        scratch_shapes=[pltpu.VMEM((GLA_HPS, GLA_DV, GLA_DK), F32), pltpu.VMEM((seq, vw), F32)],
        compiler_params=_cparams(("parallel", "parallel", "arbitrary")),
        name="gla",
    )(p, p, p, p, dec, wdf, bdf, wdb, bdb, g_gla, s0f, s0b)


def _sink_column(sink_ref, kvh, rows_per_head):
    r = lax.broadcasted_iota(jnp.int32, (SWA_GROUP * rows_per_head, 1), 0)
    col = jnp.zeros((SWA_GROUP * rows_per_head, 1), F32)
    for g in range(SWA_GROUP):
        col = jnp.where(r // rows_per_head == g, sink_ref[kvh * SWA_GROUP + g], col)
    return col


def _rope(x, cos, sin):
    outs = []
    for cb in range(x.shape[1] // LANES):
        sl = slice(cb * LANES, (cb + 1) * LANES)
        xc = x[:, sl]
        lane = lax.broadcasted_iota(jnp.int32, xc.shape, 1)
        sw = jnp.where(lane % 32 < 16, pltpu.roll(xc, LANES - 16, 1), pltpu.roll(xc, 16, 1))
        outs.append(xc * cos[:, sl] + sw * sin[:, sl])
    return jnp.concatenate(outs, axis=1)


def _swa_lat_kernel(sink_ref, q_ref, kp_ref, kc_ref, kn_ref, vp_ref, vc_ref, vn_ref, ck_ref, cv_ref,
                    cp_ref, cc_ref, cn_ref, sp_ref, sc_ref, sn_ref, o_ref, *, nb):
    n = pl.program_id(1)
    blk = SWA_BLOCK
    hd = SWA_HEAD_DIM
    cos_c, sin_c = cc_ref[...], sc_ref[...]
    kband = jnp.concatenate([_rope(kp_ref[...].astype(F32), cp_ref[...], sp_ref[...]),
                             _rope(kc_ref[...].astype(F32), cos_c, sin_c),
                             _rope(kn_ref[...].astype(F32), cn_ref[...], sn_ref[...])], axis=0).astype(BF16)
    vband = jnp.concatenate([vp_ref[...], vc_ref[...], vn_ref[...]], axis=0).astype(BF16)
    ck = ck_ref[0].astype(BF16)
    cv = cv_ref[0].astype(BF16)
    qi = lax.broadcasted_iota(jnp.int32, (SWA_GROUP * blk, 3 * blk), 0) % blk
    kj = lax.broadcasted_iota(jnp.int32, (SWA_GROUP * blk, 3 * blk), 1)
    k_abs = kj + (n - 1) * blk
    mask = (jnp.abs(kj - blk - qi) <= SWA_BLOCK) & (k_abs >= 0) & (k_abs < nb * blk)
    for kvh in range(SWA_KV_HEADS):
        ks = slice(kvh * hd, (kvh + 1) * hd)
        qr = _rope(q_ref[:, kvh * SWA_KVW:(kvh + 1) * SWA_KVW].astype(F32), cos_c, sin_c) * (hd ** -0.5)
        qg = jnp.concatenate([qr[:, g * hd:(g + 1) * hd] for g in range(SWA_GROUP)],
                             axis=0).astype(BF16)
        s_b = jnp.where(mask, _nt_dot(qg, kband[:, ks]), -1e30)
        s_c = _nt_dot(qg, ck[:, ks])
        sink = _sink_column(sink_ref, kvh, blk)
        m = jnp.maximum(jnp.maximum(jnp.max(s_b, axis=-1, keepdims=True),
                                    jnp.max(s_c, axis=-1, keepdims=True)), sink)
        p_b = jnp.exp(s_b - m)
        p_c = jnp.exp(s_c - m)
        den = jnp.exp(sink - m) + jnp.sum(p_b, axis=-1, keepdims=True) + jnp.sum(p_c, axis=-1, keepdims=True)
        o = (jnp.dot(p_b.astype(BF16), vband[:, ks], preferred_element_type=F32)
             + jnp.dot(p_c.astype(BF16), cv[:, ks], preferred_element_type=F32)) / den
        for g in range(SWA_GROUP):
            cb = (kvh * SWA_GROUP + g) * hd
            o_ref[:, cb:cb + hd] = o[g * blk:(g + 1) * blk, :].astype(BF16)


def _swa_latent(p, bsz, seq, cache_k, cache_v, sink, cos_t, sin_t):
    n = p.shape[0]
    nb = seq // SWA_BLOCK
    kcol = COL_SK // SWA_KVW
    vcol = COL_SV // SWA_KVW
    prev = lambda i: jnp.maximum(i - 1, 0)
    nxt = lambda i: jnp.minimum(i + 1, nb - 1)

    def pspec(col, f):
        return pl.BlockSpec((SWA_BLOCK, SWA_KVW), lambda b, i, s: (b * nb + f(i), col))

    def tspec(f):
        return pl.BlockSpec((SWA_BLOCK, SWA_KVW), lambda b, i, s: (f(i), 0))

    same = lambda i: i
    cspec = pl.BlockSpec((1, cache_k.shape[1], SWA_KVW), lambda b, i, s: (b, 0, 0))
    grid_spec = pltpu.PrefetchScalarGridSpec(
        num_scalar_prefetch=1,
        grid=(bsz, nb),
        in_specs=[pl.BlockSpec((SWA_BLOCK, SWA_QW), lambda b, i, s: (b * nb + i, COL_SQ // SWA_QW)),
                  pspec(kcol, prev), pspec(kcol, same), pspec(kcol, nxt),
                  pspec(vcol, prev), pspec(vcol, same), pspec(vcol, nxt),
                  cspec, cspec,
                  tspec(prev), tspec(same), tspec(nxt), tspec(prev), tspec(same), tspec(nxt)],
        out_specs=pl.BlockSpec((SWA_BLOCK, SWA_QW), lambda b, i, s: (b * nb + i, 0)),
    )
    return pl.pallas_call(
        functools.partial(_swa_lat_kernel, nb=nb),
        grid_spec=grid_spec,
        out_shape=jax.ShapeDtypeStruct((n, SWA_QW), BF16),
        compiler_params=_cparams(("parallel", "arbitrary")),
        name="swa_latent",
    )(sink, p, p, p, p, p, p, p, cache_k, cache_v, cos_t, cos_t, cos_t, sin_t, sin_t, sin_t)


def _swa_ctx_kernel(sink_ref, q_ref, k_ref, v_ref, o_ref):
    s = q_ref.shape[0]
    hd = SWA_HEAD_DIM
    kb = k_ref[...].astype(BF16)
    vb = v_ref[...].astype(BF16)
    for kvh in range(SWA_KV_HEADS):
        ks = slice(kvh * hd, (kvh + 1) * hd)
        qg = jnp.concatenate([q_ref[:, (kvh * SWA_GROUP + g) * hd:(kvh * SWA_GROUP + g + 1) * hd]
                              for g in range(SWA_GROUP)], axis=0)
        qg = (qg.astype(F32) * (hd ** -0.5)).astype(BF16)
        sc = _nt_dot(qg, kb[:, ks])
        sink = _sink_column(sink_ref, kvh, s)
        m = jnp.maximum(jnp.max(sc, axis=-1, keepdims=True), sink)
        pr = jnp.exp(sc - m)
        den = jnp.exp(sink - m) + jnp.sum(pr, axis=-1, keepdims=True)
        o = jnp.dot(pr.astype(BF16), vb[:, ks], preferred_element_type=F32) / den
        for g in range(SWA_GROUP):
            cb = (kvh * SWA_GROUP + g) * hd
            o_ref[:, cb:cb + hd] = o[g * s:(g + 1) * s, :].astype(BF16)


def _swa_context(p, bsz, seq, sink):
    n = p.shape[0]
    grid_spec = pltpu.PrefetchScalarGridSpec(
        num_scalar_prefetch=1,
        grid=(bsz,),
        in_specs=[pl.BlockSpec((seq, SWA_QW), lambda b, s: (b, COL_SQ // SWA_QW)),
                  pl.BlockSpec((seq, SWA_KVW), lambda b, s: (b, COL_SK // SWA_KVW)),
                  pl.BlockSpec((seq, SWA_KVW), lambda b, s: (b, COL_SV // SWA_KVW))],
        out_specs=pl.BlockSpec((seq, SWA_QW), lambda b, s: (b, 0)),
    )
    return pl.pallas_call(
        _swa_ctx_kernel,
        grid_spec=grid_spec,
        out_shape=jax.ShapeDtypeStruct((n, SWA_QW), BF16),
        compiler_params=_cparams(("parallel",)),
        name="swa_context",
    )(sink, p, p, p)


MERGE_TM = 512


def _branch_kernel(og_ref, os_ref, ga_ref, gb_ref, wa_ref, wb_ref, m_ref):
    a = jnp.dot(og_ref[...], wa_ref[...], preferred_element_type=F32)
    b = jnp.dot(os_ref[...], wb_ref[...], preferred_element_type=F32)
    m_ref[...] = (_sigmoid(ga_ref[...].astype(F32)) * a + _sigmoid(gb_ref[...].astype(F32)) * b).astype(BF16)


def _branch_merge(o_gla, o_swa, p, wa, wb):
    n = o_gla.shape[0]
    tm = MERGE_TM
    once = pl.Buffered(1)
    return pl.pallas_call(
        _branch_kernel,
        grid=(n // tm,),
        in_specs=[pl.BlockSpec((tm, GLA_VW), lambda i: (i, 0)),
                  pl.BlockSpec((tm, SWA_QW), lambda i: (i, 0)),
                  pl.BlockSpec((tm, D_MODEL), lambda i: (i, COL_GA // D_MODEL)),
                  pl.BlockSpec((tm, D_MODEL), lambda i: (i, COL_GB // D_MODEL)),
                  pl.BlockSpec((GLA_VW, D_MODEL), lambda i: (0, 0), pipeline_mode=once),
                  pl.BlockSpec((SWA_QW, D_MODEL), lambda i: (0, 0), pipeline_mode=once)],
        out_specs=pl.BlockSpec((tm, D_MODEL), lambda i: (i, 0)),
        out_shape=jax.ShapeDtypeStruct((n, D_MODEL), BF16),
        compiler_params=_cparams(("parallel",)),
        name="branch_merge",
    )(o_gla, o_swa, p, p, wa, wb)


def _out_kernel(m_ref, x_ref, gt1_ref, sc2_ref, sh2_ref, g2_ref, wo_ref, wr2_ref, wr1_ref,
                x1_ref, h2_ref, lg_ref):
    x1 = x_ref[...] + gt1_ref[0] * jnp.dot(m_ref[...], wo_ref[...], preferred_element_type=F32)
    x1_ref[...] = x1
    y = x1 * lax.rsqrt(jnp.mean(x1 * x1, axis=-1, keepdims=True) + EPS) * g2_ref[...]
    h2 = y * (1.0 + sc2_ref[0]) + sh2_ref[0]
    _store_row_major(h2_ref, h2)
    h_hi, h_lo = _split_bf16(h2, 2)
    l2 = (jnp.dot(h_hi, wr2_ref[...], preferred_element_type=F32)
          + jnp.dot(h_lo, wr1_ref[...], preferred_element_type=F32))
    lt = l2.T
    lt = lt[0:N_EXPERTS, :] + lt[N_EXPERTS:2 * N_EXPERTS, :]
    for cb in range(lt.shape[1] // LANES):
        lg_ref[cb] = lt[:, cb * LANES:(cb + 1) * LANES]


def _out_proj(merged, x2d, seq, gt1, sc2, sh2, g2, wo, wr2, wr1):
    n = x2d.shape[0]
    tm = MERGE_TM
    per_b = _tiles_per_mod_row(gt1, n, seq, tm)
    mod_spec = pl.BlockSpec((1, 1, D_MODEL), lambda i: (i // per_b, 0, 0))
    once = pl.Buffered(1)
    return pl.pallas_call(
        _out_kernel,
        grid=(n // tm,),
        in_specs=[pl.BlockSpec((tm, D_MODEL), lambda i: (i, 0)),
                  pl.BlockSpec((tm, D_MODEL), lambda i: (i, 0)),
                  mod_spec, mod_spec, mod_spec,
                  pl.BlockSpec((1, D_MODEL), lambda i: (0, 0)),
                  pl.BlockSpec((D_MODEL, D_MODEL), lambda i: (0, 0), pipeline_mode=once),
                  pl.BlockSpec((D_MODEL, LANES), lambda i: (0, 0)),
                  pl.BlockSpec((D_MODEL, LANES), lambda i: (0, 0))],
        out_specs=[pl.BlockSpec((tm, D_MODEL), lambda i: (i, 0)),
                   pl.BlockSpec((tm, ROW_CHUNKS, LANES), lambda i: (i, 0, 0)),
                   pl.BlockSpec((tm // LANES, N_EXPERTS, LANES), lambda i: (i, 0, 0))],
        out_shape=[jax.ShapeDtypeStruct((n, D_MODEL), F32),
                   jax.ShapeDtypeStruct((n, ROW_CHUNKS, LANES), F32),
                   jax.ShapeDtypeStruct((n // LANES, N_EXPERTS, LANES), F32)],
        compiler_params=_cparams(("parallel",)),
        name="out_proj",
    )(merged, x2d, gt1, sc2, sh2, g2, wo, wr2, wr1)


def _select_kernel(lg_ref, aff_ref, sel_ref, pos_ref, *, cap):
    nblk = lg_ref.shape[0]
    lg = lg_ref[...]
    ex = jnp.exp(lg - jnp.max(lg, axis=1, keepdims=True))
    aff = ex / jnp.sum(ex, axis=1, keepdims=True)
    aff_ref[...] = aff
    bits = lax.bitcast_convert_type(aff, jnp.int32)

    def count(pred):
        c = jnp.sum(jnp.where(pred, 1.0, 0.0), axis=0, keepdims=True)
        return jnp.sum(c, axis=2, keepdims=True)

    def bit_step(t, cur):
        cand = cur | jnp.left_shift(jnp.int32(1), 30 - t)
        return jnp.where(count(bits >= cand) >= cap, cand, cur)

    thr = lax.fori_loop(0, 31, bit_step, jnp.zeros((1, N_EXPERTS, 1), jnp.int32))
    need = (cap - count(bits > thr))[0]
    thr2 = thr[0]
    ri = lax.broadcasted_iota(jnp.int32, (LANES, LANES), 0)
    ci = lax.broadcasted_iota(jnp.int32, (LANES, LANES), 1)
    upper = (ri <= ci).astype(BF16)

    def blk_step(b, carry):
        run_eq, run_sel = carry
        bb = lax.bitcast_convert_type(aff_ref[b], jnp.int32)
        eq = (bb == thr2).astype(F32)
        eq_excl = jnp.dot(eq.astype(BF16), upper, preferred_element_type=F32) - eq + run_eq
        sel = jnp.where((bb > thr2) | ((eq > 0) & (eq_excl < need)), 1.0, 0.0)
        sel_ref[b] = sel
        pos_ref[b] = jnp.dot(sel.astype(BF16), upper, preferred_element_type=F32) - sel + run_sel
        return (run_eq + jnp.sum(eq, axis=1, keepdims=True),
                run_sel + jnp.sum(sel, axis=1, keepdims=True))

    zero = jnp.zeros((N_EXPERTS, 1), F32)
    lax.fori_loop(0, nblk, blk_step, (zero, zero))


def _select(logits3, cap):
    shp = jax.ShapeDtypeStruct(logits3.shape, F32)
    return pl.pallas_call(
        functools.partial(_select_kernel, cap=float(cap)),
        out_shape=[shp, shp, shp],
        compiler_params=pltpu.CompilerParams(vmem_limit_bytes=VMEM_LIMIT),
        name="select",
    )(logits3)


COMPACT_WIN = 136


def _compact_kernel(base_ref, aff_ref, sel_ref, pos_ref, idx_ref, gate_ref):
    nblk = aff_ref.shape[0]
    idx_ref[...] = jnp.zeros_like(idx_ref)
    gate_ref[...] = jnp.zeros_like(gate_ref)
    slot = lax.broadcasted_iota(jnp.int32, (COMPACT_WIN, LANES), 0).astype(F32)
    lane = lax.broadcasted_iota(jnp.int32, (COMPACT_WIN, LANES), 1)

    def blk_step(b, _):
        aff = aff_ref[b]
        sel = sel_ref[b]
        pos = pos_ref[b]
        tok = (lane + b * LANES).astype(F32)
        for e in range(N_EXPERTS):
            base8 = pl.multiple_of((base_ref[b * N_EXPERTS + e] >> 3) << 3, 8)
            rel = pos[e:e + 1, :] - base8.astype(F32)
            hit = (rel == slot) & (sel[e:e + 1, :] > 0)
            rows = pl.ds(base8, COMPACT_WIN)
            mine = lane == e
            ic = jnp.sum(jnp.where(hit, tok, 0.0), axis=1, keepdims=True)
            gc = jnp.sum(jnp.where(hit, aff[e:e + 1, :], 0.0), axis=1, keepdims=True)
            idx_ref[rows, :] += jnp.where(mine, ic, 0.0)
            gate_ref[rows, :] += jnp.where(mine, gc, 0.0)
        return 0

    lax.fori_loop(0, nblk, blk_step, 0)


def _compact(base, aff3, sel3, pos3, cap):
    cp = cap + 2 * LANES
    full = pl.BlockSpec(aff3.shape, lambda i, s: (0, 0, 0))
    ospec = pl.BlockSpec((cp, LANES), lambda i, s: (0, 0))
    oshape = jax.ShapeDtypeStruct((cp, LANES), F32)
    grid_spec = pltpu.PrefetchScalarGridSpec(
        num_scalar_prefetch=1, grid=(1,), in_specs=[full, full, full], out_specs=[ospec, ospec])
    return pl.pallas_call(
        _compact_kernel,
        grid_spec=grid_spec,
        out_shape=[oshape, oshape],
        compiler_params=_cparams(("arbitrary",)),
        name="compact",
    )(base, aff3, sel3, pos3)


FFN_COLS = 256


def _up_kernel(idx_ref, h2_hbm, wg_ref, wu_ref, hid_ref, gbuf, xs, sem, *, tm, total):
    step = pl.program_id(0) * pl.num_programs(1) + pl.program_id(1)

    def row_copy(step_, i):
        return pltpu.make_async_copy(h2_hbm.at[idx_ref[step_ * tm + i]], gbuf.at[i], sem)

    def wait_rows():
        pltpu.make_async_copy(h2_hbm.at[pl.ds(0, tm)], gbuf, sem).wait()

    @pl.when(step == 0)
    def _():
        def body(i, _):
            row_copy(step, i).start()
            return 0
        lax.fori_loop(0, tm, body, 0)

    wait_rows()
    slab = min(SLAB, tm)
    for tb in range(tm // slab):
        rows = slice(tb * slab, (tb + 1) * slab)
        for s, chunk in enumerate(_load_row_major(gbuf, rows)):
            xs[rows, s * LANES:(s + 1) * LANES] = chunk.astype(BF16)

    nxt = jnp.minimum(step + 1, total - 1)
    for i in range(tm):
        row_copy(nxt, i).start()

    for f in range(EXPERT_FF // FFN_COLS):
        cols = slice(f * FFN_COLS, (f + 1) * FFN_COLS)
        hg = jnp.dot(xs[...], wg_ref[0, :, cols].astype(BF16), preferred_element_type=F32)
        hu = jnp.dot(xs[...], wu_ref[0, :, cols].astype(BF16), preferred_element_type=F32)
        hid_ref[:, cols] = (_silu(hg) * hu).astype(BF16)

    @pl.when(step == total - 1)
    def _():
        wait_rows()


def _expert_up(idx_flat, h2_rm, wg, wu, cap):
    tm = min(1024, cap)
    nr = cap // tm
    grid_spec = pltpu.PrefetchScalarGridSpec(
        num_scalar_prefetch=1,
        grid=(N_EXPERTS, nr),
        in_specs=[pl.BlockSpec(memory_space=pl.ANY),
                  pl.BlockSpec((1, D_MODEL, EXPERT_FF), lambda e, r, s: (e, 0, 0)),
                  pl.BlockSpec((1, D_MODEL, EXPERT_FF), lambda e, r, s: (e, 0, 0))],
        out_specs=pl.BlockSpec((tm, EXPERT_FF), lambda e, r, s: (e * nr + r, 0)),
        scratch_shapes=[pltpu.VMEM((tm, ROW_CHUNKS, LANES), F32),
                        pltpu.VMEM((tm, D_MODEL), BF16),
                        pltpu.SemaphoreType.DMA(())],
    )
    return pl.pallas_call(
        functools.partial(_up_kernel, tm=tm, total=N_EXPERTS * nr),
        grid_spec=grid_spec,
        out_shape=jax.ShapeDtypeStruct((N_EXPERTS * cap, EXPERT_FF), BF16),
        compiler_params=_cparams(("arbitrary", "arbitrary")),
        name="expert_up",
    )(idx_flat, h2_rm, wg, wu)


def _down_kernel(hid_ref, wd_ref, o_ref, ybuf):
    for c in range(D_MODEL // FFN_COLS):
        cols = slice(c * FFN_COLS, (c + 1) * FFN_COLS)
        ybuf[:, cols] = jnp.dot(hid_ref[...], wd_ref[0, :, cols].astype(BF16), preferred_element_type=F32)
    tm = ybuf.shape[0]
    slab = min(SLAB, tm)
    for tb in range(tm // slab):
        rows = slice(tb * slab, (tb + 1) * slab)
        parts = jnp.stack([ybuf[rows, s * LANES:(s + 1) * LANES] for s in range(ROW_CHUNKS)], axis=0)
        o_ref[rows] = pltpu.einshape("stl->tsl", parts)


def _expert_down(hid, wd, cap):
    tm = min(1024, cap)
    nr = cap // tm
    return pl.pallas_call(
        _down_kernel,
        grid=(N_EXPERTS, nr),
        in_specs=[pl.BlockSpec((tm, EXPERT_FF), lambda e, r: (e * nr + r, 0)),
                  pl.BlockSpec((1, EXPERT_FF, D_MODEL), lambda e, r: (e, 0, 0))],
        out_specs=pl.BlockSpec((tm, ROW_CHUNKS, LANES), lambda e, r: (e * nr + r, 0, 0)),
        out_shape=jax.ShapeDtypeStruct((N_EXPERTS * cap, ROW_CHUNKS, LANES), F32),
        scratch_shapes=[pltpu.VMEM((tm, D_MODEL), F32)],
        compiler_params=_cparams(("parallel", "arbitrary")),
        name="expert_down",
    )(hid, wd)


COMBINE_UNROLL = 4
COMBINE_CHUNK = 32
COMBINE_NCHUNK = 3


def _combine_kernel(idx_ref, gate_ref, lo_ref, ye_hbm, x1_ref, gt2_ref, gf_ref, o_ref, stage, acc, sem,
                    *, tt, ch, cap, ntiles):
    tile = pl.program_id(0)
    slot = tile % 2

    window = COMBINE_NCHUNK * ch

    def chunk_start(first_row):
        return jnp.minimum(first_row, cap - window)

    def chunk_copy(slot_, e, start, c):
        return pltpu.make_async_copy(ye_hbm.at[pl.ds(e * cap + start + c * ch, ch)],
                                     stage.at[slot_, e, pl.ds(c * ch, ch)],
                                     sem.at[slot_, e])

    def for_needed_chunks(tile_, e, fn):
        lo = lo_ref[tile_ * N_EXPERTS + e]
        hi = lo_ref[(tile_ + 1) * N_EXPERTS + e]
        start = chunk_start(lo)
        for c in range(COMBINE_NCHUNK):
            @pl.when((start + c * ch < hi) & (start + (c + 1) * ch > lo))
            def _():
                fn(start, c)

    def issue(tile_, slot_):
        for e in range(N_EXPERTS):
            for_needed_chunks(tile_, e, lambda start, c: chunk_copy(slot_, e, start, c).start())

    @pl.when(tile == 0)
    def _():
        issue(tile, slot)

    @pl.when(tile + 1 < ntiles)
    def _():
        issue(tile + 1, 1 - slot)

    acc[...] = jnp.zeros_like(acc)

    def add_rows(e, start, r0, r1):
        def group(r, width):
            dsts, vals = [], []
            for u in range(width):
                dst = idx_ref[e * cap + r + u] - tile * tt
                dsts.append(dst)
                vals.append(acc[dst] + gate_ref[e * cap + r + u] * stage[slot, e, r + u - start])
            for dst, val in zip(dsts, vals):
                acc[dst] = val

        ngroups = (r1 - r0) // COMBINE_UNROLL

        def body_group(j, _):
            group(r0 + j * COMBINE_UNROLL, COMBINE_UNROLL)
            return 0
        lax.fori_loop(0, ngroups, body_group, 0)

        def body_one(r, _):
            group(r, 1)
            return 0
        lax.fori_loop(r0 + ngroups * COMBINE_UNROLL, r1, body_one, 0)

    for e in range(N_EXPERTS):
        lo = lo_ref[tile * N_EXPERTS + e]
        hi = lo_ref[(tile + 1) * N_EXPERTS + e]
        start = chunk_start(lo)
        for_needed_chunks(tile, e, lambda start_, c: chunk_copy(slot, e, start_, c).wait())
        first_end = jnp.minimum(hi, start + window)
        add_rows(e, start, lo, first_end)

        def more(r0):
            st = jnp.minimum(r0, cap - ch)
            cp = chunk_copy(slot, e, st, 0)
            cp.start()
            cp.wait()
            r1 = jnp.minimum(hi, st + ch)
            add_rows(e, st, r0, r1)
            return r1
        lax.while_loop(lambda r0: r0 < hi, more, first_end)

    for tb in range(tt // SLAB):
        rows = slice(tb * SLAB, (tb + 1) * SLAB)
        ssq = jnp.zeros((SLAB, 1), F32)
        for s, y in enumerate(_load_row_major(acc, rows)):
            cols = slice(s * LANES, (s + 1) * LANES)
            x2 = x1_ref[rows, cols] + gt2_ref[0][:, cols] * y
            o_ref[rows, cols] = x2
            ssq = ssq + jnp.sum(x2 * x2, axis=-1, keepdims=True)
        o_ref[rows, :] = o_ref[rows, :] * lax.rsqrt(ssq / D_MODEL + EPS) * gf_ref[...]


def _combine(idx_flat, gate_flat, lo_tab, ye_lin, x1, seq, gt2, g_final, cap, tt):
    n = x1.shape[0]
    ch = min(COMBINE_CHUNK, cap // COMBINE_NCHUNK)
    ntiles = n // tt
    per_b = _tiles_per_mod_row(gt2, n, seq, tt)
    grid_spec = pltpu.PrefetchScalarGridSpec(
        num_scalar_prefetch=3,
        grid=(ntiles,),
        in_specs=[pl.BlockSpec(memory_space=pl.ANY),
                  pl.BlockSpec((tt, D_MODEL), lambda i, a, b, c: (i, 0)),
                  pl.BlockSpec((1, 1, D_MODEL), lambda i, a, b, c: (i // per_b, 0, 0)),
                  pl.BlockSpec((1, D_MODEL), lambda i, a, b, c: (0, 0))],
        out_specs=pl.BlockSpec((tt, D_MODEL), lambda i, a, b, c: (i, 0)),
        scratch_shapes=[pltpu.VMEM((2, N_EXPERTS, COMBINE_NCHUNK * ch, ROW_CHUNKS, LANES), F32),
                        pltpu.VMEM((tt, ROW_CHUNKS, LANES), F32),
                        pltpu.SemaphoreType.DMA((2, N_EXPERTS))],
    )
    return pl.pallas_call(
        functools.partial(_combine_kernel, tt=tt, ch=ch, cap=cap, ntiles=ntiles),
        grid_spec=grid_spec,
        out_shape=jax.ShapeDtypeStruct((n, D_MODEL), F32),
        compiler_params=_cparams(("arbitrary",)),
        name="combine",
    )(idx_flat, gate_flat, lo_tab, ye_lin, x1, gt2, g_final)


def _rope_tables(seq):
    pos = jnp.arange(seq)
    row = (pos // GRID_W).astype(F32)
    col = (pos % GRID_W).astype(F32)
    npair = SWA_HEAD_DIM // 4
    inv_freq = ROPE_BASE ** (-jnp.arange(npair, dtype=F32) / npair)
    ar = row[:, None] * inv_freq[None, :]
    ac = col[:, None] * inv_freq[None, :]
    cos = jnp.concatenate([jnp.cos(ar), jnp.cos(ar), jnp.cos(ac), jnp.cos(ac)], axis=1)
    sin = jnp.concatenate([-jnp.sin(ar), jnp.sin(ar), -jnp.sin(ac), jnp.sin(ac)], axis=1)
    reps = SWA_KVW // SWA_HEAD_DIM
    return jnp.tile(cos, (1, reps)), jnp.tile(sin, (1, reps))


def _layer(x, mods, wts, latent, s0f, s0b, cache_k, cache_v):
    bsz, seq, _ = x.shape
    n = bsz * seq
    x2d = x.reshape(n, D_MODEL)
    sh1, sc1, gt1, sh2, sc2, gt2 = mods
    p, tail = _in_proj(x2d, seq, sc1, sh1, wts["g1"], wts["w_in_r"])
    o_gla, st_f, st_b = _gla(p, tail, bsz, seq, wts["wdf"], wts["bdf"], wts["wdb"], wts["bdb"], wts["g_gla"],
                             s0f, s0b)
    if latent:
        cos_t, sin_t = _rope_tables(seq)
        o_swa = _swa_latent(p, bsz, seq, cache_k, cache_v, wts["sink"], cos_t, sin_t)
    else:
        o_swa = _swa_context(p, bsz, seq, wts["sink"])
    merged = _branch_merge(o_gla, o_swa, p, wts["wa"], wts["wb"])
    x1, h2, logits3 = _out_proj(merged, x2d, seq, gt1, sc2, sh2, wts["g2"], wts["wo"], wts["wr2"], wts["wr1"])
    cap = CAPACITY_FACTOR * n // N_EXPERTS
    aff3, sel3, pos3 = _select(logits3, cap)
    base = jnp.concatenate([pos3[:, :, 0].astype(jnp.int32), jnp.full((1, N_EXPERTS), cap, jnp.int32)], axis=0)
    idx_c, gate_c = _compact(base.reshape(-1), aff3, sel3, pos3, cap)
    idx_flat = idx_c[:cap, :N_EXPERTS].T.astype(jnp.int32).reshape(-1)
    gate_flat = gate_c[:cap, :N_EXPERTS].T.reshape(-1)
    hid = _expert_up(idx_flat, h2, wts["wg"], wts["wu"], cap)
    ye = _expert_down(hid, wts["wd"], cap)
    tt = 256
    lo_tab = base[::tt // LANES]
    y = _combine(idx_flat, gate_flat, lo_tab.reshape(-1), ye, x1, seq, gt2, wts["g_final"], cap, tt)
    return y.reshape(bsz, seq, D_MODEL), tail, st_f, st_b


def kernel(x_prompt, x_sample, state_gla_fwd, state_gla_bwd, cache_k, cache_v, c, c_ctx, w_mod, b_mod,
           g_norm1, w_in, w_dec_f, b_dec_f, w_dec_b, b_dec_b, g_gla, attn_sink, w_branch_a, w_branch_b,
           w_out, g_norm2, w_router, w_exp_gate, w_exp_up, w_exp_down, g_final):
    bp, sp, _ = x_prompt.shape
    bl = x_sample.shape[0]
    l = 0
    gla_end = 2 * GLA_KW + 2 * GLA_VW
    dec_end = gla_end + 2 * GLA_RANK
    swa_end = dec_end + SWA_QW + 2 * SWA_KVW
    w = w_in[l]
    w_in_r = jnp.concatenate(
        [w[:, swa_end:].astype(BF16), w[:, :gla_end].astype(BF16), w[:, dec_end:swa_end].astype(BF16),
         w[:, gla_end:dec_end].astype(BF16), jnp.zeros((D_MODEL, IN_PAD - w.shape[1]), BF16)], axis=1)
    zpad = jnp.zeros((LANES - 2 * GLA_RANK, GLA_KW), F32)
    wr_hi = w_router[l].astype(BF16)
    wr_lo = (w_router[l] - wr_hi.astype(F32)).astype(BF16)
    wr_pad = jnp.zeros((D_MODEL, LANES - 2 * N_EXPERTS), BF16)
    zr = jnp.zeros((GLA_RANK, GLA_KW), F32)
    wts = {
        "g1": g_norm1[l].reshape(1, D_MODEL),
        "w_in_r": w_in_r,
        "wdf": jnp.concatenate([w_dec_f[l], zr, zpad], axis=0),
        "wdb": jnp.concatenate([zr, w_dec_b[l], zpad], axis=0),
        "bdf": b_dec_f[l].reshape(1, GLA_KW),
        "bdb": b_dec_b[l].reshape(1, GLA_KW),
        "g_gla": g_gla[l].reshape(1, GLA_VW),
        "sink": attn_sink[l],
        "wa": w_branch_a[l].astype(BF16),
        "wb": w_branch_b[l].astype(BF16),
        "wo": w_out[l].astype(BF16),
        "g2": g_norm2[l].reshape(1, D_MODEL),
        "wr2": jnp.concatenate([wr_hi, wr_lo, wr_pad], axis=1),
        "wr1": jnp.concatenate([wr_hi, jnp.zeros_like(wr_lo), wr_pad], axis=1),
        "wg": w_exp_gate[l],
        "wu": w_exp_up[l],
        "wd": w_exp_down[l],
        "g_final": g_final.reshape(1, D_MODEL),
    }
    cond8 = jnp.concatenate([c_ctx[None, :], c, jnp.zeros((8 - 1 - bl, D_MODEL), F32)], axis=0)
    mod = _modulation(cond8, w_mod[l], b_mod[l]).reshape(8, N_MOD, 1, D_MODEL)
    mods_ctx = tuple(mod[0:1, j] for j in range(N_MOD))
    mods_lat = tuple(mod[1:1 + bl, j] for j in range(N_MOD))

    zero_state = jnp.zeros((bp, GLA_HEADS, GLA_DK, GLA_DV), F32)
    y_prompt, tail_ctx, st_f, st_b = _layer(x_prompt, mods_ctx, wts, False, zero_state, zero_state, None, None)
    ck = cache_k[:, l].reshape(bl, -1, SWA_KVW)
    cv = cache_v[:, l].reshape(bl, -1, SWA_KVW)
    y_sample, _, _, _ = _layer(x_sample, mods_lat, wts, True, state_gla_fwd[:, l], state_gla_bwd[:, l], ck, cv)

    ksl = slice(COL_SK - TAIL_COL0, COL_SK - TAIL_COL0 + SWA_KVW)
    vsl = slice(COL_SV - TAIL_COL0, COL_SV - TAIL_COL0 + SWA_KVW)
    new_k = tail_ctx[:, ksl].reshape(bp, 1, sp, SWA_KV_HEADS, SWA_HEAD_DIM)
    new_v = tail_ctx[:, vsl].reshape(bp, 1, sp, SWA_KV_HEADS, SWA_HEAD_DIM)
    return (y_prompt, y_sample, st_f[:, None], st_b[:, None], new_k, new_v)
```

```python
import functools

import jax
import jax.numpy as jnp
from jax import lax
from jax.experimental import pallas as pl
from jax.experimental.pallas import tpu as pltpu

F32 = jnp.float32
BF16 = jnp.bfloat16
HIGHEST = lax.Precision.HIGHEST

D_MODEL = 2048
N_MOD = 6
EPS = 1e-6

GLA_HEADS = 4
GLA_DK = 128
GLA_DV = 256
GLA_KW = GLA_HEADS * GLA_DK
GLA_VW = GLA_HEADS * GLA_DV
GLA_RANK = 16
GLA_TAU = 16.0
GLA_CHUNK = 64

SWA_HEADS = 16
SWA_KV_HEADS = 4
SWA_GROUP = 4
SWA_HEAD_DIM = 64
SWA_QW = SWA_HEADS * SWA_HEAD_DIM
SWA_KVW = SWA_KV_HEADS * SWA_HEAD_DIM
SWA_BLOCK = 128
GRID_W = 64
ROPE_BASE = 10000.0

N_EXPERTS = 16
EXPERT_FF = D_MODEL // 2
CAPACITY_FACTOR = 2

LANES = 128
ROW_CHUNKS = D_MODEL // LANES
VMEM_LIMIT = 56 * 1024 * 1024

COL_GA = 0
COL_GB = 2048
COL_GQ = 4096
COL_GK = 4608
COL_GV = 5120
COL_GR = 6144
COL_SQ = 7168
COL_SK = 8192
COL_SV = 8448
COL_DEC = 8704
IN_PAD = 8960
IN_TN = 1280
TAIL_COL0 = COL_SK
TAIL_W = COL_DEC + LANES - COL_SK


def _cparams(sem, **kw):
    return pltpu.CompilerParams(dimension_semantics=sem, vmem_limit_bytes=VMEM_LIMIT, **kw)


def _silu(x):
    return x * (1.0 / (1.0 + jnp.exp(-x)))


def _sigmoid(x):
    return 1.0 / (1.0 + jnp.exp(-x))


SLAB = 128


def _store_row_major(ref, x):
    for tb in range(x.shape[0] // SLAB):
        rows = slice(tb * SLAB, (tb + 1) * SLAB)
        parts = jnp.stack([x[rows, s * LANES:(s + 1) * LANES] for s in range(ROW_CHUNKS)], axis=0)
        ref[rows] = pltpu.einshape("stl->tsl", parts)


def _load_row_major(ref, rows):
    xt = pltpu.einshape("tsl->stl", ref[rows])
    return [xt[s] for s in range(ROW_CHUNKS)]


def _tiles_per_mod_row(mod, n, seq, tile):
    return seq // tile if mod.shape[0] > 1 else n // tile


def _mod_kernel(c_ref, w_ref, b_ref, o_ref):
    a = _silu(c_ref[...]).astype(BF16)
    o_ref[...] = jnp.dot(a, w_ref[...].astype(BF16), preferred_element_type=F32) + b_ref[...]


def _modulation(cond8, w_mod, b_mod):
    n_out = w_mod.shape[1]
    tn = 1536
    return pl.pallas_call(
        _mod_kernel,
        grid=(n_out // tn,),
        in_specs=[pl.BlockSpec((8, D_MODEL), lambda j: (0, 0)),
                  pl.BlockSpec((D_MODEL, tn), lambda j: (0, j)),
                  pl.BlockSpec((1, tn), lambda j: (0, j))],
        out_specs=pl.BlockSpec((8, tn), lambda j: (0, j)),
        out_shape=jax.ShapeDtypeStruct((8, n_out), F32),
        compiler_params=_cparams(("arbitrary",)),
        name="modulation",
    )(cond8, w_mod, b_mod.reshape(1, n_out))


NORM_ROWS = 64


def _in_proj_kernel(x_ref, sc_ref, sh_ref, g_ref, w_ref, o_ref, tail_ref, h_scr):
    j = pl.program_id(1)

    @pl.when(j == 0)
    def _():
        def slab(r, _):
            rows = pl.ds(pl.multiple_of(r * NORM_ROWS, NORM_ROWS), NORM_ROWS)
            x = x_ref[rows, :]
            y = x * lax.rsqrt(jnp.mean(x * x, axis=-1, keepdims=True) + EPS) * g_ref[...]
            h_scr[rows, :] = (y * (1.0 + sc_ref[0]) + sh_ref[0]).astype(BF16)
            return 0
        lax.fori_loop(0, x_ref.shape[0] // NORM_ROWS, slab, 0)

    acc = jnp.dot(h_scr[...], w_ref[...], preferred_element_type=F32)
    o_ref[...] = acc.astype(BF16)

    @pl.when(j == IN_PAD // IN_TN - 1)
    def _():
        first = TAIL_COL0 - (IN_PAD - IN_TN)
        tail_ref[...] = acc[:, first:first + TAIL_W]


def _in_proj(x2d, seq, sc, sh, g, w_r):
    n = x2d.shape[0]
    tm = min(1024, seq if sc.shape[0] > 1 else n)
    per_b = _tiles_per_mod_row(sc, n, seq, tm)
    return pl.pallas_call(
        _in_proj_kernel,
        grid=(n // tm, IN_PAD // IN_TN),
        in_specs=[pl.BlockSpec((tm, D_MODEL), lambda i, j: (i, 0)),
                  pl.BlockSpec((1, 1, D_MODEL), lambda i, j: (i // per_b, 0, 0)),
                  pl.BlockSpec((1, 1, D_MODEL), lambda i, j: (i // per_b, 0, 0)),
                  pl.BlockSpec((1, D_MODEL), lambda i, j: (0, 0)),
                  pl.BlockSpec((D_MODEL, IN_TN), lambda i, j: (0, j))],
        out_specs=[pl.BlockSpec((tm, IN_TN), lambda i, j: (i, j)),
                   pl.BlockSpec((tm, TAIL_W), lambda i, j: (i, 0))],
        out_shape=[jax.ShapeDtypeStruct((n, IN_PAD), BF16),
                   jax.ShapeDtypeStruct((n, TAIL_W), F32)],
        scratch_shapes=[pltpu.VMEM((tm, D_MODEL), BF16)],
        compiler_params=_cparams(("parallel", "arbitrary")),
        name="in_proj",
    )(x2d, sc, sh, g, w_r)


GLA_BLK = 256
GLA_HPS = 4


def _tn_dot(a, b, precision=None):
    return lax.dot_general(a, b, (((0,), (0,)), ((), ())), precision=precision,
                           preferred_element_type=F32)


def _nt_dot(a, b, precision=None):
    return lax.dot_general(a, b, (((1,), (1,)), ((), ())), precision=precision,
                           preferred_element_type=F32)


def _split_bf16(x, parts):
    out = []
    for _ in range(parts):
        piece = x.astype(BF16)
        out.append(piece)
        x = x - piece.astype(F32)
    return out


def _dot3(x, w):
    xh, xl = _split_bf16(x, 2)
    wh, wl = _split_bf16(w, 2)
    d = functools.partial(jnp.dot, preferred_element_type=F32)
    return d(xh, wh) + d(xl, wh) + d(xh, wl)


def _gla_kernel(q_ref, k_ref, v_ref, r_ref, dec_ref, wdf_ref, bdf_ref, wdb_ref, bdb_ref, gg_ref,
                s0f_ref, s0b_ref, o_ref, stf_ref, stb_ref, state_t, of_scr, *, nblk):
    i = pl.program_id(2)
    c = GLA_CHUNK
    nch = GLA_BLK // c
    ri = lax.broadcasted_iota(jnp.int32, (GLA_BLK, GLA_BLK), 0)
    ci = lax.broadcasted_iota(jnp.int32, (GLA_BLK, GLA_BLK), 1)
    same_chunk = (ri // c) == (ci // c)

    @pl.when(i == 0)
    def _():
        for h in range(GLA_HPS):
            state_t[h] = s0f_ref[0, h].T

    @pl.when(i == nblk)
    def _():
        for h in range(GLA_HPS):
            state_t[h] = s0b_ref[0, h].T

    def run(fwd):
        blk = i if fwd else 2 * nblk - 1 - i
        keep = same_chunk & ((ci <= ri) if fwd else (ci >= ri))
        w_ref, b_ref = (wdf_ref, bdf_ref) if fwd else (wdb_ref, bdb_ref)
        logit = _dot3(dec_ref[...], w_ref[...]) + b_ref[...]
        g = (jnp.minimum(logit, 0.0) - jnp.log(1.0 + jnp.exp(-jnp.abs(logit)))) / GLA_TAU
        tri = keep.astype(BF16)
        b = sum(jnp.dot(tri, piece, preferred_element_type=F32) for piece in _split_bf16(g, 3))
        edge = c - 1 if fwd else 0
        tots = [b[ch * c + edge:ch * c + edge + 1, :] for ch in range(nch)]
        totb = jnp.concatenate([jnp.broadcast_to(t, (c, t.shape[1])) for t in tots], axis=0)
        q_in = (q_ref[...].astype(F32) * (GLA_DK ** -0.5) * jnp.exp(b)).astype(BF16)
        k = k_ref[...].astype(F32)
        k_in = (k * jnp.exp(-b)).astype(BF16)
        k_st = (k * jnp.exp(totb - b)).astype(BF16)
        srow = pl.multiple_of(blk * GLA_BLK, GLA_BLK)
        for h in range(GLA_HPS):
            ks = slice(h * GLA_DK, (h + 1) * GLA_DK)
            vs = slice(h * GLA_DV, (h + 1) * GLA_DV)
            v = v_ref[:, vs].astype(BF16)
            a = jnp.where(keep, _nt_dot(q_in[:, ks], k_in[:, ks]), 0.0).astype(BF16)
            o_intra = jnp.dot(a, v, preferred_element_type=F32)
            st = state_t[h]
            o_inter = [None] * nch
            for cc in range(nch):
                ch = cc if fwd else nch - 1 - cc
                rows = slice(ch * c, (ch + 1) * c)
                o_inter[ch] = _nt_dot(q_in[rows, ks], st.astype(BF16))
                st = st * jnp.exp(tots[ch][:, ks]) + _tn_dot(v[rows, :], k_st[rows, ks])
            state_t[h] = st
            o = o_intra + jnp.concatenate(o_inter, axis=0)
            if fwd:
                of_scr[pl.ds(srow, GLA_BLK), vs] = o
            else:
                o = o + of_scr[pl.ds(srow, GLA_BLK), vs]
                o = o * lax.rsqrt(jnp.mean(o * o, axis=-1, keepdims=True) + EPS) * gg_ref[:, vs]
                o_ref[:, vs] = (o * _silu(r_ref[:, vs].astype(F32))).astype(BF16)

    @pl.when(i < nblk)
    def _():
        run(True)

    @pl.when(i >= nblk)
    def _():
        run(False)

    @pl.when(i == nblk - 1)
    def _():
        for h in range(GLA_HPS):
            stf_ref[0, h] = state_t[h].T

    @pl.when(i == 2 * nblk - 1)
    def _():
        for h in range(GLA_HPS):
            stb_ref[0, h] = state_t[h].T


def _gla(p, dec, bsz, seq, wdf, bdf, wdb, bdb, g_gla, s0f, s0b):
    n = p.shape[0]
    nblk = seq // GLA_BLK
    kw = GLA_HPS * GLA_DK
    vw = GLA_HPS * GLA_DV

    def rb(b, i):
        return b * nblk + jnp.where(i < nblk, i, 2 * nblk - 1 - i)

    def orb(b, i):
        return b * nblk + jnp.where(i < nblk, nblk - 1, 2 * nblk - 1 - i)

    st_spec = pl.BlockSpec((1, GLA_HPS, GLA_DK, GLA_DV), lambda b, h, i: (b, h, 0, 0))
    s0_spec = pl.BlockSpec((1, GLA_HPS, GLA_DK, GLA_DV),
                           lambda b, h, i: (b if s0f.shape[0] > 1 else 0, h, 0, 0))
    st_shape = jax.ShapeDtypeStruct((bsz, GLA_HEADS, GLA_DK, GLA_DV), F32)
    return pl.pallas_call(
        functools.partial(_gla_kernel, nblk=nblk),
        grid=(bsz, GLA_HEADS // GLA_HPS, 2 * nblk),
        in_specs=[pl.BlockSpec((GLA_BLK, kw), lambda b, h, i: (rb(b, i), COL_GQ // kw + h)),
                  pl.BlockSpec((GLA_BLK, kw), lambda b, h, i: (rb(b, i), COL_GK // kw + h)),
                  pl.BlockSpec((GLA_BLK, vw), lambda b, h, i: (rb(b, i), COL_GV // vw + h)),
                  pl.BlockSpec((GLA_BLK, vw), lambda b, h, i: (rb(b, i), COL_GR // vw + h)),
                  pl.BlockSpec((GLA_BLK, LANES), lambda b, h, i: (rb(b, i), (COL_DEC - TAIL_COL0) // LANES)),
                  pl.BlockSpec((LANES, kw), lambda b, h, i: (0, h)),
                  pl.BlockSpec((1, kw), lambda b, h, i: (0, h)),
                  pl.BlockSpec((LANES, kw), lambda b, h, i: (0, h)),
                  pl.BlockSpec((1, kw), lambda b, h, i: (0, h)),
                  pl.BlockSpec((1, vw), lambda b, h, i: (0, h)),
                  s0_spec, s0_spec],
        out_specs=[pl.BlockSpec((GLA_BLK, vw), lambda b, h, i: (orb(b, i), h)), st_spec, st_spec],
        out_shape=[jax.ShapeDtypeStruct((n, GLA_VW), BF16), st_shape, st_shape],
        scratch_shapes=[pltpu.VMEM((GLA_HPS, GLA_DV, GLA_DK), F32), pltpu.VMEM((seq, vw), F32)],
        compiler_params=_cparams(("parallel", "parallel", "arbitrary")),
        name="gla",
    )(p, p, p, p, dec, wdf, bdf, wdb, bdb, g_gla, s0f, s0b)


def _sink_column(sink_ref, kvh, rows_per_head):
    r = lax.broadcasted_iota(jnp.int32, (SWA_GROUP * rows_per_head, 1), 0)
    col = jnp.zeros((SWA_GROUP * rows_per_head, 1), F32)
    for g in range(SWA_GROUP):
        col = jnp.where(r // rows_per_head == g, sink_ref[kvh * SWA_GROUP + g], col)
    return col


def _rope(x, cos, sin):
    outs = []
    for cb in range(x.shape[1] // LANES):
        sl = slice(cb * LANES, (cb + 1) * LANES)
        xc = x[:, sl]
        lane = lax.broadcasted_iota(jnp.int32, xc.shape, 1)
        sw = jnp.where(lane % 32 < 16, pltpu.roll(xc, LANES - 16, 1), pltpu.roll(xc, 16, 1))
        outs.append(xc * cos[:, sl] + sw * sin[:, sl])
    return jnp.concatenate(outs, axis=1)


def _swa_lat_kernel(sink_ref, q_ref, kp_ref, kc_ref, kn_ref, vp_ref, vc_ref, vn_ref, ck_ref, cv_ref,
                    cp_ref, cc_ref, cn_ref, sp_ref, sc_ref, sn_ref, o_ref, *, nb):
    n = pl.program_id(1)
    blk = SWA_BLOCK
    hd = SWA_HEAD_DIM
    cos_c, sin_c = cc_ref[...], sc_ref[...]
    kband = jnp.concatenate([_rope(kp_ref[...].astype(F32), cp_ref[...], sp_ref[...]),
                             _rope(kc_ref[...].astype(F32), cos_c, sin_c),
                             _rope(kn_ref[...].astype(F32), cn_ref[...], sn_ref[...])], axis=0).astype(BF16)
    vband = jnp.concatenate([vp_ref[...], vc_ref[...], vn_ref[...]], axis=0).astype(BF16)
    ck = ck_ref[0].astype(BF16)
    cv = cv_ref[0].astype(BF16)
    qi = lax.broadcasted_iota(jnp.int32, (SWA_GROUP * blk, 3 * blk), 0) % blk
    kj = lax.broadcasted_iota(jnp.int32, (SWA_GROUP * blk, 3 * blk), 1)
    k_abs = kj + (n - 1) * blk
    mask = (jnp.abs(kj - blk - qi) <= SWA_BLOCK) & (k_abs >= 0) & (k_abs < nb * blk)
    for kvh in range(SWA_KV_HEADS):
        ks = slice(kvh * hd, (kvh + 1) * hd)
        qr = _rope(q_ref[:, kvh * SWA_KVW:(kvh + 1) * SWA_KVW].astype(F32), cos_c, sin_c) * (hd ** -0.5)
        qg = jnp.concatenate([qr[:, g * hd:(g + 1) * hd] for g in range(SWA_GROUP)],
                             axis=0).astype(BF16)
        s_b = jnp.where(mask, _nt_dot(qg, kband[:, ks]), -1e30)
        s_c = _nt_dot(qg, ck[:, ks])
        sink = _sink_column(sink_ref, kvh, blk)
        m = jnp.maximum(jnp.maximum(jnp.max(s_b, axis=-1, keepdims=True),
                                    jnp.max(s_c, axis=-1, keepdims=True)), sink)
        p_b = jnp.exp(s_b - m)
        p_c = jnp.exp(s_c - m)
        den = jnp.exp(sink - m) + jnp.sum(p_b, axis=-1, keepdims=True) + jnp.sum(p_c, axis=-1, keepdims=True)
        o = (jnp.dot(p_b.astype(BF16), vband[:, ks], preferred_element_type=F32)
             + jnp.dot(p_c.astype(BF16), cv[:, ks], preferred_element_type=F32)) / den
        for g in range(SWA_GROUP):
            cb = (kvh * SWA_GROUP + g) * hd
            o_ref[:, cb:cb + hd] = o[g * blk:(g + 1) * blk, :].astype(BF16)


def _swa_latent(p, bsz, seq, cache_k, cache_v, sink, cos_t, sin_t):
    n = p.shape[0]
    nb = seq // SWA_BLOCK
    kcol = COL_SK // SWA_KVW
    vcol = COL_SV // SWA_KVW
    prev = lambda i: jnp.maximum(i - 1, 0)
    nxt = lambda i: jnp.minimum(i + 1, nb - 1)

    def pspec(col, f):
        return pl.BlockSpec((SWA_BLOCK, SWA_KVW), lambda b, i, s: (b * nb + f(i), col))

    def tspec(f):
        return pl.BlockSpec((SWA_BLOCK, SWA_KVW), lambda b, i, s: (f(i), 0))

    same = lambda i: i
    cspec = pl.BlockSpec((1, cache_k.shape[1], SWA_KVW), lambda b, i, s: (b, 0, 0))
    grid_spec = pltpu.PrefetchScalarGridSpec(
        num_scalar_prefetch=1,
        grid=(bsz, nb),
        in_specs=[pl.BlockSpec((SWA_BLOCK, SWA_QW), lambda b, i, s: (b * nb + i, COL_SQ // SWA_QW)),
                  pspec(kcol, prev), pspec(kcol, same), pspec(kcol, nxt),
                  pspec(vcol, prev), pspec(vcol, same), pspec(vcol, nxt),
                  cspec, cspec,
                  tspec(prev), tspec(same), tspec(nxt), tspec(prev), tspec(same), tspec(nxt)],
        out_specs=pl.BlockSpec((SWA_BLOCK, SWA_QW), lambda b, i, s: (b * nb + i, 0)),
    )
    return pl.pallas_call(
        functools.partial(_swa_lat_kernel, nb=nb),
        grid_spec=grid_spec,
        out_shape=jax.ShapeDtypeStruct((n, SWA_QW), BF16),
        compiler_params=_cparams(("parallel", "arbitrary")),
        name="swa_latent",
    )(sink, p, p, p, p, p, p, p, cache_k, cache_v, cos_t, cos_t, cos_t, sin_t, sin_t, sin_t)


def _swa_ctx_kernel(sink_ref, q_ref, k_ref, v_ref, o_ref):
    s = q_ref.shape[0]
    hd = SWA_HEAD_DIM
    kb = k_ref[...].astype(BF16)
    vb = v_ref[...].astype(BF16)
    for kvh in range(SWA_KV_HEADS):
        ks = slice(kvh * hd, (kvh + 1) * hd)
        qg = jnp.concatenate([q_ref[:, (kvh * SWA_GROUP + g) * hd:(kvh * SWA_GROUP + g + 1) * hd]
                              for g in range(SWA_GROUP)], axis=0)
        qg = (qg.astype(F32) * (hd ** -0.5)).astype(BF16)
        sc = _nt_dot(qg, kb[:, ks])
        sink = _sink_column(sink_ref, kvh, s)
        m = jnp.maximum(jnp.max(sc, axis=-1, keepdims=True), sink)
        pr = jnp.exp(sc - m)
        den = jnp.exp(sink - m) + jnp.sum(pr, axis=-1, keepdims=True)
        o = jnp.dot(pr.astype(BF16), vb[:, ks], preferred_element_type=F32) / den
        for g in range(SWA_GROUP):
            cb = (kvh * SWA_GROUP + g) * hd
            o_ref[:, cb:cb + hd] = o[g * s:(g + 1) * s, :].astype(BF16)


def _swa_context(p, bsz, seq, sink):
    n = p.shape[0]
    grid_spec = pltpu.PrefetchScalarGridSpec(
        num_scalar_prefetch=1,
        grid=(bsz,),
        in_specs=[pl.BlockSpec((seq, SWA_QW), lambda b, s: (b, COL_SQ // SWA_QW)),
                  pl.BlockSpec((seq, SWA_KVW), lambda b, s: (b, COL_SK // SWA_KVW)),
                  pl.BlockSpec((seq, SWA_KVW), lambda b, s: (b, COL_SV // SWA_KVW))],
        out_specs=pl.BlockSpec((seq, SWA_QW), lambda b, s: (b, 0)),
    )
    return pl.pallas_call(
        _swa_ctx_kernel,
        grid_spec=grid_spec,
        out_shape=jax.ShapeDtypeStruct((n, SWA_QW), BF16),
        compiler_params=_cparams(("parallel",)),
        name="swa_context",
    )(sink, p, p, p)


MERGE_TM = 512


def _branch_kernel(og_ref, os_ref, ga_ref, gb_ref, wa_ref, wb_ref, m_ref):
    a = jnp.dot(og_ref[...], wa_ref[...], preferred_element_type=F32)
    b = jnp.dot(os_ref[...], wb_ref[...], preferred_element_type=F32)
    m_ref[...] = (_sigmoid(ga_ref[...].astype(F32)) * a + _sigmoid(gb_ref[...].astype(F32)) * b).astype(BF16)


def _branch_merge(o_gla, o_swa, p, wa, wb):
    n = o_gla.shape[0]
    tm = MERGE_TM
    once = pl.Buffered(1)
    return pl.pallas_call(
        _branch_kernel,
        grid=(n // tm,),
        in_specs=[pl.BlockSpec((tm, GLA_VW), lambda i: (i, 0)),
                  pl.BlockSpec((tm, SWA_QW), lambda i: (i, 0)),
                  pl.BlockSpec((tm, D_MODEL), lambda i: (i, COL_GA // D_MODEL)),
                  pl.BlockSpec((tm, D_MODEL), lambda i: (i, COL_GB // D_MODEL)),
                  pl.BlockSpec((GLA_VW, D_MODEL), lambda i: (0, 0), pipeline_mode=once),
                  pl.BlockSpec((SWA_QW, D_MODEL), lambda i: (0, 0), pipeline_mode=once)],
        out_specs=pl.BlockSpec((tm, D_MODEL), lambda i: (i, 0)),
        out_shape=jax.ShapeDtypeStruct((n, D_MODEL), BF16),
        compiler_params=_cparams(("parallel",)),
        name="branch_merge",
    )(o_gla, o_swa, p, p, wa, wb)


def _out_kernel(m_ref, x_ref, gt1_ref, sc2_ref, sh2_ref, g2_ref, wo_ref, wr2_ref,
                x1_ref, h2_ref, lg_ref):
    x1 = x_ref[...] + gt1_ref[0] * jnp.dot(m_ref[...], wo_ref[...], preferred_element_type=F32)
    x1_ref[...] = x1
    y = x1 * lax.rsqrt(jnp.mean(x1 * x1, axis=-1, keepdims=True) + EPS) * g2_ref[...]
    h2 = y * (1.0 + sc2_ref[0]) + sh2_ref[0]
    _store_row_major(h2_ref, h2)
    l2 = jnp.dot(h2.astype(BF16), wr2_ref[...], preferred_element_type=F32)
    lt = l2.T
    lt = lt[0:N_EXPERTS, :] + lt[N_EXPERTS:2 * N_EXPERTS, :]
    for cb in range(lt.shape[1] // LANES):
        lg_ref[cb] = lt[:, cb * LANES:(cb + 1) * LANES]


def _out_proj(merged, x2d, seq, gt1, sc2, sh2, g2, wo, wr2):
    n = x2d.shape[0]
    tm = MERGE_TM
    per_b = _tiles_per_mod_row(gt1, n, seq, tm)
    mod_spec = pl.BlockSpec((1, 1, D_MODEL), lambda i: (i // per_b, 0, 0))
    once = pl.Buffered(1)
    return pl.pallas_call(
        _out_kernel,
        grid=(n // tm,),
        in_specs=[pl.BlockSpec((tm, D_MODEL), lambda i: (i, 0)),
                  pl.BlockSpec((tm, D_MODEL), lambda i: (i, 0)),
                  mod_spec, mod_spec, mod_spec,
                  pl.BlockSpec((1, D_MODEL), lambda i: (0, 0)),
                  pl.BlockSpec((D_MODEL, D_MODEL), lambda i: (0, 0), pipeline_mode=once),
                  pl.BlockSpec((D_MODEL, LANES), lambda i: (0, 0))],
        out_specs=[pl.BlockSpec((tm, D_MODEL), lambda i: (i, 0)),
                   pl.BlockSpec((tm, ROW_CHUNKS, LANES), lambda i: (i, 0, 0)),
                   pl.BlockSpec((tm // LANES, N_EXPERTS, LANES), lambda i: (i, 0, 0))],
        out_shape=[jax.ShapeDtypeStruct((n, D_MODEL), F32),
                   jax.ShapeDtypeStruct((n, ROW_CHUNKS, LANES), F32),
                   jax.ShapeDtypeStruct((n // LANES, N_EXPERTS, LANES), F32)],
        compiler_params=_cparams(("parallel",)),
        name="out_proj",
    )(merged, x2d, gt1, sc2, sh2, g2, wo, wr2)


def _select_kernel(lg_ref, aff_ref, sel_ref, pos_ref, *, cap):
    nblk = lg_ref.shape[0]
    lg = lg_ref[...]
    ex = jnp.exp(lg - jnp.max(lg, axis=1, keepdims=True))
    aff = ex / jnp.sum(ex, axis=1, keepdims=True)
    aff_ref[...] = aff
    bits = lax.bitcast_convert_type(aff, jnp.int32)

    def count(pred):
        c = jnp.sum(jnp.where(pred, 1.0, 0.0), axis=0, keepdims=True)
        return jnp.sum(c, axis=2, keepdims=True)

    def bit_step(t, cur):
        cand = cur | jnp.left_shift(jnp.int32(1), 30 - t)
        return jnp.where(count(bits >= cand) >= cap, cand, cur)

    thr = lax.fori_loop(0, 31, bit_step, jnp.zeros((1, N_EXPERTS, 1), jnp.int32))
    need = (cap - count(bits > thr))[0]
    thr2 = thr[0]
    ri = lax.broadcasted_iota(jnp.int32, (LANES, LANES), 0)
    ci = lax.broadcasted_iota(jnp.int32, (LANES, LANES), 1)
    upper = (ri <= ci).astype(BF16)

    def blk_step(b, carry):
        run_eq, run_sel = carry
        bb = lax.bitcast_convert_type(aff_ref[b], jnp.int32)
        eq = (bb == thr2).astype(F32)
        eq_excl = jnp.dot(eq.astype(BF16), upper, preferred_element_type=F32) - eq + run_eq
        sel = jnp.where((bb > thr2) | ((eq > 0) & (eq_excl < need)), 1.0, 0.0)
        sel_ref[b] = sel
        pos_ref[b] = jnp.dot(sel.astype(BF16), upper, preferred_element_type=F32) - sel + run_sel
        return (run_eq + jnp.sum(eq, axis=1, keepdims=True),
                run_sel + jnp.sum(sel, axis=1, keepdims=True))

    zero = jnp.zeros((N_EXPERTS, 1), F32)
    lax.fori_loop(0, nblk, blk_step, (zero, zero))


def _select(logits3, cap):
    shp = jax.ShapeDtypeStruct(logits3.shape, F32)
    return pl.pallas_call(
        functools.partial(_select_kernel, cap=float(cap)),
        out_shape=[shp, shp, shp],
        compiler_params=pltpu.CompilerParams(vmem_limit_bytes=VMEM_LIMIT),
        name="select",
    )(logits3)


COMPACT_WIN = 136


def _compact_kernel(base_ref, aff_ref, sel_ref, pos_ref, idx_ref, gate_ref):
    nblk = aff_ref.shape[0]
    idx_ref[...] = jnp.zeros_like(idx_ref)
    gate_ref[...] = jnp.zeros_like(gate_ref)
    slot = lax.broadcasted_iota(jnp.int32, (COMPACT_WIN, LANES), 0).astype(F32)
    lane = lax.broadcasted_iota(jnp.int32, (COMPACT_WIN, LANES), 1)

    def blk_step(b, _):
        aff = aff_ref[b]
        sel = sel_ref[b]
        pos = pos_ref[b]
        tok = (lane + b * LANES).astype(F32)
        for e in range(N_EXPERTS):
            base8 = pl.multiple_of((base_ref[b * N_EXPERTS + e] >> 3) << 3, 8)
            rel = pos[e:e + 1, :] - base8.astype(F32)
            hit = (rel == slot) & (sel[e:e + 1, :] > 0)
            rows = pl.ds(base8, COMPACT_WIN)
            mine = lane == e
            ic = jnp.sum(jnp.where(hit, tok, 0.0), axis=1, keepdims=True)
            gc = jnp.sum(jnp.where(hit, aff[e:e + 1, :], 0.0), axis=1, keepdims=True)
            idx_ref[rows, :] += jnp.where(mine, ic, 0.0)
            gate_ref[rows, :] += jnp.where(mine, gc, 0.0)
        return 0

    lax.fori_loop(0, nblk, blk_step, 0)


def _compact(base, aff3, sel3, pos3, cap):
    cp = cap + 2 * LANES
    full = pl.BlockSpec(aff3.shape, lambda i, s: (0, 0, 0))
    ospec = pl.BlockSpec((cp, LANES), lambda i, s: (0, 0))
    oshape = jax.ShapeDtypeStruct((cp, LANES), F32)
    grid_spec = pltpu.PrefetchScalarGridSpec(
        num_scalar_prefetch=1, grid=(1,), in_specs=[full, full, full], out_specs=[ospec, ospec])
    return pl.pallas_call(
        _compact_kernel,
        grid_spec=grid_spec,
        out_shape=[oshape, oshape],
        compiler_params=_cparams(("arbitrary",)),
        name="compact",
    )(base, aff3, sel3, pos3)


FFN_COLS = 256


def _up_kernel(idx_ref, h2_hbm, wg_ref, wu_ref, hid_ref, gbuf, xs, sem, *, tm, total):
    step = pl.program_id(0) * pl.num_programs(1) + pl.program_id(1)

    def row_copy(step_, i):
        return pltpu.make_async_copy(h2_hbm.at[idx_ref[step_ * tm + i]], gbuf.at[i], sem)

    def wait_rows():
        pltpu.make_async_copy(h2_hbm.at[pl.ds(0, tm)], gbuf, sem).wait()

    @pl.when(step == 0)
    def _():
        def body(i, _):
            row_copy(step, i).start()
            return 0
        lax.fori_loop(0, tm, body, 0)

    wait_rows()
    slab = min(SLAB, tm)
    for tb in range(tm // slab):
        rows = slice(tb * slab, (tb + 1) * slab)
        for s, chunk in enumerate(_load_row_major(gbuf, rows)):
            xs[rows, s * LANES:(s + 1) * LANES] = chunk.astype(BF16)

    nxt = jnp.minimum(step + 1, total - 1)
    for i in range(tm):
        row_copy(nxt, i).start()

    for f in range(EXPERT_FF // FFN_COLS):
        cols = slice(f * FFN_COLS, (f + 1) * FFN_COLS)
        hg = jnp.dot(xs[...], wg_ref[0, :, cols].astype(BF16), preferred_element_type=F32)
        hu = jnp.dot(xs[...], wu_ref[0, :, cols].astype(BF16), preferred_element_type=F32)
        hid_ref[:, cols] = (_silu(hg) * hu).astype(BF16)

    @pl.when(step == total - 1)
    def _():
        wait_rows()


def _expert_up(idx_flat, h2_rm, wg, wu, cap):
    tm = min(1024, cap)
    nr = cap // tm
    grid_spec = pltpu.PrefetchScalarGridSpec(
        num_scalar_prefetch=1,
        grid=(N_EXPERTS, nr),
        in_specs=[pl.BlockSpec(memory_space=pl.ANY),
                  pl.BlockSpec((1, D_MODEL, EXPERT_FF), lambda e, r, s: (e, 0, 0)),
                  pl.BlockSpec((1, D_MODEL, EXPERT_FF), lambda e, r, s: (e, 0, 0))],
        out_specs=pl.BlockSpec((tm, EXPERT_FF), lambda e, r, s: (e * nr + r, 0)),
        scratch_shapes=[pltpu.VMEM((tm, ROW_CHUNKS, LANES), F32),
                        pltpu.VMEM((tm, D_MODEL), BF16),
                        pltpu.SemaphoreType.DMA(())],
    )
    return pl.pallas_call(
        functools.partial(_up_kernel, tm=tm, total=N_EXPERTS * nr),
        grid_spec=grid_spec,
        out_shape=jax.ShapeDtypeStruct((N_EXPERTS * cap, EXPERT_FF), BF16),
        compiler_params=_cparams(("arbitrary", "arbitrary")),
        name="expert_up",
    )(idx_flat, h2_rm, wg, wu)


def _down_kernel(hid_ref, wd_ref, o_ref, ybuf):
    for c in range(D_MODEL // FFN_COLS):
        cols = slice(c * FFN_COLS, (c + 1) * FFN_COLS)
        ybuf[:, cols] = jnp.dot(hid_ref[...], wd_ref[0, :, cols].astype(BF16), preferred_element_type=F32)
    tm = ybuf.shape[0]
    slab = min(SLAB, tm)
    for tb in range(tm // slab):
        rows = slice(tb * slab, (tb + 1) * slab)
        parts = jnp.stack([ybuf[rows, s * LANES:(s + 1) * LANES] for s in range(ROW_CHUNKS)], axis=0)
        o_ref[rows] = pltpu.einshape("stl->tsl", parts)


def _expert_down(hid, wd, cap):
    tm = min(1024, cap)
    nr = cap // tm
    return pl.pallas_call(
        _down_kernel,
        grid=(N_EXPERTS, nr),
        in_specs=[pl.BlockSpec((tm, EXPERT_FF), lambda e, r: (e * nr + r, 0)),
                  pl.BlockSpec((1, EXPERT_FF, D_MODEL), lambda e, r: (e, 0, 0))],
        out_specs=pl.BlockSpec((tm, ROW_CHUNKS, LANES), lambda e, r: (e * nr + r, 0, 0)),
        out_shape=jax.ShapeDtypeStruct((N_EXPERTS * cap, ROW_CHUNKS, LANES), F32),
        scratch_shapes=[pltpu.VMEM((tm, D_MODEL), F32)],
        compiler_params=_cparams(("parallel", "arbitrary")),
        name="expert_down",
    )(hid, wd)


COMBINE_UNROLL = 4
COMBINE_CHUNK = 32
COMBINE_NCHUNK = 3


def _combine_kernel(idx_ref, gate_ref, lo_ref, ye_hbm, x1_ref, gt2_ref, gf_ref, o_ref, stage, acc, sem,
                    *, tt, ch, cap, ntiles):
    tile = pl.program_id(0)
    slot = tile % 2

    window = COMBINE_NCHUNK * ch

    def chunk_start(first_row):
        return jnp.minimum(first_row, cap - window)

    def chunk_copy(slot_, e, start, c):
        return pltpu.make_async_copy(ye_hbm.at[pl.ds(e * cap + start + c * ch, ch)],
                                     stage.at[slot_, e, pl.ds(c * ch, ch)],
                                     sem.at[slot_, e])

    def for_needed_chunks(tile_, e, fn):
        lo = lo_ref[tile_ * N_EXPERTS + e]
        hi = lo_ref[(tile_ + 1) * N_EXPERTS + e]
        start = chunk_start(lo)
        for c in range(COMBINE_NCHUNK):
            @pl.when((start + c * ch < hi) & (start + (c + 1) * ch > lo))
            def _():
                fn(start, c)

    def issue(tile_, slot_):
        for e in range(N_EXPERTS):
            for_needed_chunks(tile_, e, lambda start, c: chunk_copy(slot_, e, start, c).start())

    @pl.when(tile == 0)
    def _():
        issue(tile, slot)

    @pl.when(tile + 1 < ntiles)
    def _():
        issue(tile + 1, 1 - slot)

    acc[...] = jnp.zeros_like(acc)

    def add_rows(e, start, r0, r1):
        def group(r, width):
            dsts, vals = [], []
            for u in range(width):
                dst = idx_ref[e * cap + r + u] - tile * tt
                dsts.append(dst)
                vals.append(acc[dst] + gate_ref[e * cap + r + u] * stage[slot, e, r + u - start])
            for dst, val in zip(dsts, vals):
                acc[dst] = val

        ngroups = (r1 - r0) // COMBINE_UNROLL

        def body_group(j, _):
            group(r0 + j * COMBINE_UNROLL, COMBINE_UNROLL)
            return 0
        lax.fori_loop(0, ngroups, body_group, 0)

        def body_one(r, _):
            group(r, 1)
            return 0
        lax.fori_loop(r0 + ngroups * COMBINE_UNROLL, r1, body_one, 0)

    for e in range(N_EXPERTS):
        lo = lo_ref[tile * N_EXPERTS + e]
        hi = lo_ref[(tile + 1) * N_EXPERTS + e]
        start = chunk_start(lo)
        for_needed_chunks(tile, e, lambda start_, c: chunk_copy(slot, e, start_, c).wait())
        first_end = jnp.minimum(hi, start + window)
        add_rows(e, start, lo, first_end)

        def more(r0):
            st = jnp.minimum(r0, cap - ch)
            cp = chunk_copy(slot, e, st, 0)
            cp.start()
            cp.wait()
            r1 = jnp.minimum(hi, st + ch)
            add_rows(e, st, r0, r1)
            return r1
        lax.while_loop(lambda r0: r0 < hi, more, first_end)

    for tb in range(tt // SLAB):
        rows = slice(tb * SLAB, (tb + 1) * SLAB)
        ssq = jnp.zeros((SLAB, 1), F32)
        for s, y in enumerate(_load_row_major(acc, rows)):
            cols = slice(s * LANES, (s + 1) * LANES)
            x2 = x1_ref[rows, cols] + gt2_ref[0][:, cols] * y
            o_ref[rows, cols] = x2
            ssq = ssq + jnp.sum(x2 * x2, axis=-1, keepdims=True)
        o_ref[rows, :] = o_ref[rows, :] * lax.rsqrt(ssq / D_MODEL + EPS) * gf_ref[...]


def _combine(idx_flat, gate_flat, lo_tab, ye_lin, x1, seq, gt2, g_final, cap, tt):
    n = x1.shape[0]
    ch = min(COMBINE_CHUNK, cap // COMBINE_NCHUNK)
    ntiles = n // tt
    per_b = _tiles_per_mod_row(gt2, n, seq, tt)
    grid_spec = pltpu.PrefetchScalarGridSpec(
        num_scalar_prefetch=3,
        grid=(ntiles,),
        in_specs=[pl.BlockSpec(memory_space=pl.ANY),
                  pl.BlockSpec((tt, D_MODEL), lambda i, a, b, c: (i, 0)),
                  pl.BlockSpec((1, 1, D_MODEL), lambda i, a, b, c: (i // per_b, 0, 0)),
                  pl.BlockSpec((1, D_MODEL), lambda i, a, b, c: (0, 0))],
        out_specs=pl.BlockSpec((tt, D_MODEL), lambda i, a, b, c: (i, 0)),
        scratch_shapes=[pltpu.VMEM((2, N_EXPERTS, COMBINE_NCHUNK * ch, ROW_CHUNKS, LANES), F32),
                        pltpu.VMEM((tt, ROW_CHUNKS, LANES), F32),
                        pltpu.SemaphoreType.DMA((2, N_EXPERTS))],
    )
    return pl.pallas_call(
        functools.partial(_combine_kernel, tt=tt, ch=ch, cap=cap, ntiles=ntiles),
        grid_spec=grid_spec,
        out_shape=jax.ShapeDtypeStruct((n, D_MODEL), F32),
        compiler_params=_cparams(("arbitrary",)),
        name="combine",
    )(idx_flat, gate_flat, lo_tab, ye_lin, x1, gt2, g_final)


def _rope_tables(seq):
    pos = jnp.arange(seq)
    row = (pos // GRID_W).astype(F32)
    col = (pos % GRID_W).astype(F32)
    npair = SWA_HEAD_DIM // 4
    inv_freq = ROPE_BASE ** (-jnp.arange(npair, dtype=F32) / npair)
    ar = row[:, None] * inv_freq[None, :]
    ac = col[:, None] * inv_freq[None, :]
    cos = jnp.concatenate([jnp.cos(ar), jnp.cos(ar), jnp.cos(ac), jnp.cos(ac)], axis=1)
    sin = jnp.concatenate([-jnp.sin(ar), jnp.sin(ar), -jnp.sin(ac), jnp.sin(ac)], axis=1)
    reps = SWA_KVW // SWA_HEAD_DIM
    return jnp.tile(cos, (1, reps)), jnp.tile(sin, (1, reps))


def _layer(x, mods, wts, latent, s0f, s0b, cache_k, cache_v):
    bsz, seq, _ = x.shape
    n = bsz * seq
    x2d = x.reshape(n, D_MODEL)
    sh1, sc1, gt1, sh2, sc2, gt2 = mods
    p, tail = _in_proj(x2d, seq, sc1, sh1, wts["g1"], wts["w_in_r"])
    o_gla, st_f, st_b = _gla(p, tail, bsz, seq, wts["wdf"], wts["bdf"], wts["wdb"], wts["bdb"], wts["g_gla"],
                             s0f, s0b)
    if latent:
        cos_t, sin_t = _rope_tables(seq)
        o_swa = _swa_latent(p, bsz, seq, cache_k, cache_v, wts["sink"], cos_t, sin_t)
    else:
        o_swa = _swa_context(p, bsz, seq, wts["sink"])
    merged = _branch_merge(o_gla, o_swa, p, wts["wa"], wts["wb"])
    x1, h2, logits3 = _out_proj(merged, x2d, seq, gt1, sc2, sh2, wts["g2"], wts["wo"], wts["wr2"])
    cap = CAPACITY_FACTOR * n // N_EXPERTS
    aff3, sel3, pos3 = _select(logits3, cap)
    base = jnp.concatenate([pos3[:, :, 0].astype(jnp.int32), jnp.full((1, N_EXPERTS), cap, jnp.int32)], axis=0)
    idx_c, gate_c = _compact(base.reshape(-1), aff3, sel3, pos3, cap)
    idx_flat = idx_c[:cap, :N_EXPERTS].T.astype(jnp.int32).reshape(-1)
    gate_flat = gate_c[:cap, :N_EXPERTS].T.reshape(-1)
    hid = _expert_up(idx_flat, h2, wts["wg"], wts["wu"], cap)
    ye = _expert_down(hid, wts["wd"], cap)
    tt = 256
    lo_tab = base[::tt // LANES]
    y = _combine(idx_flat, gate_flat, lo_tab.reshape(-1), ye, x1, seq, gt2, wts["g_final"], cap, tt)
    return y.reshape(bsz, seq, D_MODEL), tail, st_f, st_b


def kernel(x_prompt, x_sample, state_gla_fwd, state_gla_bwd, cache_k, cache_v, c, c_ctx, w_mod, b_mod,
           g_norm1, w_in, w_dec_f, b_dec_f, w_dec_b, b_dec_b, g_gla, attn_sink, w_branch_a, w_branch_b,
           w_out, g_norm2, w_router, w_exp_gate, w_exp_up, w_exp_down, g_final):
    bp, sp, _ = x_prompt.shape
    bl = x_sample.shape[0]
    l = 0
    gla_end = 2 * GLA_KW + 2 * GLA_VW
    dec_end = gla_end + 2 * GLA_RANK
    swa_end = dec_end + SWA_QW + 2 * SWA_KVW
    w = w_in[l]
    w_in_r = jnp.concatenate(
        [w[:, swa_end:].astype(BF16), w[:, :gla_end].astype(BF16), w[:, dec_end:swa_end].astype(BF16),
         w[:, gla_end:dec_end].astype(BF16), jnp.zeros((D_MODEL, IN_PAD - w.shape[1]), BF16)], axis=1)
    zpad = jnp.zeros((LANES - 2 * GLA_RANK, GLA_KW), F32)
    wr_hi = w_router[l].astype(BF16)
    wr_lo = (w_router[l] - wr_hi.astype(F32)).astype(BF16)
    wr_pad = jnp.zeros((D_MODEL, LANES - 2 * N_EXPERTS), BF16)
    zr = jnp.zeros((GLA_RANK, GLA_KW), F32)
    wts = {
        "g1": g_norm1[l].reshape(1, D_MODEL),
        "w_in_r": w_in_r,
        "wdf": jnp.concatenate([w_dec_f[l], zr, zpad], axis=0),
        "wdb": jnp.concatenate([zr, w_dec_b[l], zpad], axis=0),
        "bdf": b_dec_f[l].reshape(1, GLA_KW),
        "bdb": b_dec_b[l].reshape(1, GLA_KW),
        "g_gla": g_gla[l].reshape(1, GLA_VW),
        "sink": attn_sink[l],
        "wa": w_branch_a[l].astype(BF16),
        "wb": w_branch_b[l].astype(BF16),
        "wo": w_out[l].astype(BF16),
        "g2": g_norm2[l].reshape(1, D_MODEL),
        "wr2": jnp.concatenate([wr_hi, wr_lo, wr_pad], axis=1),
        "wg": w_exp_gate[l],
        "wu": w_exp_up[l],
        "wd": w_exp_down[l],
        "g_final": g_final.reshape(1, D_MODEL),
    }
    cond8 = jnp.concatenate([c_ctx[None, :], c, jnp.zeros((8 - 1 - bl, D_MODEL), F32)], axis=0)
    mod = _modulation(cond8, w_mod[l], b_mod[l]).reshape(8, N_MOD, 1, D_MODEL)
    mods_ctx = tuple(mod[0:1, j] for j in range(N_MOD))
    mods_lat = tuple(mod[1:1 + bl, j] for j in range(N_MOD))

    zero_state = jnp.zeros((1, GLA_HEADS, GLA_DK, GLA_DV), F32)
    y_prompt, tail_ctx, st_f, st_b = _layer(x_prompt, mods_ctx, wts, False, zero_state, zero_state, None, None)
    ck = cache_k[:, l].reshape(bl, -1, SWA_KVW)
    cv = cache_v[:, l].reshape(bl, -1, SWA_KVW)
    y_sample, _, _, _ = _layer(x_sample, mods_lat, wts, True, state_gla_fwd[:, l], state_gla_bwd[:, l], ck, cv)

    ksl = slice(COL_SK - TAIL_COL0, COL_SK - TAIL_COL0 + SWA_KVW)
    vsl = slice(COL_SV - TAIL_COL0, COL_SV - TAIL_COL0 + SWA_KVW)
    new_k = tail_ctx[:, ksl].reshape(bp, 1, sp, SWA_KV_HEADS, SWA_HEAD_DIM)
    new_v = tail_ctx[:, vsl].reshape(bp, 1, sp, SWA_KV_HEADS, SWA_HEAD_DIM)
    return (y_prompt, y_sample, st_f[:, None], st_b[:, None], new_k, new_v)
```

```python
import functools

import jax
import jax.numpy as jnp
from jax import lax
from jax.experimental import pallas as pl
from jax.experimental.pallas import tpu as pltpu

F32 = jnp.float32
BF16 = jnp.bfloat16
HIGHEST = lax.Precision.HIGHEST

D_MODEL = 2048
N_MOD = 6
EPS = 1e-6

GLA_HEADS = 4
GLA_DK = 128
GLA_DV = 256
GLA_KW = GLA_HEADS * GLA_DK
GLA_VW = GLA_HEADS * GLA_DV
GLA_RANK = 16
GLA_TAU = 16.0
GLA_CHUNK = 64

SWA_HEADS = 16
SWA_KV_HEADS = 4
SWA_GROUP = 4
SWA_HEAD_DIM = 64
SWA_QW = SWA_HEADS * SWA_HEAD_DIM
SWA_KVW = SWA_KV_HEADS * SWA_HEAD_DIM
SWA_BLOCK = 128
GRID_W = 64
ROPE_BASE = 10000.0

N_EXPERTS = 16
EXPERT_FF = D_MODEL // 2
CAPACITY_FACTOR = 2

LANES = 128
ROW_CHUNKS = D_MODEL // LANES
VMEM_LIMIT = 56 * 1024 * 1024

COL_GA = 0
COL_GB = 2048
COL_GQ = 4096
COL_GK = 4608
COL_GV = 5120
COL_GR = 6144
COL_SQ = 7168
COL_SK = 8192
COL_SV = 8448
COL_DEC = 8704
IN_PAD = 8960
IN_TN = 1280
TAIL_COL0 = COL_SK
TAIL_W = COL_DEC + LANES - COL_SK


def _cparams(sem, **kw):
    return pltpu.CompilerParams(dimension_semantics=sem, vmem_limit_bytes=VMEM_LIMIT, **kw)


def _silu(x):
    return x * (1.0 / (1.0 + jnp.exp(-x)))


def _sigmoid(x):
    return 1.0 / (1.0 + jnp.exp(-x))


SLAB = 128


def _store_row_major(ref, x):
    for tb in range(x.shape[0] // SLAB):
        rows = slice(tb * SLAB, (tb + 1) * SLAB)
        parts = jnp.stack([x[rows, s * LANES:(s + 1) * LANES] for s in range(ROW_CHUNKS)], axis=0)
        ref[rows] = pltpu.einshape("stl->tsl", parts)


def _load_row_major(ref, rows):
    xt = pltpu.einshape("tsl->stl", ref[rows])
    return [xt[s] for s in range(ROW_CHUNKS)]


def _tiles_per_mod_row(mod, n, seq, tile):
    return seq // tile if mod.shape[0] > 1 else n // tile


def _mod_kernel(c_ref, w_ref, b_ref, o_ref):
    a = _silu(c_ref[...]).astype(BF16)
    o_ref[...] = jnp.dot(a, w_ref[...].astype(BF16), preferred_element_type=F32) + b_ref[...]


def _modulation(cond8, w_mod, b_mod):
    n_out = w_mod.shape[1]
    tn = 1536
    return pl.pallas_call(
        _mod_kernel,
        grid=(n_out // tn,),
        in_specs=[pl.BlockSpec((8, D_MODEL), lambda j: (0, 0)),
                  pl.BlockSpec((D_MODEL, tn), lambda j: (0, j)),
                  pl.BlockSpec((1, tn), lambda j: (0, j))],
        out_specs=pl.BlockSpec((8, tn), lambda j: (0, j)),
        out_shape=jax.ShapeDtypeStruct((8, n_out), F32),
        compiler_params=_cparams(("arbitrary",)),
        name="modulation",
    )(cond8, w_mod, b_mod.reshape(1, n_out))


NORM_ROWS = 64


def _in_proj_kernel(x_ref, sc_ref, sh_ref, g_ref, w_ref, o_ref, tail_ref, h_scr):
    j = pl.program_id(1)

    @pl.when(j == 0)
    def _():
        def slab(r, _):
            rows = pl.ds(pl.multiple_of(r * NORM_ROWS, NORM_ROWS), NORM_ROWS)
            x = x_ref[rows, :]
            y = x * lax.rsqrt(jnp.mean(x * x, axis=-1, keepdims=True) + EPS) * g_ref[...]
            h_scr[rows, :] = (y * (1.0 + sc_ref[0]) + sh_ref[0]).astype(BF16)
            return 0
        lax.fori_loop(0, x_ref.shape[0] // NORM_ROWS, slab, 0)

    acc = jnp.dot(h_scr[...], w_ref[...], preferred_element_type=F32)
    o_ref[...] = acc.astype(BF16)

    @pl.when(j == IN_PAD // IN_TN - 1)
    def _():
        first = TAIL_COL0 - (IN_PAD - IN_TN)
        tail_ref[...] = acc[:, first:first + TAIL_W]


def _in_proj(x2d, seq, sc, sh, g, w_r):
    n = x2d.shape[0]
    tm = min(1024, seq if sc.shape[0] > 1 else n)
    per_b = _tiles_per_mod_row(sc, n, seq, tm)
    return pl.pallas_call(
        _in_proj_kernel,
        grid=(n // tm, IN_PAD // IN_TN),
        in_specs=[pl.BlockSpec((tm, D_MODEL), lambda i, j: (i, 0)),
                  pl.BlockSpec((1, 1, D_MODEL), lambda i, j: (i // per_b, 0, 0)),
                  pl.BlockSpec((1, 1, D_MODEL), lambda i, j: (i // per_b, 0, 0)),
                  pl.BlockSpec((1, D_MODEL), lambda i, j: (0, 0)),
                  pl.BlockSpec((D_MODEL, IN_TN), lambda i, j: (0, j))],
        out_specs=[pl.BlockSpec((tm, IN_TN), lambda i, j: (i, j)),
                   pl.BlockSpec((tm, TAIL_W), lambda i, j: (i, 0))],
        out_shape=[jax.ShapeDtypeStruct((n, IN_PAD), BF16),
                   jax.ShapeDtypeStruct((n, TAIL_W), F32)],
        scratch_shapes=[pltpu.VMEM((tm, D_MODEL), BF16)],
        compiler_params=_cparams(("parallel", "arbitrary")),
        name="in_proj",
    )(x2d, sc, sh, g, w_r)


GLA_BLK = 256
GLA_HPS = 4


def _tn_dot(a, b, precision=None):
    return lax.dot_general(a, b, (((0,), (0,)), ((), ())), precision=precision,
                           preferred_element_type=F32)


def _nt_dot(a, b, precision=None):
    return lax.dot_general(a, b, (((1,), (1,)), ((), ())), precision=precision,
                           preferred_element_type=F32)


def _split_bf16(x, parts):
    out = []
    for _ in range(parts):
        piece = x.astype(BF16)
        out.append(piece)
        x = x - piece.astype(F32)
    return out


def _dot3(x, w):
    xh, xl = _split_bf16(x, 2)
    wh, wl = _split_bf16(w, 2)
    d = functools.partial(jnp.dot, preferred_element_type=F32)
    return d(xh, wh) + d(xl, wh) + d(xh, wl)


def _gla_kernel(q_ref, k_ref, v_ref, r_ref, dec_ref, wdf_ref, bdf_ref, wdb_ref, bdb_ref, gg_ref,
                s0f_ref, s0b_ref, o_ref, stf_ref, stb_ref, state_t, of_scr, *, nblk):
    i = pl.program_id(2)
    c = GLA_CHUNK
    nch = GLA_BLK // c
    ri = lax.broadcasted_iota(jnp.int32, (GLA_BLK, GLA_BLK), 0)
    ci = lax.broadcasted_iota(jnp.int32, (GLA_BLK, GLA_BLK), 1)
    same_chunk = (ri // c) == (ci // c)

    @pl.when(i == 0)
    def _():
        for h in range(GLA_HPS):
            state_t[h] = s0f_ref[0, h].T

    @pl.when(i == nblk)
    def _():
        for h in range(GLA_HPS):
            state_t[h] = s0b_ref[0, h].T

    def run(fwd):
        blk = i if fwd else 2 * nblk - 1 - i
        keep = same_chunk & ((ci <= ri) if fwd else (ci >= ri))
        w_ref, b_ref = (wdf_ref, bdf_ref) if fwd else (wdb_ref, bdb_ref)
        logit = _dot3(dec_ref[...], w_ref[...]) + b_ref[...]
        g = (jnp.minimum(logit, 0.0) - jnp.log(1.0 + jnp.exp(-jnp.abs(logit)))) / GLA_TAU
        tri = keep.astype(BF16)
        b = sum(jnp.dot(tri, piece, preferred_element_type=F32) for piece in _split_bf16(g, 3))
        edge = c - 1 if fwd else 0
        tots = [b[ch * c + edge:ch * c + edge + 1, :] for ch in range(nch)]
        totb = jnp.concatenate([jnp.broadcast_to(t, (c, t.shape[1])) for t in tots], axis=0)
        q_in = (q_ref[...].astype(F32) * (GLA_DK ** -0.5) * jnp.exp(b)).astype(BF16)
        k = k_ref[...].astype(F32)
        k_in = (k * jnp.exp(-b)).astype(BF16)
        k_st = (k * jnp.exp(totb - b)).astype(BF16)
        srow = pl.multiple_of(blk * GLA_BLK, GLA_BLK)
        for h in range(GLA_HPS):
            ks = slice(h * GLA_DK, (h + 1) * GLA_DK)
            vs = slice(h * GLA_DV, (h + 1) * GLA_DV)
            v = v_ref[:, vs].astype(BF16)
            a = jnp.where(keep, _nt_dot(q_in[:, ks], k_in[:, ks]), 0.0).astype(BF16)
            o_intra = jnp.dot(a, v, preferred_element_type=F32)
            st = state_t[h]
            o_inter = [None] * nch
            for cc in range(nch):
                ch = cc if fwd else nch - 1 - cc
                rows = slice(ch * c, (ch + 1) * c)
                o_inter[ch] = _nt_dot(q_in[rows, ks], st.astype(BF16))
                st = st * jnp.exp(tots[ch][:, ks]) + _tn_dot(v[rows, :], k_st[rows, ks])
            state_t[h] = st
            o = o_intra + jnp.concatenate(o_inter, axis=0)
            if fwd:
                of_scr[pl.ds(srow, GLA_BLK), vs] = o
            else:
                o = o + of_scr[pl.ds(srow, GLA_BLK), vs]
                o = o * lax.rsqrt(jnp.mean(o * o, axis=-1, keepdims=True) + EPS) * gg_ref[:, vs]
                o_ref[:, vs] = (o * _silu(r_ref[:, vs].astype(F32))).astype(BF16)

    @pl.when(i < nblk)
    def _():
        run(True)

    @pl.when(i >= nblk)
    def _():
        run(False)

    @pl.when(i == nblk - 1)
    def _():
        for h in range(GLA_HPS):
            stf_ref[0, h] = state_t[h].T

    @pl.when(i == 2 * nblk - 1)
    def _():
        for h in range(GLA_HPS):
            stb_ref[0, h] = state_t[h].T


def _gla(p, dec, bsz, seq, wdf, bdf, wdb, bdb, g_gla, s0f, s0b):
    n = p.shape[0]
    nblk = seq // GLA_BLK
    kw = GLA_HPS * GLA_DK
    vw = GLA_HPS * GLA_DV

    def rb(b, i):
        return b * nblk + jnp.where(i < nblk, i, 2 * nblk - 1 - i)

    def orb(b, i):
        return b * nblk + jnp.where(i < nblk, nblk - 1, 2 * nblk - 1 - i)

    st_spec = pl.BlockSpec((1, GLA_HPS, GLA_DK, GLA_DV), lambda b, h, i: (b, h, 0, 0))
    s0_spec = pl.BlockSpec((1, GLA_HPS, GLA_DK, GLA_DV),
                           lambda b, h, i: (b if s0f.shape[0] > 1 else 0, h, 0, 0))
    st_shape = jax.ShapeDtypeStruct((bsz, GLA_HEADS, GLA_DK, GLA_DV), F32)
    return pl.pallas_call(
        functools.partial(_gla_kernel, nblk=nblk),
        grid=(bsz, GLA_HEADS // GLA_HPS, 2 * nblk),
        in_specs=[pl.BlockSpec((GLA_BLK, kw), lambda b, h, i: (rb(b, i), COL_GQ // kw + h)),
                  pl.BlockSpec((GLA_BLK, kw), lambda b, h, i: (rb(b, i), COL_GK // kw + h)),
                  pl.BlockSpec((GLA_BLK, vw), lambda b, h, i: (rb(b, i), COL_GV // vw + h)),
                  pl.BlockSpec((GLA_BLK, vw), lambda b, h, i: (rb(b, i), COL_GR // vw + h)),
                  pl.BlockSpec((GLA_BLK, LANES), lambda b, h, i: (rb(b, i), (COL_DEC - TAIL_COL0) // LANES)),
                  pl.BlockSpec((LANES, kw), lambda b, h, i: (0, h)),
                  pl.BlockSpec((1, kw), lambda b, h, i: (0, h)),
                  pl.BlockSpec((LANES, kw), lambda b, h, i: (0, h)),
                  pl.BlockSpec((1, kw), lambda b, h, i: (0, h)),
                  pl.BlockSpec((1, vw), lambda b, h, i: (0, h)),
                  s0_spec, s0_spec],
        out_specs=[pl.BlockSpec((GLA_BLK, vw), lambda b, h, i: (orb(b, i), h)), st_spec, st_spec],
        out_shape=[jax.ShapeDtypeStruct((n, GLA_VW), BF16), st_shape, st_shape],
        scratch_shapes=[pltpu.VMEM((GLA_HPS, GLA_DV, GLA_DK), F32), pltpu.VMEM((seq, vw), F32)],
        compiler_params=_cparams(("parallel", "parallel", "arbitrary")),
        name="gla",
    )(p, p, p, p, dec, wdf, bdf, wdb, bdb, g_gla, s0f, s0b)


def _sink_column(sink_ref, kvh, rows_per_head):
    r = lax.broadcasted_iota(jnp.int32, (SWA_GROUP * rows_per_head, 1), 0)
    col = jnp.zeros((SWA_GROUP * rows_per_head, 1), F32)
    for g in range(SWA_GROUP):
        col = jnp.where(r // rows_per_head == g, sink_ref[kvh * SWA_GROUP + g], col)
    return col


def _rope(x, cos, sin):
    outs = []
    for cb in range(x.shape[1] // LANES):
        sl = slice(cb * LANES, (cb + 1) * LANES)
        xc = x[:, sl]
        lane = lax.broadcasted_iota(jnp.int32, xc.shape, 1)
        sw = jnp.where(lane % 32 < 16, pltpu.roll(xc, LANES - 16, 1), pltpu.roll(xc, 16, 1))
        outs.append(xc * cos[:, sl] + sw * sin[:, sl])
    return jnp.concatenate(outs, axis=1)


def _swa_lat_kernel(sink_ref, q_ref, kp_ref, kc_ref, kn_ref, vp_ref, vc_ref, vn_ref, ck_ref, cv_ref,
                    cp_ref, cc_ref, cn_ref, sp_ref, sc_ref, sn_ref, o_ref, *, nb):
    n = pl.program_id(1)
    blk = SWA_BLOCK
    hd = SWA_HEAD_DIM
    cos_c, sin_c = cc_ref[...], sc_ref[...]
    kband = jnp.concatenate([_rope(kp_ref[...].astype(F32), cp_ref[...], sp_ref[...]),
                             _rope(kc_ref[...].astype(F32), cos_c, sin_c),
                             _rope(kn_ref[...].astype(F32), cn_ref[...], sn_ref[...])], axis=0).astype(BF16)
    vband = jnp.concatenate([vp_ref[...], vc_ref[...], vn_ref[...]], axis=0).astype(BF16)
    ck = ck_ref[0].astype(BF16)
    cv = cv_ref[0].astype(BF16)
    qi = lax.broadcasted_iota(jnp.int32, (blk, 3 * blk), 0)
    kj = lax.broadcasted_iota(jnp.int32, (blk, 3 * blk), 1)
    k_abs = kj + (n - 1) * blk
    mask = (jnp.abs(kj - blk - qi) <= SWA_BLOCK) & (k_abs >= 0) & (k_abs < nb * blk)
    bias = jnp.concatenate([jnp.where(mask, 0.0, -1e30)] * SWA_GROUP, axis=0)
    for kvh in range(SWA_KV_HEADS):
        ks = slice(kvh * hd, (kvh + 1) * hd)
        qr = _rope(q_ref[:, kvh * SWA_KVW:(kvh + 1) * SWA_KVW].astype(F32), cos_c, sin_c) * (hd ** -0.5)
        qg = jnp.concatenate([qr[:, g * hd:(g + 1) * hd] for g in range(SWA_GROUP)],
                             axis=0).astype(BF16)
        s_b = _nt_dot(qg, kband[:, ks]) + bias
        s_c = _nt_dot(qg, ck[:, ks])
        sink = _sink_column(sink_ref, kvh, blk)
        m = jnp.maximum(jnp.maximum(jnp.max(s_b, axis=-1, keepdims=True),
                                    jnp.max(s_c, axis=-1, keepdims=True)), sink)
        p_b = jnp.exp(s_b - m)
        p_c = jnp.exp(s_c - m)
        den = jnp.exp(sink - m) + jnp.sum(p_b, axis=-1, keepdims=True) + jnp.sum(p_c, axis=-1, keepdims=True)
        o = (jnp.dot(p_b.astype(BF16), vband[:, ks], preferred_element_type=F32)
             + jnp.dot(p_c.astype(BF16), cv[:, ks], preferred_element_type=F32)) / den
        for g in range(SWA_GROUP):
            cb = (kvh * SWA_GROUP + g) * hd
            o_ref[:, cb:cb + hd] = o[g * blk:(g + 1) * blk, :].astype(BF16)


def _swa_latent(p, bsz, seq, cache_k, cache_v, sink, cos_t, sin_t):
    n = p.shape[0]
    nb = seq // SWA_BLOCK
    kcol = COL_SK // SWA_KVW
    vcol = COL_SV // SWA_KVW
    prev = lambda i: jnp.maximum(i - 1, 0)
    nxt = lambda i: jnp.minimum(i + 1, nb - 1)

    def pspec(col, f):
        return pl.BlockSpec((SWA_BLOCK, SWA_KVW), lambda b, i, s: (b * nb + f(i), col))

    def tspec(f):
        return pl.BlockSpec((SWA_BLOCK, SWA_KVW), lambda b, i, s: (f(i), 0))

    same = lambda i: i
    cspec = pl.BlockSpec((1, cache_k.shape[1], SWA_KVW), lambda b, i, s: (b, 0, 0))
    grid_spec = pltpu.PrefetchScalarGridSpec(
        num_scalar_prefetch=1,
        grid=(bsz, nb),
        in_specs=[pl.BlockSpec((SWA_BLOCK, SWA_QW), lambda b, i, s: (b * nb + i, COL_SQ // SWA_QW)),
                  pspec(kcol, prev), pspec(kcol, same), pspec(kcol, nxt),
                  pspec(vcol, prev), pspec(vcol, same), pspec(vcol, nxt),
                  cspec, cspec,
                  tspec(prev), tspec(same), tspec(nxt), tspec(prev), tspec(same), tspec(nxt)],
        out_specs=pl.BlockSpec((SWA_BLOCK, SWA_QW), lambda b, i, s: (b * nb + i, 0)),
    )
    return pl.pallas_call(
        functools.partial(_swa_lat_kernel, nb=nb),
        grid_spec=grid_spec,
        out_shape=jax.ShapeDtypeStruct((n, SWA_QW), BF16),
        compiler_params=_cparams(("parallel", "arbitrary")),
        name="swa_latent",
    )(sink, p, p, p, p, p, p, p, cache_k, cache_v, cos_t, cos_t, cos_t, sin_t, sin_t, sin_t)


def _swa_ctx_kernel(sink_ref, q_ref, k_ref, v_ref, o_ref):
    s = q_ref.shape[0]
    hd = SWA_HEAD_DIM
    kb = k_ref[...].astype(BF16)
    vb = v_ref[...].astype(BF16)
    for kvh in range(SWA_KV_HEADS):
        ks = slice(kvh * hd, (kvh + 1) * hd)
        qg = jnp.concatenate([q_ref[:, (kvh * SWA_GROUP + g) * hd:(kvh * SWA_GROUP + g + 1) * hd]
                              for g in range(SWA_GROUP)], axis=0)
        qg = (qg.astype(F32) * (hd ** -0.5)).astype(BF16)
        sc = _nt_dot(qg, kb[:, ks])
        sink = _sink_column(sink_ref, kvh, s)
        m = jnp.maximum(jnp.max(sc, axis=-1, keepdims=True), sink)
        pr = jnp.exp(sc - m)
        den = jnp.exp(sink - m) + jnp.sum(pr, axis=-1, keepdims=True)
        o = jnp.dot(pr.astype(BF16), vb[:, ks], preferred_element_type=F32) / den
        for g in range(SWA_GROUP):
            cb = (kvh * SWA_GROUP + g) * hd
            o_ref[:, cb:cb + hd] = o[g * s:(g + 1) * s, :].astype(BF16)


def _swa_context(p, bsz, seq, sink):
    n = p.shape[0]
    grid_spec = pltpu.PrefetchScalarGridSpec(
        num_scalar_prefetch=1,
        grid=(bsz,),
        in_specs=[pl.BlockSpec((seq, SWA_QW), lambda b, s: (b, COL_SQ // SWA_QW)),
                  pl.BlockSpec((seq, SWA_KVW), lambda b, s: (b, COL_SK // SWA_KVW)),
                  pl.BlockSpec((seq, SWA_KVW), lambda b, s: (b, COL_SV // SWA_KVW))],
        out_specs=pl.BlockSpec((seq, SWA_QW), lambda b, s: (b, 0)),
    )
    return pl.pallas_call(
        _swa_ctx_kernel,
        grid_spec=grid_spec,
        out_shape=jax.ShapeDtypeStruct((n, SWA_QW), BF16),
        compiler_params=_cparams(("parallel",)),
        name="swa_context",
    )(sink, p, p, p)


MERGE_TM = 512


def _branch_kernel(og_ref, os_ref, ga_ref, gb_ref, wa_ref, wb_ref, m_ref):
    a = jnp.dot(og_ref[...], wa_ref[...], preferred_element_type=F32)
    b = jnp.dot(os_ref[...], wb_ref[...], preferred_element_type=F32)
    m_ref[...] = (_sigmoid(ga_ref[...].astype(F32)) * a + _sigmoid(gb_ref[...].astype(F32)) * b).astype(BF16)


def _branch_merge(o_gla, o_swa, p, wa, wb):
    n = o_gla.shape[0]
    tm = MERGE_TM
    once = pl.Buffered(1)
    return pl.pallas_call(
        _branch_kernel,
        grid=(n // tm,),
        in_specs=[pl.BlockSpec((tm, GLA_VW), lambda i: (i, 0)),
                  pl.BlockSpec((tm, SWA_QW), lambda i: (i, 0)),
                  pl.BlockSpec((tm, D_MODEL), lambda i: (i, COL_GA // D_MODEL)),
                  pl.BlockSpec((tm, D_MODEL), lambda i: (i, COL_GB // D_MODEL)),
                  pl.BlockSpec((GLA_VW, D_MODEL), lambda i: (0, 0), pipeline_mode=once),
                  pl.BlockSpec((SWA_QW, D_MODEL), lambda i: (0, 0), pipeline_mode=once)],
        out_specs=pl.BlockSpec((tm, D_MODEL), lambda i: (i, 0)),
        out_shape=jax.ShapeDtypeStruct((n, D_MODEL), BF16),
        compiler_params=_cparams(("parallel",)),
        name="branch_merge",
    )(o_gla, o_swa, p, p, wa, wb)


def _out_kernel(m_ref, x_ref, gt1_ref, sc2_ref, sh2_ref, g2_ref, wo_ref, wr2_ref,
                x1_ref, h2_ref, lg_ref):
    x1 = x_ref[...] + gt1_ref[0] * jnp.dot(m_ref[...], wo_ref[...], preferred_element_type=F32)
    x1_ref[...] = x1
    y = x1 * lax.rsqrt(jnp.mean(x1 * x1, axis=-1, keepdims=True) + EPS) * g2_ref[...]
    h2 = y * (1.0 + sc2_ref[0]) + sh2_ref[0]
    _store_row_major(h2_ref, h2)
    l2 = jnp.dot(h2.astype(BF16), wr2_ref[...], preferred_element_type=F32)
    lt = l2.T
    lt = lt[0:N_EXPERTS, :] + lt[N_EXPERTS:2 * N_EXPERTS, :]
    for cb in range(lt.shape[1] // LANES):
        lg_ref[cb] = lt[:, cb * LANES:(cb + 1) * LANES]


def _out_proj(merged, x2d, seq, gt1, sc2, sh2, g2, wo, wr2):
    n = x2d.shape[0]
    tm = MERGE_TM
    per_b = _tiles_per_mod_row(gt1, n, seq, tm)
    mod_spec = pl.BlockSpec((1, 1, D_MODEL), lambda i: (i // per_b, 0, 0))
    once = pl.Buffered(1)
    return pl.pallas_call(
        _out_kernel,
        grid=(n // tm,),
        in_specs=[pl.BlockSpec((tm, D_MODEL), lambda i: (i, 0)),
                  pl.BlockSpec((tm, D_MODEL), lambda i: (i, 0)),
                  mod_spec, mod_spec, mod_spec,
                  pl.BlockSpec((1, D_MODEL), lambda i: (0, 0)),
                  pl.BlockSpec((D_MODEL, D_MODEL), lambda i: (0, 0), pipeline_mode=once),
                  pl.BlockSpec((D_MODEL, LANES), lambda i: (0, 0))],
        out_specs=[pl.BlockSpec((tm, D_MODEL), lambda i: (i, 0)),
                   pl.BlockSpec((tm, ROW_CHUNKS, LANES), lambda i: (i, 0, 0)),
                   pl.BlockSpec((tm // LANES, N_EXPERTS, LANES), lambda i: (i, 0, 0))],
        out_shape=[jax.ShapeDtypeStruct((n, D_MODEL), F32),
                   jax.ShapeDtypeStruct((n, ROW_CHUNKS, LANES), F32),
                   jax.ShapeDtypeStruct((n // LANES, N_EXPERTS, LANES), F32)],
        compiler_params=_cparams(("parallel",)),
        name="out_proj",
    )(merged, x2d, gt1, sc2, sh2, g2, wo, wr2)


def _select_kernel(lg_ref, aff_ref, sel_ref, pos_ref, *, cap):
    nblk = lg_ref.shape[0]
    lg = lg_ref[...]
    ex = jnp.exp(lg - jnp.max(lg, axis=1, keepdims=True))
    aff = ex / jnp.sum(ex, axis=1, keepdims=True)
    aff_ref[...] = aff
    bits = lax.bitcast_convert_type(aff, jnp.int32)

    def count(pred):
        c = jnp.sum(jnp.where(pred, 1.0, 0.0), axis=0, keepdims=True)
        return jnp.sum(c, axis=2, keepdims=True)

    def bit_step(t, cur):
        cand = cur | jnp.left_shift(jnp.int32(1), 30 - t)
        return jnp.where(count(bits >= cand) >= cap, cand, cur)

    thr = lax.fori_loop(0, 31, bit_step, jnp.zeros((1, N_EXPERTS, 1), jnp.int32))
    need = (cap - count(bits > thr))[0]
    thr2 = thr[0]
    ri = lax.broadcasted_iota(jnp.int32, (LANES, LANES), 0)
    ci = lax.broadcasted_iota(jnp.int32, (LANES, LANES), 1)
    upper = (ri <= ci).astype(BF16)

    def blk_step(b, carry):
        run_eq, run_sel = carry
        bb = lax.bitcast_convert_type(aff_ref[b], jnp.int32)
        eq = (bb == thr2).astype(F32)
        eq_excl = jnp.dot(eq.astype(BF16), upper, preferred_element_type=F32) - eq + run_eq
        sel = jnp.where((bb > thr2) | ((eq > 0) & (eq_excl < need)), 1.0, 0.0)
        sel_ref[b] = sel
        pos_ref[b] = jnp.dot(sel.astype(BF16), upper, preferred_element_type=F32) - sel + run_sel
        return (run_eq + jnp.sum(eq, axis=1, keepdims=True),
                run_sel + jnp.sum(sel, axis=1, keepdims=True))

    zero = jnp.zeros((N_EXPERTS, 1), F32)
    lax.fori_loop(0, nblk, blk_step, (zero, zero))


def _select(logits3, cap):
    shp = jax.ShapeDtypeStruct(logits3.shape, F32)
    return pl.pallas_call(
        functools.partial(_select_kernel, cap=float(cap)),
        out_shape=[shp, shp, shp],
        compiler_params=pltpu.CompilerParams(vmem_limit_bytes=VMEM_LIMIT),
        name="select",
    )(logits3)


COMPACT_WIN = 136


def _compact_kernel(base_ref, aff_ref, sel_ref, pos_ref, idx_ref, gate_ref):
    nblk = aff_ref.shape[0]
    idx_ref[...] = jnp.zeros_like(idx_ref)
    gate_ref[...] = jnp.zeros_like(gate_ref)
    slot = lax.broadcasted_iota(jnp.int32, (COMPACT_WIN, LANES), 0).astype(F32)
    lane = lax.broadcasted_iota(jnp.int32, (COMPACT_WIN, LANES), 1)

    def blk_step(b, _):
        aff = aff_ref[b]
        sel = sel_ref[b]
        pos = pos_ref[b]
        tok = (lane + b * LANES).astype(F32)
        for e in range(N_EXPERTS):
            base8 = pl.multiple_of((base_ref[b * N_EXPERTS + e] >> 3) << 3, 8)
            rel = pos[e:e + 1, :] - base8.astype(F32)
            hit = (rel == slot) & (sel[e:e + 1, :] > 0)
            rows = pl.ds(base8, COMPACT_WIN)
            mine = lane == e
            ic = jnp.sum(jnp.where(hit, tok, 0.0), axis=1, keepdims=True)
            gc = jnp.sum(jnp.where(hit, aff[e:e + 1, :], 0.0), axis=1, keepdims=True)
            idx_ref[rows, :] += jnp.where(mine, ic, 0.0)
            gate_ref[rows, :] += jnp.where(mine, gc, 0.0)
        return 0

    lax.fori_loop(0, nblk, blk_step, 0)


def _compact(base, aff3, sel3, pos3, cap):
    cp = cap + 2 * LANES
    full = pl.BlockSpec(aff3.shape, lambda i, s: (0, 0, 0))
    ospec = pl.BlockSpec((cp, LANES), lambda i, s: (0, 0))
    oshape = jax.ShapeDtypeStruct((cp, LANES), F32)
    grid_spec = pltpu.PrefetchScalarGridSpec(
        num_scalar_prefetch=1, grid=(1,), in_specs=[full, full, full], out_specs=[ospec, ospec])
    return pl.pallas_call(
        _compact_kernel,
        grid_spec=grid_spec,
        out_shape=[oshape, oshape],
        compiler_params=_cparams(("arbitrary",)),
        name="compact",
    )(base, aff3, sel3, pos3)


FFN_COLS = 256


def _up_kernel(idx_ref, h2_hbm, wg_ref, wu_ref, hid_ref, gbuf, xs, sem, *, tm, total):
    step = pl.program_id(0) * pl.num_programs(1) + pl.program_id(1)

    def row_copy(step_, i):
        return pltpu.make_async_copy(h2_hbm.at[idx_ref[step_ * tm + i]], gbuf.at[i], sem)

    def wait_rows():
        pltpu.make_async_copy(h2_hbm.at[pl.ds(0, tm)], gbuf, sem).wait()

    @pl.when(step == 0)
    def _():
        def body(i, _):
            row_copy(step, i).start()
            return 0
        lax.fori_loop(0, tm, body, 0)

    wait_rows()
    slab = min(SLAB, tm)
    for tb in range(tm // slab):
        rows = slice(tb * slab, (tb + 1) * slab)
        for s, chunk in enumerate(_load_row_major(gbuf, rows)):
            xs[rows, s * LANES:(s + 1) * LANES] = chunk.astype(BF16)

    nxt = jnp.minimum(step + 1, total - 1)
    for i in range(tm):
        row_copy(nxt, i).start()

    for f in range(EXPERT_FF // FFN_COLS):
        cols = slice(f * FFN_COLS, (f + 1) * FFN_COLS)
        hg = jnp.dot(xs[...], wg_ref[0, :, cols].astype(BF16), preferred_element_type=F32)
        hu = jnp.dot(xs[...], wu_ref[0, :, cols].astype(BF16), preferred_element_type=F32)
        hid_ref[:, cols] = (_silu(hg) * hu).astype(BF16)

    @pl.when(step == total - 1)
    def _():
        wait_rows()


def _expert_up(idx_flat, h2_rm, wg, wu, cap):
    tm = min(1024, cap)
    nr = cap // tm
    grid_spec = pltpu.PrefetchScalarGridSpec(
        num_scalar_prefetch=1,
        grid=(N_EXPERTS, nr),
        in_specs=[pl.BlockSpec(memory_space=pl.ANY),
                  pl.BlockSpec((1, D_MODEL, EXPERT_FF), lambda e, r, s: (e, 0, 0)),
                  pl.BlockSpec((1, D_MODEL, EXPERT_FF), lambda e, r, s: (e, 0, 0))],
        out_specs=pl.BlockSpec((tm, EXPERT_FF), lambda e, r, s: (e * nr + r, 0)),
        scratch_shapes=[pltpu.VMEM((tm, ROW_CHUNKS, LANES), F32),
                        pltpu.VMEM((tm, D_MODEL), BF16),
                        pltpu.SemaphoreType.DMA(())],
    )
    return pl.pallas_call(
        functools.partial(_up_kernel, tm=tm, total=N_EXPERTS * nr),
        grid_spec=grid_spec,
        out_shape=jax.ShapeDtypeStruct((N_EXPERTS * cap, EXPERT_FF), BF16),
        compiler_params=_cparams(("arbitrary", "arbitrary")),
        name="expert_up",
    )(idx_flat, h2_rm, wg, wu)


def _down_kernel(hid_ref, gate_ref, wd_ref, o_ref, ybuf):
    gate = gate_ref[...]
    for c in range(D_MODEL // FFN_COLS):
        cols = slice(c * FFN_COLS, (c + 1) * FFN_COLS)
        ybuf[:, cols] = gate * jnp.dot(hid_ref[...], wd_ref[0, :, cols].astype(BF16),
                                       preferred_element_type=F32)
    tm = ybuf.shape[0]
    slab = min(SLAB, tm)
    for tb in range(tm // slab):
        rows = slice(tb * slab, (tb + 1) * slab)
        parts = jnp.stack([ybuf[rows, s * LANES:(s + 1) * LANES] for s in range(ROW_CHUNKS)], axis=0)
        o_ref[rows] = pltpu.einshape("stl->tsl", parts)


def _expert_down(hid, gate_col, wd, cap):
    tm = min(1024, cap)
    nr = cap // tm
    return pl.pallas_call(
        _down_kernel,
        grid=(N_EXPERTS, nr),
        in_specs=[pl.BlockSpec((tm, EXPERT_FF), lambda e, r: (e * nr + r, 0)),
                  pl.BlockSpec((tm, 1), lambda e, r: (e * nr + r, 0)),
                  pl.BlockSpec((1, EXPERT_FF, D_MODEL), lambda e, r: (e, 0, 0))],
        out_specs=pl.BlockSpec((tm, ROW_CHUNKS, LANES), lambda e, r: (e * nr + r, 0, 0)),
        out_shape=jax.ShapeDtypeStruct((N_EXPERTS * cap, ROW_CHUNKS, LANES), F32),
        scratch_shapes=[pltpu.VMEM((tm, D_MODEL), F32)],
        compiler_params=_cparams(("parallel", "arbitrary")),
        name="expert_down",
    )(hid, gate_col, wd)


COMBINE_UNROLL = 4
COMBINE_CHUNK = 32
COMBINE_NCHUNK = 3


def _combine_kernel(idx_ref, lo_ref, ye_hbm, x1_ref, gt2_ref, gf_ref, o_ref, stage, acc, sem,
                    *, tt, ch, cap, ntiles):
    tile = pl.program_id(0)
    slot = tile % 2

    window = COMBINE_NCHUNK * ch

    def chunk_start(first_row):
        return jnp.minimum(first_row, cap - window)

    def chunk_copy(slot_, e, start, c):
        return pltpu.make_async_copy(ye_hbm.at[pl.ds(e * cap + start + c * ch, ch)],
                                     stage.at[slot_, e, pl.ds(c * ch, ch)],
                                     sem.at[slot_, e])

    def for_needed_chunks(tile_, e, fn):
        lo = lo_ref[tile_ * N_EXPERTS + e]
        hi = lo_ref[(tile_ + 1) * N_EXPERTS + e]
        start = chunk_start(lo)
        for c in range(COMBINE_NCHUNK):
            @pl.when((start + c * ch < hi) & (start + (c + 1) * ch > lo))
            def _():
                fn(start, c)

    def issue(tile_, slot_):
        for e in range(N_EXPERTS):
            for_needed_chunks(tile_, e, lambda start, c: chunk_copy(slot_, e, start, c).start())

    @pl.when(tile == 0)
    def _():
        issue(tile, slot)

    @pl.when(tile + 1 < ntiles)
    def _():
        issue(tile + 1, 1 - slot)

    acc[...] = jnp.zeros_like(acc)

    def add_rows(e, start, r0, r1):
        def group(r, width):
            dsts, vals = [], []
            rows = stage[slot, e, pl.ds(r - start, width)]
            for u in range(width):
                dst = idx_ref[e * cap + r + u] - tile * tt
                dsts.append(dst)
                vals.append(acc[dst] + rows[u])
            for dst, val in zip(dsts, vals):
                acc[dst] = val

        ngroups = (r1 - r0) // COMBINE_UNROLL

        def body_group(j, _):
            group(r0 + j * COMBINE_UNROLL, COMBINE_UNROLL)
            return 0
        lax.fori_loop(0, ngroups, body_group, 0)

        def body_one(r, _):
            group(r, 1)
            return 0
        lax.fori_loop(r0 + ngroups * COMBINE_UNROLL, r1, body_one, 0)

    for e in range(N_EXPERTS):
        lo = lo_ref[tile * N_EXPERTS + e]
        hi = lo_ref[(tile + 1) * N_EXPERTS + e]
        start = chunk_start(lo)
        for_needed_chunks(tile, e, lambda start_, c: chunk_copy(slot, e, start_, c).wait())
        first_end = jnp.minimum(hi, start + window)
        add_rows(e, start, lo, first_end)

        def more(r0):
            st = jnp.minimum(r0, cap - ch)
            cp = chunk_copy(slot, e, st, 0)
            cp.start()
            cp.wait()
            r1 = jnp.minimum(hi, st + ch)
            add_rows(e, st, r0, r1)
            return r1
        lax.while_loop(lambda r0: r0 < hi, more, first_end)

    for tb in range(tt // SLAB):
        rows = slice(tb * SLAB, (tb + 1) * SLAB)
        ssq = jnp.zeros((SLAB, 1), F32)
        for s, y in enumerate(_load_row_major(acc, rows)):
            cols = slice(s * LANES, (s + 1) * LANES)
            x2 = x1_ref[rows, cols] + gt2_ref[0][:, cols] * y
            o_ref[rows, cols] = x2
            ssq = ssq + jnp.sum(x2 * x2, axis=-1, keepdims=True)
        o_ref[rows, :] = o_ref[rows, :] * lax.rsqrt(ssq / D_MODEL + EPS) * gf_ref[...]


def _combine(idx_flat, lo_tab, ye_lin, x1, seq, gt2, g_final, cap, tt):
    n = x1.shape[0]
    ch = min(COMBINE_CHUNK, cap // COMBINE_NCHUNK)
    ntiles = n // tt
    per_b = _tiles_per_mod_row(gt2, n, seq, tt)
    grid_spec = pltpu.PrefetchScalarGridSpec(
        num_scalar_prefetch=2,
        grid=(ntiles,),
        in_specs=[pl.BlockSpec(memory_space=pl.ANY),
                  pl.BlockSpec((tt, D_MODEL), lambda i, a, b: (i, 0)),
                  pl.BlockSpec((1, 1, D_MODEL), lambda i, a, b: (i // per_b, 0, 0)),
                  pl.BlockSpec((1, D_MODEL), lambda i, a, b: (0, 0))],
        out_specs=pl.BlockSpec((tt, D_MODEL), lambda i, a, b: (i, 0)),
        scratch_shapes=[pltpu.VMEM((2, N_EXPERTS, COMBINE_NCHUNK * ch, ROW_CHUNKS, LANES), F32),
                        pltpu.VMEM((tt, ROW_CHUNKS, LANES), F32),
                        pltpu.SemaphoreType.DMA((2, N_EXPERTS))],
    )
    return pl.pallas_call(
        functools.partial(_combine_kernel, tt=tt, ch=ch, cap=cap, ntiles=ntiles),
        grid_spec=grid_spec,
        out_shape=jax.ShapeDtypeStruct((n, D_MODEL), F32),
        compiler_params=_cparams(("arbitrary",)),
        name="combine",
    )(idx_flat, lo_tab, ye_lin, x1, gt2, g_final)


def _rope_tables(seq):
    pos = jnp.arange(seq)
    row = (pos // GRID_W).astype(F32)
    col = (pos % GRID_W).astype(F32)
    npair = SWA_HEAD_DIM // 4
    inv_freq = ROPE_BASE ** (-jnp.arange(npair, dtype=F32) / npair)
    ar = row[:, None] * inv_freq[None, :]
    ac = col[:, None] * inv_freq[None, :]
    cos = jnp.concatenate([jnp.cos(ar), jnp.cos(ar), jnp.cos(ac), jnp.cos(ac)], axis=1)
    sin = jnp.concatenate([-jnp.sin(ar), jnp.sin(ar), -jnp.sin(ac), jnp.sin(ac)], axis=1)
    reps = SWA_KVW // SWA_HEAD_DIM
    return jnp.tile(cos, (1, reps)), jnp.tile(sin, (1, reps))


def _layer(x, mods, wts, latent, s0f, s0b, cache_k, cache_v):
    bsz, seq, _ = x.shape
    n = bsz * seq
    x2d = x.reshape(n, D_MODEL)
    sh1, sc1, gt1, sh2, sc2, gt2 = mods
    p, tail = _in_proj(x2d, seq, sc1, sh1, wts["g1"], wts["w_in_r"])
    o_gla, st_f, st_b = _gla(p, tail, bsz, seq, wts["wdf"], wts["bdf"], wts["wdb"], wts["bdb"], wts["g_gla"],
                             s0f, s0b)
    if latent:
        cos_t, sin_t = _rope_tables(seq)
        o_swa = _swa_latent(p, bsz, seq, cache_k, cache_v, wts["sink"], cos_t, sin_t)
    else:
        o_swa = _swa_context(p, bsz, seq, wts["sink"])
    merged = _branch_merge(o_gla, o_swa, p, wts["wa"], wts["wb"])
    x1, h2, logits3 = _out_proj(merged, x2d, seq, gt1, sc2, sh2, wts["g2"], wts["wo"], wts["wr2"])
    cap = CAPACITY_FACTOR * n // N_EXPERTS
    aff3, sel3, pos3 = _select(logits3, cap)
    base = jnp.concatenate([pos3[:, :, 0].astype(jnp.int32), jnp.full((1, N_EXPERTS), cap, jnp.int32)], axis=0)
    idx_c, gate_c = _compact(base.reshape(-1), aff3, sel3, pos3, cap)
    idx_flat = idx_c[:cap, :N_EXPERTS].T.astype(jnp.int32).reshape(-1)
    gate_col = gate_c[:cap, :N_EXPERTS].T.reshape(-1, 1)
    hid = _expert_up(idx_flat, h2, wts["wg"], wts["wu"], cap)
    ye = _expert_down(hid, gate_col, wts["wd"], cap)
    tt = 256
    lo_tab = base[::tt // LANES]
    y = _combine(idx_flat, lo_tab.reshape(-1), ye, x1, seq, gt2, wts["g_final"], cap, tt)
    return y.reshape(bsz, seq, D_MODEL), tail, st_f, st_b


def kernel(x_prompt, x_sample, state_gla_fwd, state_gla_bwd, cache_k, cache_v, c, c_ctx, w_mod, b_mod,
           g_norm1, w_in, w_dec_f, b_dec_f, w_dec_b, b_dec_b, g_gla, attn_sink, w_branch_a, w_branch_b,
           w_out, g_norm2, w_router, w_exp_gate, w_exp_up, w_exp_down, g_final):
    bp, sp, _ = x_prompt.shape
    bl = x_sample.shape[0]
    l = 0
    gla_end = 2 * GLA_KW + 2 * GLA_VW
    dec_end = gla_end + 2 * GLA_RANK
    swa_end = dec_end + SWA_QW + 2 * SWA_KVW
    w = w_in[l]
    w_in_r = jnp.concatenate(
        [w[:, swa_end:].astype(BF16), w[:, :gla_end].astype(BF16), w[:, dec_end:swa_end].astype(BF16),
         w[:, gla_end:dec_end].astype(BF16), jnp.zeros((D_MODEL, IN_PAD - w.shape[1]), BF16)], axis=1)
    zpad = jnp.zeros((LANES - 2 * GLA_RANK, GLA_KW), F32)
    wr_hi = w_router[l].astype(BF16)
    wr_lo = (w_router[l] - wr_hi.astype(F32)).astype(BF16)
    wr_pad = jnp.zeros((D_MODEL, LANES - 2 * N_EXPERTS), BF16)
    zr = jnp.zeros((GLA_RANK, GLA_KW), F32)
    wts = {
        "g1": g_norm1[l].reshape(1, D_MODEL),
        "w_in_r": w_in_r,
        "wdf": jnp.concatenate([w_dec_f[l], zr, zpad], axis=0),
        "wdb": jnp.concatenate([zr, w_dec_b[l], zpad], axis=0),
        "bdf": b_dec_f[l].reshape(1, GLA_KW),
        "bdb": b_dec_b[l].reshape(1, GLA_KW),
        "g_gla": g_gla[l].reshape(1, GLA_VW),
        "sink": attn_sink[l],
        "wa": w_branch_a[l].astype(BF16),
        "wb": w_branch_b[l].astype(BF16),
        "wo": w_out[l].astype(BF16),
        "g2": g_norm2[l].reshape(1, D_MODEL),
        "wr2": jnp.concatenate([wr_hi, wr_lo, wr_pad], axis=1),
        "wg": w_exp_gate[l],
        "wu": w_exp_up[l],
        "wd": w_exp_down[l],
        "g_final": g_final.reshape(1, D_MODEL),
    }
    cond8 = jnp.concatenate([c_ctx[None, :], c, jnp.zeros((8 - 1 - bl, D_MODEL), F32)], axis=0)
    mod = _modulation(cond8, w_mod[l], b_mod[l]).reshape(8, N_MOD, 1, D_MODEL)
    mods_ctx = tuple(mod[0:1, j] for j in range(N_MOD))
    mods_lat = tuple(mod[1:1 + bl, j] for j in range(N_MOD))

    zero_state = jnp.zeros((1, GLA_HEADS, GLA_DK, GLA_DV), F32)
    y_prompt, tail_ctx, st_f, st_b = _layer(x_prompt, mods_ctx, wts, False, zero_state, zero_state, None, None)
    ck = cache_k[:, l].reshape(bl, -1, SWA_KVW)
    cv = cache_v[:, l].reshape(bl, -1, SWA_KVW)
    y_sample, _, _, _ = _layer(x_sample, mods_lat, wts, True, state_gla_fwd[:, l], state_gla_bwd[:, l], ck, cv)

    ksl = slice(COL_SK - TAIL_COL0, COL_SK - TAIL_COL0 + SWA_KVW)
    vsl = slice(COL_SV - TAIL_COL0, COL_SV - TAIL_COL0 + SWA_KVW)
    new_k = tail_ctx[:, ksl].reshape(bp, 1, sp, SWA_KV_HEADS, SWA_HEAD_DIM)
    new_v = tail_ctx[:, vsl].reshape(bp, 1, sp, SWA_KV_HEADS, SWA_HEAD_DIM)
    return (y_prompt, y_sample, st_f[:, None], st_b[:, None], new_k, new_v)
```

```python
import functools

import jax
import jax.numpy as jnp
from jax import lax
from jax.experimental import pallas as pl
from jax.experimental.pallas import tpu as pltpu

F32 = jnp.float32
BF16 = jnp.bfloat16
HIGHEST = lax.Precision.HIGHEST

D_MODEL = 2048
N_MOD = 6
EPS = 1e-6

GLA_HEADS = 4
GLA_DK = 128
GLA_DV = 256
GLA_KW = GLA_HEADS * GLA_DK
GLA_VW = GLA_HEADS * GLA_DV
GLA_RANK = 16
GLA_TAU = 16.0
GLA_CHUNK = 64

SWA_HEADS = 16
SWA_KV_HEADS = 4
SWA_GROUP = 4
SWA_HEAD_DIM = 64
SWA_QW = SWA_HEADS * SWA_HEAD_DIM
SWA_KVW = SWA_KV_HEADS * SWA_HEAD_DIM
SWA_BLOCK = 128
GRID_W = 64
ROPE_BASE = 10000.0

N_EXPERTS = 16
EXPERT_FF = D_MODEL // 2
CAPACITY_FACTOR = 2

LANES = 128
ROW_CHUNKS = D_MODEL // LANES
VMEM_LIMIT = 56 * 1024 * 1024

COL_GA = 0
COL_GB = 2048
COL_GQ = 4096
COL_GK = 4608
COL_GV = 5120
COL_GR = 6144
COL_SQ = 7168
COL_SK = 8192
COL_SV = 8448
COL_DEC = 8704
IN_PAD = 8960
IN_TN = 1280
TAIL_COL0 = COL_SK
TAIL_W = COL_DEC + LANES - COL_SK


def _cparams(sem, **kw):
    return pltpu.CompilerParams(dimension_semantics=sem, vmem_limit_bytes=VMEM_LIMIT, **kw)


def _silu(x):
    return x * (1.0 / (1.0 + jnp.exp(-x)))


def _sigmoid(x):
    return 1.0 / (1.0 + jnp.exp(-x))


SLAB = 128


def _store_row_major(ref, x):
    for tb in range(x.shape[0] // SLAB):
        rows = slice(tb * SLAB, (tb + 1) * SLAB)
        parts = jnp.stack([x[rows, s * LANES:(s + 1) * LANES] for s in range(ROW_CHUNKS)], axis=0)
        ref[rows] = pltpu.einshape("stl->tsl", parts)


def _load_row_major(ref, rows):
    xt = pltpu.einshape("tsl->stl", ref[rows])
    return [xt[s] for s in range(ROW_CHUNKS)]


def _tiles_per_mod_row(mod, n, seq, tile):
    return seq // tile if mod.shape[0] > 1 else n // tile


def _mod_kernel(c_ref, w_ref, b_ref, o_ref):
    a = _silu(c_ref[...]).astype(BF16)
    o_ref[...] = jnp.dot(a, w_ref[...].astype(BF16), preferred_element_type=F32) + b_ref[...]


def _modulation(cond8, w_mod, b_mod):
    n_out = w_mod.shape[1]
    tn = 1536
    return pl.pallas_call(
        _mod_kernel,
        grid=(n_out // tn,),
        in_specs=[pl.BlockSpec((8, D_MODEL), lambda j: (0, 0)),
                  pl.BlockSpec((D_MODEL, tn), lambda j: (0, j)),
                  pl.BlockSpec((1, tn), lambda j: (0, j))],
        out_specs=pl.BlockSpec((8, tn), lambda j: (0, j)),
        out_shape=jax.ShapeDtypeStruct((8, n_out), F32),
        compiler_params=_cparams(("arbitrary",)),
        name="modulation",
    )(cond8, w_mod, b_mod.reshape(1, n_out))


NORM_ROWS = 64


def _in_proj_kernel(x_ref, sc_ref, sh_ref, g_ref, w_ref, o_ref, tail_ref, h_scr):
    j = pl.program_id(1)

    @pl.when(j == 0)
    def _():
        def slab(r, _):
            rows = pl.ds(pl.multiple_of(r * NORM_ROWS, NORM_ROWS), NORM_ROWS)
            x = x_ref[rows, :]
            y = x * lax.rsqrt(jnp.mean(x * x, axis=-1, keepdims=True) + EPS) * g_ref[...]
            h_scr[rows, :] = (y * (1.0 + sc_ref[0]) + sh_ref[0]).astype(BF16)
            return 0
        lax.fori_loop(0, x_ref.shape[0] // NORM_ROWS, slab, 0)

    acc = jnp.dot(h_scr[...], w_ref[...], preferred_element_type=F32)
    o_ref[...] = acc.astype(BF16)

    @pl.when(j == IN_PAD // IN_TN - 1)
    def _():
        first = TAIL_COL0 - (IN_PAD - IN_TN)
        tail_ref[...] = acc[:, first:first + TAIL_W]


def _in_proj(x2d, seq, sc, sh, g, w_r):
    n = x2d.shape[0]
    tm = min(1024, seq if sc.shape[0] > 1 else n)
    per_b = _tiles_per_mod_row(sc, n, seq, tm)
    return pl.pallas_call(
        _in_proj_kernel,
        grid=(n // tm, IN_PAD // IN_TN),
        in_specs=[pl.BlockSpec((tm, D_MODEL), lambda i, j: (i, 0)),
                  pl.BlockSpec((1, 1, D_MODEL), lambda i, j: (i // per_b, 0, 0)),
                  pl.BlockSpec((1, 1, D_MODEL), lambda i, j: (i // per_b, 0, 0)),
                  pl.BlockSpec((1, D_MODEL), lambda i, j: (0, 0)),
                  pl.BlockSpec((D_MODEL, IN_TN), lambda i, j: (0, j))],
        out_specs=[pl.BlockSpec((tm, IN_TN), lambda i, j: (i, j)),
                   pl.BlockSpec((tm, TAIL_W), lambda i, j: (i, 0))],
        out_shape=[jax.ShapeDtypeStruct((n, IN_PAD), BF16),
                   jax.ShapeDtypeStruct((n, TAIL_W), F32)],
        scratch_shapes=[pltpu.VMEM((tm, D_MODEL), BF16)],
        compiler_params=_cparams(("parallel", "arbitrary")),
        name="in_proj",
    )(x2d, sc, sh, g, w_r)


GLA_BLK = 256
GLA_HPS = 4


def _tn_dot(a, b, precision=None):
    return lax.dot_general(a, b, (((0,), (0,)), ((), ())), precision=precision,
                           preferred_element_type=F32)


def _nt_dot(a, b, precision=None):
    return lax.dot_general(a, b, (((1,), (1,)), ((), ())), precision=precision,
                           preferred_element_type=F32)


def _split_bf16(x, parts):
    out = []
    for _ in range(parts):
        piece = x.astype(BF16)
        out.append(piece)
        x = x - piece.astype(F32)
    return out


def _dot3(x, w):
    xh, xl = _split_bf16(x, 2)
    wh, wl = _split_bf16(w, 2)
    d = functools.partial(jnp.dot, preferred_element_type=F32)
    return d(xh, wh) + d(xl, wh) + d(xh, wl)


def _gla_kernel(q_ref, k_ref, v_ref, r_ref, dec_ref, wdf_ref, bdf_ref, wdb_ref, bdb_ref, gg_ref,
                s0f_ref, s0b_ref, o_ref, stf_ref, stb_ref, state_t, of_scr, *, nblk):
    i = pl.program_id(2)
    c = GLA_CHUNK
    nch = GLA_BLK // c
    ri = lax.broadcasted_iota(jnp.int32, (GLA_BLK, GLA_BLK), 0)
    ci = lax.broadcasted_iota(jnp.int32, (GLA_BLK, GLA_BLK), 1)
    same_chunk = (ri // c) == (ci // c)

    @pl.when(i == 0)
    def _():
        for h in range(GLA_HPS):
            state_t[h] = s0f_ref[0, h].T

    @pl.when(i == nblk)
    def _():
        for h in range(GLA_HPS):
            state_t[h] = s0b_ref[0, h].T

    def run(fwd):
        blk = i if fwd else 2 * nblk - 1 - i
        keep = same_chunk & ((ci <= ri) if fwd else (ci >= ri))
        w_ref, b_ref = (wdf_ref, bdf_ref) if fwd else (wdb_ref, bdb_ref)
        logit = _dot3(dec_ref[...], w_ref[...]) + b_ref[...]
        g = (jnp.minimum(logit, 0.0) - jnp.log(1.0 + jnp.exp(-jnp.abs(logit)))) / GLA_TAU
        tri = keep.astype(BF16)
        b = sum(jnp.dot(tri, piece, preferred_element_type=F32) for piece in _split_bf16(g, 3))
        edge = c - 1 if fwd else 0
        tots = [b[ch * c + edge:ch * c + edge + 1, :] for ch in range(nch)]
        totb = jnp.concatenate([jnp.broadcast_to(t, (c, t.shape[1])) for t in tots], axis=0)
        q_in = (q_ref[...].astype(F32) * (GLA_DK ** -0.5) * jnp.exp(b)).astype(BF16)
        k = k_ref[...].astype(F32)
        k_in = (k * jnp.exp(-b)).astype(BF16)
        k_st = (k * jnp.exp(totb - b)).astype(BF16)
        srow = pl.multiple_of(blk * GLA_BLK, GLA_BLK)
        for h in range(GLA_HPS):
            ks = slice(h * GLA_DK, (h + 1) * GLA_DK)
            vs = slice(h * GLA_DV, (h + 1) * GLA_DV)
            v = v_ref[:, vs].astype(BF16)
            a = jnp.where(keep, _nt_dot(q_in[:, ks], k_in[:, ks]), 0.0).astype(BF16)
            o_intra = jnp.dot(a, v, preferred_element_type=F32)
            st = state_t[h]
            o_inter = [None] * nch
            for cc in range(nch):
                ch = cc if fwd else nch - 1 - cc
                rows = slice(ch * c, (ch + 1) * c)
                o_inter[ch] = _nt_dot(q_in[rows, ks], st.astype(BF16))
                st = st * jnp.exp(tots[ch][:, ks]) + _tn_dot(v[rows, :], k_st[rows, ks])
            state_t[h] = st
            o = o_intra + jnp.concatenate(o_inter, axis=0)
            if fwd:
                of_scr[pl.ds(srow, GLA_BLK), vs] = o
            else:
                o = o + of_scr[pl.ds(srow, GLA_BLK), vs]
                o = o * lax.rsqrt(jnp.mean(o * o, axis=-1, keepdims=True) + EPS) * gg_ref[:, vs]
                o_ref[:, vs] = (o * _silu(r_ref[:, vs].astype(F32))).astype(BF16)

    @pl.when(i < nblk)
    def _():
        run(True)

    @pl.when(i >= nblk)
    def _():
        run(False)

    @pl.when(i == nblk - 1)
    def _():
        for h in range(GLA_HPS):
            stf_ref[0, h] = state_t[h].T

    @pl.when(i == 2 * nblk - 1)
    def _():
        for h in range(GLA_HPS):
            stb_ref[0, h] = state_t[h].T


def _gla(p, dec, bsz, seq, wdf, bdf, wdb, bdb, g_gla, s0f, s0b):
    n = p.shape[0]
    nblk = seq // GLA_BLK
    kw = GLA_HPS * GLA_DK
    vw = GLA_HPS * GLA_DV

    def rb(b, i):
        return b * nblk + jnp.where(i < nblk, i, 2 * nblk - 1 - i)

    def orb(b, i):
        return b * nblk + jnp.where(i < nblk, nblk - 1, 2 * nblk - 1 - i)

    st_spec = pl.BlockSpec((1, GLA_HPS, GLA_DK, GLA_DV), lambda b, h, i: (b, h, 0, 0))
    s0_spec = pl.BlockSpec((1, GLA_HPS, GLA_DK, GLA_DV),
                           lambda b, h, i: (b if s0f.shape[0] > 1 else 0, h, 0, 0))
    st_shape = jax.ShapeDtypeStruct((bsz, GLA_HEADS, GLA_DK, GLA_DV), F32)
    return pl.pallas_call(
        functools.partial(_gla_kernel, nblk=nblk),
        grid=(bsz, GLA_HEADS // GLA_HPS, 2 * nblk),
        in_specs=[pl.BlockSpec((GLA_BLK, kw), lambda b, h, i: (rb(b, i), COL_GQ // kw + h)),
                  pl.BlockSpec((GLA_BLK, kw), lambda b, h, i: (rb(b, i), COL_GK // kw + h)),
                  pl.BlockSpec((GLA_BLK, vw), lambda b, h, i: (rb(b, i), COL_GV // vw + h)),
                  pl.BlockSpec((GLA_BLK, vw), lambda b, h, i: (rb(b, i), COL_GR // vw + h)),
                  pl.BlockSpec((GLA_BLK, LANES), lambda b, h, i: (rb(b, i), (COL_DEC - TAIL_COL0) // LANES)),
                  pl.BlockSpec((LANES, kw), lambda b, h, i: (0, h)),
                  pl.BlockSpec((1, kw), lambda b, h, i: (0, h)),
                  pl.BlockSpec((LANES, kw), lambda b, h, i: (0, h)),
                  pl.BlockSpec((1, kw), lambda b, h, i: (0, h)),
                  pl.BlockSpec((1, vw), lambda b, h, i: (0, h)),
                  s0_spec, s0_spec],
        out_specs=[pl.BlockSpec((GLA_BLK, vw), lambda b, h, i: (orb(b, i), h)), st_spec, st_spec],
        out_shape=[jax.ShapeDtypeStruct((n, GLA_VW), BF16), st_shape, st_shape],
        scratch_shapes=[pltpu.VMEM((GLA_HPS, GLA_DV, GLA_DK), F32), pltpu.VMEM((seq, vw), F32)],
        compiler_params=_cparams(("parallel", "parallel", "arbitrary")),
        name="gla",
    )(p, p, p, p, dec, wdf, bdf, wdb, bdb, g_gla, s0f, s0b)


def _sink_column(sink_ref, kvh, rows_per_head):
    r = lax.broadcasted_iota(jnp.int32, (SWA_GROUP * rows_per_head, 1), 0)
    col = jnp.zeros((SWA_GROUP * rows_per_head, 1), F32)
    for g in range(SWA_GROUP):
        col = jnp.where(r // rows_per_head == g, sink_ref[kvh * SWA_GROUP + g], col)
    return col


def _rope(x, cos, sin):
    outs = []
    for cb in range(x.shape[1] // LANES):
        sl = slice(cb * LANES, (cb + 1) * LANES)
        xc = x[:, sl]
        lane = lax.broadcasted_iota(jnp.int32, xc.shape, 1)
        sw = jnp.where(lane % 32 < 16, pltpu.roll(xc, LANES - 16, 1), pltpu.roll(xc, 16, 1))
        outs.append(xc * cos[:, sl] + sw * sin[:, sl])
    return jnp.concatenate(outs, axis=1)


def _swa_lat_kernel(sink_ref, q_ref, kp_ref, kc_ref, kn_ref, vp_ref, vc_ref, vn_ref, ck_ref, cv_ref,
                    cp_ref, cc_ref, cn_ref, sp_ref, sc_ref, sn_ref, o_ref, *, nb):
    n = pl.program_id(1)
    blk = SWA_BLOCK
    hd = SWA_HEAD_DIM
    cos_c, sin_c = cc_ref[...], sc_ref[...]
    kband = jnp.concatenate([_rope(kp_ref[...].astype(F32), cp_ref[...], sp_ref[...]),
                             _rope(kc_ref[...].astype(F32), cos_c, sin_c),
                             _rope(kn_ref[...].astype(F32), cn_ref[...], sn_ref[...])], axis=0).astype(BF16)
    vband = jnp.concatenate([vp_ref[...], vc_ref[...], vn_ref[...]], axis=0).astype(BF16)
    ck = ck_ref[0].astype(BF16)
    cv = cv_ref[0].astype(BF16)
    qi = lax.broadcasted_iota(jnp.int32, (blk, 3 * blk), 0)
    kj = lax.broadcasted_iota(jnp.int32, (blk, 3 * blk), 1)
    k_abs = kj + (n - 1) * blk
    mask = (jnp.abs(kj - blk - qi) <= SWA_BLOCK) & (k_abs >= 0) & (k_abs < nb * blk)
    bias = jnp.concatenate([jnp.where(mask, 0.0, -1e30)] * SWA_GROUP, axis=0)
    for kvh in range(SWA_KV_HEADS):
        ks = slice(kvh * hd, (kvh + 1) * hd)
        qr = _rope(q_ref[:, kvh * SWA_KVW:(kvh + 1) * SWA_KVW].astype(F32), cos_c, sin_c) * (hd ** -0.5)
        qg = jnp.concatenate([qr[:, g * hd:(g + 1) * hd] for g in range(SWA_GROUP)],
                             axis=0).astype(BF16)
        s_b = _nt_dot(qg, kband[:, ks]) + bias
        s_c = _nt_dot(qg, ck[:, ks])
        sink = _sink_column(sink_ref, kvh, blk)
        m = jnp.maximum(jnp.maximum(jnp.max(s_b, axis=-1, keepdims=True),
                                    jnp.max(s_c, axis=-1, keepdims=True)), sink)
        p_b = jnp.exp(s_b - m)
        p_c = jnp.exp(s_c - m)
        den = jnp.exp(sink - m) + jnp.sum(p_b, axis=-1, keepdims=True) + jnp.sum(p_c, axis=-1, keepdims=True)
        o = (jnp.dot(p_b.astype(BF16), vband[:, ks], preferred_element_type=F32)
             + jnp.dot(p_c.astype(BF16), cv[:, ks], preferred_element_type=F32)) / den
        for g in range(SWA_GROUP):
            cb = (kvh * SWA_GROUP + g) * hd
            o_ref[:, cb:cb + hd] = o[g * blk:(g + 1) * blk, :].astype(BF16)


def _swa_latent(p, bsz, seq, cache_k, cache_v, sink, cos_t, sin_t):
    n = p.shape[0]
    nb = seq // SWA_BLOCK
    kcol = COL_SK // SWA_KVW
    vcol = COL_SV // SWA_KVW
    prev = lambda i: jnp.maximum(i - 1, 0)
    nxt = lambda i: jnp.minimum(i + 1, nb - 1)

    def pspec(col, f):
        return pl.BlockSpec((SWA_BLOCK, SWA_KVW), lambda b, i, s: (b * nb + f(i), col))

    def tspec(f):
        return pl.BlockSpec((SWA_BLOCK, SWA_KVW), lambda b, i, s: (f(i), 0))

    same = lambda i: i
    cspec = pl.BlockSpec((1, cache_k.shape[1], SWA_KVW), lambda b, i, s: (b, 0, 0))
    grid_spec = pltpu.PrefetchScalarGridSpec(
        num_scalar_prefetch=1,
        grid=(bsz, nb),
        in_specs=[pl.BlockSpec((SWA_BLOCK, SWA_QW), lambda b, i, s: (b * nb + i, COL_SQ // SWA_QW)),
                  pspec(kcol, prev), pspec(kcol, same), pspec(kcol, nxt),
                  pspec(vcol, prev), pspec(vcol, same), pspec(vcol, nxt),
                  cspec, cspec,
                  tspec(prev), tspec(same), tspec(nxt), tspec(prev), tspec(same), tspec(nxt)],
        out_specs=pl.BlockSpec((SWA_BLOCK, SWA_QW), lambda b, i, s: (b * nb + i, 0)),
    )
    return pl.pallas_call(
        functools.partial(_swa_lat_kernel, nb=nb),
        grid_spec=grid_spec,
        out_shape=jax.ShapeDtypeStruct((n, SWA_QW), BF16),
        compiler_params=_cparams(("parallel", "arbitrary")),
        name="swa_latent",
    )(sink, p, p, p, p, p, p, p, cache_k, cache_v, cos_t, cos_t, cos_t, sin_t, sin_t, sin_t)


def _swa_ctx_kernel(sink_ref, q_ref, k_ref, v_ref, o_ref):
    s = q_ref.shape[0]
    hd = SWA_HEAD_DIM
    kb = k_ref[...].astype(BF16)
    vb = v_ref[...].astype(BF16)
    for kvh in range(SWA_KV_HEADS):
        ks = slice(kvh * hd, (kvh + 1) * hd)
        qg = jnp.concatenate([q_ref[:, (kvh * SWA_GROUP + g) * hd:(kvh * SWA_GROUP + g + 1) * hd]
                              for g in range(SWA_GROUP)], axis=0)
        qg = (qg.astype(F32) * (hd ** -0.5)).astype(BF16)
        sc = _nt_dot(qg, kb[:, ks])
        sink = _sink_column(sink_ref, kvh, s)
        m = jnp.maximum(jnp.max(sc, axis=-1, keepdims=True), sink)
        pr = jnp.exp(sc - m)
        den = jnp.exp(sink - m) + jnp.sum(pr, axis=-1, keepdims=True)
        o = jnp.dot(pr.astype(BF16), vb[:, ks], preferred_element_type=F32) / den
        for g in range(SWA_GROUP):
            cb = (kvh * SWA_GROUP + g) * hd
            o_ref[:, cb:cb + hd] = o[g * s:(g + 1) * s, :].astype(BF16)


def _swa_context(p, bsz, seq, sink):
    n = p.shape[0]
    grid_spec = pltpu.PrefetchScalarGridSpec(
        num_scalar_prefetch=1,
        grid=(bsz,),
        in_specs=[pl.BlockSpec((seq, SWA_QW), lambda b, s: (b, COL_SQ // SWA_QW)),
                  pl.BlockSpec((seq, SWA_KVW), lambda b, s: (b, COL_SK // SWA_KVW)),
                  pl.BlockSpec((seq, SWA_KVW), lambda b, s: (b, COL_SV // SWA_KVW))],
        out_specs=pl.BlockSpec((seq, SWA_QW), lambda b, s: (b, 0)),
    )
    return pl.pallas_call(
        _swa_ctx_kernel,
        grid_spec=grid_spec,
        out_shape=jax.ShapeDtypeStruct((n, SWA_QW), BF16),
        compiler_params=_cparams(("parallel",)),
        name="swa_context",
    )(sink, p, p, p)


MERGE_TM = 512


BRANCH_TM = 1024
BRANCH_COLS = 512


def _branch_kernel(og_ref, os_ref, ga_ref, gb_ref, wa_ref, wb_ref, m_ref):
    for c in range(D_MODEL // BRANCH_COLS):
        cols = slice(c * BRANCH_COLS, (c + 1) * BRANCH_COLS)
        a = jnp.dot(og_ref[...], wa_ref[:, cols], preferred_element_type=F32)
        b = jnp.dot(os_ref[...], wb_ref[:, cols], preferred_element_type=F32)
        m_ref[:, cols] = (_sigmoid(ga_ref[:, cols].astype(F32)) * a
                          + _sigmoid(gb_ref[:, cols].astype(F32)) * b).astype(BF16)


def _branch_merge(o_gla, o_swa, p, wa, wb):
    n = o_gla.shape[0]
    tm = BRANCH_TM
    once = pl.Buffered(1)
    return pl.pallas_call(
        _branch_kernel,
        grid=(n // tm,),
        in_specs=[pl.BlockSpec((tm, GLA_VW), lambda i: (i, 0)),
                  pl.BlockSpec((tm, SWA_QW), lambda i: (i, 0)),
                  pl.BlockSpec((tm, D_MODEL), lambda i: (i, COL_GA // D_MODEL)),
                  pl.BlockSpec((tm, D_MODEL), lambda i: (i, COL_GB // D_MODEL)),
                  pl.BlockSpec((GLA_VW, D_MODEL), lambda i: (0, 0), pipeline_mode=once),
                  pl.BlockSpec((SWA_QW, D_MODEL), lambda i: (0, 0), pipeline_mode=once)],
        out_specs=pl.BlockSpec((tm, D_MODEL), lambda i: (i, 0)),
        out_shape=jax.ShapeDtypeStruct((n, D_MODEL), BF16),
        compiler_params=_cparams(("parallel",)),
        name="branch_merge",
    )(o_gla, o_swa, p, p, wa, wb)


def _out_kernel(m_ref, x_ref, gt1_ref, sc2_ref, sh2_ref, g2_ref, wo_ref, wr2_ref,
                x1_ref, h2_ref, lg_ref):
    x1 = x_ref[...] + gt1_ref[0] * jnp.dot(m_ref[...], wo_ref[...], preferred_element_type=F32)
    x1_ref[...] = x1
    y = x1 * lax.rsqrt(jnp.mean(x1 * x1, axis=-1, keepdims=True) + EPS) * g2_ref[...]
    h2 = y * (1.0 + sc2_ref[0]) + sh2_ref[0]
    _store_row_major(h2_ref, h2)
    l2 = jnp.dot(h2.astype(BF16), wr2_ref[...], preferred_element_type=F32)
    lt = l2.T
    lt = lt[0:N_EXPERTS, :] + lt[N_EXPERTS:2 * N_EXPERTS, :]
    for cb in range(lt.shape[1] // LANES):
        lg_ref[cb] = lt[:, cb * LANES:(cb + 1) * LANES]


def _out_proj(merged, x2d, seq, gt1, sc2, sh2, g2, wo, wr2):
    n = x2d.shape[0]
    tm = MERGE_TM
    per_b = _tiles_per_mod_row(gt1, n, seq, tm)
    mod_spec = pl.BlockSpec((1, 1, D_MODEL), lambda i: (i // per_b, 0, 0))
    once = pl.Buffered(1)
    return pl.pallas_call(
        _out_kernel,
        grid=(n // tm,),
        in_specs=[pl.BlockSpec((tm, D_MODEL), lambda i: (i, 0)),
                  pl.BlockSpec((tm, D_MODEL), lambda i: (i, 0)),
                  mod_spec, mod_spec, mod_spec,
                  pl.BlockSpec((1, D_MODEL), lambda i: (0, 0)),
                  pl.BlockSpec((D_MODEL, D_MODEL), lambda i: (0, 0), pipeline_mode=once),
                  pl.BlockSpec((D_MODEL, LANES), lambda i: (0, 0))],
        out_specs=[pl.BlockSpec((tm, D_MODEL), lambda i: (i, 0)),
                   pl.BlockSpec((tm, ROW_CHUNKS, LANES), lambda i: (i, 0, 0)),
                   pl.BlockSpec((tm // LANES, N_EXPERTS, LANES), lambda i: (i, 0, 0))],
        out_shape=[jax.ShapeDtypeStruct((n, D_MODEL), F32),
                   jax.ShapeDtypeStruct((n, ROW_CHUNKS, LANES), F32),
                   jax.ShapeDtypeStruct((n // LANES, N_EXPERTS, LANES), F32)],
        compiler_params=_cparams(("parallel",)),
        name="out_proj",
    )(merged, x2d, gt1, sc2, sh2, g2, wo, wr2)


def _select_kernel(lg_ref, aff_ref, sel_ref, pos_ref, *, cap):
    nblk = lg_ref.shape[0]
    lg = lg_ref[...]
    ex = jnp.exp(lg - jnp.max(lg, axis=1, keepdims=True))
    aff = ex / jnp.sum(ex, axis=1, keepdims=True)
    aff_ref[...] = aff
    bits = lax.bitcast_convert_type(aff, jnp.int32)

    def count(pred):
        c = jnp.sum(jnp.where(pred, 1.0, 0.0), axis=0, keepdims=True)
        return jnp.sum(c, axis=2, keepdims=True)

    def bit_step(t, cur):
        cand = cur | jnp.left_shift(jnp.int32(1), 30 - t)
        return jnp.where(count(bits >= cand) >= cap, cand, cur)

    thr = lax.fori_loop(0, 31, bit_step, jnp.zeros((1, N_EXPERTS, 1), jnp.int32))
    need = (cap - count(bits > thr))[0]
    thr2 = thr[0]
    ri = lax.broadcasted_iota(jnp.int32, (LANES, LANES), 0)
    ci = lax.broadcasted_iota(jnp.int32, (LANES, LANES), 1)
    upper = (ri <= ci).astype(BF16)

    def blk_step(b, carry):
        run_eq, run_sel = carry
        bb = lax.bitcast_convert_type(aff_ref[b], jnp.int32)
        eq = (bb == thr2).astype(F32)
        eq_excl = jnp.dot(eq.astype(BF16), upper, preferred_element_type=F32) - eq + run_eq
        sel = jnp.where((bb > thr2) | ((eq > 0) & (eq_excl < need)), 1.0, 0.0)
        sel_ref[b] = sel
        pos_ref[b] = jnp.dot(sel.astype(BF16), upper, preferred_element_type=F32) - sel + run_sel
        return (run_eq + jnp.sum(eq, axis=1, keepdims=True),
                run_sel + jnp.sum(sel, axis=1, keepdims=True))

    zero = jnp.zeros((N_EXPERTS, 1), F32)
    lax.fori_loop(0, nblk, blk_step, (zero, zero))


def _select(logits3, cap):
    shp = jax.ShapeDtypeStruct(logits3.shape, F32)
    return pl.pallas_call(
        functools.partial(_select_kernel, cap=float(cap)),
        out_shape=[shp, shp, shp],
        compiler_params=pltpu.CompilerParams(vmem_limit_bytes=VMEM_LIMIT),
        name="select",
    )(logits3)


COMPACT_WIN = 136


def _compact_kernel(base_ref, aff_ref, sel_ref, pos_ref, list_ref):
    nblk = aff_ref.shape[0]
    list_ref[...] = jnp.zeros_like(list_ref)
    slot = lax.broadcasted_iota(jnp.int32, (COMPACT_WIN, LANES), 0).astype(F32)
    lane = lax.broadcasted_iota(jnp.int32, (COMPACT_WIN, LANES), 1)

    def blk_step(b, _):
        aff = aff_ref[b]
        pos = jnp.where(sel_ref[b] > 0, pos_ref[b], -1.0)
        tok = (lane + b * LANES).astype(F32)
        for e in range(N_EXPERTS):
            base8 = pl.multiple_of((base_ref[b * N_EXPERTS + e] >> 3) << 3, 8)
            hit = pos[e:e + 1, :] == slot + base8.astype(F32)
            ic = jnp.sum(jnp.where(hit, tok, 0.0), axis=1, keepdims=True)
            gc = jnp.sum(jnp.where(hit, aff[e:e + 1, :], 0.0), axis=1, keepdims=True)
            rows = pl.ds(base8, COMPACT_WIN)
            list_ref[rows, :] += jnp.where(lane == e, ic, jnp.where(lane == N_EXPERTS + e, gc, 0.0))
        return 0

    lax.fori_loop(0, nblk, blk_step, 0)


def _compact(base, aff3, sel3, pos3, cap):
    cp = cap + 2 * LANES
    full = pl.BlockSpec(aff3.shape, lambda i, s: (0, 0, 0))
    ospec = pl.BlockSpec((cp, LANES), lambda i, s: (0, 0))
    oshape = jax.ShapeDtypeStruct((cp, LANES), F32)
    grid_spec = pltpu.PrefetchScalarGridSpec(
        num_scalar_prefetch=1, grid=(1,), in_specs=[full, full, full], out_specs=ospec)
    return pl.pallas_call(
        _compact_kernel,
        grid_spec=grid_spec,
        out_shape=oshape,
        compiler_params=_cparams(("arbitrary",)),
        name="compact",
    )(base, aff3, sel3, pos3)


FFN_COLS = 256


def _up_kernel(idx_ref, h2_hbm, wg_ref, wu_ref, hid_ref, gbuf, xs, sem, *, tm, total):
    step = pl.program_id(0) * pl.num_programs(1) + pl.program_id(1)

    def row_copy(step_, i):
        return pltpu.make_async_copy(h2_hbm.at[idx_ref[step_ * tm + i]], gbuf.at[i], sem)

    def wait_rows():
        pltpu.make_async_copy(h2_hbm.at[pl.ds(0, tm)], gbuf, sem).wait()

    @pl.when(step == 0)
    def _():
        def body(i, _):
            row_copy(step, i).start()
            return 0
        lax.fori_loop(0, tm, body, 0)

    wait_rows()
    slab = min(SLAB, tm)
    for tb in range(tm // slab):
        rows = slice(tb * slab, (tb + 1) * slab)
        for s, chunk in enumerate(_load_row_major(gbuf, rows)):
            xs[rows, s * LANES:(s + 1) * LANES] = chunk.astype(BF16)

    nxt = jnp.minimum(step + 1, total - 1)
    for i in range(tm):
        row_copy(nxt, i).start(priority=i % 2)

    for f in range(EXPERT_FF // FFN_COLS):
        cols = slice(f * FFN_COLS, (f + 1) * FFN_COLS)
        hg = jnp.dot(xs[...], wg_ref[0, :, cols].astype(BF16), preferred_element_type=F32)
        hu = jnp.dot(xs[...], wu_ref[0, :, cols].astype(BF16), preferred_element_type=F32)
        hid_ref[:, cols] = (_silu(hg) * hu).astype(BF16)

    @pl.when(step == total - 1)
    def _():
        wait_rows()


def _expert_up(idx_flat, h2_rm, wg, wu, cap):
    tm = min(1024, cap)
    nr = cap // tm
    grid_spec = pltpu.PrefetchScalarGridSpec(
        num_scalar_prefetch=1,
        grid=(N_EXPERTS, nr),
        in_specs=[pl.BlockSpec(memory_space=pl.ANY),
                  pl.BlockSpec((1, D_MODEL, EXPERT_FF), lambda e, r, s: (e, 0, 0)),
                  pl.BlockSpec((1, D_MODEL, EXPERT_FF), lambda e, r, s: (e, 0, 0))],
        out_specs=pl.BlockSpec((tm, EXPERT_FF), lambda e, r, s: (e * nr + r, 0)),
        scratch_shapes=[pltpu.VMEM((tm, ROW_CHUNKS, LANES), F32),
                        pltpu.VMEM((tm, D_MODEL), BF16),
                        pltpu.SemaphoreType.DMA(())],
    )
    return pl.pallas_call(
        functools.partial(_up_kernel, tm=tm, total=N_EXPERTS * nr),
        grid_spec=grid_spec,
        out_shape=jax.ShapeDtypeStruct((N_EXPERTS * cap, EXPERT_FF), BF16),
        compiler_params=_cparams(("arbitrary", "arbitrary")),
        name="expert_up",
    )(idx_flat, h2_rm, wg, wu)


def _down_kernel(hid_ref, gate_ref, wd_ref, o_ref, ybuf):
    gate = gate_ref[...]
    for c in range(D_MODEL // FFN_COLS):
        cols = slice(c * FFN_COLS, (c + 1) * FFN_COLS)
        ybuf[:, cols] = gate * jnp.dot(hid_ref[...], wd_ref[0, :, cols].astype(BF16),
                                       preferred_element_type=F32)
    tm = ybuf.shape[0]
    slab = min(SLAB, tm)
    for tb in range(tm // slab):
        rows = slice(tb * slab, (tb + 1) * slab)
        parts = jnp.stack([ybuf[rows, s * LANES:(s + 1) * LANES] for s in range(ROW_CHUNKS)], axis=0)
        o_ref[rows] = pltpu.einshape("stl->tsl", parts)


def _expert_down(hid, gate_col, wd, cap):
    tm = min(1024, cap)
    nr = cap // tm
    return pl.pallas_call(
        _down_kernel,
        grid=(N_EXPERTS, nr),
        in_specs=[pl.BlockSpec((tm, EXPERT_FF), lambda e, r: (e * nr + r, 0)),
                  pl.BlockSpec((tm, 1), lambda e, r: (e * nr + r, 0)),
                  pl.BlockSpec((1, EXPERT_FF, D_MODEL), lambda e, r: (e, 0, 0))],
        out_specs=pl.BlockSpec((tm, ROW_CHUNKS, LANES), lambda e, r: (e * nr + r, 0, 0)),
        out_shape=jax.ShapeDtypeStruct((N_EXPERTS * cap, ROW_CHUNKS, LANES), F32),
        scratch_shapes=[pltpu.VMEM((tm, D_MODEL), F32)],
        compiler_params=_cparams(("parallel", "arbitrary")),
        name="expert_down",
    )(hid, gate_col, wd)


COMBINE_UNROLL = 4
COMBINE_CHUNK = 32
COMBINE_NCHUNK = 3


def _combine_kernel(idx_ref, lo_ref, ye_hbm, x1_ref, gt2_ref, gf_ref, o_ref, stage, acc, sem,
                    *, tt, ch, cap, ntiles):
    tile = pl.program_id(0)
    slot = tile % 2

    window = COMBINE_NCHUNK * ch

    def chunk_start(first_row):
        return jnp.minimum(first_row, cap - window)

    def chunk_copy(slot_, e, start, c):
        return pltpu.make_async_copy(ye_hbm.at[pl.ds(e * cap + start + c * ch, ch)],
                                     stage.at[slot_, e, pl.ds(c * ch, ch)],
                                     sem.at[slot_, e])

    def for_needed_chunks(tile_, e, fn):
        lo = lo_ref[tile_ * N_EXPERTS + e]
        hi = lo_ref[(tile_ + 1) * N_EXPERTS + e]
        start = chunk_start(lo)
        for c in range(COMBINE_NCHUNK):
            @pl.when((start + c * ch < hi) & (start + (c + 1) * ch > lo))
            def _():
                fn(start, c)

    def issue(tile_, slot_):
        for e in range(N_EXPERTS):
            for_needed_chunks(tile_, e, lambda start, c: chunk_copy(slot_, e, start, c).start())

    @pl.when(tile == 0)
    def _():
        issue(tile, slot)

    @pl.when(tile + 1 < ntiles)
    def _():
        issue(tile + 1, 1 - slot)

    acc[...] = jnp.zeros_like(acc)

    def add_rows(e, start, r0, r1):
        def group(r, width):
            dsts, vals = [], []
            rows = stage[slot, e, pl.ds(r - start, width)]
            for u in range(width):
                dst = idx_ref[e * cap + r + u] - tile * tt
                dsts.append(dst)
                vals.append(acc[dst] + rows[u])
            for dst, val in zip(dsts, vals):
                acc[dst] = val

        ngroups = (r1 - r0) // COMBINE_UNROLL

        def body_group(j, _):
            group(r0 + j * COMBINE_UNROLL, COMBINE_UNROLL)
            return 0
        lax.fori_loop(0, ngroups, body_group, 0)

        def body_one(r, _):
            group(r, 1)
            return 0
        lax.fori_loop(r0 + ngroups * COMBINE_UNROLL, r1, body_one, 0)

    for e in range(N_EXPERTS):
        lo = lo_ref[tile * N_EXPERTS + e]
        hi = lo_ref[(tile + 1) * N_EXPERTS + e]
        start = chunk_start(lo)
        for_needed_chunks(tile, e, lambda start_, c: chunk_copy(slot, e, start_, c).wait())
        first_end = jnp.minimum(hi, start + window)
        add_rows(e, start, lo, first_end)

        def more(r0):
            st = jnp.minimum(r0, cap - ch)
            cp = chunk_copy(slot, e, st, 0)
            cp.start()
            cp.wait()
            r1 = jnp.minimum(hi, st + ch)
            add_rows(e, st, r0, r1)
            return r1
        lax.while_loop(lambda r0: r0 < hi, more, first_end)

    for tb in range(tt // SLAB):
        rows = slice(tb * SLAB, (tb + 1) * SLAB)
        ssq = jnp.zeros((SLAB, 1), F32)
        for s, y in enumerate(_load_row_major(acc, rows)):
            cols = slice(s * LANES, (s + 1) * LANES)
            x2 = x1_ref[rows, cols] + gt2_ref[0][:, cols] * y
            o_ref[rows, cols] = x2
            ssq = ssq + jnp.sum(x2 * x2, axis=-1, keepdims=True)
        o_ref[rows, :] = o_ref[rows, :] * lax.rsqrt(ssq / D_MODEL + EPS) * gf_ref[...]


def _combine(idx_flat, lo_tab, ye_lin, x1, seq, gt2, g_final, cap, tt):
    n = x1.shape[0]
    ch = min(COMBINE_CHUNK, cap // COMBINE_NCHUNK)
    ntiles = n // tt
    per_b = _tiles_per_mod_row(gt2, n, seq, tt)
    grid_spec = pltpu.PrefetchScalarGridSpec(
        num_scalar_prefetch=2,
        grid=(ntiles,),
        in_specs=[pl.BlockSpec(memory_space=pl.ANY),
                  pl.BlockSpec((tt, D_MODEL), lambda i, a, b: (i, 0)),
                  pl.BlockSpec((1, 1, D_MODEL), lambda i, a, b: (i // per_b, 0, 0)),
                  pl.BlockSpec((1, D_MODEL), lambda i, a, b: (0, 0))],
        out_specs=pl.BlockSpec((tt, D_MODEL), lambda i, a, b: (i, 0)),
        scratch_shapes=[pltpu.VMEM((2, N_EXPERTS, COMBINE_NCHUNK * ch, ROW_CHUNKS, LANES), F32),
                        pltpu.VMEM((tt, ROW_CHUNKS, LANES), F32),
                        pltpu.SemaphoreType.DMA((2, N_EXPERTS))],
    )
    return pl.pallas_call(
        functools.partial(_combine_kernel, tt=tt, ch=ch, cap=cap, ntiles=ntiles),
        grid_spec=grid_spec,
        out_shape=jax.ShapeDtypeStruct((n, D_MODEL), F32),
        compiler_params=_cparams(("arbitrary",)),
        name="combine",
    )(idx_flat, lo_tab, ye_lin, x1, gt2, g_final)


def _rope_tables(seq):
    pos = jnp.arange(seq)
    row = (pos // GRID_W).astype(F32)
    col = (pos % GRID_W).astype(F32)
    npair = SWA_HEAD_DIM // 4
    inv_freq = ROPE_BASE ** (-jnp.arange(npair, dtype=F32) / npair)
    ar = row[:, None] * inv_freq[None, :]
    ac = col[:, None] * inv_freq[None, :]
    cos = jnp.concatenate([jnp.cos(ar), jnp.cos(ar), jnp.cos(ac), jnp.cos(ac)], axis=1)
    sin = jnp.concatenate([-jnp.sin(ar), jnp.sin(ar), -jnp.sin(ac), jnp.sin(ac)], axis=1)
    reps = SWA_KVW // SWA_HEAD_DIM
    return jnp.tile(cos, (1, reps)), jnp.tile(sin, (1, reps))


def _layer(x, mods, wts, latent, s0f, s0b, cache_k, cache_v):
    bsz, seq, _ = x.shape
    n = bsz * seq
    x2d = x.reshape(n, D_MODEL)
    sh1, sc1, gt1, sh2, sc2, gt2 = mods
    p, tail = _in_proj(x2d, seq, sc1, sh1, wts["g1"], wts["w_in_r"])
    o_gla, st_f, st_b = _gla(p, tail, bsz, seq, wts["wdf"], wts["bdf"], wts["wdb"], wts["bdb"], wts["g_gla"],
                             s0f, s0b)
    if latent:
        cos_t, sin_t = _rope_tables(seq)
        o_swa = _swa_latent(p, bsz, seq, cache_k, cache_v, wts["sink"], cos_t, sin_t)
    else:
        o_swa = _swa_context(p, bsz, seq, wts["sink"])
    merged = _branch_merge(o_gla, o_swa, p, wts["wa"], wts["wb"])
    x1, h2, logits3 = _out_proj(merged, x2d, seq, gt1, sc2, sh2, wts["g2"], wts["wo"], wts["wr2"])
    cap = CAPACITY_FACTOR * n // N_EXPERTS
    aff3, sel3, pos3 = _select(logits3, cap)
    base = jnp.concatenate([pos3[:, :, 0].astype(jnp.int32), jnp.full((1, N_EXPERTS), cap, jnp.int32)], axis=0)
    lists = _compact(base.reshape(-1), aff3, sel3, pos3, cap)
    idx_flat = lists[:cap, :N_EXPERTS].T.astype(jnp.int32).reshape(-1)
    gate_col = lists[:cap, N_EXPERTS:2 * N_EXPERTS].T.reshape(-1, 1)
    hid = _expert_up(idx_flat, h2, wts["wg"], wts["wu"], cap)
    ye = _expert_down(hid, gate_col, wts["wd"], cap)
    tt = 256
    lo_tab = base[::tt // LANES]
    y = _combine(idx_flat, lo_tab.reshape(-1), ye, x1, seq, gt2, wts["g_final"], cap, tt)
    return y.reshape(bsz, seq, D_MODEL), tail, st_f, st_b


def kernel(x_prompt, x_sample, state_gla_fwd, state_gla_bwd, cache_k, cache_v, c, c_ctx, w_mod, b_mod,
           g_norm1, w_in, w_dec_f, b_dec_f, w_dec_b, b_dec_b, g_gla, attn_sink, w_branch_a, w_branch_b,
           w_out, g_norm2, w_router, w_exp_gate, w_exp_up, w_exp_down, g_final):
    bp, sp, _ = x_prompt.shape
    bl = x_sample.shape[0]
    l = 0
    gla_end = 2 * GLA_KW + 2 * GLA_VW
    dec_end = gla_end + 2 * GLA_RANK
    swa_end = dec_end + SWA_QW + 2 * SWA_KVW
    w = w_in[l]
    w_in_r = jnp.concatenate(
        [w[:, swa_end:].astype(BF16), w[:, :gla_end].astype(BF16), w[:, dec_end:swa_end].astype(BF16),
         w[:, gla_end:dec_end].astype(BF16), jnp.zeros((D_MODEL, IN_PAD - w.shape[1]), BF16)], axis=1)
    zpad = jnp.zeros((LANES - 2 * GLA_RANK, GLA_KW), F32)
    wr_hi = w_router[l].astype(BF16)
    wr_lo = (w_router[l] - wr_hi.astype(F32)).astype(BF16)
    wr_pad = jnp.zeros((D_MODEL, LANES - 2 * N_EXPERTS), BF16)
    zr = jnp.zeros((GLA_RANK, GLA_KW), F32)
    wts = {
        "g1": g_norm1[l].reshape(1, D_MODEL),
        "w_in_r": w_in_r,
        "wdf": jnp.concatenate([w_dec_f[l], zr, zpad], axis=0),
        "wdb": jnp.concatenate([zr, w_dec_b[l], zpad], axis=0),
        "bdf": b_dec_f[l].reshape(1, GLA_KW),
        "bdb": b_dec_b[l].reshape(1, GLA_KW),
        "g_gla": g_gla[l].reshape(1, GLA_VW),
        "sink": attn_sink[l],
        "wa": w_branch_a[l].astype(BF16),
        "wb": w_branch_b[l].astype(BF16),
        "wo": w_out[l].astype(BF16),
        "g2": g_norm2[l].reshape(1, D_MODEL),
        "wr2": jnp.concatenate([wr_hi, wr_lo, wr_pad], axis=1),
        "wg": w_exp_gate[l],
        "wu": w_exp_up[l],
        "wd": w_exp_down[l],
        "g_final": g_final.reshape(1, D_MODEL),
    }
    cond8 = jnp.concatenate([c_ctx[None, :], c, jnp.zeros((8 - 1 - bl, D_MODEL), F32)], axis=0)
    mod = _modulation(cond8, w_mod[l], b_mod[l]).reshape(8, N_MOD, 1, D_MODEL)
    mods_ctx = tuple(mod[0:1, j] for j in range(N_MOD))
    mods_lat = tuple(mod[1:1 + bl, j] for j in range(N_MOD))

    zero_state = jnp.zeros((1, GLA_HEADS, GLA_DK, GLA_DV), F32)
    y_prompt, tail_ctx, st_f, st_b = _layer(x_prompt, mods_ctx, wts, False, zero_state, zero_state, None, None)
    ck = cache_k[:, l].reshape(bl, -1, SWA_KVW)
    cv = cache_v[:, l].reshape(bl, -1, SWA_KVW)
    y_sample, _, _, _ = _layer(x_sample, mods_lat, wts, True, state_gla_fwd[:, l], state_gla_bwd[:, l], ck, cv)

    ksl = slice(COL_SK - TAIL_COL0, COL_SK - TAIL_COL0 + SWA_KVW)
    vsl = slice(COL_SV - TAIL_COL0, COL_SV - TAIL_COL0 + SWA_KVW)
    new_k = tail_ctx[:, ksl].reshape(bp, 1, sp, SWA_KV_HEADS, SWA_HEAD_DIM)
    new_v = tail_ctx[:, vsl].reshape(bp, 1, sp, SWA_KV_HEADS, SWA_HEAD_DIM)
    return (y_prompt, y_sample, st_f[:, None], st_b[:, None], new_k, new_v)
```

```python
import functools

import jax
import jax.numpy as jnp
import numpy as np
from jax import lax
from jax.experimental import pallas as pl
from jax.experimental.pallas import tpu as pltpu

F32 = jnp.float32
BF16 = jnp.bfloat16
HIGHEST = lax.Precision.HIGHEST

D_MODEL = 2048
N_MOD = 6
EPS = 1e-6

GLA_HEADS = 4
GLA_DK = 128
GLA_DV = 256
GLA_KW = GLA_HEADS * GLA_DK
GLA_VW = GLA_HEADS * GLA_DV
GLA_RANK = 16
GLA_TAU = 16.0
GLA_CHUNK = 64

SWA_HEADS = 16
SWA_KV_HEADS = 4
SWA_GROUP = 4
SWA_HEAD_DIM = 64
SWA_QW = SWA_HEADS * SWA_HEAD_DIM
SWA_KVW = SWA_KV_HEADS * SWA_HEAD_DIM
SWA_BLOCK = 128
GRID_W = 64
ROPE_BASE = 10000.0

N_EXPERTS = 16
EXPERT_FF = D_MODEL // 2
CAPACITY_FACTOR = 2

LANES = 128
ROW_CHUNKS = D_MODEL // LANES
VMEM_LIMIT = 56 * 1024 * 1024

COL_GA = 0
COL_GB = 2048
COL_GQ = 4096
COL_GK = 4608
COL_GV = 5120
COL_GR = 6144
COL_SQ = 7168
COL_SK = 8192
COL_SV = 8448
COL_DEC = 8704
IN_PAD = 8960
IN_TN = 1280
TAIL_COL0 = COL_SK
TAIL_W = COL_DEC + LANES - COL_SK


def _cparams(sem, **kw):
    return pltpu.CompilerParams(dimension_semantics=sem, vmem_limit_bytes=VMEM_LIMIT, **kw)


def _silu(x):
    return x * (1.0 / (1.0 + jnp.exp(-x)))


def _sigmoid(x):
    return 1.0 / (1.0 + jnp.exp(-x))


SLAB = 128


def _store_row_major(ref, x):
    for tb in range(x.shape[0] // SLAB):
        rows = slice(tb * SLAB, (tb + 1) * SLAB)
        parts = jnp.stack([x[rows, s * LANES:(s + 1) * LANES] for s in range(ROW_CHUNKS)], axis=0)
        ref[rows] = pltpu.einshape("stl->tsl", parts)


def _load_row_major(ref, rows):
    xt = pltpu.einshape("tsl->stl", ref[rows])
    return [xt[s] for s in range(ROW_CHUNKS)]


def _tiles_per_mod_row(mod, n, seq, tile):
    return seq // tile if mod.shape[0] > 1 else n // tile


def _mod_kernel(c_ref, w_ref, b_ref, o_ref):
    a = _silu(c_ref[...]).astype(BF16)
    o_ref[...] = jnp.dot(a, w_ref[...].astype(BF16), preferred_element_type=F32) + b_ref[...]


def _modulation(cond8, w_mod, b_mod):
    n_out = w_mod.shape[1]
    tn = 1536
    return pl.pallas_call(
        _mod_kernel,
        grid=(n_out // tn,),
        in_specs=[pl.BlockSpec((8, D_MODEL), lambda j: (0, 0)),
                  pl.BlockSpec((D_MODEL, tn), lambda j: (0, j)),
                  pl.BlockSpec((1, tn), lambda j: (0, j))],
        out_specs=pl.BlockSpec((8, tn), lambda j: (0, j)),
        out_shape=jax.ShapeDtypeStruct((8, n_out), F32),
        compiler_params=_cparams(("arbitrary",)),
        name="modulation",
    )(cond8, w_mod, b_mod.reshape(1, n_out))


NORM_ROWS = 64


def _in_proj_kernel(x_ref, sc_ref, sh_ref, g_ref, w_ref, o_ref, tail_ref, h_scr):
    j = pl.program_id(1)

    @pl.when(j == 0)
    def _():
        def slab(r, _):
            rows = pl.ds(pl.multiple_of(r * NORM_ROWS, NORM_ROWS), NORM_ROWS)
            x = x_ref[rows, :]
            y = x * lax.rsqrt(jnp.mean(x * x, axis=-1, keepdims=True) + EPS) * g_ref[...]
            h_scr[rows, :] = (y * (1.0 + sc_ref[0]) + sh_ref[0]).astype(BF16)
            return 0
        lax.fori_loop(0, x_ref.shape[0] // NORM_ROWS, slab, 0)

    acc = jnp.dot(h_scr[...], w_ref[...], preferred_element_type=F32)
    o_ref[...] = acc.astype(BF16)

    @pl.when(j == IN_PAD // IN_TN - 1)
    def _():
        first = TAIL_COL0 - (IN_PAD - IN_TN)
        tail_ref[...] = acc[:, first:first + TAIL_W]


def _in_proj(x2d, seq, sc, sh, g, w_r):
    n = x2d.shape[0]
    tm = min(1024, seq if sc.shape[0] > 1 else n)
    per_b = _tiles_per_mod_row(sc, n, seq, tm)
    return pl.pallas_call(
        _in_proj_kernel,
        grid=(n // tm, IN_PAD // IN_TN),
        in_specs=[pl.BlockSpec((tm, D_MODEL), lambda i, j: (i, 0)),
                  pl.BlockSpec((1, 1, D_MODEL), lambda i, j: (i // per_b, 0, 0)),
                  pl.BlockSpec((1, 1, D_MODEL), lambda i, j: (i // per_b, 0, 0)),
                  pl.BlockSpec((1, D_MODEL), lambda i, j: (0, 0)),
                  pl.BlockSpec((D_MODEL, IN_TN), lambda i, j: (0, j))],
        out_specs=[pl.BlockSpec((tm, IN_TN), lambda i, j: (i, j)),
                   pl.BlockSpec((tm, TAIL_W), lambda i, j: (i, 0))],
        out_shape=[jax.ShapeDtypeStruct((n, IN_PAD), BF16),
                   jax.ShapeDtypeStruct((n, TAIL_W), F32)],
        scratch_shapes=[pltpu.VMEM((tm, D_MODEL), BF16)],
        compiler_params=_cparams(("parallel", "arbitrary")),
        name="in_proj",
    )(x2d, sc, sh, g, w_r)


GLA_BLK = 256
GLA_HPS = 4


def _tn_dot(a, b, precision=None):
    return lax.dot_general(a, b, (((0,), (0,)), ((), ())), precision=precision,
                           preferred_element_type=F32)


def _nt_dot(a, b, precision=None):
    return lax.dot_general(a, b, (((1,), (1,)), ((), ())), precision=precision,
                           preferred_element_type=F32)


def _split_bf16(x, parts):
    out = []
    for _ in range(parts):
        piece = x.astype(BF16)
        out.append(piece)
        x = x - piece.astype(F32)
    return out


def _dot3(x, w):
    xh, xl = _split_bf16(x, 2)
    wh, wl = _split_bf16(w, 2)
    d = functools.partial(jnp.dot, preferred_element_type=F32)
    return d(xh, wh) + d(xl, wh) + d(xh, wl)


def _gla_kernel(q_ref, k_ref, v_ref, r_ref, dec_ref, wdf_ref, bdf_ref, wdb_ref, bdb_ref, gg_ref,
                s0f_ref, s0b_ref, o_ref, stf_ref, stb_ref, state_t, of_scr, *, nblk):
    i = pl.program_id(2)
    c = GLA_CHUNK
    nch = GLA_BLK // c
    ri = lax.broadcasted_iota(jnp.int32, (GLA_BLK, GLA_BLK), 0)
    ci = lax.broadcasted_iota(jnp.int32, (GLA_BLK, GLA_BLK), 1)
    same_chunk = (ri // c) == (ci // c)

    @pl.when(i == 0)
    def _():
        for h in range(GLA_HPS):
            state_t[h] = s0f_ref[0, h].T

    @pl.when(i == nblk)
    def _():
        for h in range(GLA_HPS):
            state_t[h] = s0b_ref[0, h].T

    def run(fwd):
        blk = i if fwd else 2 * nblk - 1 - i
        keep = same_chunk & ((ci <= ri) if fwd else (ci >= ri))
        w_ref, b_ref = (wdf_ref, bdf_ref) if fwd else (wdb_ref, bdb_ref)
        logit = _dot3(dec_ref[...], w_ref[...]) + b_ref[...]
        g = (jnp.minimum(logit, 0.0) - jnp.log(1.0 + jnp.exp(-jnp.abs(logit)))) / GLA_TAU
        tri = keep.astype(BF16)
        b = sum(jnp.dot(tri, piece, preferred_element_type=F32) for piece in _split_bf16(g, 3))
        edge = c - 1 if fwd else 0
        tots = [b[ch * c + edge:ch * c + edge + 1, :] for ch in range(nch)]
        totb = jnp.concatenate([jnp.broadcast_to(t, (c, t.shape[1])) for t in tots], axis=0)
        q_in = (q_ref[...].astype(F32) * (GLA_DK ** -0.5) * jnp.exp(b)).astype(BF16)
        k = k_ref[...].astype(F32)
        k_in = (k * jnp.exp(-b)).astype(BF16)
        k_st = (k * jnp.exp(totb - b)).astype(BF16)
        srow = pl.multiple_of(blk * GLA_BLK, GLA_BLK)
        for h in range(GLA_HPS):
            ks = slice(h * GLA_DK, (h + 1) * GLA_DK)
            vs = slice(h * GLA_DV, (h + 1) * GLA_DV)
            v = v_ref[:, vs].astype(BF16)
            a = jnp.where(keep, _nt_dot(q_in[:, ks], k_in[:, ks]), 0.0).astype(BF16)
            o_intra = jnp.dot(a, v, preferred_element_type=F32)
            st = state_t[h]
            o_inter = [None] * nch
            for cc in range(nch):
                ch = cc if fwd else nch - 1 - cc
                rows = slice(ch * c, (ch + 1) * c)
                o_inter[ch] = _nt_dot(q_in[rows, ks], st.astype(BF16))
                st = st * jnp.exp(tots[ch][:, ks]) + _tn_dot(v[rows, :], k_st[rows, ks])
            state_t[h] = st
            o = o_intra + jnp.concatenate(o_inter, axis=0)
            if fwd:
                of_scr[pl.ds(srow, GLA_BLK), vs] = o
            else:
                o = o + of_scr[pl.ds(srow, GLA_BLK), vs]
                o = o * lax.rsqrt(jnp.mean(o * o, axis=-1, keepdims=True) + EPS) * gg_ref[:, vs]
                o_ref[:, vs] = (o * _silu(r_ref[:, vs].astype(F32))).astype(BF16)

    @pl.when(i < nblk)
    def _():
        run(True)

    @pl.when(i >= nblk)
    def _():
        run(False)

    @pl.when(i == nblk - 1)
    def _():
        for h in range(GLA_HPS):
            stf_ref[0, h] = state_t[h].T

    @pl.when(i == 2 * nblk - 1)
    def _():
        for h in range(GLA_HPS):
            stb_ref[0, h] = state_t[h].T


def _gla(p, dec, bsz, seq, wdf, bdf, wdb, bdb, g_gla, s0f, s0b):
    n = p.shape[0]
    nblk = seq // GLA_BLK
    kw = GLA_HPS * GLA_DK
    vw = GLA_HPS * GLA_DV

    def rb(b, i):
        return b * nblk + jnp.where(i < nblk, i, 2 * nblk - 1 - i)

    def orb(b, i):
        return b * nblk + jnp.where(i < nblk, nblk - 1, 2 * nblk - 1 - i)

    st_spec = pl.BlockSpec((1, GLA_HPS, GLA_DK, GLA_DV), lambda b, h, i: (b, h, 0, 0))
    s0_spec = pl.BlockSpec((1, GLA_HPS, GLA_DK, GLA_DV),
                           lambda b, h, i: (b if s0f.shape[0] > 1 else 0, h, 0, 0))
    st_shape = jax.ShapeDtypeStruct((bsz, GLA_HEADS, GLA_DK, GLA_DV), F32)
    return pl.pallas_call(
        functools.partial(_gla_kernel, nblk=nblk),
        grid=(bsz, GLA_HEADS // GLA_HPS, 2 * nblk),
        in_specs=[pl.BlockSpec((GLA_BLK, kw), lambda b, h, i: (rb(b, i), COL_GQ // kw + h)),
                  pl.BlockSpec((GLA_BLK, kw), lambda b, h, i: (rb(b, i), COL_GK // kw + h)),
                  pl.BlockSpec((GLA_BLK, vw), lambda b, h, i: (rb(b, i), COL_GV // vw + h)),
                  pl.BlockSpec((GLA_BLK, vw), lambda b, h, i: (rb(b, i), COL_GR // vw + h)),
                  pl.BlockSpec((GLA_BLK, LANES), lambda b, h, i: (rb(b, i), (COL_DEC - TAIL_COL0) // LANES)),
                  pl.BlockSpec((LANES, kw), lambda b, h, i: (0, h)),
                  pl.BlockSpec((1, kw), lambda b, h, i: (0, h)),
                  pl.BlockSpec((LANES, kw), lambda b, h, i: (0, h)),
                  pl.BlockSpec((1, kw), lambda b, h, i: (0, h)),
                  pl.BlockSpec((1, vw), lambda b, h, i: (0, h)),
                  s0_spec, s0_spec],
        out_specs=[pl.BlockSpec((GLA_BLK, vw), lambda b, h, i: (orb(b, i), h)), st_spec, st_spec],
        out_shape=[jax.ShapeDtypeStruct((n, GLA_VW), BF16), st_shape, st_shape],
        scratch_shapes=[pltpu.VMEM((GLA_HPS, GLA_DV, GLA_DK), F32), pltpu.VMEM((seq, vw), F32)],
        compiler_params=_cparams(("parallel", "parallel", "arbitrary")),
        name="gla",
    )(p, p, p, p, dec, wdf, bdf, wdb, bdb, g_gla, s0f, s0b)


def _sink_column(sink_ref, kvh, rows_per_head):
    r = lax.broadcasted_iota(jnp.int32, (SWA_GROUP * rows_per_head, 1), 0)
    col = jnp.zeros((SWA_GROUP * rows_per_head, 1), F32)
    for g in range(SWA_GROUP):
        col = jnp.where(r // rows_per_head == g, sink_ref[kvh * SWA_GROUP + g], col)
    return col


def _rope(x, cos, sin):
    outs = []
    for cb in range(x.shape[1] // LANES):
        sl = slice(cb * LANES, (cb + 1) * LANES)
        xc = x[:, sl]
        lane = lax.broadcasted_iota(jnp.int32, xc.shape, 1)
        sw = jnp.where(lane % 32 < 16, pltpu.roll(xc, LANES - 16, 1), pltpu.roll(xc, 16, 1))
        outs.append(xc * cos[:, sl] + sw * sin[:, sl])
    return jnp.concatenate(outs, axis=1)


def _swa_lat_kernel(sink_ref, q_ref, kp_ref, kc_ref, kn_ref, vp_ref, vc_ref, vn_ref, ck_ref, cv_ref,
                    cp_ref, cc_ref, cn_ref, sp_ref, sc_ref, sn_ref, o_ref, *, nb):
    n = pl.program_id(1)
    blk = SWA_BLOCK
    hd = SWA_HEAD_DIM
    cos_c, sin_c = cc_ref[...], sc_ref[...]
    kband = jnp.concatenate([_rope(kp_ref[...].astype(F32), cp_ref[...], sp_ref[...]),
                             _rope(kc_ref[...].astype(F32), cos_c, sin_c),
                             _rope(kn_ref[...].astype(F32), cn_ref[...], sn_ref[...])], axis=0).astype(BF16)
    vband = jnp.concatenate([vp_ref[...], vc_ref[...], vn_ref[...]], axis=0).astype(BF16)
    ck = ck_ref[0].astype(BF16)
    cv = cv_ref[0].astype(BF16)
    qi = lax.broadcasted_iota(jnp.int32, (blk, 3 * blk), 0)
    kj = lax.broadcasted_iota(jnp.int32, (blk, 3 * blk), 1)
    k_abs = kj + (n - 1) * blk
    mask = (jnp.abs(kj - blk - qi) <= SWA_BLOCK) & (k_abs >= 0) & (k_abs < nb * blk)
    bias = jnp.concatenate([jnp.where(mask, 0.0, -1e30)] * SWA_GROUP, axis=0)
    nctx = ck.shape[0]
    kall = jnp.concatenate([ck, kband], axis=0)
    vall = jnp.concatenate([cv, vband], axis=0)
    for kvh in range(SWA_KV_HEADS):
        ks = slice(kvh * hd, (kvh + 1) * hd)
        qr = _rope(q_ref[:, kvh * SWA_KVW:(kvh + 1) * SWA_KVW].astype(F32), cos_c, sin_c) * (hd ** -0.5)
        qg = jnp.concatenate([qr[:, g * hd:(g + 1) * hd] for g in range(SWA_GROUP)],
                             axis=0).astype(BF16)
        s = _nt_dot(qg, kall[:, ks])
        s = jnp.concatenate([s[:, :nctx], s[:, nctx:] + bias], axis=1)
        sink = _sink_column(sink_ref, kvh, blk)
        m = jnp.maximum(jnp.max(s, axis=-1, keepdims=True), sink)
        pr = jnp.exp(s - m)
        den = jnp.exp(sink - m) + jnp.sum(pr, axis=-1, keepdims=True)
        o = jnp.dot(pr.astype(BF16), vall[:, ks], preferred_element_type=F32) / den
        for g in range(SWA_GROUP):
            cb = (kvh * SWA_GROUP + g) * hd
            o_ref[:, cb:cb + hd] = o[g * blk:(g + 1) * blk, :].astype(BF16)


def _swa_latent(p, bsz, seq, cache_k, cache_v, sink, cos_t, sin_t):
    n = p.shape[0]
    nb = seq // SWA_BLOCK
    kcol = COL_SK // SWA_KVW
    vcol = COL_SV // SWA_KVW
    prev = lambda i: jnp.maximum(i - 1, 0)
    nxt = lambda i: jnp.minimum(i + 1, nb - 1)

    def pspec(col, f):
        return pl.BlockSpec((SWA_BLOCK, SWA_KVW), lambda b, i, s: (b * nb + f(i), col))

    def tspec(f):
        return pl.BlockSpec((SWA_BLOCK, SWA_KVW), lambda b, i, s: (f(i), 0))

    same = lambda i: i
    cspec = pl.BlockSpec((1, cache_k.shape[1], SWA_KVW), lambda b, i, s: (b, 0, 0))
    grid_spec = pltpu.PrefetchScalarGridSpec(
        num_scalar_prefetch=1,
        grid=(bsz, nb),
        in_specs=[pl.BlockSpec((SWA_BLOCK, SWA_QW), lambda b, i, s: (b * nb + i, COL_SQ // SWA_QW)),
                  pspec(kcol, prev), pspec(kcol, same), pspec(kcol, nxt),
                  pspec(vcol, prev), pspec(vcol, same), pspec(vcol, nxt),
                  cspec, cspec,
                  tspec(prev), tspec(same), tspec(nxt), tspec(prev), tspec(same), tspec(nxt)],
        out_specs=pl.BlockSpec((SWA_BLOCK, SWA_QW), lambda b, i, s: (b * nb + i, 0)),
    )
    return pl.pallas_call(
        functools.partial(_swa_lat_kernel, nb=nb),
        grid_spec=grid_spec,
        out_shape=jax.ShapeDtypeStruct((n, SWA_QW), BF16),
        compiler_params=_cparams(("parallel", "arbitrary")),
        name="swa_latent",
    )(sink, p, p, p, p, p, p, p, cache_k, cache_v, cos_t, cos_t, cos_t, sin_t, sin_t, sin_t)


def _swa_ctx_kernel(sink_ref, q_ref, k_ref, v_ref, o_ref):
    s = q_ref.shape[0]
    hd = SWA_HEAD_DIM
    kb = k_ref[...].astype(BF16)
    vb = v_ref[...].astype(BF16)
    for kvh in range(SWA_KV_HEADS):
        ks = slice(kvh * hd, (kvh + 1) * hd)
        qg = jnp.concatenate([q_ref[:, (kvh * SWA_GROUP + g) * hd:(kvh * SWA_GROUP + g + 1) * hd]
                              for g in range(SWA_GROUP)], axis=0)
        qg = (qg.astype(F32) * (hd ** -0.5)).astype(BF16)
        sc = _nt_dot(qg, kb[:, ks])
        sink = _sink_column(sink_ref, kvh, s)
        m = jnp.maximum(jnp.max(sc, axis=-1, keepdims=True), sink)
        pr = jnp.exp(sc - m)
        den = jnp.exp(sink - m) + jnp.sum(pr, axis=-1, keepdims=True)
        o = jnp.dot(pr.astype(BF16), vb[:, ks], preferred_element_type=F32) / den
        for g in range(SWA_GROUP):
            cb = (kvh * SWA_GROUP + g) * hd
            o_ref[:, cb:cb + hd] = o[g * s:(g + 1) * s, :].astype(BF16)


def _swa_context(p, bsz, seq, sink):
    n = p.shape[0]
    grid_spec = pltpu.PrefetchScalarGridSpec(
        num_scalar_prefetch=1,
        grid=(bsz,),
        in_specs=[pl.BlockSpec((seq, SWA_QW), lambda b, s: (b, COL_SQ // SWA_QW)),
                  pl.BlockSpec((seq, SWA_KVW), lambda b, s: (b, COL_SK // SWA_KVW)),
                  pl.BlockSpec((seq, SWA_KVW), lambda b, s: (b, COL_SV // SWA_KVW))],
        out_specs=pl.BlockSpec((seq, SWA_QW), lambda b, s: (b, 0)),
    )
    return pl.pallas_call(
        _swa_ctx_kernel,
        grid_spec=grid_spec,
        out_shape=jax.ShapeDtypeStruct((n, SWA_QW), BF16),
        compiler_params=_cparams(("parallel",)),
        name="swa_context",
    )(sink, p, p, p)


MERGE_TM = 512


BRANCH_TM = 1024
BRANCH_COLS = 512


def _branch_kernel(og_ref, os_ref, ga_ref, gb_ref, wa_ref, wb_ref, m_ref):
    for c in range(D_MODEL // BRANCH_COLS):
        cols = slice(c * BRANCH_COLS, (c + 1) * BRANCH_COLS)
        a = jnp.dot(og_ref[...], wa_ref[:, cols], preferred_element_type=F32)
        b = jnp.dot(os_ref[...], wb_ref[:, cols], preferred_element_type=F32)
        m_ref[:, cols] = (_sigmoid(ga_ref[:, cols].astype(F32)) * a
                          + _sigmoid(gb_ref[:, cols].astype(F32)) * b).astype(BF16)


def _branch_merge(o_gla, o_swa, p, wa, wb):
    n = o_gla.shape[0]
    tm = BRANCH_TM
    once = pl.Buffered(1)
    return pl.pallas_call(
        _branch_kernel,
        grid=(n // tm,),
        in_specs=[pl.BlockSpec((tm, GLA_VW), lambda i: (i, 0)),
                  pl.BlockSpec((tm, SWA_QW), lambda i: (i, 0)),
                  pl.BlockSpec((tm, D_MODEL), lambda i: (i, COL_GA // D_MODEL)),
                  pl.BlockSpec((tm, D_MODEL), lambda i: (i, COL_GB // D_MODEL)),
                  pl.BlockSpec((GLA_VW, D_MODEL), lambda i: (0, 0), pipeline_mode=once),
                  pl.BlockSpec((SWA_QW, D_MODEL), lambda i: (0, 0), pipeline_mode=once)],
        out_specs=pl.BlockSpec((tm, D_MODEL), lambda i: (i, 0)),
        out_shape=jax.ShapeDtypeStruct((n, D_MODEL), BF16),
        compiler_params=_cparams(("parallel",)),
        name="branch_merge",
    )(o_gla, o_swa, p, p, wa, wb)


def _out_kernel(m_ref, x_ref, gt1_ref, sc2_ref, sh2_ref, g2_ref, wo_ref, wr2_ref,
                x1_ref, h2_ref, lg_ref):
    x1 = x_ref[...] + gt1_ref[0] * jnp.dot(m_ref[...], wo_ref[...], preferred_element_type=F32)
    x1_ref[...] = x1
    y = x1 * lax.rsqrt(jnp.mean(x1 * x1, axis=-1, keepdims=True) + EPS) * g2_ref[...]
    h2 = y * (1.0 + sc2_ref[0]) + sh2_ref[0]
    _store_row_major(h2_ref, h2)
    l2 = jnp.dot(h2.astype(BF16), wr2_ref[...], preferred_element_type=F32)
    lt = l2.T
    lt = lt[0:N_EXPERTS, :] + lt[N_EXPERTS:2 * N_EXPERTS, :]
    for cb in range(lt.shape[1] // LANES):
        lg_ref[cb] = lt[:, cb * LANES:(cb + 1) * LANES]


def _out_proj(merged, x2d, seq, gt1, sc2, sh2, g2, wo, wr2):
    n = x2d.shape[0]
    tm = MERGE_TM
    per_b = _tiles_per_mod_row(gt1, n, seq, tm)
    mod_spec = pl.BlockSpec((1, 1, D_MODEL), lambda i: (i // per_b, 0, 0))
    once = pl.Buffered(1)
    return pl.pallas_call(
        _out_kernel,
        grid=(n // tm,),
        in_specs=[pl.BlockSpec((tm, D_MODEL), lambda i: (i, 0)),
                  pl.BlockSpec((tm, D_MODEL), lambda i: (i, 0)),
                  mod_spec, mod_spec, mod_spec,
                  pl.BlockSpec((1, D_MODEL), lambda i: (0, 0)),
                  pl.BlockSpec((D_MODEL, D_MODEL), lambda i: (0, 0), pipeline_mode=once),
                  pl.BlockSpec((D_MODEL, LANES), lambda i: (0, 0))],
        out_specs=[pl.BlockSpec((tm, D_MODEL), lambda i: (i, 0)),
                   pl.BlockSpec((tm, ROW_CHUNKS, LANES), lambda i: (i, 0, 0)),
                   pl.BlockSpec((tm // LANES, N_EXPERTS, LANES), lambda i: (i, 0, 0))],
        out_shape=[jax.ShapeDtypeStruct((n, D_MODEL), F32),
                   jax.ShapeDtypeStruct((n, ROW_CHUNKS, LANES), F32),
                   jax.ShapeDtypeStruct((n // LANES, N_EXPERTS, LANES), F32)],
        compiler_params=_cparams(("parallel",)),
        name="out_proj",
    )(merged, x2d, gt1, sc2, sh2, g2, wo, wr2)


def _select_kernel(lg_ref, aff_ref, sel_ref, pos_ref, *, cap):
    nblk = lg_ref.shape[0]
    lg = lg_ref[...]
    ex = jnp.exp(lg - jnp.max(lg, axis=1, keepdims=True))
    aff = ex / jnp.sum(ex, axis=1, keepdims=True)
    aff_ref[...] = aff
    bits = lax.bitcast_convert_type(aff, jnp.int32)

    def count(pred):
        c = jnp.sum(jnp.where(pred, 1.0, 0.0), axis=0, keepdims=True)
        return jnp.sum(c, axis=2, keepdims=True)

    def bit_step(t, cur):
        cand = cur | jnp.left_shift(jnp.int32(1), 30 - t)
        return jnp.where(count(bits >= cand) >= cap, cand, cur)

    thr = lax.fori_loop(0, 31, bit_step, jnp.zeros((1, N_EXPERTS, 1), jnp.int32))
    need = (cap - count(bits > thr))[0]
    thr2 = thr[0]
    ri = lax.broadcasted_iota(jnp.int32, (LANES, LANES), 0)
    ci = lax.broadcasted_iota(jnp.int32, (LANES, LANES), 1)
    upper = (ri <= ci).astype(BF16)

    def blk_step(b, carry):
        run_eq, run_sel = carry
        bb = lax.bitcast_convert_type(aff_ref[b], jnp.int32)
        eq = (bb == thr2).astype(F32)
        eq_excl = jnp.dot(eq.astype(BF16), upper, preferred_element_type=F32) - eq + run_eq
        sel = jnp.where((bb > thr2) | ((eq > 0) & (eq_excl < need)), 1.0, 0.0)
        sel_ref[b] = sel
        pos_ref[b] = jnp.dot(sel.astype(BF16), upper, preferred_element_type=F32) - sel + run_sel
        return (run_eq + jnp.sum(eq, axis=1, keepdims=True),
                run_sel + jnp.sum(sel, axis=1, keepdims=True))

    zero = jnp.zeros((N_EXPERTS, 1), F32)
    lax.fori_loop(0, nblk, blk_step, (zero, zero))


def _select(logits3, cap):
    shp = jax.ShapeDtypeStruct(logits3.shape, F32)
    return pl.pallas_call(
        functools.partial(_select_kernel, cap=float(cap)),
        out_shape=[shp, shp, shp],
        compiler_params=pltpu.CompilerParams(vmem_limit_bytes=VMEM_LIMIT),
        name="select",
    )(logits3)


COMPACT_WIN = 136


def _compact_kernel(base_ref, aff_ref, sel_ref, pos_ref, list_ref):
    nblk = aff_ref.shape[0]
    list_ref[...] = jnp.zeros_like(list_ref)
    slot = lax.broadcasted_iota(jnp.int32, (COMPACT_WIN, LANES), 0).astype(F32)
    lane = lax.broadcasted_iota(jnp.int32, (COMPACT_WIN, LANES), 1)

    def blk_step(b, _):
        aff = aff_ref[b]
        pos = jnp.where(sel_ref[b] > 0, pos_ref[b], -1.0)
        tok = (lane + b * LANES).astype(F32)
        for e in range(N_EXPERTS):
            base8 = pl.multiple_of((base_ref[b * N_EXPERTS + e] >> 3) << 3, 8)
            hit = pos[e:e + 1, :] == slot + base8.astype(F32)
            ic = jnp.sum(jnp.where(hit, tok, 0.0), axis=1, keepdims=True)
            gc = jnp.sum(jnp.where(hit, aff[e:e + 1, :], 0.0), axis=1, keepdims=True)
            rows = pl.ds(base8, COMPACT_WIN)
            list_ref[rows, :] += jnp.where(lane == e, ic, jnp.where(lane == N_EXPERTS + e, gc, 0.0))
        return 0

    lax.fori_loop(0, nblk, blk_step, 0)


def _compact(base, aff3, sel3, pos3, cap):
    cp = cap + 2 * LANES
    full = pl.BlockSpec(aff3.shape, lambda i, s: (0, 0, 0))
    ospec = pl.BlockSpec((cp, LANES), lambda i, s: (0, 0))
    oshape = jax.ShapeDtypeStruct((cp, LANES), F32)
    grid_spec = pltpu.PrefetchScalarGridSpec(
        num_scalar_prefetch=1, grid=(1,), in_specs=[full, full, full], out_specs=ospec)
    return pl.pallas_call(
        _compact_kernel,
        grid_spec=grid_spec,
        out_shape=oshape,
        compiler_params=_cparams(("arbitrary",)),
        name="compact",
    )(base, aff3, sel3, pos3)


FFN_COLS = 256


def _up_kernel(idx_ref, h2_hbm, wg_ref, wu_ref, hid_ref, gbuf, xs, sem, *, tm, total):
    step = pl.program_id(0) * pl.num_programs(1) + pl.program_id(1)

    def row_copy(step_, i):
        return pltpu.make_async_copy(h2_hbm.at[idx_ref[step_ * tm + i]], gbuf.at[i], sem)

    def wait_rows():
        pltpu.make_async_copy(h2_hbm.at[pl.ds(0, tm)], gbuf, sem).wait()

    @pl.when(step == 0)
    def _():
        def body(i, _):
            row_copy(step, i).start()
            return 0
        lax.fori_loop(0, tm, body, 0)

    wait_rows()
    slab = min(SLAB, tm)
    for tb in range(tm // slab):
        rows = slice(tb * slab, (tb + 1) * slab)
        for s, chunk in enumerate(_load_row_major(gbuf, rows)):
            xs[rows, s * LANES:(s + 1) * LANES] = chunk.astype(BF16)

    nxt = jnp.minimum(step + 1, total - 1)
    for i in range(tm):
        row_copy(nxt, i).start(priority=i % 2)

    for f in range(EXPERT_FF // FFN_COLS):
        cols = slice(f * FFN_COLS, (f + 1) * FFN_COLS)
        hg = jnp.dot(xs[...], wg_ref[0, :, cols].astype(BF16), preferred_element_type=F32)
        hu = jnp.dot(xs[...], wu_ref[0, :, cols].astype(BF16), preferred_element_type=F32)
        hid_ref[:, cols] = (_silu(hg) * hu).astype(BF16)

    @pl.when(step == total - 1)
    def _():
        wait_rows()


def _expert_up(idx_flat, h2_rm, wg, wu, cap):
    tm = min(1024, cap)
    nr = cap // tm
    grid_spec = pltpu.PrefetchScalarGridSpec(
        num_scalar_prefetch=1,
        grid=(N_EXPERTS, nr),
        in_specs=[pl.BlockSpec(memory_space=pl.ANY),
                  pl.BlockSpec((1, D_MODEL, EXPERT_FF), lambda e, r, s: (e, 0, 0)),
                  pl.BlockSpec((1, D_MODEL, EXPERT_FF), lambda e, r, s: (e, 0, 0))],
        out_specs=pl.BlockSpec((tm, EXPERT_FF), lambda e, r, s: (e * nr + r, 0)),
        scratch_shapes=[pltpu.VMEM((tm, ROW_CHUNKS, LANES), F32),
                        pltpu.VMEM((tm, D_MODEL), BF16),
                        pltpu.SemaphoreType.DMA(())],
    )
    return pl.pallas_call(
        functools.partial(_up_kernel, tm=tm, total=N_EXPERTS * nr),
        grid_spec=grid_spec,
        out_shape=jax.ShapeDtypeStruct((N_EXPERTS * cap, EXPERT_FF), BF16),
        compiler_params=_cparams(("arbitrary", "arbitrary")),
        name="expert_up",
    )(idx_flat, h2_rm, wg, wu)


def _down_kernel(hid_ref, gate_ref, wd_ref, o_ref, ybuf):
    gate = gate_ref[...]
    for c in range(D_MODEL // FFN_COLS):
        cols = slice(c * FFN_COLS, (c + 1) * FFN_COLS)
        ybuf[:, cols] = gate * jnp.dot(hid_ref[...], wd_ref[0, :, cols].astype(BF16),
                                       preferred_element_type=F32)
    tm = ybuf.shape[0]
    slab = min(SLAB, tm)
    for tb in range(tm // slab):
        rows = slice(tb * slab, (tb + 1) * slab)
        parts = jnp.stack([ybuf[rows, s * LANES:(s + 1) * LANES] for s in range(ROW_CHUNKS)], axis=0)
        o_ref[rows] = pltpu.einshape("stl->tsl", parts)


def _expert_down(hid, gate_col, wd, cap):
    tm = min(1024, cap)
    nr = cap // tm
    return pl.pallas_call(
        _down_kernel,
        grid=(N_EXPERTS, nr),
        in_specs=[pl.BlockSpec((tm, EXPERT_FF), lambda e, r: (e * nr + r, 0)),
                  pl.BlockSpec((tm, 1), lambda e, r: (e * nr + r, 0)),
                  pl.BlockSpec((1, EXPERT_FF, D_MODEL), lambda e, r: (e, 0, 0))],
        out_specs=pl.BlockSpec((tm, ROW_CHUNKS, LANES), lambda e, r: (e * nr + r, 0, 0)),
        out_shape=jax.ShapeDtypeStruct((N_EXPERTS * cap, ROW_CHUNKS, LANES), F32),
        scratch_shapes=[pltpu.VMEM((tm, D_MODEL), F32)],
        compiler_params=_cparams(("parallel", "arbitrary")),
        name="expert_down",
    )(hid, gate_col, wd)


COMBINE_UNROLL = 4
COMBINE_CHUNK = 32
COMBINE_NCHUNK = 3


def _combine_kernel(idx_ref, lo_ref, ye_hbm, x1_ref, gt2_ref, gf_ref, o_ref, stage, acc, sem,
                    *, tt, ch, cap, ntiles):
    tile = pl.program_id(0)
    slot = tile % 2

    window = COMBINE_NCHUNK * ch

    def chunk_start(first_row):
        return jnp.minimum(first_row, cap - window)

    def chunk_copy(slot_, e, start, c):
        return pltpu.make_async_copy(ye_hbm.at[pl.ds(e * cap + start + c * ch, ch)],
                                     stage.at[slot_, e, pl.ds(c * ch, ch)],
                                     sem.at[slot_, e])

    def for_needed_chunks(tile_, e, fn):
        lo = lo_ref[tile_ * N_EXPERTS + e]
        hi = lo_ref[(tile_ + 1) * N_EXPERTS + e]
        start = chunk_start(lo)
        for c in range(COMBINE_NCHUNK):
            @pl.when((start + c * ch < hi) & (start + (c + 1) * ch > lo))
            def _():
                fn(start, c)

    def issue(tile_, slot_):
        for e in range(N_EXPERTS):
            for_needed_chunks(tile_, e, lambda start, c: chunk_copy(slot_, e, start, c).start())

    @pl.when(tile == 0)
    def _():
        issue(tile, slot)

    @pl.when(tile + 1 < ntiles)
    def _():
        issue(tile + 1, 1 - slot)

    acc[...] = jnp.zeros_like(acc)

    def add_rows(e, start, r0, r1):
        def group(r, width):
            dsts, vals = [], []
            rows = stage[slot, e, pl.ds(r - start, width)]
            for u in range(width):
                dst = idx_ref[e * cap + r + u] - tile * tt
                dsts.append(dst)
                vals.append(acc[dst] + rows[u])
            for dst, val in zip(dsts, vals):
                acc[dst] = val

        ngroups = (r1 - r0) // COMBINE_UNROLL

        def body_group(j, _):
            group(r0 + j * COMBINE_UNROLL, COMBINE_UNROLL)
            return 0
        lax.fori_loop(0, ngroups, body_group, 0)

        def body_one(r, _):
            group(r, 1)
            return 0
        lax.fori_loop(r0 + ngroups * COMBINE_UNROLL, r1, body_one, 0)

    for e in range(N_EXPERTS):
        lo = lo_ref[tile * N_EXPERTS + e]
        hi = lo_ref[(tile + 1) * N_EXPERTS + e]
        start = chunk_start(lo)
        for_needed_chunks(tile, e, lambda start_, c: chunk_copy(slot, e, start_, c).wait())
        first_end = jnp.minimum(hi, start + window)
        add_rows(e, start, lo, first_end)

        def more(r0):
            st = jnp.minimum(r0, cap - ch)
            cp = chunk_copy(slot, e, st, 0)
            cp.start()
            cp.wait()
            r1 = jnp.minimum(hi, st + ch)
            add_rows(e, st, r0, r1)
            return r1
        lax.while_loop(lambda r0: r0 < hi, more, first_end)

    for tb in range(tt // SLAB):
        rows = slice(tb * SLAB, (tb + 1) * SLAB)
        ssq = jnp.zeros((SLAB, 1), F32)
        for s, y in enumerate(_load_row_major(acc, rows)):
            cols = slice(s * LANES, (s + 1) * LANES)
            x2 = x1_ref[rows, cols] + gt2_ref[0][:, cols] * y
            o_ref[rows, cols] = x2
            ssq = ssq + jnp.sum(x2 * x2, axis=-1, keepdims=True)
        o_ref[rows, :] = o_ref[rows, :] * lax.rsqrt(ssq / D_MODEL + EPS) * gf_ref[...]


def _combine(idx_flat, lo_tab, ye_lin, x1, seq, gt2, g_final, cap, tt):
    n = x1.shape[0]
    ch = min(COMBINE_CHUNK, cap // COMBINE_NCHUNK)
    ntiles = n // tt
    per_b = _tiles_per_mod_row(gt2, n, seq, tt)
    grid_spec = pltpu.PrefetchScalarGridSpec(
        num_scalar_prefetch=2,
        grid=(ntiles,),
        in_specs=[pl.BlockSpec(memory_space=pl.ANY),
                  pl.BlockSpec((tt, D_MODEL), lambda i, a, b: (i, 0)),
                  pl.BlockSpec((1, 1, D_MODEL), lambda i, a, b: (i // per_b, 0, 0)),
                  pl.BlockSpec((1, D_MODEL), lambda i, a, b: (0, 0))],
        out_specs=pl.BlockSpec((tt, D_MODEL), lambda i, a, b: (i, 0)),
        scratch_shapes=[pltpu.VMEM((2, N_EXPERTS, COMBINE_NCHUNK * ch, ROW_CHUNKS, LANES), F32),
                        pltpu.VMEM((tt, ROW_CHUNKS, LANES), F32),
                        pltpu.SemaphoreType.DMA((2, N_EXPERTS))],
    )
    return pl.pallas_call(
        functools.partial(_combine_kernel, tt=tt, ch=ch, cap=cap, ntiles=ntiles),
        grid_spec=grid_spec,
        out_shape=jax.ShapeDtypeStruct((n, D_MODEL), F32),
        compiler_params=_cparams(("arbitrary",)),
        name="combine",
    )(idx_flat, lo_tab, ye_lin, x1, gt2, g_final)


def _rope_tables(seq):
    pos = np.arange(seq)
    row = (pos // GRID_W).astype(np.float32)
    col = (pos % GRID_W).astype(np.float32)
    npair = SWA_HEAD_DIM // 4
    inv_freq = (ROPE_BASE ** (-np.arange(npair, dtype=np.float32) / npair)).astype(np.float32)
    ar = (row[:, None] * inv_freq[None, :]).astype(np.float64)
    ac = (col[:, None] * inv_freq[None, :]).astype(np.float64)
    cos = np.concatenate([np.cos(ar), np.cos(ar), np.cos(ac), np.cos(ac)], axis=1)
    sin = np.concatenate([-np.sin(ar), np.sin(ar), -np.sin(ac), np.sin(ac)], axis=1)
    reps = SWA_KVW // SWA_HEAD_DIM
    return (jnp.asarray(np.tile(cos, (1, reps)), F32), jnp.asarray(np.tile(sin, (1, reps)), F32))


def _layer(x, mods, wts, latent, s0f, s0b, cache_k, cache_v):
    bsz, seq, _ = x.shape
    n = bsz * seq
    x2d = x.reshape(n, D_MODEL)
    sh1, sc1, gt1, sh2, sc2, gt2 = mods
    p, tail = _in_proj(x2d, seq, sc1, sh1, wts["g1"], wts["w_in_r"])
    o_gla, st_f, st_b = _gla(p, tail, bsz, seq, wts["wdf"], wts["bdf"], wts["wdb"], wts["bdb"], wts["g_gla"],
                             s0f, s0b)
    if latent:
        cos_t, sin_t = _rope_tables(seq)
        o_swa = _swa_latent(p, bsz, seq, cache_k, cache_v, wts["sink"], cos_t, sin_t)
    else:
        o_swa = _swa_context(p, bsz, seq, wts["sink"])
    merged = _branch_merge(o_gla, o_swa, p, wts["wa"], wts["wb"])
    x1, h2, logits3 = _out_proj(merged, x2d, seq, gt1, sc2, sh2, wts["g2"], wts["wo"], wts["wr2"])
    cap = CAPACITY_FACTOR * n // N_EXPERTS
    aff3, sel3, pos3 = _select(logits3, cap)
    base = jnp.concatenate([pos3[:, :, 0].astype(jnp.int32), jnp.full((1, N_EXPERTS), cap, jnp.int32)], axis=0)
    lists = _compact(base.reshape(-1), aff3, sel3, pos3, cap)
    idx_flat = lists[:cap, :N_EXPERTS].T.astype(jnp.int32).reshape(-1)
    gate_col = lists[:cap, N_EXPERTS:2 * N_EXPERTS].T.reshape(-1, 1)
    hid = _expert_up(idx_flat, h2, wts["wg"], wts["wu"], cap)
    ye = _expert_down(hid, gate_col, wts["wd"], cap)
    tt = 256
    lo_tab = base[::tt // LANES]
    y = _combine(idx_flat, lo_tab.reshape(-1), ye, x1, seq, gt2, wts["g_final"], cap, tt)
    return y.reshape(bsz, seq, D_MODEL), tail, st_f, st_b


def kernel(x_prompt, x_sample, state_gla_fwd, state_gla_bwd, cache_k, cache_v, c, c_ctx, w_mod, b_mod,
           g_norm1, w_in, w_dec_f, b_dec_f, w_dec_b, b_dec_b, g_gla, attn_sink, w_branch_a, w_branch_b,
           w_out, g_norm2, w_router, w_exp_gate, w_exp_up, w_exp_down, g_final):
    bp, sp, _ = x_prompt.shape
    bl = x_sample.shape[0]
    l = 0
    gla_end = 2 * GLA_KW + 2 * GLA_VW
    dec_end = gla_end + 2 * GLA_RANK
    swa_end = dec_end + SWA_QW + 2 * SWA_KVW
    w = w_in[l]
    w_in_r = jnp.concatenate(
        [w[:, swa_end:].astype(BF16), w[:, :gla_end].astype(BF16), w[:, dec_end:swa_end].astype(BF16),
         w[:, gla_end:dec_end].astype(BF16), jnp.zeros((D_MODEL, IN_PAD - w.shape[1]), BF16)], axis=1)
    zpad = jnp.zeros((LANES - 2 * GLA_RANK, GLA_KW), F32)
    wr_hi = w_router[l].astype(BF16)
    wr_lo = (w_router[l] - wr_hi.astype(F32)).astype(BF16)
    wr_pad = jnp.zeros((D_MODEL, LANES - 2 * N_EXPERTS), BF16)
    zr = jnp.zeros((GLA_RANK, GLA_KW), F32)
    wts = {
        "g1": g_norm1[l].reshape(1, D_MODEL),
        "w_in_r": w_in_r,
        "wdf": jnp.concatenate([w_dec_f[l], zr, zpad], axis=0),
        "wdb": jnp.concatenate([zr, w_dec_b[l], zpad], axis=0),
        "bdf": b_dec_f[l].reshape(1, GLA_KW),
        "bdb": b_dec_b[l].reshape(1, GLA_KW),
        "g_gla": g_gla[l].reshape(1, GLA_VW),
        "sink": attn_sink[l],
        "wa": w_branch_a[l].astype(BF16),
        "wb": w_branch_b[l].astype(BF16),
        "wo": w_out[l].astype(BF16),
        "g2": g_norm2[l].reshape(1, D_MODEL),
        "wr2": jnp.concatenate([wr_hi, wr_lo, wr_pad], axis=1),
        "wg": w_exp_gate[l],
        "wu": w_exp_up[l],
        "wd": w_exp_down[l],
        "g_final": g_final.reshape(1, D_MODEL),
    }
    cond8 = jnp.concatenate([c_ctx[None, :], c, jnp.zeros((8 - 1 - bl, D_MODEL), F32)], axis=0)
    mod = _modulation(cond8, w_mod[l], b_mod[l]).reshape(8, N_MOD, 1, D_MODEL)
    mods_ctx = tuple(mod[0:1, j] for j in range(N_MOD))
    mods_lat = tuple(mod[1:1 + bl, j] for j in range(N_MOD))

    zero_state = jnp.zeros((1, GLA_HEADS, GLA_DK, GLA_DV), F32)
    y_prompt, tail_ctx, st_f, st_b = _layer(x_prompt, mods_ctx, wts, False, zero_state, zero_state, None, None)
    ck = cache_k[:, l].reshape(bl, -1, SWA_KVW)
    cv = cache_v[:, l].reshape(bl, -1, SWA_KVW)
    y_sample, _, _, _ = _layer(x_sample, mods_lat, wts, True, state_gla_fwd[:, l], state_gla_bwd[:, l], ck, cv)

    ksl = slice(COL_SK - TAIL_COL0, COL_SK - TAIL_COL0 + SWA_KVW)
    vsl = slice(COL_SV - TAIL_COL0, COL_SV - TAIL_COL0 + SWA_KVW)
    new_k = tail_ctx[:, ksl].reshape(bp, 1, sp, SWA_KV_HEADS, SWA_HEAD_DIM)
    new_v = tail_ctx[:, vsl].reshape(bp, 1, sp, SWA_KV_HEADS, SWA_HEAD_DIM)
    return (y_prompt, y_sample, st_f[:, None], st_b[:, None], new_k, new_v)
```

```python
import functools

import jax
import jax.numpy as jnp
import numpy as np
from jax import lax
from jax.experimental import pallas as pl
from jax.experimental.pallas import tpu as pltpu

F32 = jnp.float32
BF16 = jnp.bfloat16
HIGHEST = lax.Precision.HIGHEST

D_MODEL = 2048
N_MOD = 6
EPS = 1e-6

GLA_HEADS = 4
GLA_DK = 128
GLA_DV = 256
GLA_KW = GLA_HEADS * GLA_DK
GLA_VW = GLA_HEADS * GLA_DV
GLA_RANK = 16
GLA_TAU = 16.0
GLA_CHUNK = 64

SWA_HEADS = 16
SWA_KV_HEADS = 4
SWA_GROUP = 4
SWA_HEAD_DIM = 64
SWA_QW = SWA_HEADS * SWA_HEAD_DIM
SWA_KVW = SWA_KV_HEADS * SWA_HEAD_DIM
SWA_BLOCK = 128
GRID_W = 64
ROPE_BASE = 10000.0

N_EXPERTS = 16
EXPERT_FF = D_MODEL // 2
CAPACITY_FACTOR = 2

LANES = 128
ROW_CHUNKS = D_MODEL // LANES
VMEM_LIMIT = 56 * 1024 * 1024

COL_GA = 0
COL_GB = 2048
COL_GQ = 4096
COL_GK = 4608
COL_GV = 5120
COL_GR = 6144
COL_SQ = 7168
COL_SK = 8192
COL_SV = 8448
COL_DEC = 8704
IN_PAD = 8960
IN_TN = 1280
TAIL_COL0 = COL_SK
TAIL_W = COL_DEC + LANES - COL_SK


def _cparams(sem, **kw):
    return pltpu.CompilerParams(dimension_semantics=sem, vmem_limit_bytes=VMEM_LIMIT, **kw)


def _silu(x):
    return x * (1.0 / (1.0 + jnp.exp(-x)))


def _sigmoid(x):
    return 1.0 / (1.0 + jnp.exp(-x))


SLAB = 128


def _store_row_major(ref, x):
    for tb in range(x.shape[0] // SLAB):
        rows = slice(tb * SLAB, (tb + 1) * SLAB)
        parts = jnp.stack([x[rows, s * LANES:(s + 1) * LANES] for s in range(ROW_CHUNKS)], axis=0)
        ref[rows] = pltpu.einshape("stl->tsl", parts)


def _load_row_major(ref, rows):
    xt = pltpu.einshape("tsl->stl", ref[rows])
    return [xt[s] for s in range(ROW_CHUNKS)]


def _tiles_per_mod_row(mod, n, seq, tile):
    return seq // tile if mod.shape[0] > 1 else n // tile


def _mod_kernel(c_ref, w_ref, b_ref, o_ref):
    a = _silu(c_ref[...]).astype(BF16)
    o_ref[...] = jnp.dot(a, w_ref[...].astype(BF16), preferred_element_type=F32) + b_ref[...]


def _modulation(cond8, w_mod, b_mod):
    n_out = w_mod.shape[1]
    tn = 1536
    return pl.pallas_call(
        _mod_kernel,
        grid=(n_out // tn,),
        in_specs=[pl.BlockSpec((8, D_MODEL), lambda j: (0, 0)),
                  pl.BlockSpec((D_MODEL, tn), lambda j: (0, j)),
                  pl.BlockSpec((1, tn), lambda j: (0, j))],
        out_specs=pl.BlockSpec((8, tn), lambda j: (0, j)),
        out_shape=jax.ShapeDtypeStruct((8, n_out), F32),
        compiler_params=_cparams(("arbitrary",)),
        name="modulation",
    )(cond8, w_mod, b_mod.reshape(1, n_out))


SRC_GLA_END = 2 * GLA_KW + 2 * GLA_VW
SRC_DEC_END = SRC_GLA_END + 2 * GLA_RANK
SRC_SWA_END = SRC_DEC_END + SWA_QW + 2 * SWA_KVW
SRC_END = SRC_SWA_END + 2 * D_MODEL
REORDER_TN = 256
LANE_SHIFT = SRC_DEC_END % LANES


def _reorder_kernel(a_ref, b_ref, c_ref, t_ref, o_ref):
    j = pl.program_id(0)
    lane = lax.broadcasted_iota(jnp.int32, (D_MODEL, LANES), 1)
    n_gate = (COL_GQ - COL_GA) // REORDER_TN
    n_gla = (COL_SQ - COL_GQ) // REORDER_TN
    n_swa = (COL_DEC - COL_SQ) // REORDER_TN

    def shifted(x, y):
        return jnp.where(lane < LANES - LANE_SHIFT, pltpu.roll(x, LANES - LANE_SHIFT, 1),
                         pltpu.roll(y, LANES - LANE_SHIFT, 1))

    @pl.when((j < n_gate) | ((j >= n_gate + n_gla) & (j < n_gate + n_gla + n_swa)))
    def _():
        c = jnp.where(j == n_gate - 1, t_ref[...], c_ref[...])
        o_ref[:, :LANES] = shifted(a_ref[...], b_ref[...]).astype(BF16)
        o_ref[:, LANES:] = shifted(b_ref[...], c).astype(BF16)

    @pl.when((j >= n_gate) & (j < n_gate + n_gla))
    def _():
        o_ref[:, :LANES] = a_ref[...].astype(BF16)
        o_ref[:, LANES:] = b_ref[...].astype(BF16)

    @pl.when(j == n_gate + n_gla + n_swa)
    def _():
        o_ref[:, :LANES] = jnp.where(lane < LANE_SHIFT, a_ref[...], 0.0).astype(BF16)
        o_ref[:, LANES:] = jnp.zeros((D_MODEL, LANES), BF16)


def _reorder_w_in(w):
    assert SRC_END == w.shape[1] and SRC_SWA_END % LANES == LANE_SHIFT and IN_PAD - COL_DEC == REORDER_TN
    n_gate = (COL_GQ - COL_GA) // REORDER_TN
    n_gla = (COL_SQ - COL_GQ) // REORDER_TN
    n_swa = (COL_DEC - COL_SQ) // REORDER_TN
    per = REORDER_TN // LANES
    last_full = SRC_END // LANES - 1

    def first_block(j):
        gate = per * j + SRC_SWA_END // LANES
        gla = per * (j - n_gate)
        swa = per * (j - n_gate - n_gla) + SRC_DEC_END // LANES
        dec = SRC_GLA_END // LANES
        return jnp.where(j < n_gate, gate, jnp.where(j < n_gate + n_gla, gla,
                                                    jnp.where(j < n_gate + n_gla + n_swa, swa, dec)))

    def spec(k):
        return pl.BlockSpec((D_MODEL, LANES), lambda j: (0, jnp.minimum(first_block(j) + k, last_full)))

    tail = jnp.pad(w[:, last_full * LANES + LANES:], ((0, 0), (0, LANES - (SRC_END % LANES))))
    return pl.pallas_call(
        _reorder_kernel,
        grid=(IN_PAD // REORDER_TN,),
        in_specs=[spec(0), spec(1), spec(2), pl.BlockSpec((D_MODEL, LANES), lambda j: (0, 0))],
        out_specs=pl.BlockSpec((D_MODEL, REORDER_TN), lambda j: (0, j)),
        out_shape=jax.ShapeDtypeStruct((D_MODEL, IN_PAD), BF16),
        compiler_params=_cparams(("parallel",)),
        name="reorder_w_in",
    )(w, w, w, tail)


NORM_ROWS = 64


def _in_proj_kernel(x_ref, sc_ref, sh_ref, g_ref, w_ref, o_ref, tail_ref, h_scr):
    j = pl.program_id(1)

    @pl.when(j == 0)
    def _():
        def slab(r, _):
            rows = pl.ds(pl.multiple_of(r * NORM_ROWS, NORM_ROWS), NORM_ROWS)
            x = x_ref[rows, :]
            y = x * lax.rsqrt(jnp.mean(x * x, axis=-1, keepdims=True) + EPS) * g_ref[...]
            h_scr[rows, :] = (y * (1.0 + sc_ref[0]) + sh_ref[0]).astype(BF16)
            return 0
        lax.fori_loop(0, x_ref.shape[0] // NORM_ROWS, slab, 0)

    acc = jnp.dot(h_scr[...], w_ref[...], preferred_element_type=F32)
    o_ref[...] = acc.astype(BF16)

    @pl.when(j == IN_PAD // IN_TN - 1)
    def _():
        first = TAIL_COL0 - (IN_PAD - IN_TN)
        tail_ref[...] = acc[:, first:first + TAIL_W]


def _in_proj(x2d, seq, sc, sh, g, w_r):
    n = x2d.shape[0]
    tm = min(1024, seq if sc.shape[0] > 1 else n)
    per_b = _tiles_per_mod_row(sc, n, seq, tm)
    return pl.pallas_call(
        _in_proj_kernel,
        grid=(n // tm, IN_PAD // IN_TN),
        in_specs=[pl.BlockSpec((tm, D_MODEL), lambda i, j: (i, 0)),
                  pl.BlockSpec((1, 1, D_MODEL), lambda i, j: (i // per_b, 0, 0)),
                  pl.BlockSpec((1, 1, D_MODEL), lambda i, j: (i // per_b, 0, 0)),
                  pl.BlockSpec((1, D_MODEL), lambda i, j: (0, 0)),
                  pl.BlockSpec((D_MODEL, IN_TN), lambda i, j: (0, j))],
        out_specs=[pl.BlockSpec((tm, IN_TN), lambda i, j: (i, j)),
                   pl.BlockSpec((tm, TAIL_W), lambda i, j: (i, 0))],
        out_shape=[jax.ShapeDtypeStruct((n, IN_PAD), BF16),
                   jax.ShapeDtypeStruct((n, TAIL_W), F32)],
        scratch_shapes=[pltpu.VMEM((tm, D_MODEL), BF16)],
        compiler_params=_cparams(("parallel", "arbitrary")),
        name="in_proj",
    )(x2d, sc, sh, g, w_r)


GLA_BLK = 256
GLA_HPS = 4


def _tn_dot(a, b, precision=None):
    return lax.dot_general(a, b, (((0,), (0,)), ((), ())), precision=precision,
                           preferred_element_type=F32)


def _nt_dot(a, b, precision=None):
    return lax.dot_general(a, b, (((1,), (1,)), ((), ())), precision=precision,
                           preferred_element_type=F32)


def _split_bf16(x, parts):
    out = []
    for _ in range(parts):
        piece = x.astype(BF16)
        out.append(piece)
        x = x - piece.astype(F32)
    return out


def _dot3(x, w):
    xh, xl = _split_bf16(x, 2)
    wh, wl = _split_bf16(w, 2)
    d = functools.partial(jnp.dot, preferred_element_type=F32)
    return d(xh, wh) + d(xl, wh) + d(xh, wl)


def _gla_kernel(q_ref, k_ref, v_ref, r_ref, dec_ref, wdf_ref, bdf_ref, wdb_ref, bdb_ref, gg_ref,
                s0f_ref, s0b_ref, o_ref, stf_ref, stb_ref, state_t, of_scr, *, nblk):
    i = pl.program_id(2)
    c = GLA_CHUNK
    nch = GLA_BLK // c
    ri = lax.broadcasted_iota(jnp.int32, (GLA_BLK, GLA_BLK), 0)
    ci = lax.broadcasted_iota(jnp.int32, (GLA_BLK, GLA_BLK), 1)
    same_chunk = (ri // c) == (ci // c)

    @pl.when(i == 0)
    def _():
        for h in range(GLA_HPS):
            state_t[h] = s0f_ref[0, h].T

    @pl.when(i == nblk)
    def _():
        for h in range(GLA_HPS):
            state_t[h] = s0b_ref[0, h].T

    def run(fwd):
        blk = i if fwd else 2 * nblk - 1 - i
        keep = same_chunk & ((ci <= ri) if fwd else (ci >= ri))
        w_ref, b_ref = (wdf_ref, bdf_ref) if fwd else (wdb_ref, bdb_ref)
        logit = _dot3(dec_ref[...], w_ref[...]) + b_ref[...]
        g = (jnp.minimum(logit, 0.0) - jnp.log(1.0 + jnp.exp(-jnp.abs(logit)))) / GLA_TAU
        tri = keep.astype(BF16)
        b = sum(jnp.dot(tri, piece, preferred_element_type=F32) for piece in _split_bf16(g, 3))
        edge = c - 1 if fwd else 0
        tots = [b[ch * c + edge:ch * c + edge + 1, :] for ch in range(nch)]
        totb = jnp.concatenate([jnp.broadcast_to(t, (c, t.shape[1])) for t in tots], axis=0)
        q_in = (q_ref[...].astype(F32) * (GLA_DK ** -0.5) * jnp.exp(b)).astype(BF16)
        k = k_ref[...].astype(F32)
        k_in = (k * jnp.exp(-b)).astype(BF16)
        k_st = (k * jnp.exp(totb - b)).astype(BF16)
        srow = pl.multiple_of(blk * GLA_BLK, GLA_BLK)
        for h in range(GLA_HPS):
            ks = slice(h * GLA_DK, (h + 1) * GLA_DK)
            vs = slice(h * GLA_DV, (h + 1) * GLA_DV)
            v = v_ref[:, vs].astype(BF16)
            a = jnp.where(keep, _nt_dot(q_in[:, ks], k_in[:, ks]), 0.0).astype(BF16)
            o_intra = jnp.dot(a, v, preferred_element_type=F32)
            st = state_t[h]
            o_inter = [None] * nch
            for cc in range(nch):
                ch = cc if fwd else nch - 1 - cc
                rows = slice(ch * c, (ch + 1) * c)
                o_inter[ch] = _nt_dot(q_in[rows, ks], st.astype(BF16))
                st = st * jnp.exp(tots[ch][:, ks]) + _tn_dot(v[rows, :], k_st[rows, ks])
            state_t[h] = st
            o = o_intra + jnp.concatenate(o_inter, axis=0)
            if fwd:
                of_scr[pl.ds(srow, GLA_BLK), vs] = o
            else:
                o = o + of_scr[pl.ds(srow, GLA_BLK), vs]
                o = o * lax.rsqrt(jnp.mean(o * o, axis=-1, keepdims=True) + EPS) * gg_ref[:, vs]
                o_ref[:, vs] = (o * _silu(r_ref[:, vs].astype(F32))).astype(BF16)

    @pl.when(i < nblk)
    def _():
        run(True)

    @pl.when(i >= nblk)
    def _():
        run(False)

    @pl.when(i == nblk - 1)
    def _():
        for h in range(GLA_HPS):
            stf_ref[0, h] = state_t[h].T

    @pl.when(i == 2 * nblk - 1)
    def _():
        for h in range(GLA_HPS):
            stb_ref[0, h] = state_t[h].T


def _gla(p, dec, bsz, seq, wdf, bdf, wdb, bdb, g_gla, s0f, s0b):
    n = p.shape[0]
    nblk = seq // GLA_BLK
    kw = GLA_HPS * GLA_DK
    vw = GLA_HPS * GLA_DV

    def rb(b, i):
        return b * nblk + jnp.where(i < nblk, i, 2 * nblk - 1 - i)

    def orb(b, i):
        return b * nblk + jnp.where(i < nblk, nblk - 1, 2 * nblk - 1 - i)

    st_spec = pl.BlockSpec((1, GLA_HPS, GLA_DK, GLA_DV), lambda b, h, i: (b, h, 0, 0))
    s0_spec = pl.BlockSpec((1, GLA_HPS, GLA_DK, GLA_DV),
                           lambda b, h, i: (b if s0f.shape[0] > 1 else 0, h, 0, 0))
    st_shape = jax.ShapeDtypeStruct((bsz, GLA_HEADS, GLA_DK, GLA_DV), F32)
    return pl.pallas_call(
        functools.partial(_gla_kernel, nblk=nblk),
        grid=(bsz, GLA_HEADS // GLA_HPS, 2 * nblk),
        in_specs=[pl.BlockSpec((GLA_BLK, kw), lambda b, h, i: (rb(b, i), COL_GQ // kw + h)),
                  pl.BlockSpec((GLA_BLK, kw), lambda b, h, i: (rb(b, i), COL_GK // kw + h)),
                  pl.BlockSpec((GLA_BLK, vw), lambda b, h, i: (rb(b, i), COL_GV // vw + h)),
                  pl.BlockSpec((GLA_BLK, vw), lambda b, h, i: (rb(b, i), COL_GR // vw + h)),
                  pl.BlockSpec((GLA_BLK, LANES), lambda b, h, i: (rb(b, i), (COL_DEC - TAIL_COL0) // LANES)),
                  pl.BlockSpec((LANES, kw), lambda b, h, i: (0, h)),
                  pl.BlockSpec((1, kw), lambda b, h, i: (0, h)),
                  pl.BlockSpec((LANES, kw), lambda b, h, i: (0, h)),
                  pl.BlockSpec((1, kw), lambda b, h, i: (0, h)),
                  pl.BlockSpec((1, vw), lambda b, h, i: (0, h)),
                  s0_spec, s0_spec],
        out_specs=[pl.BlockSpec((GLA_BLK, vw), lambda b, h, i: (orb(b, i), h)), st_spec, st_spec],
        out_shape=[jax.ShapeDtypeStruct((n, GLA_VW), BF16), st_shape, st_shape],
        scratch_shapes=[pltpu.VMEM((GLA_HPS, GLA_DV, GLA_DK), F32), pltpu.VMEM((seq, vw), F32)],
        compiler_params=_cparams(("parallel", "parallel", "arbitrary")),
        name="gla",
    )(p, p, p, p, dec, wdf, bdf, wdb, bdb, g_gla, s0f, s0b)


def _sink_column(sink_ref, kvh, rows_per_head):
    r = lax.broadcasted_iota(jnp.int32, (SWA_GROUP * rows_per_head, 1), 0)
    col = jnp.zeros((SWA_GROUP * rows_per_head, 1), F32)
    for g in range(SWA_GROUP):
        col = jnp.where(r // rows_per_head == g, sink_ref[kvh * SWA_GROUP + g], col)
    return col


def _rope(x, cos, sin):
    outs = []
    for cb in range(x.shape[1] // LANES):
        sl = slice(cb * LANES, (cb + 1) * LANES)
        xc = x[:, sl]
        lane = lax.broadcasted_iota(jnp.int32, xc.shape, 1)
        sw = jnp.where(lane % 32 < 16, pltpu.roll(xc, LANES - 16, 1), pltpu.roll(xc, 16, 1))
        outs.append(xc * cos[:, sl] + sw * sin[:, sl])
    return jnp.concatenate(outs, axis=1)


def _swa_lat_kernel(sink_ref, q_ref, kp_ref, kc_ref, kn_ref, vp_ref, vc_ref, vn_ref, ck_ref, cv_ref,
                    cp_ref, cc_ref, cn_ref, sp_ref, sc_ref, sn_ref, o_ref, *, nb):
    n = pl.program_id(1)
    blk = SWA_BLOCK
    hd = SWA_HEAD_DIM
    cos_c, sin_c = cc_ref[...], sc_ref[...]
    kband = jnp.concatenate([_rope(kp_ref[...].astype(F32), cp_ref[...], sp_ref[...]),
                             _rope(kc_ref[...].astype(F32), cos_c, sin_c),
                             _rope(kn_ref[...].astype(F32), cn_ref[...], sn_ref[...])], axis=0).astype(BF16)
    vband = jnp.concatenate([vp_ref[...], vc_ref[...], vn_ref[...]], axis=0).astype(BF16)
    ck = ck_ref[0].astype(BF16)
    cv = cv_ref[0].astype(BF16)
    qi = lax.broadcasted_iota(jnp.int32, (blk, 3 * blk), 0)
    kj = lax.broadcasted_iota(jnp.int32, (blk, 3 * blk), 1)
    k_abs = kj + (n - 1) * blk
    mask = (jnp.abs(kj - blk - qi) <= SWA_BLOCK) & (k_abs >= 0) & (k_abs < nb * blk)
    bias = jnp.concatenate([jnp.where(mask, 0.0, -1e30)] * SWA_GROUP, axis=0)
    nctx = ck.shape[0]
    kall = jnp.concatenate([ck, kband], axis=0)
    vall = jnp.concatenate([cv, vband], axis=0)
    for kvh in range(SWA_KV_HEADS):
        ks = slice(kvh * hd, (kvh + 1) * hd)
        qr = _rope(q_ref[:, kvh * SWA_KVW:(kvh + 1) * SWA_KVW].astype(F32), cos_c, sin_c) * (hd ** -0.5)
        qg = jnp.concatenate([qr[:, g * hd:(g + 1) * hd] for g in range(SWA_GROUP)],
                             axis=0).astype(BF16)
        s = _nt_dot(qg, kall[:, ks])
        s = jnp.concatenate([s[:, :nctx], s[:, nctx:] + bias], axis=1)
        sink = _sink_column(sink_ref, kvh, blk)
        m = jnp.maximum(jnp.max(s, axis=-1, keepdims=True), sink)
        pr = jnp.exp(s - m)
        den = jnp.exp(sink - m) + jnp.sum(pr, axis=-1, keepdims=True)
        o = jnp.dot(pr.astype(BF16), vall[:, ks], preferred_element_type=F32) / den
        for g in range(SWA_GROUP):
            cb = (kvh * SWA_GROUP + g) * hd
            o_ref[:, cb:cb + hd] = o[g * blk:(g + 1) * blk, :].astype(BF16)


def _swa_latent(p, bsz, seq, cache_k, cache_v, sink, cos_t, sin_t):
    n = p.shape[0]
    nb = seq // SWA_BLOCK
    kcol = COL_SK // SWA_KVW
    vcol = COL_SV // SWA_KVW
    prev = lambda i: jnp.maximum(i - 1, 0)
    nxt = lambda i: jnp.minimum(i + 1, nb - 1)

    def pspec(col, f):
        return pl.BlockSpec((SWA_BLOCK, SWA_KVW), lambda b, i, s: (b * nb + f(i), col))

    def tspec(f):
        return pl.BlockSpec((SWA_BLOCK, SWA_KVW), lambda b, i, s: (f(i), 0))

    same = lambda i: i
    cspec = pl.BlockSpec((1, cache_k.shape[1], SWA_KVW), lambda b, i, s: (b, 0, 0))
    grid_spec = pltpu.PrefetchScalarGridSpec(
        num_scalar_prefetch=1,
        grid=(bsz, nb),
        in_specs=[pl.BlockSpec((SWA_BLOCK, SWA_QW), lambda b, i, s: (b * nb + i, COL_SQ // SWA_QW)),
                  pspec(kcol, prev), pspec(kcol, same), pspec(kcol, nxt),
                  pspec(vcol, prev), pspec(vcol, same), pspec(vcol, nxt),
                  cspec, cspec,
                  tspec(prev), tspec(same), tspec(nxt), tspec(prev), tspec(same), tspec(nxt)],
        out_specs=pl.BlockSpec((SWA_BLOCK, SWA_QW), lambda b, i, s: (b * nb + i, 0)),
    )
    return pl.pallas_call(
        functools.partial(_swa_lat_kernel, nb=nb),
        grid_spec=grid_spec,
        out_shape=jax.ShapeDtypeStruct((n, SWA_QW), BF16),
        compiler_params=_cparams(("parallel", "arbitrary")),
        name="swa_latent",
    )(sink, p, p, p, p, p, p, p, cache_k, cache_v, cos_t, cos_t, cos_t, sin_t, sin_t, sin_t)


def _swa_ctx_kernel(sink_ref, q_ref, k_ref, v_ref, o_ref):
    s = q_ref.shape[0]
    hd = SWA_HEAD_DIM
    kb = k_ref[...].astype(BF16)
    vb = v_ref[...].astype(BF16)
    for kvh in range(SWA_KV_HEADS):
        ks = slice(kvh * hd, (kvh + 1) * hd)
        qg = jnp.concatenate([q_ref[:, (kvh * SWA_GROUP + g) * hd:(kvh * SWA_GROUP + g + 1) * hd]
                              for g in range(SWA_GROUP)], axis=0)
        qg = (qg.astype(F32) * (hd ** -0.5)).astype(BF16)
        sc = _nt_dot(qg, kb[:, ks])
        sink = _sink_column(sink_ref, kvh, s)
        m = jnp.maximum(jnp.max(sc, axis=-1, keepdims=True), sink)
        pr = jnp.exp(sc - m)
        den = jnp.exp(sink - m) + jnp.sum(pr, axis=-1, keepdims=True)
        o = jnp.dot(pr.astype(BF16), vb[:, ks], preferred_element_type=F32) / den
        for g in range(SWA_GROUP):
            cb = (kvh * SWA_GROUP + g) * hd
            o_ref[:, cb:cb + hd] = o[g * s:(g + 1) * s, :].astype(BF16)


def _swa_context(p, bsz, seq, sink):
    n = p.shape[0]
    grid_spec = pltpu.PrefetchScalarGridSpec(
        num_scalar_prefetch=1,
        grid=(bsz,),
        in_specs=[pl.BlockSpec((seq, SWA_QW), lambda b, s: (b, COL_SQ // SWA_QW)),
                  pl.BlockSpec((seq, SWA_KVW), lambda b, s: (b, COL_SK // SWA_KVW)),
                  pl.BlockSpec((seq, SWA_KVW), lambda b, s: (b, COL_SV // SWA_KVW))],
        out_specs=pl.BlockSpec((seq, SWA_QW), lambda b, s: (b, 0)),
    )
    return pl.pallas_call(
        _swa_ctx_kernel,
        grid_spec=grid_spec,
        out_shape=jax.ShapeDtypeStruct((n, SWA_QW), BF16),
        compiler_params=_cparams(("parallel",)),
        name="swa_context",
    )(sink, p, p, p)


MERGE_TM = 512


BRANCH_TM = 1024
BRANCH_COLS = 512


def _branch_kernel(og_ref, os_ref, ga_ref, gb_ref, wa_ref, wb_ref, m_ref):
    for c in range(D_MODEL // BRANCH_COLS):
        cols = slice(c * BRANCH_COLS, (c + 1) * BRANCH_COLS)
        a = jnp.dot(og_ref[...], wa_ref[:, cols], preferred_element_type=F32)
        b = jnp.dot(os_ref[...], wb_ref[:, cols], preferred_element_type=F32)
        m_ref[:, cols] = (_sigmoid(ga_ref[:, cols].astype(F32)) * a
                          + _sigmoid(gb_ref[:, cols].astype(F32)) * b).astype(BF16)


def _branch_merge(o_gla, o_swa, p, wa, wb):
    n = o_gla.shape[0]
    tm = BRANCH_TM
    once = pl.Buffered(1)
    return pl.pallas_call(
        _branch_kernel,
        grid=(n // tm,),
        in_specs=[pl.BlockSpec((tm, GLA_VW), lambda i: (i, 0)),
                  pl.BlockSpec((tm, SWA_QW), lambda i: (i, 0)),
                  pl.BlockSpec((tm, D_MODEL), lambda i: (i, COL_GA // D_MODEL)),
                  pl.BlockSpec((tm, D_MODEL), lambda i: (i, COL_GB // D_MODEL)),
                  pl.BlockSpec((GLA_VW, D_MODEL), lambda i: (0, 0), pipeline_mode=once),
                  pl.BlockSpec((SWA_QW, D_MODEL), lambda i: (0, 0), pipeline_mode=once)],
        out_specs=pl.BlockSpec((tm, D_MODEL), lambda i: (i, 0)),
        out_shape=jax.ShapeDtypeStruct((n, D_MODEL), BF16),
        compiler_params=_cparams(("parallel",)),
        name="branch_merge",
    )(o_gla, o_swa, p, p, wa, wb)


def _out_kernel(m_ref, x_ref, gt1_ref, sc2_ref, sh2_ref, g2_ref, wo_ref, wr2_ref,
                x1_ref, h2_ref, lg_ref):
    x1 = x_ref[...] + gt1_ref[0] * jnp.dot(m_ref[...], wo_ref[...], preferred_element_type=F32)
    x1_ref[...] = x1
    y = x1 * lax.rsqrt(jnp.mean(x1 * x1, axis=-1, keepdims=True) + EPS) * g2_ref[...]
    h2 = y * (1.0 + sc2_ref[0]) + sh2_ref[0]
    _store_row_major(h2_ref, h2)
    l2 = jnp.dot(h2.astype(BF16), wr2_ref[...], preferred_element_type=F32)
    lt = l2.T
    lt = lt[0:N_EXPERTS, :] + lt[N_EXPERTS:2 * N_EXPERTS, :]
    for cb in range(lt.shape[1] // LANES):
        lg_ref[cb] = lt[:, cb * LANES:(cb + 1) * LANES]


def _out_proj(merged, x2d, seq, gt1, sc2, sh2, g2, wo, wr2):
    n = x2d.shape[0]
    tm = MERGE_TM
    per_b = _tiles_per_mod_row(gt1, n, seq, tm)
    mod_spec = pl.BlockSpec((1, 1, D_MODEL), lambda i: (i // per_b, 0, 0))
    once = pl.Buffered(1)
    return pl.pallas_call(
        _out_kernel,
        grid=(n // tm,),
        in_specs=[pl.BlockSpec((tm, D_MODEL), lambda i: (i, 0)),
                  pl.BlockSpec((tm, D_MODEL), lambda i: (i, 0)),
                  mod_spec, mod_spec, mod_spec,
                  pl.BlockSpec((1, D_MODEL), lambda i: (0, 0)),
                  pl.BlockSpec((D_MODEL, D_MODEL), lambda i: (0, 0), pipeline_mode=once),
                  pl.BlockSpec((D_MODEL, LANES), lambda i: (0, 0))],
        out_specs=[pl.BlockSpec((tm, D_MODEL), lambda i: (i, 0)),
                   pl.BlockSpec((tm, ROW_CHUNKS, LANES), lambda i: (i, 0, 0)),
                   pl.BlockSpec((tm // LANES, N_EXPERTS, LANES), lambda i: (i, 0, 0))],
        out_shape=[jax.ShapeDtypeStruct((n, D_MODEL), F32),
                   jax.ShapeDtypeStruct((n, ROW_CHUNKS, LANES), F32),
                   jax.ShapeDtypeStruct((n // LANES, N_EXPERTS, LANES), F32)],
        compiler_params=_cparams(("parallel",)),
        name="out_proj",
    )(merged, x2d, gt1, sc2, sh2, g2, wo, wr2)


def _select_kernel(lg_ref, aff_ref, sel_ref, pos_ref, *, cap):
    nblk = lg_ref.shape[0]
    lg = lg_ref[...]
    ex = jnp.exp(lg - jnp.max(lg, axis=1, keepdims=True))
    aff = ex / jnp.sum(ex, axis=1, keepdims=True)
    aff_ref[...] = aff
    bits = lax.bitcast_convert_type(aff, jnp.int32)

    def count(pred):
        c = jnp.sum(jnp.where(pred, 1.0, 0.0), axis=0, keepdims=True)
        return jnp.sum(c, axis=2, keepdims=True)

    def bit_step(t, cur):
        cand = cur | jnp.left_shift(jnp.int32(1), 30 - t)
        return jnp.where(count(bits >= cand) >= cap, cand, cur)

    thr = lax.fori_loop(0, 31, bit_step, jnp.zeros((1, N_EXPERTS, 1), jnp.int32))
    need = (cap - count(bits > thr))[0]
    thr2 = thr[0]
    ri = lax.broadcasted_iota(jnp.int32, (LANES, LANES), 0)
    ci = lax.broadcasted_iota(jnp.int32, (LANES, LANES), 1)
    upper = (ri <= ci).astype(BF16)

    def blk_step(b, carry):
        run_eq, run_sel = carry
        bb = lax.bitcast_convert_type(aff_ref[b], jnp.int32)
        eq = (bb == thr2).astype(F32)
        eq_excl = jnp.dot(eq.astype(BF16), upper, preferred_element_type=F32) - eq + run_eq
        sel = jnp.where((bb > thr2) | ((eq > 0) & (eq_excl < need)), 1.0, 0.0)
        sel_ref[b] = sel
        pos_ref[b] = jnp.dot(sel.astype(BF16), upper, preferred_element_type=F32) - sel + run_sel
        return (run_eq + jnp.sum(eq, axis=1, keepdims=True),
                run_sel + jnp.sum(sel, axis=1, keepdims=True))

    zero = jnp.zeros((N_EXPERTS, 1), F32)
    lax.fori_loop(0, nblk, blk_step, (zero, zero))


def _select(logits3, cap):
    shp = jax.ShapeDtypeStruct(logits3.shape, F32)
    return pl.pallas_call(
        functools.partial(_select_kernel, cap=float(cap)),
        out_shape=[shp, shp, shp],
        compiler_params=pltpu.CompilerParams(vmem_limit_bytes=VMEM_LIMIT),
        name="select",
    )(logits3)


COMPACT_WIN = 136


def _compact_kernel(base_ref, aff_ref, sel_ref, pos_ref, list_ref):
    nblk = aff_ref.shape[0]
    list_ref[...] = jnp.zeros_like(list_ref)
    slot = lax.broadcasted_iota(jnp.int32, (COMPACT_WIN, LANES), 0).astype(F32)
    lane = lax.broadcasted_iota(jnp.int32, (COMPACT_WIN, LANES), 1)

    def blk_step(b, _):
        aff = aff_ref[b]
        pos = jnp.where(sel_ref[b] > 0, pos_ref[b], -1.0)
        tok = (lane + b * LANES).astype(F32)
        for e in range(N_EXPERTS):
            base8 = pl.multiple_of((base_ref[b * N_EXPERTS + e] >> 3) << 3, 8)
            hit = pos[e:e + 1, :] == slot + base8.astype(F32)
            ic = jnp.sum(jnp.where(hit, tok, 0.0), axis=1, keepdims=True)
            gc = jnp.sum(jnp.where(hit, aff[e:e + 1, :], 0.0), axis=1, keepdims=True)
            rows = pl.ds(base8, COMPACT_WIN)
            list_ref[rows, :] += jnp.where(lane == e, ic, jnp.where(lane == N_EXPERTS + e, gc, 0.0))
        return 0

    lax.fori_loop(0, nblk, blk_step, 0)


def _compact(base, aff3, sel3, pos3, cap):
    cp = cap + 2 * LANES
    full = pl.BlockSpec(aff3.shape, lambda i, s: (0, 0, 0))
    ospec = pl.BlockSpec((cp, LANES), lambda i, s: (0, 0))
    oshape = jax.ShapeDtypeStruct((cp, LANES), F32)
    grid_spec = pltpu.PrefetchScalarGridSpec(
        num_scalar_prefetch=1, grid=(1,), in_specs=[full, full, full], out_specs=ospec)
    return pl.pallas_call(
        _compact_kernel,
        grid_spec=grid_spec,
        out_shape=oshape,
        compiler_params=_cparams(("arbitrary",)),
        name="compact",
    )(base, aff3, sel3, pos3)


FFN_COLS = 256


def _up_kernel(idx_ref, h2_hbm, wg_ref, wu_ref, hid_ref, gbuf, xs, sem, *, tm, total):
    step = pl.program_id(0) * pl.num_programs(1) + pl.program_id(1)

    def row_copy(step_, i):
        return pltpu.make_async_copy(h2_hbm.at[idx_ref[step_ * tm + i]], gbuf.at[i], sem)

    def wait_rows():
        pltpu.make_async_copy(h2_hbm.at[pl.ds(0, tm)], gbuf, sem).wait()

    @pl.when(step == 0)
    def _():
        def body(i, _):
            row_copy(step, i).start()
            return 0
        lax.fori_loop(0, tm, body, 0)

    wait_rows()
    slab = min(SLAB, tm)
    for tb in range(tm // slab):
        rows = slice(tb * slab, (tb + 1) * slab)
        for s, chunk in enumerate(_load_row_major(gbuf, rows)):
            xs[rows, s * LANES:(s + 1) * LANES] = chunk.astype(BF16)

    nxt = jnp.minimum(step + 1, total - 1)
    for i in range(tm):
        row_copy(nxt, i).start(priority=i % 2)

    for f in range(EXPERT_FF // FFN_COLS):
        cols = slice(f * FFN_COLS, (f + 1) * FFN_COLS)
        hg = jnp.dot(xs[...], wg_ref[0, :, cols].astype(BF16), preferred_element_type=F32)
        hu = jnp.dot(xs[...], wu_ref[0, :, cols].astype(BF16), preferred_element_type=F32)
        hid_ref[:, cols] = (_silu(hg) * hu).astype(BF16)

    @pl.when(step == total - 1)
    def _():
        wait_rows()


def _expert_up(idx_flat, h2_rm, wg, wu, cap):
    tm = min(1024, cap)
    nr = cap // tm
    grid_spec = pltpu.PrefetchScalarGridSpec(
        num_scalar_prefetch=1,
        grid=(N_EXPERTS, nr),
        in_specs=[pl.BlockSpec(memory_space=pl.ANY),
                  pl.BlockSpec((1, D_MODEL, EXPERT_FF), lambda e, r, s: (e, 0, 0)),
                  pl.BlockSpec((1, D_MODEL, EXPERT_FF), lambda e, r, s: (e, 0, 0))],
        out_specs=pl.BlockSpec((tm, EXPERT_FF), lambda e, r, s: (e * nr + r, 0)),
        scratch_shapes=[pltpu.VMEM((tm, ROW_CHUNKS, LANES), F32),
                        pltpu.VMEM((tm, D_MODEL), BF16),
                        pltpu.SemaphoreType.DMA(())],
    )
    return pl.pallas_call(
        functools.partial(_up_kernel, tm=tm, total=N_EXPERTS * nr),
        grid_spec=grid_spec,
        out_shape=jax.ShapeDtypeStruct((N_EXPERTS * cap, EXPERT_FF), BF16),
        compiler_params=_cparams(("arbitrary", "arbitrary")),
        name="expert_up",
    )(idx_flat, h2_rm, wg, wu)


def _down_kernel(hid_ref, gate_ref, wd_ref, o_ref, ybuf):
    gate = gate_ref[...]
    for c in range(D_MODEL // FFN_COLS):
        cols = slice(c * FFN_COLS, (c + 1) * FFN_COLS)
        ybuf[:, cols] = gate * jnp.dot(hid_ref[...], wd_ref[0, :, cols].astype(BF16),
                                       preferred_element_type=F32)
    tm = ybuf.shape[0]
    slab = min(SLAB, tm)
    for tb in range(tm // slab):
        rows = slice(tb * slab, (tb + 1) * slab)
        parts = jnp.stack([ybuf[rows, s * LANES:(s + 1) * LANES] for s in range(ROW_CHUNKS)], axis=0)
        o_ref[rows] = pltpu.einshape("stl->tsl", parts)


def _expert_down(hid, gate_col, wd, cap):
    tm = min(1024, cap)
    nr = cap // tm
    return pl.pallas_call(
        _down_kernel,
        grid=(N_EXPERTS, nr),
        in_specs=[pl.BlockSpec((tm, EXPERT_FF), lambda e, r: (e * nr + r, 0)),
                  pl.BlockSpec((tm, 1), lambda e, r: (e * nr + r, 0)),
                  pl.BlockSpec((1, EXPERT_FF, D_MODEL), lambda e, r: (e, 0, 0))],
        out_specs=pl.BlockSpec((tm, ROW_CHUNKS, LANES), lambda e, r: (e * nr + r, 0, 0)),
        out_shape=jax.ShapeDtypeStruct((N_EXPERTS * cap, ROW_CHUNKS, LANES), F32),
        scratch_shapes=[pltpu.VMEM((tm, D_MODEL), F32)],
        compiler_params=_cparams(("parallel", "arbitrary")),
        name="expert_down",
    )(hid, gate_col, wd)


COMBINE_UNROLL = 4
COMBINE_CHUNK = 32
COMBINE_NCHUNK = 3


def _combine_kernel(idx_ref, lo_ref, ye_hbm, x1_ref, gt2_ref, gf_ref, o_ref, stage, acc, sem,
                    *, tt, ch, cap, ntiles):
    tile = pl.program_id(0)
    slot = tile % 2

    window = COMBINE_NCHUNK * ch

    def chunk_start(first_row):
        return jnp.minimum(first_row, cap - window)

    def chunk_copy(slot_, e, start, c):
        return pltpu.make_async_copy(ye_hbm.at[pl.ds(e * cap + start + c * ch, ch)],
                                     stage.at[slot_, e, pl.ds(c * ch, ch)],
                                     sem.at[slot_, e])

    def for_needed_chunks(tile_, e, fn):
        lo = lo_ref[tile_ * N_EXPERTS + e]
        hi = lo_ref[(tile_ + 1) * N_EXPERTS + e]
        start = chunk_start(lo)
        for c in range(COMBINE_NCHUNK):
            @pl.when((start + c * ch < hi) & (start + (c + 1) * ch > lo))
            def _():
                fn(start, c)

    def issue(tile_, slot_):
        for e in range(N_EXPERTS):
            for_needed_chunks(tile_, e, lambda start, c: chunk_copy(slot_, e, start, c).start())

    @pl.when(tile == 0)
    def _():
        issue(tile, slot)

    @pl.when(tile + 1 < ntiles)
    def _():
        issue(tile + 1, 1 - slot)

    acc[...] = jnp.zeros_like(acc)

    def add_rows(e, start, r0, r1):
        def group(r, width):
            dsts, vals = [], []
            rows = stage[slot, e, pl.ds(r - start, width)]
            for u in range(width):
                dst = idx_ref[e * cap + r + u] - tile * tt
                dsts.append(dst)
                vals.append(acc[dst] + rows[u])
            for dst, val in zip(dsts, vals):
                acc[dst] = val

        ngroups = (r1 - r0) // COMBINE_UNROLL

        def body_group(j, _):
            group(r0 + j * COMBINE_UNROLL, COMBINE_UNROLL)
            return 0
        lax.fori_loop(0, ngroups, body_group, 0)

        def body_one(r, _):
            group(r, 1)
            return 0
        lax.fori_loop(r0 + ngroups * COMBINE_UNROLL, r1, body_one, 0)

    for e in range(N_EXPERTS):
        lo = lo_ref[tile * N_EXPERTS + e]
        hi = lo_ref[(tile + 1) * N_EXPERTS + e]
        start = chunk_start(lo)
        for_needed_chunks(tile, e, lambda start_, c: chunk_copy(slot, e, start_, c).wait())
        first_end = jnp.minimum(hi, start + window)
        add_rows(e, start, lo, first_end)

        def more(r0):
            st = jnp.minimum(r0, cap - ch)
            cp = chunk_copy(slot, e, st, 0)
            cp.start()
            cp.wait()
            r1 = jnp.minimum(hi, st + ch)
            add_rows(e, st, r0, r1)
            return r1
        lax.while_loop(lambda r0: r0 < hi, more, first_end)

    for tb in range(tt // SLAB):
        rows = slice(tb * SLAB, (tb + 1) * SLAB)
        ssq = jnp.zeros((SLAB, 1), F32)
        for s, y in enumerate(_load_row_major(acc, rows)):
            cols = slice(s * LANES, (s + 1) * LANES)
            x2 = x1_ref[rows, cols] + gt2_ref[0][:, cols] * y
            o_ref[rows, cols] = x2
            ssq = ssq + jnp.sum(x2 * x2, axis=-1, keepdims=True)
        o_ref[rows, :] = o_ref[rows, :] * lax.rsqrt(ssq / D_MODEL + EPS) * gf_ref[...]


def _combine(idx_flat, lo_tab, ye_lin, x1, seq, gt2, g_final, cap, tt):
    n = x1.shape[0]
    ch = min(COMBINE_CHUNK, cap // COMBINE_NCHUNK)
    ntiles = n // tt
    per_b = _tiles_per_mod_row(gt2, n, seq, tt)
    grid_spec = pltpu.PrefetchScalarGridSpec(
        num_scalar_prefetch=2,
        grid=(ntiles,),
        in_specs=[pl.BlockSpec(memory_space=pl.ANY),
                  pl.BlockSpec((tt, D_MODEL), lambda i, a, b: (i, 0)),
                  pl.BlockSpec((1, 1, D_MODEL), lambda i, a, b: (i // per_b, 0, 0)),
                  pl.BlockSpec((1, D_MODEL), lambda i, a, b: (0, 0))],
        out_specs=pl.BlockSpec((tt, D_MODEL), lambda i, a, b: (i, 0)),
        scratch_shapes=[pltpu.VMEM((2, N_EXPERTS, COMBINE_NCHUNK * ch, ROW_CHUNKS, LANES), F32),
                        pltpu.VMEM((tt, ROW_CHUNKS, LANES), F32),
                        pltpu.SemaphoreType.DMA((2, N_EXPERTS))],
    )
    return pl.pallas_call(
        functools.partial(_combine_kernel, tt=tt, ch=ch, cap=cap, ntiles=ntiles),
        grid_spec=grid_spec,
        out_shape=jax.ShapeDtypeStruct((n, D_MODEL), F32),
        compiler_params=_cparams(("arbitrary",)),
        name="combine",
    )(idx_flat, lo_tab, ye_lin, x1, gt2, g_final)


def _rope_tables(seq):
    pos = np.arange(seq)
    row = (pos // GRID_W).astype(np.float32)
    col = (pos % GRID_W).astype(np.float32)
    npair = SWA_HEAD_DIM // 4
    inv_freq = (ROPE_BASE ** (-np.arange(npair, dtype=np.float32) / npair)).astype(np.float32)
    ar = (row[:, None] * inv_freq[None, :]).astype(np.float64)
    ac = (col[:, None] * inv_freq[None, :]).astype(np.float64)
    cos = np.concatenate([np.cos(ar), np.cos(ar), np.cos(ac), np.cos(ac)], axis=1)
    sin = np.concatenate([-np.sin(ar), np.sin(ar), -np.sin(ac), np.sin(ac)], axis=1)
    reps = SWA_KVW // SWA_HEAD_DIM
    return (jnp.asarray(np.tile(cos, (1, reps)), F32), jnp.asarray(np.tile(sin, (1, reps)), F32))


def _layer(x, mods, wts, latent, s0f, s0b, cache_k, cache_v):
    bsz, seq, _ = x.shape
    n = bsz * seq
    x2d = x.reshape(n, D_MODEL)
    sh1, sc1, gt1, sh2, sc2, gt2 = mods
    p, tail = _in_proj(x2d, seq, sc1, sh1, wts["g1"], wts["w_in_r"])
    o_gla, st_f, st_b = _gla(p, tail, bsz, seq, wts["wdf"], wts["bdf"], wts["wdb"], wts["bdb"], wts["g_gla"],
                             s0f, s0b)
    if latent:
        cos_t, sin_t = _rope_tables(seq)
        o_swa = _swa_latent(p, bsz, seq, cache_k, cache_v, wts["sink"], cos_t, sin_t)
    else:
        o_swa = _swa_context(p, bsz, seq, wts["sink"])
    merged = _branch_merge(o_gla, o_swa, p, wts["wa"], wts["wb"])
    x1, h2, logits3 = _out_proj(merged, x2d, seq, gt1, sc2, sh2, wts["g2"], wts["wo"], wts["wr2"])
    cap = CAPACITY_FACTOR * n // N_EXPERTS
    aff3, sel3, pos3 = _select(logits3, cap)
    base = jnp.concatenate([pos3[:, :, 0].astype(jnp.int32), jnp.full((1, N_EXPERTS), cap, jnp.int32)], axis=0)
    lists = _compact(base.reshape(-1), aff3, sel3, pos3, cap)
    idx_flat = lists[:cap, :N_EXPERTS].T.astype(jnp.int32).reshape(-1)
    gate_col = lists[:cap, N_EXPERTS:2 * N_EXPERTS].T.reshape(-1, 1)
    hid = _expert_up(idx_flat, h2, wts["wg"], wts["wu"], cap)
    ye = _expert_down(hid, gate_col, wts["wd"], cap)
    tt = 256
    lo_tab = base[::tt // LANES]
    y = _combine(idx_flat, lo_tab.reshape(-1), ye, x1, seq, gt2, wts["g_final"], cap, tt)
    return y.reshape(bsz, seq, D_MODEL), tail, st_f, st_b


def kernel(x_prompt, x_sample, state_gla_fwd, state_gla_bwd, cache_k, cache_v, c, c_ctx, w_mod, b_mod,
           g_norm1, w_in, w_dec_f, b_dec_f, w_dec_b, b_dec_b, g_gla, attn_sink, w_branch_a, w_branch_b,
           w_out, g_norm2, w_router, w_exp_gate, w_exp_up, w_exp_down, g_final):
    bp, sp, _ = x_prompt.shape
    bl = x_sample.shape[0]
    l = 0
    w_in_r = _reorder_w_in(w_in[l])
    zpad = jnp.zeros((LANES - 2 * GLA_RANK, GLA_KW), F32)
    wr_hi = w_router[l].astype(BF16)
    wr_lo = (w_router[l] - wr_hi.astype(F32)).astype(BF16)
    wr_pad = jnp.zeros((D_MODEL, LANES - 2 * N_EXPERTS), BF16)
    zr = jnp.zeros((GLA_RANK, GLA_KW), F32)
    wts = {
        "g1": g_norm1[l].reshape(1, D_MODEL),
        "w_in_r": w_in_r,
        "wdf": jnp.concatenate([w_dec_f[l], zr, zpad], axis=0),
        "wdb": jnp.concatenate([zr, w_dec_b[l], zpad], axis=0),
        "bdf": b_dec_f[l].reshape(1, GLA_KW),
        "bdb": b_dec_b[l].reshape(1, GLA_KW),
        "g_gla": g_gla[l].reshape(1, GLA_VW),
        "sink": attn_sink[l],
        "wa": w_branch_a[l].astype(BF16),
        "wb": w_branch_b[l].astype(BF16),
        "wo": w_out[l].astype(BF16),
        "g2": g_norm2[l].reshape(1, D_MODEL),
        "wr2": jnp.concatenate([wr_hi, wr_lo, wr_pad], axis=1),
        "wg": w_exp_gate[l],
        "wu": w_exp_up[l],
        "wd": w_exp_down[l],
        "g_final": g_final.reshape(1, D_MODEL),
    }
    cond8 = jnp.concatenate([c_ctx[None, :], c, jnp.zeros((8 - 1 - bl, D_MODEL), F32)], axis=0)
    mod = _modulation(cond8, w_mod[l], b_mod[l]).reshape(8, N_MOD, 1, D_MODEL)
    mods_ctx = tuple(mod[0:1, j] for j in range(N_MOD))
    mods_lat = tuple(mod[1:1 + bl, j] for j in range(N_MOD))

    zero_state = jnp.zeros((1, GLA_HEADS, GLA_DK, GLA_DV), F32)
    y_prompt, tail_ctx, st_f, st_b = _layer(x_prompt, mods_ctx, wts, False, zero_state, zero_state, None, None)
    ck = cache_k[:, l].reshape(bl, -1, SWA_KVW)
    cv = cache_v[:, l].reshape(bl, -1, SWA_KVW)
    y_sample, _, _, _ = _layer(x_sample, mods_lat, wts, True, state_gla_fwd[:, l], state_gla_bwd[:, l], ck, cv)

    ksl = slice(COL_SK - TAIL_COL0, COL_SK - TAIL_COL0 + SWA_KVW)
    vsl = slice(COL_SV - TAIL_COL0, COL_SV - TAIL_COL0 + SWA_KVW)
    new_k = tail_ctx[:, ksl].reshape(bp, 1, sp, SWA_KV_HEADS, SWA_HEAD_DIM)
    new_v = tail_ctx[:, vsl].reshape(bp, 1, sp, SWA_KV_HEADS, SWA_HEAD_DIM)
    return (y_prompt, y_sample, st_f[:, None], st_b[:, None], new_k, new_v)
```

```python
import functools

import jax
import jax.numpy as jnp
import numpy as np
from jax import lax
from jax.experimental import pallas as pl
from jax.experimental.pallas import tpu as pltpu

F32 = jnp.float32
BF16 = jnp.bfloat16
HIGHEST = lax.Precision.HIGHEST

D_MODEL = 2048
N_MOD = 6
EPS = 1e-6

GLA_HEADS = 4
GLA_DK = 128
GLA_DV = 256
GLA_KW = GLA_HEADS * GLA_DK
GLA_VW = GLA_HEADS * GLA_DV
GLA_RANK = 16
GLA_TAU = 16.0
GLA_CHUNK = 64

SWA_HEADS = 16
SWA_KV_HEADS = 4
SWA_GROUP = 4
SWA_HEAD_DIM = 64
SWA_QW = SWA_HEADS * SWA_HEAD_DIM
SWA_KVW = SWA_KV_HEADS * SWA_HEAD_DIM
SWA_BLOCK = 128
GRID_W = 64
ROPE_BASE = 10000.0

N_EXPERTS = 16
EXPERT_FF = D_MODEL // 2
CAPACITY_FACTOR = 2

LANES = 128
ROW_CHUNKS = D_MODEL // LANES
VMEM_LIMIT = 56 * 1024 * 1024

COL_GA = 0
COL_GB = 2048
COL_GQ = 4096
COL_GK = 4608
COL_GV = 5120
COL_GR = 6144
COL_SQ = 7168
COL_SK = 8192
COL_SV = 8448
COL_DEC = 8704
IN_PAD = 8960
IN_TN = 1280
TAIL_COL0 = COL_SK
TAIL_W = COL_DEC + LANES - COL_SK


def _cparams(sem, **kw):
    return pltpu.CompilerParams(dimension_semantics=sem, vmem_limit_bytes=VMEM_LIMIT, **kw)


def _silu(x):
    return x * (1.0 / (1.0 + jnp.exp(-x)))


def _sigmoid(x):
    return 1.0 / (1.0 + jnp.exp(-x))


SLAB = 128


def _store_row_major(ref, x):
    for tb in range(x.shape[0] // SLAB):
        rows = slice(tb * SLAB, (tb + 1) * SLAB)
        parts = jnp.stack([x[rows, s * LANES:(s + 1) * LANES] for s in range(ROW_CHUNKS)], axis=0)
        ref[rows] = pltpu.einshape("stl->tsl", parts)


def _load_row_major(ref, rows):
    xt = pltpu.einshape("tsl->stl", ref[rows])
    return [xt[s] for s in range(ROW_CHUNKS)]


def _tiles_per_mod_row(mod, n, seq, tile):
    return seq // tile if mod.shape[0] > 1 else n // tile


def _mod_kernel(c_ref, w_ref, b_ref, o_ref):
    a = _silu(c_ref[...]).astype(BF16)
    o_ref[...] = jnp.dot(a, w_ref[...].astype(BF16), preferred_element_type=F32) + b_ref[...]


def _modulation(cond8, w_mod, b_mod):
    n_out = w_mod.shape[1]
    tn = 1536
    return pl.pallas_call(
        _mod_kernel,
        grid=(n_out // tn,),
        in_specs=[pl.BlockSpec((8, D_MODEL), lambda j: (0, 0)),
                  pl.BlockSpec((D_MODEL, tn), lambda j: (0, j)),
                  pl.BlockSpec((1, tn), lambda j: (0, j))],
        out_specs=pl.BlockSpec((8, tn), lambda j: (0, j)),
        out_shape=jax.ShapeDtypeStruct((8, n_out), F32),
        compiler_params=_cparams(("arbitrary",)),
        name="modulation",
    )(cond8, w_mod, b_mod.reshape(1, n_out))


SRC_GLA_END = 2 * GLA_KW + 2 * GLA_VW
SRC_DEC_END = SRC_GLA_END + 2 * GLA_RANK
SRC_SWA_END = SRC_DEC_END + SWA_QW + 2 * SWA_KVW
SRC_END = SRC_SWA_END + 2 * D_MODEL
REORDER_TN = 256
LANE_SHIFT = SRC_DEC_END % LANES


def _reorder_kernel(a_ref, b_ref, c_ref, t_ref, o_ref):
    j = pl.program_id(0)
    lane = lax.broadcasted_iota(jnp.int32, (D_MODEL, LANES), 1)
    n_gate = (COL_GQ - COL_GA) // REORDER_TN
    n_gla = (COL_SQ - COL_GQ) // REORDER_TN
    n_swa = (COL_DEC - COL_SQ) // REORDER_TN

    def shifted(x, y):
        return jnp.where(lane < LANES - LANE_SHIFT, pltpu.roll(x, LANES - LANE_SHIFT, 1),
                         pltpu.roll(y, LANES - LANE_SHIFT, 1))

    @pl.when((j < n_gate) | ((j >= n_gate + n_gla) & (j < n_gate + n_gla + n_swa)))
    def _():
        c = jnp.where(j == n_gate - 1, t_ref[...], c_ref[...])
        o_ref[:, :LANES] = shifted(a_ref[...], b_ref[...]).astype(BF16)
        o_ref[:, LANES:] = shifted(b_ref[...], c).astype(BF16)

    @pl.when((j >= n_gate) & (j < n_gate + n_gla))
    def _():
        o_ref[:, :LANES] = a_ref[...].astype(BF16)
        o_ref[:, LANES:] = b_ref[...].astype(BF16)

    @pl.when(j == n_gate + n_gla + n_swa)
    def _():
        o_ref[:, :LANES] = jnp.where(lane < LANE_SHIFT, a_ref[...], 0.0).astype(BF16)
        o_ref[:, LANES:] = jnp.zeros((D_MODEL, LANES), BF16)


def _reorder_w_in(w_all, layer):
    assert SRC_END == w_all.shape[2] and SRC_SWA_END % LANES == LANE_SHIFT and IN_PAD - COL_DEC == REORDER_TN
    n_gate = (COL_GQ - COL_GA) // REORDER_TN
    n_gla = (COL_SQ - COL_GQ) // REORDER_TN
    n_swa = (COL_DEC - COL_SQ) // REORDER_TN
    per = REORDER_TN // LANES
    last_full = SRC_END // LANES - 1

    def first_block(j):
        gate = per * j + SRC_SWA_END // LANES
        gla = per * (j - n_gate)
        swa = per * (j - n_gate - n_gla) + SRC_DEC_END // LANES
        dec = SRC_GLA_END // LANES
        return jnp.where(j < n_gate, gate, jnp.where(j < n_gate + n_gla, gla,
                                                    jnp.where(j < n_gate + n_gla + n_swa, swa, dec)))

    def spec(k):
        return pl.BlockSpec((None, D_MODEL, LANES),
                            lambda j: (layer, 0, jnp.minimum(first_block(j) + k, last_full)))

    tail = jnp.pad(w_all[layer, :, last_full * LANES + LANES:], ((0, 0), (0, LANES - (SRC_END % LANES))))
    return pl.pallas_call(
        _reorder_kernel,
        grid=(IN_PAD // REORDER_TN,),
        in_specs=[spec(0), spec(1), spec(2), pl.BlockSpec((D_MODEL, LANES), lambda j: (0, 0))],
        out_specs=pl.BlockSpec((D_MODEL, REORDER_TN), lambda j: (0, j)),
        out_shape=jax.ShapeDtypeStruct((D_MODEL, IN_PAD), BF16),
        compiler_params=_cparams(("parallel",)),
        name="reorder_w_in",
    )(w_all, w_all, w_all, tail)


NORM_ROWS = 64


def _in_proj_kernel(x_ref, sc_ref, sh_ref, g_ref, w_ref, o_ref, tail_ref, h_scr):
    j = pl.program_id(1)

    @pl.when(j == 0)
    def _():
        def slab(r, _):
            rows = pl.ds(pl.multiple_of(r * NORM_ROWS, NORM_ROWS), NORM_ROWS)
            x = x_ref[rows, :]
            y = x * lax.rsqrt(jnp.mean(x * x, axis=-1, keepdims=True) + EPS) * g_ref[...]
            h_scr[rows, :] = (y * (1.0 + sc_ref[0]) + sh_ref[0]).astype(BF16)
            return 0
        lax.fori_loop(0, x_ref.shape[0] // NORM_ROWS, slab, 0)

    acc = jnp.dot(h_scr[...], w_ref[...], preferred_element_type=F32)
    o_ref[...] = acc.astype(BF16)

    @pl.when(j == IN_PAD // IN_TN - 1)
    def _():
        first = TAIL_COL0 - (IN_PAD - IN_TN)
        tail_ref[...] = acc[:, first:first + TAIL_W]


def _in_proj(x2d, seq, sc, sh, g, w_r):
    n = x2d.shape[0]
    tm = min(1024, seq if sc.shape[0] > 1 else n)
    per_b = _tiles_per_mod_row(sc, n, seq, tm)
    return pl.pallas_call(
        _in_proj_kernel,
        grid=(n // tm, IN_PAD // IN_TN),
        in_specs=[pl.BlockSpec((tm, D_MODEL), lambda i, j: (i, 0)),
                  pl.BlockSpec((1, 1, D_MODEL), lambda i, j: (i // per_b, 0, 0)),
                  pl.BlockSpec((1, 1, D_MODEL), lambda i, j: (i // per_b, 0, 0)),
                  pl.BlockSpec((1, D_MODEL), lambda i, j: (0, 0)),
                  pl.BlockSpec((D_MODEL, IN_TN), lambda i, j: (0, j))],
        out_specs=[pl.BlockSpec((tm, IN_TN), lambda i, j: (i, j)),
                   pl.BlockSpec((tm, TAIL_W), lambda i, j: (i, 0))],
        out_shape=[jax.ShapeDtypeStruct((n, IN_PAD), BF16),
                   jax.ShapeDtypeStruct((n, TAIL_W), F32)],
        scratch_shapes=[pltpu.VMEM((tm, D_MODEL), BF16)],
        compiler_params=_cparams(("parallel", "arbitrary")),
        name="in_proj",
    )(x2d, sc, sh, g, w_r)


GLA_BLK = 256
GLA_HPS = 4


def _tn_dot(a, b, precision=None):
    return lax.dot_general(a, b, (((0,), (0,)), ((), ())), precision=precision,
                           preferred_element_type=F32)


def _nt_dot(a, b, precision=None):
    return lax.dot_general(a, b, (((1,), (1,)), ((), ())), precision=precision,
                           preferred_element_type=F32)


def _split_bf16(x, parts):
    out = []
    for _ in range(parts):
        piece = x.astype(BF16)
        out.append(piece)
        x = x - piece.astype(F32)
    return out


def _dot3(x, w):
    xh, xl = _split_bf16(x, 2)
    wh, wl = _split_bf16(w, 2)
    d = functools.partial(jnp.dot, preferred_element_type=F32)
    return d(xh, wh) + d(xl, wh) + d(xh, wl)


def _gla_kernel(q_ref, k_ref, v_ref, r_ref, dec_ref, wdf_ref, bdf_ref, wdb_ref, bdb_ref, gg_ref,
                s0f_ref, s0b_ref, o_ref, stf_ref, stb_ref, state_t, of_scr, *, nblk):
    i = pl.program_id(2)
    c = GLA_CHUNK
    nch = GLA_BLK // c
    ri = lax.broadcasted_iota(jnp.int32, (GLA_BLK, GLA_BLK), 0)
    ci = lax.broadcasted_iota(jnp.int32, (GLA_BLK, GLA_BLK), 1)
    same_chunk = (ri // c) == (ci // c)

    @pl.when(i == 0)
    def _():
        for h in range(GLA_HPS):
            state_t[h] = s0f_ref[0, h].T

    @pl.when(i == nblk)
    def _():
        for h in range(GLA_HPS):
            state_t[h] = s0b_ref[0, h].T

    def run(fwd):
        blk = i if fwd else 2 * nblk - 1 - i
        keep = same_chunk & ((ci <= ri) if fwd else (ci >= ri))
        w_ref, b_ref = (wdf_ref, bdf_ref) if fwd else (wdb_ref, bdb_ref)
        logit = _dot3(dec_ref[...], w_ref[...]) + b_ref[...]
        g = (jnp.minimum(logit, 0.0) - jnp.log(1.0 + jnp.exp(-jnp.abs(logit)))) / GLA_TAU
        tri = keep.astype(BF16)
        b = sum(jnp.dot(tri, piece, preferred_element_type=F32) for piece in _split_bf16(g, 3))
        edge = c - 1 if fwd else 0
        tots = [b[ch * c + edge:ch * c + edge + 1, :] for ch in range(nch)]
        totb = jnp.concatenate([jnp.broadcast_to(t, (c, t.shape[1])) for t in tots], axis=0)
        q_in = (q_ref[...].astype(F32) * (GLA_DK ** -0.5) * jnp.exp(b)).astype(BF16)
        k = k_ref[...].astype(F32)
        k_in = (k * jnp.exp(-b)).astype(BF16)
        k_st = (k * jnp.exp(totb - b)).astype(BF16)
        srow = pl.multiple_of(blk * GLA_BLK, GLA_BLK)
        for h in range(GLA_HPS):
            ks = slice(h * GLA_DK, (h + 1) * GLA_DK)
            vs = slice(h * GLA_DV, (h + 1) * GLA_DV)
            v = v_ref[:, vs].astype(BF16)
            a = jnp.where(keep, _nt_dot(q_in[:, ks], k_in[:, ks]), 0.0).astype(BF16)
            o_intra = jnp.dot(a, v, preferred_element_type=F32)
            st = state_t[h]
            o_inter = [None] * nch
            for cc in range(nch):
                ch = cc if fwd else nch - 1 - cc
                rows = slice(ch * c, (ch + 1) * c)
                o_inter[ch] = _nt_dot(q_in[rows, ks], st.astype(BF16))
                st = st * jnp.exp(tots[ch][:, ks]) + _tn_dot(v[rows, :], k_st[rows, ks])
            state_t[h] = st
            o = o_intra + jnp.concatenate(o_inter, axis=0)
            if fwd:
                of_scr[pl.ds(srow, GLA_BLK), vs] = o
            else:
                o = o + of_scr[pl.ds(srow, GLA_BLK), vs]
                o = o * lax.rsqrt(jnp.mean(o * o, axis=-1, keepdims=True) + EPS) * gg_ref[:, vs]
                o_ref[:, vs] = (o * _silu(r_ref[:, vs].astype(F32))).astype(BF16)

    @pl.when(i < nblk)
    def _():
        run(True)

    @pl.when(i >= nblk)
    def _():
        run(False)

    @pl.when(i == nblk - 1)
    def _():
        for h in range(GLA_HPS):
            stf_ref[0, h] = state_t[h].T

    @pl.when(i == 2 * nblk - 1)
    def _():
        for h in range(GLA_HPS):
            stb_ref[0, h] = state_t[h].T


def _gla(p, dec, bsz, seq, wdf, bdf, wdb, bdb, g_gla, s0f, s0b):
    n = p.shape[0]
    nblk = seq // GLA_BLK
    kw = GLA_HPS * GLA_DK
    vw = GLA_HPS * GLA_DV

    def rb(b, i):
        return b * nblk + jnp.where(i < nblk, i, 2 * nblk - 1 - i)

    def orb(b, i):
        return b * nblk + jnp.where(i < nblk, nblk - 1, 2 * nblk - 1 - i)

    st_spec = pl.BlockSpec((1, GLA_HPS, GLA_DK, GLA_DV), lambda b, h, i: (b, h, 0, 0))
    s0_spec = pl.BlockSpec((1, GLA_HPS, GLA_DK, GLA_DV),
                           lambda b, h, i: (b if s0f.shape[0] > 1 else 0, h, 0, 0))
    st_shape = jax.ShapeDtypeStruct((bsz, GLA_HEADS, GLA_DK, GLA_DV), F32)
    return pl.pallas_call(
        functools.partial(_gla_kernel, nblk=nblk),
        grid=(bsz, GLA_HEADS // GLA_HPS, 2 * nblk),
        in_specs=[pl.BlockSpec((GLA_BLK, kw), lambda b, h, i: (rb(b, i), COL_GQ // kw + h)),
                  pl.BlockSpec((GLA_BLK, kw), lambda b, h, i: (rb(b, i), COL_GK // kw + h)),
                  pl.BlockSpec((GLA_BLK, vw), lambda b, h, i: (rb(b, i), COL_GV // vw + h)),
                  pl.BlockSpec((GLA_BLK, vw), lambda b, h, i: (rb(b, i), COL_GR // vw + h)),
                  pl.BlockSpec((GLA_BLK, LANES), lambda b, h, i: (rb(b, i), (COL_DEC - TAIL_COL0) // LANES)),
                  pl.BlockSpec((LANES, kw), lambda b, h, i: (0, h)),
                  pl.BlockSpec((1, kw), lambda b, h, i: (0, h)),
                  pl.BlockSpec((LANES, kw), lambda b, h, i: (0, h)),
                  pl.BlockSpec((1, kw), lambda b, h, i: (0, h)),
                  pl.BlockSpec((1, vw), lambda b, h, i: (0, h)),
                  s0_spec, s0_spec],
        out_specs=[pl.BlockSpec((GLA_BLK, vw), lambda b, h, i: (orb(b, i), h)), st_spec, st_spec],
        out_shape=[jax.ShapeDtypeStruct((n, GLA_VW), BF16), st_shape, st_shape],
        scratch_shapes=[pltpu.VMEM((GLA_HPS, GLA_DV, GLA_DK), F32), pltpu.VMEM((seq, vw), F32)],
        compiler_params=_cparams(("parallel", "parallel", "arbitrary")),
        name="gla",
    )(p, p, p, p, dec, wdf, bdf, wdb, bdb, g_gla, s0f, s0b)


def _sink_column(sink_ref, kvh, rows_per_head):
    r = lax.broadcasted_iota(jnp.int32, (SWA_GROUP * rows_per_head, 1), 0)
    col = jnp.zeros((SWA_GROUP * rows_per_head, 1), F32)
    for g in range(SWA_GROUP):
        col = jnp.where(r // rows_per_head == g, sink_ref[kvh * SWA_GROUP + g], col)
    return col


def _rope(x, cos, sin):
    outs = []
    for cb in range(x.shape[1] // LANES):
        sl = slice(cb * LANES, (cb + 1) * LANES)
        xc = x[:, sl]
        lane = lax.broadcasted_iota(jnp.int32, xc.shape, 1)
        sw = jnp.where(lane % 32 < 16, pltpu.roll(xc, LANES - 16, 1), pltpu.roll(xc, 16, 1))
        outs.append(xc * cos[:, sl] + sw * sin[:, sl])
    return jnp.concatenate(outs, axis=1)


def _swa_lat_kernel(sink_ref, q_ref, kp_ref, kc_ref, kn_ref, vp_ref, vc_ref, vn_ref, ck_ref, cv_ref,
                    cp_ref, cc_ref, cn_ref, sp_ref, sc_ref, sn_ref, o_ref, *, nb):
    n = pl.program_id(1)
    blk = SWA_BLOCK
    hd = SWA_HEAD_DIM
    cos_c, sin_c = cc_ref[...], sc_ref[...]
    kband = jnp.concatenate([_rope(kp_ref[...].astype(F32), cp_ref[...], sp_ref[...]),
                             _rope(kc_ref[...].astype(F32), cos_c, sin_c),
                             _rope(kn_ref[...].astype(F32), cn_ref[...], sn_ref[...])], axis=0).astype(BF16)
    vband = jnp.concatenate([vp_ref[...], vc_ref[...], vn_ref[...]], axis=0).astype(BF16)
    ck = ck_ref[0].astype(BF16)
    cv = cv_ref[0].astype(BF16)
    qi = lax.broadcasted_iota(jnp.int32, (blk, 3 * blk), 0)
    kj = lax.broadcasted_iota(jnp.int32, (blk, 3 * blk), 1)
    k_abs = kj + (n - 1) * blk
    mask = (jnp.abs(kj - blk - qi) <= SWA_BLOCK) & (k_abs >= 0) & (k_abs < nb * blk)
    bias = jnp.concatenate([jnp.where(mask, 0.0, -1e30)] * SWA_GROUP, axis=0)
    nctx = ck.shape[0]
    kall = jnp.concatenate([ck, kband], axis=0)
    vall = jnp.concatenate([cv, vband], axis=0)
    for kvh in range(SWA_KV_HEADS):
        ks = slice(kvh * hd, (kvh + 1) * hd)
        qr = _rope(q_ref[:, kvh * SWA_KVW:(kvh + 1) * SWA_KVW].astype(F32), cos_c, sin_c) * (hd ** -0.5)
        qg = jnp.concatenate([qr[:, g * hd:(g + 1) * hd] for g in range(SWA_GROUP)],
                             axis=0).astype(BF16)
        s = _nt_dot(qg, kall[:, ks])
        s = jnp.concatenate([s[:, :nctx], s[:, nctx:] + bias], axis=1)
        sink = _sink_column(sink_ref, kvh, blk)
        m = jnp.maximum(jnp.max(s, axis=-1, keepdims=True), sink)
        pr = jnp.exp(s - m)
        den = jnp.exp(sink - m) + jnp.sum(pr, axis=-1, keepdims=True)
        o = jnp.dot(pr.astype(BF16), vall[:, ks], preferred_element_type=F32) / den
        for g in range(SWA_GROUP):
            cb = (kvh * SWA_GROUP + g) * hd
            o_ref[:, cb:cb + hd] = o[g * blk:(g + 1) * blk, :].astype(BF16)


def _swa_latent(p, bsz, seq, cache_k, cache_v, sink, cos_t, sin_t):
    n = p.shape[0]
    nb = seq // SWA_BLOCK
    kcol = COL_SK // SWA_KVW
    vcol = COL_SV // SWA_KVW
    prev = lambda i: jnp.maximum(i - 1, 0)
    nxt = lambda i: jnp.minimum(i + 1, nb - 1)

    def pspec(col, f):
        return pl.BlockSpec((SWA_BLOCK, SWA_KVW), lambda b, i, s: (b * nb + f(i), col))

    def tspec(f):
        return pl.BlockSpec((SWA_BLOCK, SWA_KVW), lambda b, i, s: (f(i), 0))

    same = lambda i: i
    cspec = pl.BlockSpec((1, cache_k.shape[1], SWA_KVW), lambda b, i, s: (b, 0, 0))
    grid_spec = pltpu.PrefetchScalarGridSpec(
        num_scalar_prefetch=1,
        grid=(bsz, nb),
        in_specs=[pl.BlockSpec((SWA_BLOCK, SWA_QW), lambda b, i, s: (b * nb + i, COL_SQ // SWA_QW)),
                  pspec(kcol, prev), pspec(kcol, same), pspec(kcol, nxt),
                  pspec(vcol, prev), pspec(vcol, same), pspec(vcol, nxt),
                  cspec, cspec,
                  tspec(prev), tspec(same), tspec(nxt), tspec(prev), tspec(same), tspec(nxt)],
        out_specs=pl.BlockSpec((SWA_BLOCK, SWA_QW), lambda b, i, s: (b * nb + i, 0)),
    )
    return pl.pallas_call(
        functools.partial(_swa_lat_kernel, nb=nb),
        grid_spec=grid_spec,
        out_shape=jax.ShapeDtypeStruct((n, SWA_QW), BF16),
        compiler_params=_cparams(("parallel", "arbitrary")),
        name="swa_latent",
    )(sink, p, p, p, p, p, p, p, cache_k, cache_v, cos_t, cos_t, cos_t, sin_t, sin_t, sin_t)


def _swa_ctx_kernel(sink_ref, q_ref, k_ref, v_ref, o_ref):
    s = q_ref.shape[0]
    hd = SWA_HEAD_DIM
    kb = k_ref[...].astype(BF16)
    vb = v_ref[...].astype(BF16)
    for kvh in range(SWA_KV_HEADS):
        ks = slice(kvh * hd, (kvh + 1) * hd)
        qg = jnp.concatenate([q_ref[:, (kvh * SWA_GROUP + g) * hd:(kvh * SWA_GROUP + g + 1) * hd]
                              for g in range(SWA_GROUP)], axis=0)
        qg = (qg.astype(F32) * (hd ** -0.5)).astype(BF16)
        sc = _nt_dot(qg, kb[:, ks])
        sink = _sink_column(sink_ref, kvh, s)
        m = jnp.maximum(jnp.max(sc, axis=-1, keepdims=True), sink)
        pr = jnp.exp(sc - m)
        den = jnp.exp(sink - m) + jnp.sum(pr, axis=-1, keepdims=True)
        o = jnp.dot(pr.astype(BF16), vb[:, ks], preferred_element_type=F32) / den
        for g in range(SWA_GROUP):
            cb = (kvh * SWA_GROUP + g) * hd
            o_ref[:, cb:cb + hd] = o[g * s:(g + 1) * s, :].astype(BF16)


def _swa_context(p, bsz, seq, sink):
    n = p.shape[0]
    grid_spec = pltpu.PrefetchScalarGridSpec(
        num_scalar_prefetch=1,
        grid=(bsz,),
        in_specs=[pl.BlockSpec((seq, SWA_QW), lambda b, s: (b, COL_SQ // SWA_QW)),
                  pl.BlockSpec((seq, SWA_KVW), lambda b, s: (b, COL_SK // SWA_KVW)),
                  pl.BlockSpec((seq, SWA_KVW), lambda b, s: (b, COL_SV // SWA_KVW))],
        out_specs=pl.BlockSpec((seq, SWA_QW), lambda b, s: (b, 0)),
    )
    return pl.pallas_call(
        _swa_ctx_kernel,
        grid_spec=grid_spec,
        out_shape=jax.ShapeDtypeStruct((n, SWA_QW), BF16),
        compiler_params=_cparams(("parallel",)),
        name="swa_context",
    )(sink, p, p, p)


MERGE_TM = 512


BRANCH_TM = 1024
BRANCH_COLS = 512


def _branch_kernel(og_ref, os_ref, ga_ref, gb_ref, wa_ref, wb_ref, m_ref):
    for c in range(D_MODEL // BRANCH_COLS):
        cols = slice(c * BRANCH_COLS, (c + 1) * BRANCH_COLS)
        a = jnp.dot(og_ref[...], wa_ref[:, cols], preferred_element_type=F32)
        b = jnp.dot(os_ref[...], wb_ref[:, cols], preferred_element_type=F32)
        m_ref[:, cols] = (_sigmoid(ga_ref[:, cols].astype(F32)) * a
                          + _sigmoid(gb_ref[:, cols].astype(F32)) * b).astype(BF16)


def _branch_merge(o_gla, o_swa, p, wa, wb):
    n = o_gla.shape[0]
    tm = BRANCH_TM
    once = pl.Buffered(1)
    return pl.pallas_call(
        _branch_kernel,
        grid=(n // tm,),
        in_specs=[pl.BlockSpec((tm, GLA_VW), lambda i: (i, 0)),
                  pl.BlockSpec((tm, SWA_QW), lambda i: (i, 0)),
                  pl.BlockSpec((tm, D_MODEL), lambda i: (i, COL_GA // D_MODEL)),
                  pl.BlockSpec((tm, D_MODEL), lambda i: (i, COL_GB // D_MODEL)),
                  pl.BlockSpec((GLA_VW, D_MODEL), lambda i: (0, 0), pipeline_mode=once),
                  pl.BlockSpec((SWA_QW, D_MODEL), lambda i: (0, 0), pipeline_mode=once)],
        out_specs=pl.BlockSpec((tm, D_MODEL), lambda i: (i, 0)),
        out_shape=jax.ShapeDtypeStruct((n, D_MODEL), BF16),
        compiler_params=_cparams(("parallel",)),
        name="branch_merge",
    )(o_gla, o_swa, p, p, wa, wb)


def _out_kernel(m_ref, x_ref, gt1_ref, sc2_ref, sh2_ref, g2_ref, wo_ref, wr2_ref,
                x1_ref, h2_ref, lg_ref):
    x1 = x_ref[...] + gt1_ref[0] * jnp.dot(m_ref[...], wo_ref[...], preferred_element_type=F32)
    x1_ref[...] = x1
    y = x1 * lax.rsqrt(jnp.mean(x1 * x1, axis=-1, keepdims=True) + EPS) * g2_ref[...]
    h2 = y * (1.0 + sc2_ref[0]) + sh2_ref[0]
    _store_row_major(h2_ref, h2)
    l2 = jnp.dot(h2.astype(BF16), wr2_ref[...], preferred_element_type=F32)
    lt = l2.T
    lt = lt[0:N_EXPERTS, :] + lt[N_EXPERTS:2 * N_EXPERTS, :]
    for cb in range(lt.shape[1] // LANES):
        lg_ref[cb] = lt[:, cb * LANES:(cb + 1) * LANES]


def _out_proj(merged, x2d, seq, gt1, sc2, sh2, g2, wo, wr2):
    n = x2d.shape[0]
    tm = MERGE_TM
    per_b = _tiles_per_mod_row(gt1, n, seq, tm)
    mod_spec = pl.BlockSpec((1, 1, D_MODEL), lambda i: (i // per_b, 0, 0))
    once = pl.Buffered(1)
    return pl.pallas_call(
        _out_kernel,
        grid=(n // tm,),
        in_specs=[pl.BlockSpec((tm, D_MODEL), lambda i: (i, 0)),
                  pl.BlockSpec((tm, D_MODEL), lambda i: (i, 0)),
                  mod_spec, mod_spec, mod_spec,
                  pl.BlockSpec((1, D_MODEL), lambda i: (0, 0)),
                  pl.BlockSpec((D_MODEL, D_MODEL), lambda i: (0, 0), pipeline_mode=once),
                  pl.BlockSpec((D_MODEL, LANES), lambda i: (0, 0))],
        out_specs=[pl.BlockSpec((tm, D_MODEL), lambda i: (i, 0)),
                   pl.BlockSpec((tm, ROW_CHUNKS, LANES), lambda i: (i, 0, 0)),
                   pl.BlockSpec((tm // LANES, N_EXPERTS, LANES), lambda i: (i, 0, 0))],
        out_shape=[jax.ShapeDtypeStruct((n, D_MODEL), F32),
                   jax.ShapeDtypeStruct((n, ROW_CHUNKS, LANES), F32),
                   jax.ShapeDtypeStruct((n // LANES, N_EXPERTS, LANES), F32)],
        compiler_params=_cparams(("parallel",)),
        name="out_proj",
    )(merged, x2d, gt1, sc2, sh2, g2, wo, wr2)


def _select_kernel(lg_ref, aff_ref, sel_ref, pos_ref, *, cap):
    nblk = lg_ref.shape[0]
    lg = lg_ref[...]
    ex = jnp.exp(lg - jnp.max(lg, axis=1, keepdims=True))
    aff = ex / jnp.sum(ex, axis=1, keepdims=True)
    aff_ref[...] = aff
    bits = lax.bitcast_convert_type(aff, jnp.int32)

    def count(pred):
        c = jnp.sum(jnp.where(pred, 1.0, 0.0), axis=0, keepdims=True)
        return jnp.sum(c, axis=2, keepdims=True)

    def bit_step(t, cur):
        cand = cur | jnp.left_shift(jnp.int32(1), 30 - t)
        return jnp.where(count(bits >= cand) >= cap, cand, cur)

    thr = lax.fori_loop(0, 31, bit_step, jnp.zeros((1, N_EXPERTS, 1), jnp.int32))
    need = (cap - count(bits > thr))[0]
    thr2 = thr[0]
    ri = lax.broadcasted_iota(jnp.int32, (LANES, LANES), 0)
    ci = lax.broadcasted_iota(jnp.int32, (LANES, LANES), 1)
    upper = (ri <= ci).astype(BF16)

    def blk_step(b, carry):
        run_eq, run_sel = carry
        bb = lax.bitcast_convert_type(aff_ref[b], jnp.int32)
        eq = (bb == thr2).astype(F32)
        eq_excl = jnp.dot(eq.astype(BF16), upper, preferred_element_type=F32) - eq + run_eq
        sel = jnp.where((bb > thr2) | ((eq > 0) & (eq_excl < need)), 1.0, 0.0)
        sel_ref[b] = sel
        pos_ref[b] = jnp.dot(sel.astype(BF16), upper, preferred_element_type=F32) - sel + run_sel
        return (run_eq + jnp.sum(eq, axis=1, keepdims=True),
                run_sel + jnp.sum(sel, axis=1, keepdims=True))

    zero = jnp.zeros((N_EXPERTS, 1), F32)
    lax.fori_loop(0, nblk, blk_step, (zero, zero))


def _select(logits3, cap):
    shp = jax.ShapeDtypeStruct(logits3.shape, F32)
    return pl.pallas_call(
        functools.partial(_select_kernel, cap=float(cap)),
        out_shape=[shp, shp, shp],
        compiler_params=pltpu.CompilerParams(vmem_limit_bytes=VMEM_LIMIT),
        name="select",
    )(logits3)


COMPACT_WIN = 136


def _compact_kernel(base_ref, aff_ref, sel_ref, pos_ref, list_ref):
    nblk = aff_ref.shape[0]
    list_ref[...] = jnp.zeros_like(list_ref)
    slot = lax.broadcasted_iota(jnp.int32, (COMPACT_WIN, LANES), 0).astype(F32)
    lane = lax.broadcasted_iota(jnp.int32, (COMPACT_WIN, LANES), 1)

    def blk_step(b, _):
        aff = aff_ref[b]
        pos = jnp.where(sel_ref[b] > 0, pos_ref[b], -1.0)
        tok = (lane + b * LANES).astype(F32)
        for e in range(N_EXPERTS):
            base8 = pl.multiple_of((base_ref[b * N_EXPERTS + e] >> 3) << 3, 8)
            hit = pos[e:e + 1, :] == slot + base8.astype(F32)
            ic = jnp.sum(jnp.where(hit, tok, 0.0), axis=1, keepdims=True)
            gc = jnp.sum(jnp.where(hit, aff[e:e + 1, :], 0.0), axis=1, keepdims=True)
            rows = pl.ds(base8, COMPACT_WIN)
            list_ref[rows, :] += jnp.where(lane == e, ic, jnp.where(lane == N_EXPERTS + e, gc, 0.0))
        return 0

    lax.fori_loop(0, nblk, blk_step, 0)


def _compact(base, aff3, sel3, pos3, cap):
    cp = cap + 2 * LANES
    full = pl.BlockSpec(aff3.shape, lambda i, s: (0, 0, 0))
    ospec = pl.BlockSpec((cp, LANES), lambda i, s: (0, 0))
    oshape = jax.ShapeDtypeStruct((cp, LANES), F32)
    grid_spec = pltpu.PrefetchScalarGridSpec(
        num_scalar_prefetch=1, grid=(1,), in_specs=[full, full, full], out_specs=ospec)
    return pl.pallas_call(
        _compact_kernel,
        grid_spec=grid_spec,
        out_shape=oshape,
        compiler_params=_cparams(("arbitrary",)),
        name="compact",
    )(base, aff3, sel3, pos3)


FFN_COLS = 256


def _up_kernel(idx_ref, h2_hbm, wg_ref, wu_ref, hid_ref, gbuf, xs, sem, *, tm, total):
    step = pl.program_id(0) * pl.num_programs(1) + pl.program_id(1)

    def row_copy(step_, i):
        return pltpu.make_async_copy(h2_hbm.at[idx_ref[step_ * tm + i]], gbuf.at[i], sem)

    def wait_rows():
        pltpu.make_async_copy(h2_hbm.at[pl.ds(0, tm)], gbuf, sem).wait()

    @pl.when(step == 0)
    def _():
        def body(i, _):
            row_copy(step, i).start()
            return 0
        lax.fori_loop(0, tm, body, 0)

    wait_rows()
    slab = min(SLAB, tm)
    for tb in range(tm // slab):
        rows = slice(tb * slab, (tb + 1) * slab)
        for s, chunk in enumerate(_load_row_major(gbuf, rows)):
            xs[rows, s * LANES:(s + 1) * LANES] = chunk.astype(BF16)

    nxt = jnp.minimum(step + 1, total - 1)
    for i in range(tm):
        row_copy(nxt, i).start(priority=i % 2)

    for f in range(EXPERT_FF // FFN_COLS):
        cols = slice(f * FFN_COLS, (f + 1) * FFN_COLS)
        hg = jnp.dot(xs[...], wg_ref[0, :, cols].astype(BF16), preferred_element_type=F32)
        hu = jnp.dot(xs[...], wu_ref[0, :, cols].astype(BF16), preferred_element_type=F32)
        hid_ref[:, cols] = (_silu(hg) * hu).astype(BF16)

    @pl.when(step == total - 1)
    def _():
        wait_rows()


def _expert_up(idx_flat, h2_rm, wg, wu, cap):
    tm = min(1024, cap)
    nr = cap // tm
    grid_spec = pltpu.PrefetchScalarGridSpec(
        num_scalar_prefetch=1,
        grid=(N_EXPERTS, nr),
        in_specs=[pl.BlockSpec(memory_space=pl.ANY),
                  pl.BlockSpec((1, D_MODEL, EXPERT_FF), lambda e, r, s: (e, 0, 0)),
                  pl.BlockSpec((1, D_MODEL, EXPERT_FF), lambda e, r, s: (e, 0, 0))],
        out_specs=pl.BlockSpec((tm, EXPERT_FF), lambda e, r, s: (e * nr + r, 0)),
        scratch_shapes=[pltpu.VMEM((tm, ROW_CHUNKS, LANES), F32),
                        pltpu.VMEM((tm, D_MODEL), BF16),
                        pltpu.SemaphoreType.DMA(())],
    )
    return pl.pallas_call(
        functools.partial(_up_kernel, tm=tm, total=N_EXPERTS * nr),
        grid_spec=grid_spec,
        out_shape=jax.ShapeDtypeStruct((N_EXPERTS * cap, EXPERT_FF), BF16),
        compiler_params=_cparams(("arbitrary", "arbitrary")),
        name="expert_up",
    )(idx_flat, h2_rm, wg, wu)


def _down_kernel(hid_ref, gate_ref, wd_ref, o_ref, ybuf):
    gate = gate_ref[...]
    for c in range(D_MODEL // FFN_COLS):
        cols = slice(c * FFN_COLS, (c + 1) * FFN_COLS)
        ybuf[:, cols] = gate * jnp.dot(hid_ref[...], wd_ref[0, :, cols].astype(BF16),
                                       preferred_element_type=F32)
    tm = ybuf.shape[0]
    slab = min(SLAB, tm)
    for tb in range(tm // slab):
        rows = slice(tb * slab, (tb + 1) * slab)
        parts = jnp.stack([ybuf[rows, s * LANES:(s + 1) * LANES] for s in range(ROW_CHUNKS)], axis=0)
        o_ref[rows] = pltpu.einshape("stl->tsl", parts)


def _expert_down(hid, gate_col, wd, cap):
    tm = min(1024, cap)
    nr = cap // tm
    return pl.pallas_call(
        _down_kernel,
        grid=(N_EXPERTS, nr),
        in_specs=[pl.BlockSpec((tm, EXPERT_FF), lambda e, r: (e * nr + r, 0)),
                  pl.BlockSpec((tm, 1), lambda e, r: (e * nr + r, 0)),
                  pl.BlockSpec((1, EXPERT_FF, D_MODEL), lambda e, r: (e, 0, 0))],
        out_specs=pl.BlockSpec((tm, ROW_CHUNKS, LANES), lambda e, r: (e * nr + r, 0, 0)),
        out_shape=jax.ShapeDtypeStruct((N_EXPERTS * cap, ROW_CHUNKS, LANES), F32),
        scratch_shapes=[pltpu.VMEM((tm, D_MODEL), F32)],
        compiler_params=_cparams(("parallel", "arbitrary")),
        name="expert_down",
    )(hid, gate_col, wd)


COMBINE_UNROLL = 4
COMBINE_CHUNK = 32
COMBINE_NCHUNK = 3


def _combine_kernel(idx_ref, lo_ref, ye_hbm, x1_ref, gt2_ref, gf_ref, o_ref, stage, acc, sem,
                    *, tt, ch, cap, ntiles):
    tile = pl.program_id(0)
    slot = tile % 2

    window = COMBINE_NCHUNK * ch

    def chunk_start(first_row):
        return jnp.minimum(first_row, cap - window)

    def chunk_copy(slot_, e, start, c):
        return pltpu.make_async_copy(ye_hbm.at[pl.ds(e * cap + start + c * ch, ch)],
                                     stage.at[slot_, e, pl.ds(c * ch, ch)],
                                     sem.at[slot_, e])

    def for_needed_chunks(tile_, e, fn):
        lo = lo_ref[tile_ * N_EXPERTS + e]
        hi = lo_ref[(tile_ + 1) * N_EXPERTS + e]
        start = chunk_start(lo)
        for c in range(COMBINE_NCHUNK):
            @pl.when((start + c * ch < hi) & (start + (c + 1) * ch > lo))
            def _():
                fn(start, c)

    def issue(tile_, slot_):
        for e in range(N_EXPERTS):
            for_needed_chunks(tile_, e, lambda start, c: chunk_copy(slot_, e, start, c).start())

    @pl.when(tile == 0)
    def _():
        issue(tile, slot)

    @pl.when(tile + 1 < ntiles)
    def _():
        issue(tile + 1, 1 - slot)

    acc[...] = jnp.zeros_like(acc)

    def add_rows(e, start, r0, r1):
        def group(r, width):
            dsts, vals = [], []
            rows = stage[slot, e, pl.ds(r - start, width)]
            for u in range(width):
                dst = idx_ref[e * cap + r + u] - tile * tt
                dsts.append(dst)
                vals.append(acc[dst] + rows[u])
            for dst, val in zip(dsts, vals):
                acc[dst] = val

        ngroups = (r1 - r0) // COMBINE_UNROLL

        def body_group(j, _):
            group(r0 + j * COMBINE_UNROLL, COMBINE_UNROLL)
            return 0
        lax.fori_loop(0, ngroups, body_group, 0)

        def body_one(r, _):
            group(r, 1)
            return 0
        lax.fori_loop(r0 + ngroups * COMBINE_UNROLL, r1, body_one, 0)

    for e in range(N_EXPERTS):
        lo = lo_ref[tile * N_EXPERTS + e]
        hi = lo_ref[(tile + 1) * N_EXPERTS + e]
        start = chunk_start(lo)
        for_needed_chunks(tile, e, lambda start_, c: chunk_copy(slot, e, start_, c).wait())
        first_end = jnp.minimum(hi, start + window)
        add_rows(e, start, lo, first_end)

        def more(r0):
            st = jnp.minimum(r0, cap - ch)
            cp = chunk_copy(slot, e, st, 0)
            cp.start()
            cp.wait()
            r1 = jnp.minimum(hi, st + ch)
            add_rows(e, st, r0, r1)
            return r1
        lax.while_loop(lambda r0: r0 < hi, more, first_end)

    for tb in range(tt // SLAB):
        rows = slice(tb * SLAB, (tb + 1) * SLAB)
        ssq = jnp.zeros((SLAB, 1), F32)
        for s, y in enumerate(_load_row_major(acc, rows)):
            cols = slice(s * LANES, (s + 1) * LANES)
            x2 = x1_ref[rows, cols] + gt2_ref[0][:, cols] * y
            o_ref[rows, cols] = x2
            ssq = ssq + jnp.sum(x2 * x2, axis=-1, keepdims=True)
        o_ref[rows, :] = o_ref[rows, :] * lax.rsqrt(ssq / D_MODEL + EPS) * gf_ref[...]


def _combine(idx_flat, lo_tab, ye_lin, x1, seq, gt2, g_final, cap, tt):
    n = x1.shape[0]
    ch = min(COMBINE_CHUNK, cap // COMBINE_NCHUNK)
    ntiles = n // tt
    per_b = _tiles_per_mod_row(gt2, n, seq, tt)
    grid_spec = pltpu.PrefetchScalarGridSpec(
        num_scalar_prefetch=2,
        grid=(ntiles,),
        in_specs=[pl.BlockSpec(memory_space=pl.ANY),
                  pl.BlockSpec((tt, D_MODEL), lambda i, a, b: (i, 0)),
                  pl.BlockSpec((1, 1, D_MODEL), lambda i, a, b: (i // per_b, 0, 0)),
                  pl.BlockSpec((1, D_MODEL), lambda i, a, b: (0, 0))],
        out_specs=pl.BlockSpec((tt, D_MODEL), lambda i, a, b: (i, 0)),
        scratch_shapes=[pltpu.VMEM((2, N_EXPERTS, COMBINE_NCHUNK * ch, ROW_CHUNKS, LANES), F32),
                        pltpu.VMEM((tt, ROW_CHUNKS, LANES), F32),
                        pltpu.SemaphoreType.DMA((2, N_EXPERTS))],
    )
    return pl.pallas_call(
        functools.partial(_combine_kernel, tt=tt, ch=ch, cap=cap, ntiles=ntiles),
        grid_spec=grid_spec,
        out_shape=jax.ShapeDtypeStruct((n, D_MODEL), F32),
        compiler_params=_cparams(("arbitrary",)),
        name="combine",
    )(idx_flat, lo_tab, ye_lin, x1, gt2, g_final)


def _rope_tables(seq):
    pos = np.arange(seq)
    row = (pos // GRID_W).astype(np.float32)
    col = (pos % GRID_W).astype(np.float32)
    npair = SWA_HEAD_DIM // 4
    inv_freq = (ROPE_BASE ** (-np.arange(npair, dtype=np.float32) / npair)).astype(np.float32)
    ar = (row[:, None] * inv_freq[None, :]).astype(np.float64)
    ac = (col[:, None] * inv_freq[None, :]).astype(np.float64)
    cos = np.concatenate([np.cos(ar), np.cos(ar), np.cos(ac), np.cos(ac)], axis=1)
    sin = np.concatenate([-np.sin(ar), np.sin(ar), -np.sin(ac), np.sin(ac)], axis=1)
    reps = SWA_KVW // SWA_HEAD_DIM
    return (jnp.asarray(np.tile(cos, (1, reps)), F32), jnp.asarray(np.tile(sin, (1, reps)), F32))


def _layer(x, mods, wts, latent, s0f, s0b, cache_k, cache_v):
    bsz, seq, _ = x.shape
    n = bsz * seq
    x2d = x.reshape(n, D_MODEL)
    sh1, sc1, gt1, sh2, sc2, gt2 = mods
    p, tail = _in_proj(x2d, seq, sc1, sh1, wts["g1"], wts["w_in_r"])
    o_gla, st_f, st_b = _gla(p, tail, bsz, seq, wts["wdf"], wts["bdf"], wts["wdb"], wts["bdb"], wts["g_gla"],
                             s0f, s0b)
    if latent:
        cos_t, sin_t = _rope_tables(seq)
        o_swa = _swa_latent(p, bsz, seq, cache_k, cache_v, wts["sink"], cos_t, sin_t)
    else:
        o_swa = _swa_context(p, bsz, seq, wts["sink"])
    merged = _branch_merge(o_gla, o_swa, p, wts["wa"], wts["wb"])
    x1, h2, logits3 = _out_proj(merged, x2d, seq, gt1, sc2, sh2, wts["g2"], wts["wo"], wts["wr2"])
    cap = CAPACITY_FACTOR * n // N_EXPERTS
    aff3, sel3, pos3 = _select(logits3, cap)
    base = jnp.concatenate([pos3[:, :, 0].astype(jnp.int32), jnp.full((1, N_EXPERTS), cap, jnp.int32)], axis=0)
    lists = _compact(base.reshape(-1), aff3, sel3, pos3, cap)
    idx_flat = lists[:cap, :N_EXPERTS].T.astype(jnp.int32).reshape(-1)
    gate_col = lists[:cap, N_EXPERTS:2 * N_EXPERTS].T.reshape(-1, 1)
    hid = _expert_up(idx_flat, h2, wts["wg"], wts["wu"], cap)
    ye = _expert_down(hid, gate_col, wts["wd"], cap)
    tt = 256
    lo_tab = base[::tt // LANES]
    y = _combine(idx_flat, lo_tab.reshape(-1), ye, x1, seq, gt2, wts["g_final"], cap, tt)
    return y.reshape(bsz, seq, D_MODEL), tail, st_f, st_b


def kernel(x_prompt, x_sample, state_gla_fwd, state_gla_bwd, cache_k, cache_v, c, c_ctx, w_mod, b_mod,
           g_norm1, w_in, w_dec_f, b_dec_f, w_dec_b, b_dec_b, g_gla, attn_sink, w_branch_a, w_branch_b,
           w_out, g_norm2, w_router, w_exp_gate, w_exp_up, w_exp_down, g_final):
    bp, sp, _ = x_prompt.shape
    bl = x_sample.shape[0]
    l = 0
    w_in_r = _reorder_w_in(w_in, l)
    zpad = jnp.zeros((LANES - 2 * GLA_RANK, GLA_KW), F32)
    wr_hi = w_router[l].astype(BF16)
    wr_lo = (w_router[l] - wr_hi.astype(F32)).astype(BF16)
    wr_pad = jnp.zeros((D_MODEL, LANES - 2 * N_EXPERTS), BF16)
    zr = jnp.zeros((GLA_RANK, GLA_KW), F32)
    wts = {
        "g1": g_norm1[l].reshape(1, D_MODEL),
        "w_in_r": w_in_r,
        "wdf": jnp.concatenate([w_dec_f[l], zr, zpad], axis=0),
        "wdb": jnp.concatenate([zr, w_dec_b[l], zpad], axis=0),
        "bdf": b_dec_f[l].reshape(1, GLA_KW),
        "bdb": b_dec_b[l].reshape(1, GLA_KW),
        "g_gla": g_gla[l].reshape(1, GLA_VW),
        "sink": attn_sink[l],
        "wa": w_branch_a[l].astype(BF16),
        "wb": w_branch_b[l].astype(BF16),
        "wo": w_out[l].astype(BF16),
        "g2": g_norm2[l].reshape(1, D_MODEL),
        "wr2": jnp.concatenate([wr_hi, wr_lo, wr_pad], axis=1),
        "wg": w_exp_gate[l],
        "wu": w_exp_up[l],
        "wd": w_exp_down[l],
        "g_final": g_final.reshape(1, D_MODEL),
    }
    cond8 = jnp.concatenate([c_ctx[None, :], c, jnp.zeros((8 - 1 - bl, D_MODEL), F32)], axis=0)
    mod = _modulation(cond8, w_mod[l], b_mod[l]).reshape(8, N_MOD, 1, D_MODEL)
    mods_ctx = tuple(mod[0:1, j] for j in range(N_MOD))
    mods_lat = tuple(mod[1:1 + bl, j] for j in range(N_MOD))

    zero_state = jnp.zeros((1, GLA_HEADS, GLA_DK, GLA_DV), F32)
    y_prompt, tail_ctx, st_f, st_b = _layer(x_prompt, mods_ctx, wts, False, zero_state, zero_state, None, None)
    ck = cache_k[:, l].reshape(bl, -1, SWA_KVW)
    cv = cache_v[:, l].reshape(bl, -1, SWA_KVW)
    y_sample, _, _, _ = _layer(x_sample, mods_lat, wts, True, state_gla_fwd[:, l], state_gla_bwd[:, l], ck, cv)

    ksl = slice(COL_SK - TAIL_COL0, COL_SK - TAIL_COL0 + SWA_KVW)
    vsl = slice(COL_SV - TAIL_COL0, COL_SV - TAIL_COL0 + SWA_KVW)
    new_k = tail_ctx[:, ksl].reshape(bp, 1, sp, SWA_KV_HEADS, SWA_HEAD_DIM)
    new_v = tail_ctx[:, vsl].reshape(bp, 1, sp, SWA_KV_HEADS, SWA_HEAD_DIM)
    return (y_prompt, y_sample, st_f[:, None], st_b[:, None], new_k, new_v)
```

```python
import functools

import jax
import jax.numpy as jnp
import numpy as np
from jax import lax
from jax.experimental import pallas as pl
from jax.experimental.pallas import tpu as pltpu

F32 = jnp.float32
BF16 = jnp.bfloat16
HIGHEST = lax.Precision.HIGHEST

D_MODEL = 2048
N_MOD = 6
EPS = 1e-6

GLA_HEADS = 4
GLA_DK = 128
GLA_DV = 256
GLA_KW = GLA_HEADS * GLA_DK
GLA_VW = GLA_HEADS * GLA_DV
GLA_RANK = 16
GLA_TAU = 16.0
GLA_CHUNK = 64

SWA_HEADS = 16
SWA_KV_HEADS = 4
SWA_GROUP = 4
SWA_HEAD_DIM = 64
SWA_QW = SWA_HEADS * SWA_HEAD_DIM
SWA_KVW = SWA_KV_HEADS * SWA_HEAD_DIM
SWA_BLOCK = 128
GRID_W = 64
ROPE_BASE = 10000.0

N_EXPERTS = 16
EXPERT_FF = D_MODEL // 2
CAPACITY_FACTOR = 2

LANES = 128
ROW_CHUNKS = D_MODEL // LANES
VMEM_LIMIT = 56 * 1024 * 1024

COL_GA = 0
COL_GB = 2048
COL_GQ = 4096
COL_GK = 4608
COL_GV = 5120
COL_GR = 6144
COL_SQ = 7168
COL_SK = 8192
COL_SV = 8448
COL_DEC = 8704
IN_PAD = 8960
IN_TN = 1280
TAIL_COL0 = COL_SK
TAIL_W = COL_DEC + LANES - COL_SK


def _cparams(sem, **kw):
    return pltpu.CompilerParams(dimension_semantics=sem, vmem_limit_bytes=VMEM_LIMIT, **kw)


def _silu(x):
    return x * (1.0 / (1.0 + jnp.exp(-x)))


def _sigmoid(x):
    return 1.0 / (1.0 + jnp.exp(-x))


SLAB = 128


def _store_row_major(ref, x):
    for tb in range(x.shape[0] // SLAB):
        rows = slice(tb * SLAB, (tb + 1) * SLAB)
        parts = jnp.stack([x[rows, s * LANES:(s + 1) * LANES] for s in range(ROW_CHUNKS)], axis=0)
        ref[rows] = pltpu.einshape("stl->tsl", parts)


def _load_row_major(ref, rows):
    xt = pltpu.einshape("tsl->stl", ref[rows])
    return [xt[s] for s in range(ROW_CHUNKS)]


def _tiles_per_mod_row(mod, n, seq, tile):
    return seq // tile if mod.shape[0] > 1 else n // tile


def _mod_kernel(c_ref, w_ref, b_ref, o_ref):
    a = _silu(c_ref[...]).astype(BF16)
    o_ref[...] = jnp.dot(a, w_ref[...].astype(BF16), preferred_element_type=F32) + b_ref[...]


def _modulation(cond8, w_mod, b_mod):
    n_out = w_mod.shape[1]
    tn = 1536
    return pl.pallas_call(
        _mod_kernel,
        grid=(n_out // tn,),
        in_specs=[pl.BlockSpec((8, D_MODEL), lambda j: (0, 0)),
                  pl.BlockSpec((D_MODEL, tn), lambda j: (0, j)),
                  pl.BlockSpec((1, tn), lambda j: (0, j))],
        out_specs=pl.BlockSpec((8, tn), lambda j: (0, j)),
        out_shape=jax.ShapeDtypeStruct((8, n_out), F32),
        compiler_params=_cparams(("arbitrary",)),
        name="modulation",
    )(cond8, w_mod, b_mod.reshape(1, n_out))


SRC_GLA_END = 2 * GLA_KW + 2 * GLA_VW
SRC_DEC_END = SRC_GLA_END + 2 * GLA_RANK
SRC_SWA_END = SRC_DEC_END + SWA_QW + 2 * SWA_KVW
SRC_END = SRC_SWA_END + 2 * D_MODEL
REORDER_TN = 256


def _reorder_kernel(wt_hbm, o_ref, buf, sem, *, layer):
    j = pl.program_id(0)
    ntiles = pl.num_programs(0)
    n_gate = (COL_GQ - COL_GA) // REORDER_TN
    n_gla = (COL_SQ - COL_GQ) // REORDER_TN
    n_swa = (COL_DEC - COL_SQ) // REORDER_TN

    def tile_copy(t, slot):
        start = jnp.where(t < n_gate, SRC_SWA_END + REORDER_TN * t,
                          jnp.where(t < n_gate + n_gla, REORDER_TN * (t - n_gate),
                                    jnp.where(t < n_gate + n_gla + n_swa,
                                              SRC_DEC_END + REORDER_TN * (t - n_gate - n_gla), SRC_GLA_END)))
        return pltpu.make_async_copy(wt_hbm.at[layer, pl.ds(pl.multiple_of(start, 8), REORDER_TN), :],
                                     buf.at[slot], sem.at[slot])

    slot = j % 2

    @pl.when(j == 0)
    def _():
        tile_copy(j, slot).start()

    @pl.when(j + 1 < ntiles)
    def _():
        tile_copy(j + 1, 1 - slot).start()

    tile_copy(j, slot).wait()
    x = buf[slot]
    row = lax.broadcasted_iota(jnp.int32, x.shape, 0)
    keep = jnp.where(j == ntiles - 1, SRC_DEC_END - SRC_GLA_END, REORDER_TN)
    o_ref[...] = jnp.where(row < keep, x, 0.0).T.astype(BF16)


def _reorder_w_in(w_all, layer):
    assert SRC_END == w_all.shape[2] and IN_PAD - COL_DEC == REORDER_TN
    assert SRC_SWA_END % 8 == 0 and SRC_DEC_END % 8 == 0 and SRC_GLA_END + REORDER_TN <= SRC_END
    wt = jnp.swapaxes(w_all, 1, 2)
    return pl.pallas_call(
        functools.partial(_reorder_kernel, layer=layer),
        grid=(IN_PAD // REORDER_TN,),
        in_specs=[pl.BlockSpec(memory_space=pl.ANY)],
        out_specs=pl.BlockSpec((D_MODEL, REORDER_TN), lambda j: (0, j)),
        out_shape=jax.ShapeDtypeStruct((D_MODEL, IN_PAD), BF16),
        scratch_shapes=[pltpu.VMEM((2, REORDER_TN, D_MODEL), F32), pltpu.SemaphoreType.DMA((2,))],
        compiler_params=_cparams(("arbitrary",)),
        name="reorder_w_in",
    )(wt)


NORM_ROWS = 64


def _in_proj_kernel(x_ref, sc_ref, sh_ref, g_ref, w_ref, o_ref, tail_ref, h_scr):
    j = pl.program_id(1)

    @pl.when(j == 0)
    def _():
        def slab(r, _):
            rows = pl.ds(pl.multiple_of(r * NORM_ROWS, NORM_ROWS), NORM_ROWS)
            x = x_ref[rows, :]
            y = x * lax.rsqrt(jnp.mean(x * x, axis=-1, keepdims=True) + EPS) * g_ref[...]
            h_scr[rows, :] = (y * (1.0 + sc_ref[0]) + sh_ref[0]).astype(BF16)
            return 0
        lax.fori_loop(0, x_ref.shape[0] // NORM_ROWS, slab, 0)

    acc = jnp.dot(h_scr[...], w_ref[...], preferred_element_type=F32)
    o_ref[...] = acc.astype(BF16)

    @pl.when(j == IN_PAD // IN_TN - 1)
    def _():
        first = TAIL_COL0 - (IN_PAD - IN_TN)
        tail_ref[...] = acc[:, first:first + TAIL_W]


def _in_proj(x2d, seq, sc, sh, g, w_r):
    n = x2d.shape[0]
    tm = min(1024, seq if sc.shape[0] > 1 else n)
    per_b = _tiles_per_mod_row(sc, n, seq, tm)
    return pl.pallas_call(
        _in_proj_kernel,
        grid=(n // tm, IN_PAD // IN_TN),
        in_specs=[pl.BlockSpec((tm, D_MODEL), lambda i, j: (i, 0)),
                  pl.BlockSpec((1, 1, D_MODEL), lambda i, j: (i // per_b, 0, 0)),
                  pl.BlockSpec((1, 1, D_MODEL), lambda i, j: (i // per_b, 0, 0)),
                  pl.BlockSpec((1, D_MODEL), lambda i, j: (0, 0)),
                  pl.BlockSpec((D_MODEL, IN_TN), lambda i, j: (0, j))],
        out_specs=[pl.BlockSpec((tm, IN_TN), lambda i, j: (i, j)),
                   pl.BlockSpec((tm, TAIL_W), lambda i, j: (i, 0))],
        out_shape=[jax.ShapeDtypeStruct((n, IN_PAD), BF16),
                   jax.ShapeDtypeStruct((n, TAIL_W), F32)],
        scratch_shapes=[pltpu.VMEM((tm, D_MODEL), BF16)],
        compiler_params=_cparams(("parallel", "arbitrary")),
        name="in_proj",
    )(x2d, sc, sh, g, w_r)


GLA_BLK = 256
GLA_HPS = 4


def _tn_dot(a, b, precision=None):
    return lax.dot_general(a, b, (((0,), (0,)), ((), ())), precision=precision,
                           preferred_element_type=F32)


def _nt_dot(a, b, precision=None):
    return lax.dot_general(a, b, (((1,), (1,)), ((), ())), precision=precision,
                           preferred_element_type=F32)


def _split_bf16(x, parts):
    out = []
    for _ in range(parts):
        piece = x.astype(BF16)
        out.append(piece)
        x = x - piece.astype(F32)
    return out


def _dot3(x, w):
    xh, xl = _split_bf16(x, 2)
    wh, wl = _split_bf16(w, 2)
    d = functools.partial(jnp.dot, preferred_element_type=F32)
    return d(xh, wh) + d(xl, wh) + d(xh, wl)


def _gla_kernel(q_ref, k_ref, v_ref, r_ref, dec_ref, wdf_ref, bdf_ref, wdb_ref, bdb_ref, gg_ref,
                s0f_ref, s0b_ref, o_ref, stf_ref, stb_ref, state_t, of_scr, *, nblk):
    i = pl.program_id(2)
    c = GLA_CHUNK
    nch = GLA_BLK // c
    ri = lax.broadcasted_iota(jnp.int32, (GLA_BLK, GLA_BLK), 0)
    ci = lax.broadcasted_iota(jnp.int32, (GLA_BLK, GLA_BLK), 1)
    same_chunk = (ri // c) == (ci // c)

    @pl.when(i == 0)
    def _():
        for h in range(GLA_HPS):
            state_t[h] = s0f_ref[0, h].T

    @pl.when(i == nblk)
    def _():
        for h in range(GLA_HPS):
            state_t[h] = s0b_ref[0, h].T

    def run(fwd):
        blk = i if fwd else 2 * nblk - 1 - i
        keep = same_chunk & ((ci <= ri) if fwd else (ci >= ri))
        w_ref, b_ref = (wdf_ref, bdf_ref) if fwd else (wdb_ref, bdb_ref)
        logit = _dot3(dec_ref[...], w_ref[...]) + b_ref[...]
        g = (jnp.minimum(logit, 0.0) - jnp.log(1.0 + jnp.exp(-jnp.abs(logit)))) / GLA_TAU
        tri = keep.astype(BF16)
        b = sum(jnp.dot(tri, piece, preferred_element_type=F32) for piece in _split_bf16(g, 3))
        edge = c - 1 if fwd else 0
        tots = [b[ch * c + edge:ch * c + edge + 1, :] for ch in range(nch)]
        totb = jnp.concatenate([jnp.broadcast_to(t, (c, t.shape[1])) for t in tots], axis=0)
        q_in = (q_ref[...].astype(F32) * (GLA_DK ** -0.5) * jnp.exp(b)).astype(BF16)
        k = k_ref[...].astype(F32)
        k_in = (k * jnp.exp(-b)).astype(BF16)
        k_st = (k * jnp.exp(totb - b)).astype(BF16)
        srow = pl.multiple_of(blk * GLA_BLK, GLA_BLK)
        for h in range(GLA_HPS):
            ks = slice(h * GLA_DK, (h + 1) * GLA_DK)
            vs = slice(h * GLA_DV, (h + 1) * GLA_DV)
            v = v_ref[:, vs].astype(BF16)
            a = jnp.where(keep, _nt_dot(q_in[:, ks], k_in[:, ks]), 0.0).astype(BF16)
            o_intra = jnp.dot(a, v, preferred_element_type=F32)
            st = state_t[h]
            o_inter = [None] * nch
            for cc in range(nch):
                ch = cc if fwd else nch - 1 - cc
                rows = slice(ch * c, (ch + 1) * c)
                o_inter[ch] = _nt_dot(q_in[rows, ks], st.astype(BF16))
                st = st * jnp.exp(tots[ch][:, ks]) + _tn_dot(v[rows, :], k_st[rows, ks])
            state_t[h] = st
            o = o_intra + jnp.concatenate(o_inter, axis=0)
            if fwd:
                of_scr[pl.ds(srow, GLA_BLK), vs] = o
            else:
                o = o + of_scr[pl.ds(srow, GLA_BLK), vs]
                o = o * lax.rsqrt(jnp.mean(o * o, axis=-1, keepdims=True) + EPS) * gg_ref[:, vs]
                o_ref[:, vs] = (o * _silu(r_ref[:, vs].astype(F32))).astype(BF16)

    @pl.when(i < nblk)
    def _():
        run(True)

    @pl.when(i >= nblk)
    def _():
        run(False)

    @pl.when(i == nblk - 1)
    def _():
        for h in range(GLA_HPS):
            stf_ref[0, h] = state_t[h].T

    @pl.when(i == 2 * nblk - 1)
    def _():
        for h in range(GLA_HPS):
            stb_ref[0, h] = state_t[h].T


def _gla(p, dec, bsz, seq, wdf, bdf, wdb, bdb, g_gla, s0f, s0b):
    n = p.shape[0]
    nblk = seq // GLA_BLK
    kw = GLA_HPS * GLA_DK
    vw = GLA_HPS * GLA_DV

    def rb(b, i):
        return b * nblk + jnp.where(i < nblk, i, 2 * nblk - 1 - i)

    def orb(b, i):
        return b * nblk + jnp.where(i < nblk, nblk - 1, 2 * nblk - 1 - i)

    st_spec = pl.BlockSpec((1, GLA_HPS, GLA_DK, GLA_DV), lambda b, h, i: (b, h, 0, 0))
    s0_spec = pl.BlockSpec((1, GLA_HPS, GLA_DK, GLA_DV),
                           lambda b, h, i: (b if s0f.shape[0] > 1 else 0, h, 0, 0))
    st_shape = jax.ShapeDtypeStruct((bsz, GLA_HEADS, GLA_DK, GLA_DV), F32)
    return pl.pallas_call(
        functools.partial(_gla_kernel, nblk=nblk),
        grid=(bsz, GLA_HEADS // GLA_HPS, 2 * nblk),
        in_specs=[pl.BlockSpec((GLA_BLK, kw), lambda b, h, i: (rb(b, i), COL_GQ // kw + h)),
                  pl.BlockSpec((GLA_BLK, kw), lambda b, h, i: (rb(b, i), COL_GK // kw + h)),
                  pl.BlockSpec((GLA_BLK, vw), lambda b, h, i: (rb(b, i), COL_GV // vw + h)),
                  pl.BlockSpec((GLA_BLK, vw), lambda b, h, i: (rb(b, i), COL_GR // vw + h)),
                  pl.BlockSpec((GLA_BLK, LANES), lambda b, h, i: (rb(b, i), (COL_DEC - TAIL_COL0) // LANES)),
                  pl.BlockSpec((LANES, kw), lambda b, h, i: (0, h)),
                  pl.BlockSpec((1, kw), lambda b, h, i: (0, h)),
                  pl.BlockSpec((LANES, kw), lambda b, h, i: (0, h)),
                  pl.BlockSpec((1, kw), lambda b, h, i: (0, h)),
                  pl.BlockSpec((1, vw), lambda b, h, i: (0, h)),
                  s0_spec, s0_spec],
        out_specs=[pl.BlockSpec((GLA_BLK, vw), lambda b, h, i: (orb(b, i), h)), st_spec, st_spec],
        out_shape=[jax.ShapeDtypeStruct((n, GLA_VW), BF16), st_shape, st_shape],
        scratch_shapes=[pltpu.VMEM((GLA_HPS, GLA_DV, GLA_DK), F32), pltpu.VMEM((seq, vw), F32)],
        compiler_params=_cparams(("parallel", "parallel", "arbitrary")),
        name="gla",
    )(p, p, p, p, dec, wdf, bdf, wdb, bdb, g_gla, s0f, s0b)


def _sink_column(sink_ref, kvh, rows_per_head):
    r = lax.broadcasted_iota(jnp.int32, (SWA_GROUP * rows_per_head, 1), 0)
    col = jnp.zeros((SWA_GROUP * rows_per_head, 1), F32)
    for g in range(SWA_GROUP):
        col = jnp.where(r // rows_per_head == g, sink_ref[kvh * SWA_GROUP + g], col)
    return col


def _rope(x, cos, sin):
    outs = []
    for cb in range(x.shape[1] // LANES):
        sl = slice(cb * LANES, (cb + 1) * LANES)
        xc = x[:, sl]
        lane = lax.broadcasted_iota(jnp.int32, xc.shape, 1)
        sw = jnp.where(lane % 32 < 16, pltpu.roll(xc, LANES - 16, 1), pltpu.roll(xc, 16, 1))
        outs.append(xc * cos[:, sl] + sw * sin[:, sl])
    return jnp.concatenate(outs, axis=1)


def _swa_lat_kernel(sink_ref, q_ref, kp_ref, kc_ref, kn_ref, vp_ref, vc_ref, vn_ref, ck_ref, cv_ref,
                    cp_ref, cc_ref, cn_ref, sp_ref, sc_ref, sn_ref, o_ref, *, nb):
    n = pl.program_id(1)
    blk = SWA_BLOCK
    hd = SWA_HEAD_DIM
    cos_c, sin_c = cc_ref[...], sc_ref[...]
    kband = jnp.concatenate([_rope(kp_ref[...].astype(F32), cp_ref[...], sp_ref[...]),
                             _rope(kc_ref[...].astype(F32), cos_c, sin_c),
                             _rope(kn_ref[...].astype(F32), cn_ref[...], sn_ref[...])], axis=0).astype(BF16)
    vband = jnp.concatenate([vp_ref[...], vc_ref[...], vn_ref[...]], axis=0).astype(BF16)
    ck = ck_ref[0].astype(BF16)
    cv = cv_ref[0].astype(BF16)
    qi = lax.broadcasted_iota(jnp.int32, (blk, 3 * blk), 0)
    kj = lax.broadcasted_iota(jnp.int32, (blk, 3 * blk), 1)
    k_abs = kj + (n - 1) * blk
    mask = (jnp.abs(kj - blk - qi) <= SWA_BLOCK) & (k_abs >= 0) & (k_abs < nb * blk)
    bias = jnp.concatenate([jnp.where(mask, 0.0, -1e30)] * SWA_GROUP, axis=0)
    nctx = ck.shape[0]
    kall = jnp.concatenate([ck, kband], axis=0)
    vall = jnp.concatenate([cv, vband], axis=0)
    for kvh in range(SWA_KV_HEADS):
        ks = slice(kvh * hd, (kvh + 1) * hd)
        qr = _rope(q_ref[:, kvh * SWA_KVW:(kvh + 1) * SWA_KVW].astype(F32), cos_c, sin_c) * (hd ** -0.5)
        qg = jnp.concatenate([qr[:, g * hd:(g + 1) * hd] for g in range(SWA_GROUP)],
                             axis=0).astype(BF16)
        s = _nt_dot(qg, kall[:, ks])
        s = jnp.concatenate([s[:, :nctx], s[:, nctx:] + bias], axis=1)
        sink = _sink_column(sink_ref, kvh, blk)
        m = jnp.maximum(jnp.max(s, axis=-1, keepdims=True), sink)
        pr = jnp.exp(s - m)
        den = jnp.exp(sink - m) + jnp.sum(pr, axis=-1, keepdims=True)
        o = jnp.dot(pr.astype(BF16), vall[:, ks], preferred_element_type=F32) / den
        for g in range(SWA_GROUP):
            cb = (kvh * SWA_GROUP + g) * hd
            o_ref[:, cb:cb + hd] = o[g * blk:(g + 1) * blk, :].astype(BF16)


def _swa_latent(p, bsz, seq, cache_k, cache_v, sink, cos_t, sin_t):
    n = p.shape[0]
    nb = seq // SWA_BLOCK
    kcol = COL_SK // SWA_KVW
    vcol = COL_SV // SWA_KVW
    prev = lambda i: jnp.maximum(i - 1, 0)
    nxt = lambda i: jnp.minimum(i + 1, nb - 1)

    def pspec(col, f):
        return pl.BlockSpec((SWA_BLOCK, SWA_KVW), lambda b, i, s: (b * nb + f(i), col))

    def tspec(f):
        return pl.BlockSpec((SWA_BLOCK, SWA_KVW), lambda b, i, s: (f(i), 0))

    same = lambda i: i
    cspec = pl.BlockSpec((1, cache_k.shape[1], SWA_KVW), lambda b, i, s: (b, 0, 0))
    grid_spec = pltpu.PrefetchScalarGridSpec(
        num_scalar_prefetch=1,
        grid=(bsz, nb),
        in_specs=[pl.BlockSpec((SWA_BLOCK, SWA_QW), lambda b, i, s: (b * nb + i, COL_SQ // SWA_QW)),
                  pspec(kcol, prev), pspec(kcol, same), pspec(kcol, nxt),
                  pspec(vcol, prev), pspec(vcol, same), pspec(vcol, nxt),
                  cspec, cspec,
                  tspec(prev), tspec(same), tspec(nxt), tspec(prev), tspec(same), tspec(nxt)],
        out_specs=pl.BlockSpec((SWA_BLOCK, SWA_QW), lambda b, i, s: (b * nb + i, 0)),
    )
    return pl.pallas_call(
        functools.partial(_swa_lat_kernel, nb=nb),
        grid_spec=grid_spec,
        out_shape=jax.ShapeDtypeStruct((n, SWA_QW), BF16),
        compiler_params=_cparams(("parallel", "arbitrary")),
        name="swa_latent",
    )(sink, p, p, p, p, p, p, p, cache_k, cache_v, cos_t, cos_t, cos_t, sin_t, sin_t, sin_t)


def _swa_ctx_kernel(sink_ref, q_ref, k_ref, v_ref, o_ref):
    s = q_ref.shape[0]
    hd = SWA_HEAD_DIM
    kb = k_ref[...].astype(BF16)
    vb = v_ref[...].astype(BF16)
    for kvh in range(SWA_KV_HEADS):
        ks = slice(kvh * hd, (kvh + 1) * hd)
        qg = jnp.concatenate([q_ref[:, (kvh * SWA_GROUP + g) * hd:(kvh * SWA_GROUP + g + 1) * hd]
                              for g in range(SWA_GROUP)], axis=0)
        qg = (qg.astype(F32) * (hd ** -0.5)).astype(BF16)
        sc = _nt_dot(qg, kb[:, ks])
        sink = _sink_column(sink_ref, kvh, s)
        m = jnp.maximum(jnp.max(sc, axis=-1, keepdims=True), sink)
        pr = jnp.exp(sc - m)
        den = jnp.exp(sink - m) + jnp.sum(pr, axis=-1, keepdims=True)
        o = jnp.dot(pr.astype(BF16), vb[:, ks], preferred_element_type=F32) / den
        for g in range(SWA_GROUP):
            cb = (kvh * SWA_GROUP + g) * hd
            o_ref[:, cb:cb + hd] = o[g * s:(g + 1) * s, :].astype(BF16)


def _swa_context(p, bsz, seq, sink):
    n = p.shape[0]
    grid_spec = pltpu.PrefetchScalarGridSpec(
        num_scalar_prefetch=1,
        grid=(bsz,),
        in_specs=[pl.BlockSpec((seq, SWA_QW), lambda b, s: (b, COL_SQ // SWA_QW)),
                  pl.BlockSpec((seq, SWA_KVW), lambda b, s: (b, COL_SK // SWA_KVW)),
                  pl.BlockSpec((seq, SWA_KVW), lambda b, s: (b, COL_SV // SWA_KVW))],
        out_specs=pl.BlockSpec((seq, SWA_QW), lambda b, s: (b, 0)),
    )
    return pl.pallas_call(
        _swa_ctx_kernel,
        grid_spec=grid_spec,
        out_shape=jax.ShapeDtypeStruct((n, SWA_QW), BF16),
        compiler_params=_cparams(("parallel",)),
        name="swa_context",
    )(sink, p, p, p)


MERGE_TM = 512


BRANCH_TM = 1024
BRANCH_COLS = 512


def _branch_kernel(og_ref, os_ref, ga_ref, gb_ref, wa_ref, wb_ref, m_ref):
    for c in range(D_MODEL // BRANCH_COLS):
        cols = slice(c * BRANCH_COLS, (c + 1) * BRANCH_COLS)
        a = jnp.dot(og_ref[...], wa_ref[:, cols], preferred_element_type=F32)
        b = jnp.dot(os_ref[...], wb_ref[:, cols], preferred_element_type=F32)
        m_ref[:, cols] = (_sigmoid(ga_ref[:, cols].astype(F32)) * a
                          + _sigmoid(gb_ref[:, cols].astype(F32)) * b).astype(BF16)


def _branch_merge(o_gla, o_swa, p, wa, wb):
    n = o_gla.shape[0]
    tm = BRANCH_TM
    once = pl.Buffered(1)
    return pl.pallas_call(
        _branch_kernel,
        grid=(n // tm,),
        in_specs=[pl.BlockSpec((tm, GLA_VW), lambda i: (i, 0)),
                  pl.BlockSpec((tm, SWA_QW), lambda i: (i, 0)),
                  pl.BlockSpec((tm, D_MODEL), lambda i: (i, COL_GA // D_MODEL)),
                  pl.BlockSpec((tm, D_MODEL), lambda i: (i, COL_GB // D_MODEL)),
                  pl.BlockSpec((GLA_VW, D_MODEL), lambda i: (0, 0), pipeline_mode=once),
                  pl.BlockSpec((SWA_QW, D_MODEL), lambda i: (0, 0), pipeline_mode=once)],
        out_specs=pl.BlockSpec((tm, D_MODEL), lambda i: (i, 0)),
        out_shape=jax.ShapeDtypeStruct((n, D_MODEL), BF16),
        compiler_params=_cparams(("parallel",)),
        name="branch_merge",
    )(o_gla, o_swa, p, p, wa, wb)


def _out_kernel(m_ref, x_ref, gt1_ref, sc2_ref, sh2_ref, g2_ref, wo_ref, wr2_ref,
                x1_ref, h2_ref, lg_ref):
    x1 = x_ref[...] + gt1_ref[0] * jnp.dot(m_ref[...], wo_ref[...], preferred_element_type=F32)
    x1_ref[...] = x1
    y = x1 * lax.rsqrt(jnp.mean(x1 * x1, axis=-1, keepdims=True) + EPS) * g2_ref[...]
    h2 = y * (1.0 + sc2_ref[0]) + sh2_ref[0]
    _store_row_major(h2_ref, h2)
    l2 = jnp.dot(h2.astype(BF16), wr2_ref[...], preferred_element_type=F32)
    lt = l2.T
    lt = lt[0:N_EXPERTS, :] + lt[N_EXPERTS:2 * N_EXPERTS, :]
    for cb in range(lt.shape[1] // LANES):
        lg_ref[cb] = lt[:, cb * LANES:(cb + 1) * LANES]


def _out_proj(merged, x2d, seq, gt1, sc2, sh2, g2, wo, wr2):
    n = x2d.shape[0]
    tm = MERGE_TM
    per_b = _tiles_per_mod_row(gt1, n, seq, tm)
    mod_spec = pl.BlockSpec((1, 1, D_MODEL), lambda i: (i // per_b, 0, 0))
    once = pl.Buffered(1)
    return pl.pallas_call(
        _out_kernel,
        grid=(n // tm,),
        in_specs=[pl.BlockSpec((tm, D_MODEL), lambda i: (i, 0)),
                  pl.BlockSpec((tm, D_MODEL), lambda i: (i, 0)),
                  mod_spec, mod_spec, mod_spec,
                  pl.BlockSpec((1, D_MODEL), lambda i: (0, 0)),
                  pl.BlockSpec((D_MODEL, D_MODEL), lambda i: (0, 0), pipeline_mode=once),
                  pl.BlockSpec((D_MODEL, LANES), lambda i: (0, 0))],
        out_specs=[pl.BlockSpec((tm, D_MODEL), lambda i: (i, 0)),
                   pl.BlockSpec((tm, ROW_CHUNKS, LANES), lambda i: (i, 0, 0)),
                   pl.BlockSpec((tm // LANES, N_EXPERTS, LANES), lambda i: (i, 0, 0))],
        out_shape=[jax.ShapeDtypeStruct((n, D_MODEL), F32),
                   jax.ShapeDtypeStruct((n, ROW_CHUNKS, LANES), F32),
                   jax.ShapeDtypeStruct((n // LANES, N_EXPERTS, LANES), F32)],
        compiler_params=_cparams(("parallel",)),
        name="out_proj",
    )(merged, x2d, gt1, sc2, sh2, g2, wo, wr2)


def _select_kernel(lg_ref, aff_ref, sel_ref, pos_ref, *, cap):
    nblk = lg_ref.shape[0]
    lg = lg_ref[...]
    ex = jnp.exp(lg - jnp.max(lg, axis=1, keepdims=True))
    aff = ex / jnp.sum(ex, axis=1, keepdims=True)
    aff_ref[...] = aff
    bits = lax.bitcast_convert_type(aff, jnp.int32)

    def count(pred):
        c = jnp.sum(jnp.where(pred, 1.0, 0.0), axis=0, keepdims=True)
        return jnp.sum(c, axis=2, keepdims=True)

    def bit_step(t, cur):
        cand = cur | jnp.left_shift(jnp.int32(1), 30 - t)
        return jnp.where(count(bits >= cand) >= cap, cand, cur)

    thr = lax.fori_loop(0, 31, bit_step, jnp.zeros((1, N_EXPERTS, 1), jnp.int32))
    need = (cap - count(bits > thr))[0]
    thr2 = thr[0]
    ri = lax.broadcasted_iota(jnp.int32, (LANES, LANES), 0)
    ci = lax.broadcasted_iota(jnp.int32, (LANES, LANES), 1)
    upper = (ri <= ci).astype(BF16)

    def blk_step(b, carry):
        run_eq, run_sel = carry
        bb = lax.bitcast_convert_type(aff_ref[b], jnp.int32)
        eq = (bb == thr2).astype(F32)
        eq_excl = jnp.dot(eq.astype(BF16), upper, preferred_element_type=F32) - eq + run_eq
        sel = jnp.where((bb > thr2) | ((eq > 0) & (eq_excl < need)), 1.0, 0.0)
        sel_ref[b] = sel
        pos_ref[b] = jnp.dot(sel.astype(BF16), upper, preferred_element_type=F32) - sel + run_sel
        return (run_eq + jnp.sum(eq, axis=1, keepdims=True),
                run_sel + jnp.sum(sel, axis=1, keepdims=True))

    zero = jnp.zeros((N_EXPERTS, 1), F32)
    lax.fori_loop(0, nblk, blk_step, (zero, zero))


def _select(logits3, cap):
    shp = jax.ShapeDtypeStruct(logits3.shape, F32)
    return pl.pallas_call(
        functools.partial(_select_kernel, cap=float(cap)),
        out_shape=[shp, shp, shp],
        compiler_params=pltpu.CompilerParams(vmem_limit_bytes=VMEM_LIMIT),
        name="select",
    )(logits3)


COMPACT_WIN = 136


def _compact_kernel(base_ref, aff_ref, sel_ref, pos_ref, list_ref):
    nblk = aff_ref.shape[0]
    list_ref[...] = jnp.zeros_like(list_ref)
    slot = lax.broadcasted_iota(jnp.int32, (COMPACT_WIN, LANES), 0).astype(F32)
    lane = lax.broadcasted_iota(jnp.int32, (COMPACT_WIN, LANES), 1)

    def blk_step(b, _):
        aff = aff_ref[b]
        pos = jnp.where(sel_ref[b] > 0, pos_ref[b], -1.0)
        tok = (lane + b * LANES).astype(F32)
        for e in range(N_EXPERTS):
            base8 = pl.multiple_of((base_ref[b * N_EXPERTS + e] >> 3) << 3, 8)
            hit = pos[e:e + 1, :] == slot + base8.astype(F32)
            ic = jnp.sum(jnp.where(hit, tok, 0.0), axis=1, keepdims=True)
            gc = jnp.sum(jnp.where(hit, aff[e:e + 1, :], 0.0), axis=1, keepdims=True)
            rows = pl.ds(base8, COMPACT_WIN)
            list_ref[rows, :] += jnp.where(lane == e, ic, jnp.where(lane == N_EXPERTS + e, gc, 0.0))
        return 0

    lax.fori_loop(0, nblk, blk_step, 0)


def _compact(base, aff3, sel3, pos3, cap):
    cp = cap + 2 * LANES
    full = pl.BlockSpec(aff3.shape, lambda i, s: (0, 0, 0))
    ospec = pl.BlockSpec((cp, LANES), lambda i, s: (0, 0))
    oshape = jax.ShapeDtypeStruct((cp, LANES), F32)
    grid_spec = pltpu.PrefetchScalarGridSpec(
        num_scalar_prefetch=1, grid=(1,), in_specs=[full, full, full], out_specs=ospec)
    return pl.pallas_call(
        _compact_kernel,
        grid_spec=grid_spec,
        out_shape=oshape,
        compiler_params=_cparams(("arbitrary",)),
        name="compact",
    )(base, aff3, sel3, pos3)


FFN_COLS = 256


def _up_kernel(idx_ref, h2_hbm, wg_ref, wu_ref, hid_ref, gbuf, xs, sem, *, tm, total):
    step = pl.program_id(0) * pl.num_programs(1) + pl.program_id(1)

    def row_copy(step_, i):
        return pltpu.make_async_copy(h2_hbm.at[idx_ref[step_ * tm + i]], gbuf.at[i], sem)

    def wait_rows():
        pltpu.make_async_copy(h2_hbm.at[pl.ds(0, tm)], gbuf, sem).wait()

    @pl.when(step == 0)
    def _():
        def body(i, _):
            row_copy(step, i).start()
            return 0
        lax.fori_loop(0, tm, body, 0)

    wait_rows()
    slab = min(SLAB, tm)
    for tb in range(tm // slab):
        rows = slice(tb * slab, (tb + 1) * slab)
        for s, chunk in enumerate(_load_row_major(gbuf, rows)):
            xs[rows, s * LANES:(s + 1) * LANES] = chunk.astype(BF16)

    nxt = jnp.minimum(step + 1, total - 1)
    for i in range(tm):
        row_copy(nxt, i).start(priority=i % 2)

    for f in range(EXPERT_FF // FFN_COLS):
        cols = slice(f * FFN_COLS, (f + 1) * FFN_COLS)
        hg = jnp.dot(xs[...], wg_ref[0, :, cols].astype(BF16), preferred_element_type=F32)
        hu = jnp.dot(xs[...], wu_ref[0, :, cols].astype(BF16), preferred_element_type=F32)
        hid_ref[:, cols] = (_silu(hg) * hu).astype(BF16)

    @pl.when(step == total - 1)
    def _():
        wait_rows()


def _expert_up(idx_flat, h2_rm, wg, wu, cap):
    tm = min(1024, cap)
    nr = cap // tm
    grid_spec = pltpu.PrefetchScalarGridSpec(
        num_scalar_prefetch=1,
        grid=(N_EXPERTS, nr),
        in_specs=[pl.BlockSpec(memory_space=pl.ANY),
                  pl.BlockSpec((1, D_MODEL, EXPERT_FF), lambda e, r, s: (e, 0, 0)),
                  pl.BlockSpec((1, D_MODEL, EXPERT_FF), lambda e, r, s: (e, 0, 0))],
        out_specs=pl.BlockSpec((tm, EXPERT_FF), lambda e, r, s: (e * nr + r, 0)),
        scratch_shapes=[pltpu.VMEM((tm, ROW_CHUNKS, LANES), F32),
                        pltpu.VMEM((tm, D_MODEL), BF16),
                        pltpu.SemaphoreType.DMA(())],
    )
    return pl.pallas_call(
        functools.partial(_up_kernel, tm=tm, total=N_EXPERTS * nr),
        grid_spec=grid_spec,
        out_shape=jax.ShapeDtypeStruct((N_EXPERTS * cap, EXPERT_FF), BF16),
        compiler_params=_cparams(("arbitrary", "arbitrary")),
        name="expert_up",
    )(idx_flat, h2_rm, wg, wu)


def _down_kernel(hid_ref, gate_ref, wd_ref, o_ref, ybuf):
    gate = gate_ref[...]
    for c in range(D_MODEL // FFN_COLS):
        cols = slice(c * FFN_COLS, (c + 1) * FFN_COLS)
        ybuf[:, cols] = gate * jnp.dot(hid_ref[...], wd_ref[0, :, cols].astype(BF16),
                                       preferred_element_type=F32)
    tm = ybuf.shape[0]
    slab = min(SLAB, tm)
    for tb in range(tm // slab):
        rows = slice(tb * slab, (tb + 1) * slab)
        parts = jnp.stack([ybuf[rows, s * LANES:(s + 1) * LANES] for s in range(ROW_CHUNKS)], axis=0)
        o_ref[rows] = pltpu.einshape("stl->tsl", parts)


def _expert_down(hid, gate_col, wd, cap):
    tm = min(1024, cap)
    nr = cap // tm
    return pl.pallas_call(
        _down_kernel,
        grid=(N_EXPERTS, nr),
        in_specs=[pl.BlockSpec((tm, EXPERT_FF), lambda e, r: (e * nr + r, 0)),
                  pl.BlockSpec((tm, 1), lambda e, r: (e * nr + r, 0)),
                  pl.BlockSpec((1, EXPERT_FF, D_MODEL), lambda e, r: (e, 0, 0))],
        out_specs=pl.BlockSpec((tm, ROW_CHUNKS, LANES), lambda e, r: (e * nr + r, 0, 0)),
        out_shape=jax.ShapeDtypeStruct((N_EXPERTS * cap, ROW_CHUNKS, LANES), F32),
        scratch_shapes=[pltpu.VMEM((tm, D_MODEL), F32)],
        compiler_params=_cparams(("parallel", "arbitrary")),
        name="expert_down",
    )(hid, gate_col, wd)


COMBINE_UNROLL = 4
COMBINE_CHUNK = 32
COMBINE_NCHUNK = 3


def _combine_kernel(idx_ref, lo_ref, ye_hbm, x1_ref, gt2_ref, gf_ref, o_ref, stage, acc, sem,
                    *, tt, ch, cap, ntiles):
    tile = pl.program_id(0)
    slot = tile % 2

    window = COMBINE_NCHUNK * ch

    def chunk_start(first_row):
        return jnp.minimum(first_row, cap - window)

    def chunk_copy(slot_, e, start, c):
        return pltpu.make_async_copy(ye_hbm.at[pl.ds(e * cap + start + c * ch, ch)],
                                     stage.at[slot_, e, pl.ds(c * ch, ch)],
                                     sem.at[slot_, e])

    def for_needed_chunks(tile_, e, fn):
        lo = lo_ref[tile_ * N_EXPERTS + e]
        hi = lo_ref[(tile_ + 1) * N_EXPERTS + e]
        start = chunk_start(lo)
        for c in range(COMBINE_NCHUNK):
            @pl.when((start + c * ch < hi) & (start + (c + 1) * ch > lo))
            def _():
                fn(start, c)

    def issue(tile_, slot_):
        for e in range(N_EXPERTS):
            for_needed_chunks(tile_, e, lambda start, c: chunk_copy(slot_, e, start, c).start())

    @pl.when(tile == 0)
    def _():
        issue(tile, slot)

    @pl.when(tile + 1 < ntiles)
    def _():
        issue(tile + 1, 1 - slot)

    acc[...] = jnp.zeros_like(acc)

    def add_rows(e, start, r0, r1):
        def group(r, width):
            dsts, vals = [], []
            rows = stage[slot, e, pl.ds(r - start, width)]
            for u in range(width):
                dst = idx_ref[e * cap + r + u] - tile * tt
                dsts.append(dst)
                vals.append(acc[dst] + rows[u])
            for dst, val in zip(dsts, vals):
                acc[dst] = val

        ngroups = (r1 - r0) // COMBINE_UNROLL

        def body_group(j, _):
            group(r0 + j * COMBINE_UNROLL, COMBINE_UNROLL)
            return 0
        lax.fori_loop(0, ngroups, body_group, 0)

        def body_one(r, _):
            group(r, 1)
            return 0
        lax.fori_loop(r0 + ngroups * COMBINE_UNROLL, r1, body_one, 0)

    for e in range(N_EXPERTS):
        lo = lo_ref[tile * N_EXPERTS + e]
        hi = lo_ref[(tile + 1) * N_EXPERTS + e]
        start = chunk_start(lo)
        for_needed_chunks(tile, e, lambda start_, c: chunk_copy(slot, e, start_, c).wait())
        first_end = jnp.minimum(hi, start + window)
        add_rows(e, start, lo, first_end)

        def more(r0):
            st = jnp.minimum(r0, cap - ch)
            cp = chunk_copy(slot, e, st, 0)
            cp.start()
            cp.wait()
            r1 = jnp.minimum(hi, st + ch)
            add_rows(e, st, r0, r1)
            return r1
        lax.while_loop(lambda r0: r0 < hi, more, first_end)

    for tb in range(tt // SLAB):
        rows = slice(tb * SLAB, (tb + 1) * SLAB)
        ssq = jnp.zeros((SLAB, 1), F32)
        for s, y in enumerate(_load_row_major(acc, rows)):
            cols = slice(s * LANES, (s + 1) * LANES)
            x2 = x1_ref[rows, cols] + gt2_ref[0][:, cols] * y
            o_ref[rows, cols] = x2
            ssq = ssq + jnp.sum(x2 * x2, axis=-1, keepdims=True)
        o_ref[rows, :] = o_ref[rows, :] * lax.rsqrt(ssq / D_MODEL + EPS) * gf_ref[...]


def _combine(idx_flat, lo_tab, ye_lin, x1, seq, gt2, g_final, cap, tt):
    n = x1.shape[0]
    ch = min(COMBINE_CHUNK, cap // COMBINE_NCHUNK)
    ntiles = n // tt
    per_b = _tiles_per_mod_row(gt2, n, seq, tt)
    grid_spec = pltpu.PrefetchScalarGridSpec(
        num_scalar_prefetch=2,
        grid=(ntiles,),
        in_specs=[pl.BlockSpec(memory_space=pl.ANY),
                  pl.BlockSpec((tt, D_MODEL), lambda i, a, b: (i, 0)),
                  pl.BlockSpec((1, 1, D_MODEL), lambda i, a, b: (i // per_b, 0, 0)),
                  pl.BlockSpec((1, D_MODEL), lambda i, a, b: (0, 0))],
        out_specs=pl.BlockSpec((tt, D_MODEL), lambda i, a, b: (i, 0)),
        scratch_shapes=[pltpu.VMEM((2, N_EXPERTS, COMBINE_NCHUNK * ch, ROW_CHUNKS, LANES), F32),
                        pltpu.VMEM((tt, ROW_CHUNKS, LANES), F32),
                        pltpu.SemaphoreType.DMA((2, N_EXPERTS))],
    )
    return pl.pallas_call(
        functools.partial(_combine_kernel, tt=tt, ch=ch, cap=cap, ntiles=ntiles),
        grid_spec=grid_spec,
        out_shape=jax.ShapeDtypeStruct((n, D_MODEL), F32),
        compiler_params=_cparams(("arbitrary",)),
        name="combine",
    )(idx_flat, lo_tab, ye_lin, x1, gt2, g_final)


def _rope_tables(seq):
    pos = np.arange(seq)
    row = (pos // GRID_W).astype(np.float32)
    col = (pos % GRID_W).astype(np.float32)
    npair = SWA_HEAD_DIM // 4
    inv_freq = (ROPE_BASE ** (-np.arange(npair, dtype=np.float32) / npair)).astype(np.float32)
    ar = (row[:, None] * inv_freq[None, :]).astype(np.float64)
    ac = (col[:, None] * inv_freq[None, :]).astype(np.float64)
    cos = np.concatenate([np.cos(ar), np.cos(ar), np.cos(ac), np.cos(ac)], axis=1)
    sin = np.concatenate([-np.sin(ar), np.sin(ar), -np.sin(ac), np.sin(ac)], axis=1)
    reps = SWA_KVW // SWA_HEAD_DIM
    return (jnp.asarray(np.tile(cos, (1, reps)), F32), jnp.asarray(np.tile(sin, (1, reps)), F32))


def _layer(x, mods, wts, latent, s0f, s0b, cache_k, cache_v):
    bsz, seq, _ = x.shape
    n = bsz * seq
    x2d = x.reshape(n, D_MODEL)
    sh1, sc1, gt1, sh2, sc2, gt2 = mods
    p, tail = _in_proj(x2d, seq, sc1, sh1, wts["g1"], wts["w_in_r"])
    o_gla, st_f, st_b = _gla(p, tail, bsz, seq, wts["wdf"], wts["bdf"], wts["wdb"], wts["bdb"], wts["g_gla"],
                             s0f, s0b)
    if latent:
        cos_t, sin_t = _rope_tables(seq)
        o_swa = _swa_latent(p, bsz, seq, cache_k, cache_v, wts["sink"], cos_t, sin_t)
    else:
        o_swa = _swa_context(p, bsz, seq, wts["sink"])
    merged = _branch_merge(o_gla, o_swa, p, wts["wa"], wts["wb"])
    x1, h2, logits3 = _out_proj(merged, x2d, seq, gt1, sc2, sh2, wts["g2"], wts["wo"], wts["wr2"])
    cap = CAPACITY_FACTOR * n // N_EXPERTS
    aff3, sel3, pos3 = _select(logits3, cap)
    base = jnp.concatenate([pos3[:, :, 0].astype(jnp.int32), jnp.full((1, N_EXPERTS), cap, jnp.int32)], axis=0)
    lists = _compact(base.reshape(-1), aff3, sel3, pos3, cap)
    idx_flat = lists[:cap, :N_EXPERTS].T.astype(jnp.int32).reshape(-1)
    gate_col = lists[:cap, N_EXPERTS:2 * N_EXPERTS].T.reshape(-1, 1)
    hid = _expert_up(idx_flat, h2, wts["wg"], wts["wu"], cap)
    ye = _expert_down(hid, gate_col, wts["wd"], cap)
    tt = 256
    lo_tab = base[::tt // LANES]
    y = _combine(idx_flat, lo_tab.reshape(-1), ye, x1, seq, gt2, wts["g_final"], cap, tt)
    return y.reshape(bsz, seq, D_MODEL), tail, st_f, st_b


def kernel(x_prompt, x_sample, state_gla_fwd, state_gla_bwd, cache_k, cache_v, c, c_ctx, w_mod, b_mod,
           g_norm1, w_in, w_dec_f, b_dec_f, w_dec_b, b_dec_b, g_gla, attn_sink, w_branch_a, w_branch_b,
           w_out, g_norm2, w_router, w_exp_gate, w_exp_up, w_exp_down, g_final):
    bp, sp, _ = x_prompt.shape
    bl = x_sample.shape[0]
    l = 0
    w_in_r = _reorder_w_in(w_in, l)
    zpad = jnp.zeros((LANES - 2 * GLA_RANK, GLA_KW), F32)
    wr_hi = w_router[l].astype(BF16)
    wr_lo = (w_router[l] - wr_hi.astype(F32)).astype(BF16)
    wr_pad = jnp.zeros((D_MODEL, LANES - 2 * N_EXPERTS), BF16)
    zr = jnp.zeros((GLA_RANK, GLA_KW), F32)
    wts = {
        "g1": g_norm1[l].reshape(1, D_MODEL),
        "w_in_r": w_in_r,
        "wdf": jnp.concatenate([w_dec_f[l], zr, zpad], axis=0),
        "wdb": jnp.concatenate([zr, w_dec_b[l], zpad], axis=0),
        "bdf": b_dec_f[l].reshape(1, GLA_KW),
        "bdb": b_dec_b[l].reshape(1, GLA_KW),
        "g_gla": g_gla[l].reshape(1, GLA_VW),
        "sink": attn_sink[l],
        "wa": w_branch_a[l].astype(BF16),
        "wb": w_branch_b[l].astype(BF16),
        "wo": w_out[l].astype(BF16),
        "g2": g_norm2[l].reshape(1, D_MODEL),
        "wr2": jnp.concatenate([wr_hi, wr_lo, wr_pad], axis=1),
        "wg": w_exp_gate[l],
        "wu": w_exp_up[l],
        "wd": w_exp_down[l],
        "g_final": g_final.reshape(1, D_MODEL),
    }
    cond8 = jnp.concatenate([c_ctx[None, :], c, jnp.zeros((8 - 1 - bl, D_MODEL), F32)], axis=0)
    mod = _modulation(cond8, w_mod[l], b_mod[l]).reshape(8, N_MOD, 1, D_MODEL)
    mods_ctx = tuple(mod[0:1, j] for j in range(N_MOD))
    mods_lat = tuple(mod[1:1 + bl, j] for j in range(N_MOD))

    zero_state = jnp.zeros((1, GLA_HEADS, GLA_DK, GLA_DV), F32)
    y_prompt, tail_ctx, st_f, st_b = _layer(x_prompt, mods_ctx, wts, False, zero_state, zero_state, None, None)
    ck = cache_k[:, l].reshape(bl, -1, SWA_KVW)
    cv = cache_v[:, l].reshape(bl, -1, SWA_KVW)
    y_sample, _, _, _ = _layer(x_sample, mods_lat, wts, True, state_gla_fwd[:, l], state_gla_bwd[:, l], ck, cv)

    ksl = slice(COL_SK - TAIL_COL0, COL_SK - TAIL_COL0 + SWA_KVW)
    vsl = slice(COL_SV - TAIL_COL0, COL_SV - TAIL_COL0 + SWA_KVW)
    new_k = tail_ctx[:, ksl].reshape(bp, 1, sp, SWA_KV_HEADS, SWA_HEAD_DIM)
    new_v = tail_ctx[:, vsl].reshape(bp, 1, sp, SWA_KV_HEADS, SWA_HEAD_DIM)
    return (y_prompt, y_sample, st_f[:, None], st_b[:, None], new_k, new_v)
```
